```python
import math
import jax
import jax.numpy as jnp
from jax import lax
import numpy as np

D_MODEL = 1024
BATCH = 16
SEQ = 2048
DEPTH = 1

D_MIX = D_MODEL
DA_HEADS = 4
DA_HEAD_DIM = D_MIX // 16
DA_V_DIM = 2 * DA_HEAD_DIM
DA_WIDTH = DA_HEADS * DA_V_DIM
ML_HEADS = 4
ML_WIDTH = D_MIX - DA_WIDTH
ML_HEAD_DIM = ML_WIDTH // ML_HEADS
ML_CHUNK = 128
CONV_K = 4
ROPE_THETA = 10000.0
Q_BLOCK = 128
RMS_EPS = 1e-6
N_GROUPS = 4
EXPERTS_PER_GROUP = 8
N_EXPERTS = N_GROUPS * EXPERTS_PER_GROUP
TOP_K = 2
D_EXPERT = D_MODEL // 2
MOE_BLOCK = 128
COL_WIDTHS = (DA_HEADS * 2 * DA_HEAD_DIM, DA_HEADS * 2 * DA_HEAD_DIM, DA_WIDTH,
              ML_WIDTH, ML_WIDTH, ML_WIDTH, ML_WIDTH, ML_HEADS, ML_HEADS)
D_IN = sum(COL_WIDTHS)

kernel_name = 'hybrid_diffattn_mlstm_hmoe_layer'


def rmsnorm(x, g):
    xf = x.astype(jnp.float32)
    y = xf * lax.rsqrt(jnp.mean(xf * xf, axis=-1, keepdims=True) + RMS_EPS)
    return (y * g.astype(jnp.float32)).astype(x.dtype)


def split_cols(t):
    out, off = [], 0
    for w in COL_WIDTHS:
        out.append(t[..., off:off + w])
        off += w
    return out


def rope_tables(seq, dim):
    inv = 1.0 / (ROPE_THETA ** (jnp.arange(0, dim, 2, dtype=jnp.float32) / dim))
    ang = jnp.arange(seq, dtype=jnp.float32)[:, None] * inv[None, :]
    return jnp.cos(ang), jnp.sin(ang)


def apply_rope(x, cos, sin):
    x1, x2 = jnp.split(x, 2, axis=-1)
    c = cos.astype(x.dtype)
    s = sin.astype(x.dtype)
    return jnp.concatenate([x1 * c - x2 * s, x1 * s + x2 * c], axis=-1)


def diff_attention(q, k, v, lam):
    B, H, _, S, dh = q.shape
    nb = S // Q_BLOCK
    scale = dh ** -0.5
    qb = q.reshape(B, H, 2, nb, Q_BLOCK, dh).transpose(3, 0, 1, 2, 4, 5)
    key_pos = jnp.arange(S)

    def block(args):
        qi, bi = args
        s = jnp.einsum('bhcqd,bhckd->bhcqk', qi, k).astype(jnp.float32) * scale
        q_pos = bi * Q_BLOCK + jnp.arange(Q_BLOCK)
        mask = key_pos[None, :] <= q_pos[:, None]
        p = jax.nn.softmax(jnp.where(mask, s, -jnp.inf), axis=-1)
        a = p[:, :, 0] - lam * p[:, :, 1]
        return jnp.einsum('bhqk,bhkd->bhqd', a.astype(v.dtype), v)

    o = lax.map(block, (qb, jnp.arange(nb)))
    return o.transpose(1, 2, 0, 3, 4).reshape(B, H, S, v.shape[-1])


def causal_conv(x, w, b):
    S = x.shape[1]
    xp = jnp.pad(x, ((0, 0), (CONV_K - 1, 0), (0, 0)))
    y = b
    for j in range(CONV_K):
        y = y + xp[:, j:j + S] * w[j]
    return y


def mlstm_chunkwise(q, k, v, i_pre, f_pre):
    dtype = q.dtype
    f32 = jnp.float32
    B, H, S, d = q.shape
    L = ML_CHUNK
    nc = S // L
    q = q.astype(f32)
    k = k.astype(f32) * (d ** -0.5)
    v = v.astype(f32)
    log_f = jax.nn.log_sigmoid(f_pre.astype(f32))
    log_i = i_pre.astype(f32)

    def to_chunks(t):
        return jnp.moveaxis(t.reshape(B, H, nc, L, *t.shape[3:]), 2, 0)

    xs = (to_chunks(q), to_chunks(k), to_chunks(v), to_chunks(log_f), to_chunks(log_i))
    causal = jnp.tril(jnp.ones((L, L), dtype=bool))

    def step(carry, inp):
        C, n, m = carry
        qj, kj, vj, lf, li = inp
        b = jnp.cumsum(lf, axis=-1)
        Dm = jnp.where(causal, b[..., :, None] - b[..., None, :] + li[..., None, :], -jnp.inf)
        inter = b + m[..., None]
        m_row = jnp.maximum(inter, jnp.max(Dm, axis=-1))
        Sm = jnp.exp(Dm - m_row[..., None]) * jnp.einsum('bhld,bhsd->bhls', qj, kj)
        sc_in = jnp.exp(inter - m_row)
        num = jnp.einsum('bhls,bhsd->bhld', Sm, vj) + sc_in[..., None] * jnp.einsum('bhvk,bhlk->bhlv', C, qj)
        den = jnp.sum(Sm, axis=-1) + sc_in * jnp.einsum('bhk,bhlk->bhl', n, qj)
        h = num / jnp.maximum(jnp.abs(den), jnp.exp(-m_row))[..., None]
        bL = b[..., -1]
        dec = bL[..., None] - b + li
        m_new = jnp.maximum(bL + m, jnp.max(dec, axis=-1))
        ws = jnp.exp(dec - m_new[..., None])
        sc = jnp.exp(bL + m - m_new)
        C_new = sc[..., None, None] * C + jnp.einsum('bhs,bhsv,bhsk->bhvk', ws, vj, kj)
        n_new = sc[..., None] * n + jnp.einsum('bhs,bhsk->bhk', ws, kj)
        return (C_new, n_new, m_new), h

    init = (jnp.zeros((B, H, d, d), f32), jnp.zeros((B, H, d), f32), jnp.zeros((B, H), f32))
    _, h = lax.scan(step, init, xs)
    return jnp.moveaxis(h, 0, 2).reshape(B, H, S, d).astype(dtype)


def hier_moe(x, w_grp, b_grp, w_erouter, b_erouter, w1, w3, w2):
    T, D = x.shape
    g_logits = (x @ w_grp).astype(jnp.float32) + b_grp
    g_prob = jax.nn.softmax(g_logits, axis=-1)
    g_sel = jnp.argmax(g_logits, axis=-1).astype(jnp.int32)
    g_w = jnp.take_along_axis(g_prob, g_sel[:, None], axis=-1)[:, 0]
    e_all = jnp.einsum('td,gde->tge', x, w_erouter).astype(jnp.float32) + b_erouter
    e_logits = jnp.take_along_axis(e_all, g_sel[:, None, None], axis=1)[:, 0]
    top_v, top_i = lax.top_k(e_logits, TOP_K)
    e_w = jax.nn.softmax(top_v, axis=-1) * g_w[:, None]
    e_id = g_sel[:, None] * EXPERTS_PER_GROUP + top_i.astype(jnp.int32)

    A = T * TOP_K
    flat_e = e_id.reshape(A)
    flat_tok = jnp.repeat(jnp.arange(T, dtype=jnp.int32), TOP_K)
    flat_w = e_w.reshape(A)
    order = jnp.argsort(flat_e)
    se, stok, sw = flat_e[order], flat_tok[order], flat_w[order]
    counts = jnp.bincount(flat_e, length=N_EXPERTS).astype(jnp.int32)
    starts = jnp.cumsum(counts) - counts
    pcounts = (counts + MOE_BLOCK - 1) // MOE_BLOCK * MOE_BLOCK
    pend = jnp.cumsum(pcounts)
    pstart = pend - pcounts
    dest = pstart[se] + (jnp.arange(A, dtype=jnp.int32) - starts[se])
    P = A + N_EXPERTS * MOE_BLOCK
    nb = P // MOE_BLOCK
    tok_buf = jnp.zeros((P,), jnp.int32).at[dest].set(stok)
    w_buf = jnp.zeros((P,), x.dtype).at[dest].set(sw.astype(x.dtype))
    blk_e = jnp.minimum(jnp.searchsorted(pend, jnp.arange(nb, dtype=jnp.int32) * MOE_BLOCK, side='right'),
                        N_EXPERTS - 1)
    xb = x[tok_buf].reshape(nb, MOE_BLOCK, D)

    def expert_block(args):
        xi, e = args
        hdn = jax.nn.silu(xi @ w1[e]) * (xi @ w3[e])
        return hdn @ w2[e]

    yb = lax.map(expert_block, (xb, blk_e)).reshape(P, D)
    return jnp.zeros_like(x).at[tok_buf].add(yb * w_buf[:, None])


def setup_inputs(seed: int = 0) -> dict:
    key = jax.random.key(seed)
    ks = jax.random.split(key, 24)
    f32 = jnp.float32

    def nrm(k, shape, s):
        return jax.random.normal(k, shape, f32) * s

    x = nrm(ks[0], (BATCH, SEQ, D_MODEL), 1.0)
    w_in = nrm(ks[1], (DEPTH, D_MODEL, D_IN), D_MODEL ** -0.5)
    conv_w = nrm(ks[2], (DEPTH, CONV_K, 2 * ML_WIDTH), CONV_K ** -0.5)
    conv_b = nrm(ks[3], (DEPTH, 2 * ML_WIDTH), 0.01)
    i_b = nrm(ks[4], (DEPTH, 1, ML_HEADS), 0.1)
    f_b = 3.0 + 3.0 * jax.random.uniform(ks[5], (DEPTH, 1, ML_HEADS), f32)
    gate_b = jnp.concatenate([i_b, f_b], axis=1)
    lam_qk = nrm(ks[6], (DEPTH, 4, DA_HEAD_DIM), 0.1)
    subln_g = 1.0 + nrm(ks[7], (DEPTH, DA_V_DIM), 0.02)
    mhnorm_g = 1.0 + nrm(ks[8], (DEPTH, ML_HEADS, ML_HEAD_DIM), 0.02)
    w_out = nrm(ks[9], (DEPTH, D_MIX, D_MODEL), D_MIX ** -0.5)
    g_mix = 1.0 + nrm(ks[10], (DEPTH, D_MODEL), 0.02)
    g_ffn = 1.0 + nrm(ks[11], (DEPTH, D_MODEL), 0.02)
    w_grp = nrm(ks[12], (DEPTH, D_MODEL, N_GROUPS), D_MODEL ** -0.5)
    b_grp = nrm(ks[13], (DEPTH, N_GROUPS), 0.01)
    w_erouter = nrm(ks[14], (DEPTH, N_GROUPS, D_MODEL, EXPERTS_PER_GROUP), D_MODEL ** -0.5)
    b_erouter = nrm(ks[15], (DEPTH, N_GROUPS, EXPERTS_PER_GROUP), 0.01)
    w1 = nrm(ks[16], (DEPTH, N_EXPERTS, D_MODEL, D_EXPERT), D_MODEL ** -0.5)
    w3 = nrm(ks[17], (DEPTH, N_EXPERTS, D_MODEL, D_EXPERT), D_MODEL ** -0.5)
    w2 = nrm(ks[18], (DEPTH, N_EXPERTS, D_EXPERT, D_MODEL), D_EXPERT ** -0.5)
    g_final = 1.0 + nrm(ks[19], (D_MODEL,), 0.02)
    return {'x': x, 'w_in': w_in, 'conv_w': conv_w, 'conv_b': conv_b, 'gate_b': gate_b,
            'lam_qk': lam_qk, 'subln_g': subln_g, 'mhnorm_g': mhnorm_g, 'w_out': w_out,
            'g_mix': g_mix, 'g_ffn': g_ffn, 'w_grp': w_grp, 'b_grp': b_grp,
            'w_erouter': w_erouter, 'b_erouter': b_erouter, 'w1': w1, 'w3': w3, 'w2': w2,
            'g_final': g_final}


def reference(x, w_in, conv_w, conv_b, gate_b, lam_qk, subln_g, mhnorm_g, w_out,
              g_mix, g_ffn, w_grp, b_grp, w_erouter, b_erouter, w1, w3, w2, g_final):
    B, S, D = x.shape
    cos, sin = rope_tables(S, DA_HEAD_DIM)
    for l in range(DEPTH):
        h = rmsnorm(x, g_mix[l])
        proj = h @ w_in[l]
        qa, ka, va, qm, km, vm, om, ig, fg = split_cols(proj)

        lambda_init = 0.8 - 0.6 * math.exp(-0.3 * l)
        lq = lam_qk[l].astype(jnp.float32)
        lam = jnp.exp(jnp.sum(lq[0] * lq[1])) - jnp.exp(jnp.sum(lq[2] * lq[3])) + lambda_init
        qa = apply_rope(qa.reshape(B, S, DA_HEADS, 2, DA_HEAD_DIM).transpose(0, 2, 3, 1, 4), cos, sin)
        ka = apply_rope(ka.reshape(B, S, DA_HEADS, 2, DA_HEAD_DIM).transpose(0, 2, 3, 1, 4), cos, sin)
        va = va.reshape(B, S, DA_HEADS, DA_V_DIM).transpose(0, 2, 1, 3)
        oa = diff_attention(qa, ka, va, lam)
        oa = rmsnorm(oa, subln_g[l]) * (1.0 - lambda_init)
        oa = oa.transpose(0, 2, 1, 3).reshape(B, S, DA_WIDTH)

        qk = jax.nn.silu(causal_conv(jnp.concatenate([qm, km], axis=-1), conv_w[l], conv_b[l]))
        qm, km = qk[..., :ML_WIDTH], qk[..., ML_WIDTH:]

        def heads(t):
            return t.reshape(B, S, ML_HEADS, ML_HEAD_DIM).transpose(0, 2, 1, 3)

        i_pre = (ig + gate_b[l, 0]).transpose(0, 2, 1)
        f_pre = (fg + gate_b[l, 1]).transpose(0, 2, 1)
        hm = mlstm_chunkwise(heads(qm), heads(km), heads(vm), i_pre, f_pre)
        hm = hm.transpose(0, 2, 1, 3) * jax.nn.sigmoid(om.reshape(B, S, ML_HEADS, ML_HEAD_DIM))
        hm = rmsnorm(hm, mhnorm_g[l]).reshape(B, S, ML_WIDTH)

        x = x + jnp.concatenate([oa, hm], axis=-1) @ w_out[l]

        hn = rmsnorm(x, g_ffn[l]).reshape(B * S, D)
        y = hier_moe(hn, w_grp[l], b_grp[l], w_erouter[l], b_erouter[l], w1[l], w3[l], w2[l])
        x = x + y.reshape(B, S, D)
    return rmsnorm(x, g_final)
```

```python
import functools
import math

import jax
import jax.numpy as jnp
from jax import lax
from jax.experimental import pallas as pl
from jax.experimental.pallas import tpu as pltpu

F32 = jnp.float32
BF16 = jnp.bfloat16
I32 = jnp.int32

D_MODEL = 1024
DA_HEADS = 4
DA_HEAD_DIM = 64
DA_V_DIM = 128
DA_WIDTH = 512
ML_HEADS = 4
ML_WIDTH = 512
ML_HEAD_DIM = 128
ML_CHUNK = 128
CONV_K = 4
ROPE_THETA = 10000.0
RMS_EPS = 1e-6
N_GROUPS = 4
EXPERTS_PER_GROUP = 8
N_EXPERTS = 32
D_EXPERT = 512
LAMBDA_INIT = 0.8 - 0.6 * math.exp(-0.3 * 0)

LANES = 128
N_MAIN = 7 * 512
NEG = -1e30

TM_PROJ = 512
TQ = 256
TK = 256
TM = 256
BLK = 256
SUB = 8
DCH = 32
CCH = 16
DLOC_ROWS = 768
LOC_ROWS = 2 * TM + N_EXPERTS * CCH
XW = D_MODEL // 2 + LANES
EXP_LANE0 = 4


def _nt_dot(a, b):
    return lax.dot_general(a, b, (((1,), (1,)), ((), ())), preferred_element_type=F32)


def _tn_dot(a, b):
    return lax.dot_general(a, b, (((0,), (0,)), ((), ())), preferred_element_type=F32)


def _dot(a, b):
    return jnp.dot(a, b, preferred_element_type=F32)


def _split3(x):
    hi = x.astype(BF16)
    r = x - hi.astype(F32)
    mid = r.astype(BF16)
    lo = (r - mid.astype(F32)).astype(BF16)
    return hi, mid, lo


def _rms(x, g):
    return x * lax.rsqrt(jnp.mean(x * x, axis=-1, keepdims=True) + RMS_EPS) * g


def _inproj_kernel(x_ref, g_ref, w_ref, wg_ref, wgt_ref, proj_ref, gcol_ref, grow_ref):
    h = _rms(x_ref[...], g_ref[...]).astype(BF16)
    ct = 512
    for c in range(N_MAIN // ct):
        proj_ref[:, c * ct:(c + 1) * ct] = _dot(h, w_ref[:, c * ct:(c + 1) * ct]).astype(BF16)
    gcol_ref[...] = _dot(h, wg_ref[...])
    grow_ref[...] = _nt_dot(wgt_ref[...], h)


def _inproj(x2d, g_mix, w_main, wg, wgt):
    T = x2d.shape[0]
    tm = TM_PROJ
    return pl.pallas_call(
        _inproj_kernel,
        grid=(T // tm,),
        in_specs=[
            pl.BlockSpec((tm, D_MODEL), lambda i: (i, 0)),
            pl.BlockSpec((1, D_MODEL), lambda i: (0, 0)),
            pl.BlockSpec((D_MODEL, N_MAIN), lambda i: (0, 0)),
            pl.BlockSpec((D_MODEL, LANES), lambda i: (0, 0)),
            pl.BlockSpec((8, D_MODEL), lambda i: (0, 0)),
        ],
        out_specs=[
            pl.BlockSpec((tm, N_MAIN), lambda i: (i, 0)),
            pl.BlockSpec((tm, LANES), lambda i: (i, 0)),
            pl.BlockSpec((8, tm), lambda i: (0, i)),
        ],
        out_shape=[
            jax.ShapeDtypeStruct((T, N_MAIN), BF16),
            jax.ShapeDtypeStruct((T, LANES), F32),
            jax.ShapeDtypeStruct((8, T), F32),
        ],
        compiler_params=pltpu.CompilerParams(
            dimension_semantics=("arbitrary",), vmem_limit_bytes=56 * 1024 * 1024),
        name="inproj",
    )(x2d, g_mix, w_main, wg, wgt)


def _attn_kernel(q_ref, k_ref, v_ref, c_ref, sa_ref, sb_ref, lam_ref, g_ref, o_ref,
                 q1_s, q2_s, k_s, m_s, l_s, acc_s):
    S = q_ref.shape[0]
    lane = lax.broadcasted_iota(I32, (TQ, LANES), 1)

    def rope(x, rows):
        return (x * c_ref[rows, :] + pltpu.roll(x, 96, 1) * sa_ref[rows, :]
                + pltpu.roll(x, 32, 1) * sb_ref[rows, :])

    for r in range(S // TQ):
        rows = slice(r * TQ, (r + 1) * TQ)
        qr = rope(q_ref[rows, :].astype(F32), rows) * (DA_HEAD_DIM ** -0.5)
        q1_s[rows, :] = jnp.where(lane < DA_HEAD_DIM, qr, 0.0).astype(BF16)
        q2_s[rows, :] = jnp.where(lane >= DA_HEAD_DIM, qr, 0.0).astype(BF16)
        k_s[rows, :] = rope(k_ref[rows, :].astype(F32), rows).astype(BF16)

    lq = lam_ref[...]
    lam = (jnp.exp(jnp.sum(lq[0:1] * lq[1:2], axis=-1, keepdims=True))
           - jnp.exp(jnp.sum(lq[2:3] * lq[3:4], axis=-1, keepdims=True)) + LAMBDA_INIT)

    row = lax.broadcasted_iota(I32, (2 * TQ, TK), 0)
    col = lax.broadcasted_iota(I32, (2 * TQ, TK), 1)
    causal = col <= jnp.where(row >= TQ, row - TQ, row)

    def qtile(i, carry):
        qo = pl.multiple_of(i * TQ, TQ)
        qq = jnp.concatenate([q1_s[pl.ds(qo, TQ), :], q2_s[pl.ds(qo, TQ), :]], axis=0)
        m_s[...] = jnp.full(m_s.shape, NEG, F32)
        l_s[...] = jnp.zeros(l_s.shape, F32)
        acc_s[...] = jnp.zeros(acc_s.shape, F32)

        def kvstep(j, masked):
            ko = pl.multiple_of(j * TK, TK)
            s = _nt_dot(qq, k_s[pl.ds(ko, TK), :])
            if masked:
                s = jnp.where(causal, s, NEG)
            m_prev = m_s[...]
            m_new = jnp.maximum(m_prev, jnp.max(s, axis=-1, keepdims=True))
            alpha = jnp.exp(m_prev - m_new)
            p = jnp.exp(s - m_new)
            l_s[...] = alpha * l_s[...] + jnp.sum(p, axis=-1, keepdims=True)
            acc_s[...] = alpha * acc_s[...] + _dot(p.astype(BF16), v_ref[pl.ds(ko, TK), :])
            m_s[...] = m_new

        def kvbody(j, c):
            kvstep(j, False)
            return c

        lax.fori_loop(0, i, kvbody, 0)
        kvstep(i, True)

        acc = acc_s[...]
        l = l_s[...]
        o = acc[:TQ] / l[:TQ] - lam * (acc[TQ:] / l[TQ:])
        o = _rms(o, g_ref[...]) * (1.0 - LAMBDA_INIT)
        o_ref[pl.ds(qo, TQ), :] = o.astype(BF16)
        return carry

    lax.fori_loop(0, S // TQ, qtile, 0)


def _attention(proj, rope_c, rope_sa, rope_sb, lam_qk, subln_g, B, S):
    T = B * S
    nh = DA_WIDTH // LANES
    full = lambda shape: pl.BlockSpec(shape, lambda b, h: (0, 0))
    return pl.pallas_call(
        _attn_kernel,
        grid=(B, DA_HEADS),
        in_specs=[
            pl.BlockSpec((S, LANES), lambda b, h: (b, h)),
            pl.BlockSpec((S, LANES), lambda b, h: (b, nh + h)),
            pl.BlockSpec((S, LANES), lambda b, h: (b, 2 * nh + h)),
            full((S, LANES)), full((S, LANES)), full((S, LANES)),
            full((4, DA_HEAD_DIM)), full((1, LANES)),
        ],
        out_specs=pl.BlockSpec((S, LANES), lambda b, h: (b, h)),
        out_shape=jax.ShapeDtypeStruct((T, DA_WIDTH), BF16),
        scratch_shapes=[
            pltpu.VMEM((S, LANES), BF16), pltpu.VMEM((S, LANES), BF16), pltpu.VMEM((S, LANES), BF16),
            pltpu.VMEM((2 * TQ, 1), F32), pltpu.VMEM((2 * TQ, 1), F32),
            pltpu.VMEM((2 * TQ, LANES), F32),
        ],
        compiler_params=pltpu.CompilerParams(
            dimension_semantics=("arbitrary", "arbitrary"), vmem_limit_bytes=48 * 1024 * 1024),
        name="attn",
    )(proj, proj, proj, rope_c, rope_sa, rope_sb, lam_qk, subln_g)


def _log_sigmoid(x):
    return jnp.minimum(x, 0.0) - jnp.log(1.0 + jnp.exp(-jnp.abs(x)))


def _mlstm_kernel(q_ref, k_ref, v_ref, o_ref, gcol_ref, grow_ref, cw_ref, cb_ref, gbc_ref, gbr_ref,
                  mg_ref, out_ref, qc_s, kc_s, ct_s, m_s):
    S = q_ref.shape[0]
    L = ML_CHUNK
    nc = S // L
    row16 = lax.broadcasted_iota(I32, (16, ML_WIDTH), 0)

    def conv_silu(x_ref, c, ro, w, b):
        cur = x_ref[pl.ds(ro, L), :].astype(F32)
        po = pl.multiple_of(jnp.maximum(ro - 16, 0), 16)
        tail = jnp.where(c > 0, x_ref[pl.ds(po, 16), :].astype(F32), 0.0)
        y = b + cur * w[3:4]
        for s in (1, 2, 3):
            sh = pltpu.roll(cur, s, 0)
            top = jnp.where(row16 < s, pltpu.roll(tail, s, 0), sh[:16])
            sh = jnp.concatenate([top, sh[16:]], axis=0)
            y = y + sh * w[3 - s:4 - s]
        return y * jax.nn.sigmoid(y)

    def conv_body(c, carry):
        ro = pl.multiple_of(c * L, L)
        qc_s[pl.ds(ro, L), :] = conv_silu(
            q_ref, c, ro, cw_ref[:, :ML_WIDTH], cb_ref[:, :ML_WIDTH]).astype(BF16)
        kc = conv_silu(k_ref, c, ro, cw_ref[:, ML_WIDTH:], cb_ref[:, ML_WIDTH:])
        kc_s[pl.ds(ro, L), :] = (kc * (ML_HEAD_DIM ** -0.5)).astype(BF16)
        return carry

    lax.fori_loop(0, nc, conv_body, 0)

    ct_s[...] = jnp.zeros(ct_s.shape, F32)
    m_s[...] = jnp.zeros(m_s.shape, F32)

    ri = lax.broadcasted_iota(I32, (L, L), 0)
    ci = lax.broadcasted_iota(I32, (L, L), 1)
    causal = ci <= ri
    tril = jnp.where(causal, 1.0, 0.0).astype(BF16)
    triu = jnp.where(ri <= ci, 1.0, 0.0).astype(BF16)
    ones_v = jnp.ones((L, ML_HEAD_DIM), BF16)

    def chunk(c, carry):
        ro = pl.multiple_of(c * L, L)
        gc = gcol_ref[pl.ds(ro, L), :] + gbc_ref[...]
        gr = grow_ref[:, pl.ds(ro, L)] + gbr_ref[...]
        lf_c = _log_sigmoid(gc)
        lf_r = _log_sigmoid(gr)
        b_c = sum(_dot(tril, p) for p in _split3(lf_c))
        b_r = sum(_dot(p, triu) for p in _split3(lf_r))
        for h in range(ML_HEADS):
            hs = slice(h * ML_HEAD_DIM, (h + 1) * ML_HEAD_DIM)
            bc = b_c[:, 4 + h:5 + h]
            lic = gc[:, h:h + 1]
            br = b_r[4 + h:5 + h, :]
            lir = gr[h:h + 1, :]
            m = m_s[h:h + 1, 0:1]
            q = qc_s[pl.ds(ro, L), hs]
            k = kc_s[pl.ds(ro, L), hs]
            vaug = jnp.concatenate([v_ref[pl.ds(ro, L), hs], ones_v], axis=1)
            dm = jnp.where(causal, bc - br + lir, NEG)
            inter = bc + m
            m_row = jnp.maximum(inter, jnp.max(dm, axis=-1, keepdims=True))
            sm = jnp.exp(dm - m_row) * _nt_dot(q, k)
            sc_in = jnp.exp(inter - m_row)
            ca = ct_s[h]
            a = _dot(sm.astype(BF16), vaug) + sc_in * _dot(q, ca.astype(BF16))
            num = a[:, :ML_HEAD_DIM]
            den = a[:, ML_HEAD_DIM:]
            hh = num / jnp.maximum(jnp.abs(den), jnp.exp(-m_row))
            bl = br[:, L - 1:L]
            dec = bl - bc + lic
            m_new = jnp.maximum(bl + m, jnp.max(dec, axis=0, keepdims=True))
            ws = jnp.exp(dec - m_new)
            sc = jnp.exp(bl + m - m_new)
            kw = (k.astype(F32) * ws).astype(BF16)
            ct_s[h] = sc * ca + _tn_dot(kw, vaug)
            m_s[h:h + 1, :] = jnp.broadcast_to(m_new, (1, LANES))
            hg = hh * jax.nn.sigmoid(o_ref[pl.ds(ro, L), hs].astype(F32))
            out_ref[pl.ds(ro, L), hs] = _rms(hg, mg_ref[:, hs]).astype(BF16)
        return carry

    lax.fori_loop(0, nc, chunk, 0)


def _mlstm(proj, gcol, grow, conv_w, conv_b, gb_col, gb_row, mh_g, B, S):
    T = B * S
    full = lambda shape: pl.BlockSpec(shape, lambda b: (0, 0))
    cb = lambda j: pl.BlockSpec((S, ML_WIDTH), lambda b, j=j: (b, j))
    return pl.pallas_call(
        _mlstm_kernel,
        grid=(B,),
        in_specs=[
            cb(3), cb(4), cb(5), cb(6),
            pl.BlockSpec((S, LANES), lambda b: (b, 0)),
            pl.BlockSpec((8, S), lambda b: (0, b)),
            full((CONV_K, 2 * ML_WIDTH)), full((1, 2 * ML_WIDTH)),
            full((1, LANES)), full((8, LANES)), full((1, ML_WIDTH)),
        ],
        out_specs=pl.BlockSpec((S, ML_WIDTH), lambda b: (b, 0)),
        out_shape=jax.ShapeDtypeStruct((T, ML_WIDTH), BF16),
        scratch_shapes=[
            pltpu.VMEM((S, ML_WIDTH), BF16), pltpu.VMEM((S, ML_WIDTH), BF16),
            pltpu.VMEM((ML_HEADS, ML_HEAD_DIM, 2 * ML_HEAD_DIM), F32),
            pltpu.VMEM((8, LANES), F32),
        ],
        compiler_params=pltpu.CompilerParams(
            dimension_semantics=("arbitrary",), vmem_limit_bytes=56 * 1024 * 1024),
        name="mlstm",
    )(proj, proj, proj, proj, gcol, grow, conv_w, conv_b, gb_col, gb_row, mh_g)


def _router_kernel(oa_ref, hm_ref, x_ref, wo_ref, g_ref, wrh_ref, wrl_ref, br_ref,
                   x2_ref, hn_ref, ri_ref, rw_ref, cnt_ref):
    mixo = _dot(oa_ref[...], wo_ref[:DA_WIDTH, :]) + _dot(hm_ref[...], wo_ref[DA_WIDTH:, :])
    x2 = x_ref[...] + mixo
    x2_ref[...] = x2
    hn = _rms(x2, g_ref[...])
    hn_hi = hn.astype(BF16)
    hn_ref[...] = hn_hi
    hn_lo = (hn - hn_hi.astype(F32)).astype(BF16)
    logits = (_dot(hn_hi, wrh_ref[...]) + _dot(hn_lo, wrh_ref[...]) + _dot(hn_hi, wrl_ref[...])
              + br_ref[...])
    lane = lax.broadcasted_iota(I32, (TM, LANES), 1)
    lane_f = lane.astype(F32)

    def lmax(v):
        return jnp.max(v, axis=-1, keepdims=True)

    def first_idx(mask):
        return jnp.min(jnp.where(mask, lane_f, 1e6), axis=-1, keepdims=True).astype(I32)

    gl = jnp.where(lane < N_GROUPS, logits, NEG)
    gmax = lmax(gl)
    gsel = first_idx(gl == gmax)
    g_w = 1.0 / jnp.sum(jnp.exp(gl - gmax), axis=-1, keepdims=True)
    elo = EXP_LANE0 + gsel * EXPERTS_PER_GROUP
    el = jnp.where((lane >= elo) & (lane < elo + EXPERTS_PER_GROUP), logits, NEG)
    v1 = lmax(el)
    i1 = first_idx(el == v1)
    el2 = jnp.where(lane == i1, NEG, el)
    v2 = lmax(el2)
    i2 = first_idx(el2 == v2)
    t = jnp.exp(v2 - v1)
    w0 = g_w / (1.0 + t)
    w1 = g_w * t / (1.0 + t)

    oh0 = jnp.where(lane == i1, 1.0, 0.0)
    oh1 = jnp.where(lane == i2, 1.0, 0.0)
    mh = oh0 + oh1
    r_i = lax.broadcasted_iota(I32, (TM, TM), 0)
    c_i = lax.broadcasted_iota(I32, (TM, TM), 1)
    lstrict = jnp.where(c_i < r_i, 1.0, 0.0).astype(BF16)
    pre = _dot(lstrict, mh.astype(BF16))
    cnt = jnp.sum(mh, axis=0, keepdims=True)
    cnt8 = jnp.floor((cnt + (SUB - 1)) * (1.0 / SUB)) * SUB
    cnt16 = jnp.floor((cnt + (CCH - 1)) * (1.0 / CCH)) * CCH
    u_r = lax.broadcasted_iota(I32, (LANES, LANES), 0)
    u_c = lax.broadcasted_iota(I32, (LANES, LANES), 1)
    ustrict = jnp.where(u_r < u_c, 1.0, 0.0).astype(BF16)
    lo = _dot(jnp.broadcast_to(cnt8, (8, LANES)).astype(BF16), ustrict)[0:1]
    lo16 = _dot(jnp.broadcast_to(cnt16, (8, LANES)).astype(BF16), ustrict)[0:1]

    def lsum(v):
        return jnp.sum(v, axis=-1, keepdims=True)

    pos0 = lsum(oh0 * (pre + lo))
    pos1 = lsum(oh1 * (pre + lo))
    pad0 = lsum(oh0 * (pre + lo16))
    pad1 = lsum(oh1 * (pre + lo16))

    def put(vals):
        out = jnp.zeros((TM, LANES), vals[0].dtype)
        for j, v in enumerate(vals):
            out = jnp.where(lane == j, v, out)
        return out

    ri_ref[...] = put([i1 - EXP_LANE0, i2 - EXP_LANE0, pos0.astype(I32), pos1.astype(I32),
                       pad0.astype(I32), pad1.astype(I32)])
    a0 = jnp.floor(pos0 * (1.0 / 16.0))
    a1 = jnp.floor(pos1 * (1.0 / 16.0))
    rw_ref[...] = put([w0, w1, a0, pos0 - 16.0 * a0, a1, pos1 - 16.0 * a1])
    cnt_ref[0] = cnt.astype(I32)


def _router(oa, hm, x2d, w_out, g_ffn, wr_hi, wr_lo, b_r):
    T = x2d.shape[0]
    nt = T // TM
    full = lambda shape: pl.BlockSpec(shape, lambda i: (0, 0))
    return pl.pallas_call(
        _router_kernel,
        grid=(nt,),
        in_specs=[
            pl.BlockSpec((TM, DA_WIDTH), lambda i: (i, 0)),
            pl.BlockSpec((TM, ML_WIDTH), lambda i: (i, 0)),
            pl.BlockSpec((TM, D_MODEL), lambda i: (i, 0)),
            full((D_MODEL, D_MODEL)), full((1, D_MODEL)),
            full((D_MODEL, LANES)), full((D_MODEL, LANES)), full((1, LANES)),
        ],
        out_specs=[
            pl.BlockSpec((TM, D_MODEL), lambda i: (i, 0)),
            pl.BlockSpec((TM, D_MODEL), lambda i: (i, 0)),
            pl.BlockSpec((TM, LANES), lambda i: (i, 0)),
            pl.BlockSpec((TM, LANES), lambda i: (i, 0)),
            pl.BlockSpec((1, 1, LANES), lambda i: (i, 0, 0)),
        ],
        out_shape=[
            jax.ShapeDtypeStruct((T, D_MODEL), F32),
            jax.ShapeDtypeStruct((T, D_MODEL), BF16),
            jax.ShapeDtypeStruct((T, LANES), I32),
            jax.ShapeDtypeStruct((T, LANES), F32),
            jax.ShapeDtypeStruct((nt, 1, LANES), I32),
        ],
        compiler_params=pltpu.CompilerParams(
            dimension_semantics=("arbitrary",), vmem_limit_bytes=40 * 1024 * 1024),
        name="router",
    )(oa, hm, x2d, w_out, g_ffn, wr_hi, wr_lo, b_r)


def _pack_halves(x):
    lo = lax.bitcast_convert_type(x[:, :D_MODEL // 2], I32)
    hi = lax.bitcast_convert_type(x[:, D_MODEL // 2:], I32)
    return lax.shift_right_logical(lo, 16) | (hi & jnp.int32(-65536))


def _unpack_halves(w):
    lo = lax.bitcast_convert_type(lax.shift_left(w, 16), F32)
    hi = lax.bitcast_convert_type(w & jnp.int32(-65536), F32)
    return lo.astype(BF16), hi.astype(BF16)


def _dispatch_kernel(cnt_sm, lo_sm, dst_sm, hn_ref, rw_ref, xb_in, xb_ref, xs_s, nd_s, semx):
    del xb_in
    i = pl.program_id(0)
    nt = pl.num_programs(0)
    slot = i % 2
    half = D_MODEL // 2

    @pl.when(i == 0)
    def _():
        for s in range(2):
            xs_s[s, DLOC_ROWS:, :] = jnp.zeros((DCH, XW), I32)

    rw = rw_ref[...]
    lane8 = lax.broadcasted_iota(I32, (8, LANES), 1)
    sub8 = lax.broadcasted_iota(I32, (8, LANES), 0)
    sel = jnp.where((sub8 == 0) & (lane8 == 2), 16.0,
                    jnp.where((sub8 == 0) & (lane8 == 3), 1.0,
                              jnp.where((sub8 == 1) & (lane8 == 4), 16.0,
                                        jnp.where((sub8 == 1) & (lane8 == 5), 1.0, 0.0))))
    post = _nt_dot(sel.astype(BF16), rw.astype(BF16))
    r_i = lax.broadcasted_iota(I32, (DLOC_ROWS, TM), 0).astype(F32)
    perm0 = jnp.where(r_i == post[0:1, :], 1.0, 0.0).astype(BF16)
    perm1 = jnp.where(r_i == post[1:2, :], 1.0, 0.0).astype(BF16)
    xs = _dot(perm0 + perm1, hn_ref[...])
    xs_s[slot, :DLOC_ROWS, :half] = _pack_halves(xs)
    wsort = jnp.zeros((DLOC_ROWS, LANES), F32)
    for k, perm in enumerate((perm0, perm1)):
        wrep = jnp.broadcast_to(rw[:, k:k + 1], (TM, LANES))
        for part in _split3(wrep):
            wsort = wsort + _dot(perm, part)
    xs_s[slot, :DLOC_ROWS, half:] = lax.bitcast_convert_type(wsort, I32)

    def xcopy(s, src, dst):
        return pltpu.make_async_copy(xs_s.at[s, pl.ds(src, DCH), :], xb_ref.at[pl.ds(dst, DCH), :], semx)

    def wait_all(n):
        def wbody(j, c):
            xcopy(0, 0, 0).wait()
            return c
        lax.fori_loop(0, n, wbody, 0)

    @pl.when(i > 0)
    def _():
        wait_all(nd_s[0])

    def ebody(e, nd):
        n = cnt_sm[i * N_EXPERTS + e]
        lo = pl.multiple_of(lo_sm[i * N_EXPERTS + e], SUB)
        dst = pl.multiple_of(dst_sm[i * N_EXPERTS + e], SUB)

        def cbody(c, nd2):
            xcopy(slot, lo + c * DCH, dst + c * DCH).start()
            return nd2 + 1

        return lax.fori_loop(0, (n + DCH - 1) // DCH, cbody, nd)

    nd = lax.fori_loop(0, N_EXPERTS, ebody, jnp.int32(0))
    nd_s[0] = nd

    @pl.when(i == nt - 1)
    def _():
        wait_all(nd)


def _dispatch(cnt_f, lo_f, dst_f, hn, rw, xb0):
    T = hn.shape[0]
    nt = T // TM
    grid_spec = pltpu.PrefetchScalarGridSpec(
        num_scalar_prefetch=3,
        grid=(nt,),
        in_specs=[
            pl.BlockSpec((TM, D_MODEL), lambda i, *_: (i, 0)),
            pl.BlockSpec((TM, LANES), lambda i, *_: (i, 0)),
            pl.BlockSpec(memory_space=pl.ANY),
        ],
        out_specs=pl.BlockSpec(memory_space=pl.ANY),
        scratch_shapes=[
            pltpu.VMEM((2, DLOC_ROWS + DCH, XW), I32),
            pltpu.SMEM((1,), I32),
            pltpu.SemaphoreType.DMA(()),
        ],
    )
    return pl.pallas_call(
        _dispatch_kernel,
        grid_spec=grid_spec,
        out_shape=jax.ShapeDtypeStruct(xb0.shape, I32),
        input_output_aliases={5: 0},
        compiler_params=pltpu.CompilerParams(
            dimension_semantics=("arbitrary",), vmem_limit_bytes=40 * 1024 * 1024),
        name="dispatch",
    )(cnt_f, lo_f, dst_f, hn, rw, xb0)


def _expert_kernel(be_sm, nu_sm, xb_ref, w1_ref, w3_ref, w2_ref, yb_ref):
    p = pl.program_id(0)

    @pl.when(p < nu_sm[0])
    def _():
        half = D_MODEL // 2
        xlo, xhi = _unpack_halves(xb_ref[:, :half])
        h1 = _dot(xlo, w1_ref[:half, :]) + _dot(xhi, w1_ref[half:, :])
        h3 = _dot(xlo, w3_ref[:half, :]) + _dot(xhi, w3_ref[half:, :])
        wrep = lax.bitcast_convert_type(xb_ref[:, half:], F32)
        wfull = jnp.concatenate([wrep] * (D_EXPERT // LANES), axis=1)
        hdn = (h1 * jax.nn.sigmoid(h1) * h3 * wfull).astype(BF16)
        y = _dot(hdn, w2_ref[...]).astype(BF16).astype(F32)
        yb_ref[...] = _pack_halves(y)

    @pl.when(p >= nu_sm[0])
    def _():
        yb_ref[...] = jnp.zeros(yb_ref.shape, I32)


def _experts(blk_e, nused, xb, w1, w3, w2, nb):
    def rows(p, be, nu):
        return (jnp.minimum(p, nu[0] - 1), 0)

    def wsel(p, be, nu):
        return (be[jnp.minimum(p, nu[0] - 1)], 0, 0)

    grid_spec = pltpu.PrefetchScalarGridSpec(
        num_scalar_prefetch=2,
        grid=(nb,),
        in_specs=[
            pl.BlockSpec((BLK, XW), rows),
            pl.BlockSpec((None, D_MODEL, D_EXPERT), wsel),
            pl.BlockSpec((None, D_MODEL, D_EXPERT), wsel),
            pl.BlockSpec((None, D_EXPERT, D_MODEL), wsel),
        ],
        out_specs=pl.BlockSpec((BLK, D_MODEL // 2), lambda p, be, nu: (p, 0)),
    )
    return pl.pallas_call(
        _expert_kernel,
        grid_spec=grid_spec,
        out_shape=jax.ShapeDtypeStruct((nb * BLK, D_MODEL // 2), I32),
        compiler_params=pltpu.CompilerParams(
            dimension_semantics=("arbitrary",), vmem_limit_bytes=40 * 1024 * 1024),
        name="experts",
    )(blk_e, nused, xb, w1, w3, w2)


def _combine_kernel(cnt_sm, lo_sm, dst_sm, yb_ref, x2_ref, ri_ref, g_ref, out_ref, yl_s, nd_s, sem):
    i = pl.program_id(0)
    nt = pl.num_programs(0)
    slot = i % 2

    def ycopy(s, src, dst):
        return pltpu.make_async_copy(yb_ref.at[pl.ds(src, CCH), :], yl_s.at[s, pl.ds(dst, CCH), :],
                                     sem.at[s])

    def issue(tile, s):
        def ebody(e, nd):
            n = cnt_sm[tile * N_EXPERTS + e]
            lo = pl.multiple_of(lo_sm[tile * N_EXPERTS + e], CCH)
            src = pl.multiple_of(dst_sm[tile * N_EXPERTS + e], SUB)

            def cbody(c, nd2):
                ycopy(s, src + c * CCH, lo + c * CCH).start()
                return nd2 + 1

            return lax.fori_loop(0, (n + CCH - 1) // CCH, cbody, nd)

        nd_s[s] = lax.fori_loop(0, N_EXPERTS, ebody, jnp.int32(0))

    @pl.when(i == 0)
    def _():
        yl_s[...] = jnp.zeros(yl_s.shape, I32)
        issue(0, 0)

    @pl.when(i + 1 < nt)
    def _():
        issue(i + 1, 1 - slot)

    def wbody(j, c):
        ycopy(slot, 0, 0).wait()
        return c

    lax.fori_loop(0, nd_s[slot], wbody, 0)

    ri = ri_ref[...]
    pad0 = ri[:, 4:5]
    pad1 = ri[:, 5:6]
    lane = lax.broadcasted_iota(I32, (TM, BLK), 1)
    ylo = jnp.zeros((TM, D_MODEL // 2), F32)
    yhi = jnp.zeros((TM, D_MODEL // 2), F32)
    for c in range(LOC_ROWS // BLK):
        r = lane + c * BLK
        selm = (jnp.where(pad0 == r, 1.0, 0.0) + jnp.where(pad1 == r, 1.0, 0.0)).astype(BF16)
        lo_h, hi_h = _unpack_halves(yl_s[slot, c * BLK:(c + 1) * BLK, :])
        ylo = ylo + _dot(selm, lo_h)
        yhi = yhi + _dot(selm, hi_h)
    x3 = x2_ref[...] + jnp.concatenate([ylo, yhi], axis=1)
    out_ref[...] = _rms(x3, g_ref[...])


def _combine(cnt_f, lo16_f, dst_f, yb, x2, ri, g_final):
    T = x2.shape[0]
    nt = T // TM
    grid_spec = pltpu.PrefetchScalarGridSpec(
        num_scalar_prefetch=3,
        grid=(nt,),
        in_specs=[
            pl.BlockSpec(memory_space=pl.ANY),
            pl.BlockSpec((TM, D_MODEL), lambda i, *_: (i, 0)),
            pl.BlockSpec((TM, LANES), lambda i, *_: (i, 0)),
            pl.BlockSpec((1, D_MODEL), lambda i, *_: (0, 0)),
        ],
        out_specs=pl.BlockSpec((TM, D_MODEL), lambda i, *_: (i, 0)),
        scratch_shapes=[
            pltpu.VMEM((2, LOC_ROWS, D_MODEL // 2), I32),
            pltpu.SMEM((2,), I32),
            pltpu.SemaphoreType.DMA((2,)),
        ],
    )
    return pl.pallas_call(
        _combine_kernel,
        grid_spec=grid_spec,
        out_shape=jax.ShapeDtypeStruct((T, D_MODEL), F32),
        compiler_params=pltpu.CompilerParams(
            dimension_semantics=("arbitrary",), vmem_limit_bytes=40 * 1024 * 1024),
        name="combine",
    )(cnt_f, lo16_f, dst_f, yb, x2, ri, g_final)


def _rope_tables(S):
    half = DA_HEAD_DIM // 2
    inv = 1.0 / (ROPE_THETA ** (jnp.arange(0, DA_HEAD_DIM, 2, dtype=F32) / DA_HEAD_DIM))
    ang = jnp.arange(S, dtype=F32)[:, None] * inv[None, :]
    cos, sin = jnp.cos(ang), jnp.sin(ang)
    lane = jnp.arange(LANES)
    idx = lane % half
    lower = (lane % DA_HEAD_DIM) < half
    c = cos[:, idx]
    s = sin[:, idx]
    return c, jnp.where(lower[None, :], -s, 0.0), jnp.where(lower[None, :], 0.0, s)


def kernel(x, w_in, conv_w, conv_b, gate_b, lam_qk, subln_g, mhnorm_g, w_out, g_mix, g_ffn, w_grp,
           b_grp, w_erouter, b_erouter, w1, w3, w2, g_final):
    B, S, D = x.shape
    T = B * S
    nt = T // TM
    x2d = x.reshape(T, D)
    l = 0

    w_main = w_in[l, :, :N_MAIN].astype(BF16)
    wg8 = w_in[l, :, N_MAIN:]
    wg = jnp.pad(wg8, ((0, 0), (0, LANES - 8))).astype(BF16)
    wgt = wg8.T.astype(BF16)
    gb8 = gate_b[l].reshape(8)
    gb_col = jnp.pad(gb8, (0, LANES - 8)).reshape(1, LANES)
    gb_row = jnp.broadcast_to(gb8[:, None], (8, LANES))
    rope_c, rope_sa, rope_sb = _rope_tables(S)
    w_r = jnp.concatenate(
        [w_grp[l], w_erouter[l].transpose(1, 0, 2).reshape(D, N_EXPERTS)], axis=1)
    w_r = jnp.pad(w_r, ((0, 0), (0, LANES - w_r.shape[1])))
    wr_hi = w_r.astype(BF16)
    wr_lo = (w_r - wr_hi.astype(F32)).astype(BF16)
    b_r = jnp.pad(jnp.concatenate([b_grp[l], b_erouter[l].reshape(N_EXPERTS)]),
                  (0, LANES - N_GROUPS - N_EXPERTS)).reshape(1, LANES)

    proj, gcol, grow = _inproj(x2d, g_mix[l].reshape(1, D), w_main, wg, wgt)
    oa = _attention(proj, rope_c, rope_sa, rope_sb, lam_qk[l], subln_g[l].reshape(1, LANES), B, S)
    hm = _mlstm(proj, gcol, grow, conv_w[l], conv_b[l].reshape(1, -1), gb_col, gb_row,
                mhnorm_g[l].reshape(1, ML_WIDTH), B, S)
    x2, hn, ri, rw, cnt3 = _router(oa, hm, x2d, w_out[l].astype(BF16), g_ffn[l].reshape(1, D),
                                   wr_hi, wr_lo, b_r)

    cnt = cnt3[:, 0, EXP_LANE0:EXP_LANE0 + N_EXPERTS]
    c8 = (cnt + SUB - 1) // SUB * SUB
    pcount = (jnp.sum(c8, axis=0) + DCH + BLK - 1) // BLK * BLK
    pend = jnp.cumsum(pcount)
    pstart = pend - pcount
    dst = pstart[None, :] + jnp.cumsum(c8, axis=0) - c8
    lo = jnp.cumsum(c8, axis=1) - c8
    c16 = (cnt + CCH - 1) // CCH * CCH
    lo16 = jnp.cumsum(c16, axis=1) - c16
    nb = (2 * T + nt * N_EXPERTS * (SUB - 1) + N_EXPERTS * (DCH + BLK - 1)) // BLK + 1
    blk_e = jnp.minimum(
        jnp.searchsorted(pend, jnp.arange(nb, dtype=I32) * BLK, side='right'), N_EXPERTS - 1
    ).astype(I32)
    nused = (pend[-1] // BLK).astype(I32).reshape(1)
    cnt_f = cnt.reshape(-1).astype(I32)
    dst_f = dst.reshape(-1).astype(I32)

    xb0 = jnp.zeros((nb * BLK, XW), I32)
    xb = _dispatch(cnt_f, lo.reshape(-1).astype(I32), dst_f, hn, rw, xb0)
    yb = _experts(blk_e, nused, xb, w1[l].astype(BF16), w3[l].astype(BF16), w2[l].astype(BF16), nb)
    out = _combine(cnt_f, lo16.reshape(-1).astype(I32), dst_f, yb, x2, ri, g_final.reshape(1, D))
    return out.reshape(B, S, D)
```

```python
import functools
import math

import jax
import jax.numpy as jnp
from jax import lax
from jax.experimental import pallas as pl
from jax.experimental.pallas import tpu as pltpu

F32 = jnp.float32
BF16 = jnp.bfloat16
I32 = jnp.int32

D_MODEL = 1024
DA_HEADS = 4
DA_HEAD_DIM = 64
DA_V_DIM = 128
DA_WIDTH = 512
ML_HEADS = 4
ML_WIDTH = 512
ML_HEAD_DIM = 128
ML_CHUNK = 128
CONV_K = 4
ROPE_THETA = 10000.0
RMS_EPS = 1e-6
N_GROUPS = 4
EXPERTS_PER_GROUP = 8
N_EXPERTS = 32
D_EXPERT = 512
LAMBDA_INIT = 0.8 - 0.6 * math.exp(-0.3 * 0)

LANES = 128
N_MAIN = 7 * 512
NEG = -1e30

TM_PROJ = 512
TQ = 256
TK = 256
VT_ROWS = DA_V_DIM + 16
TM = 256
BLK = 256
SUB = 8
DCH = 32
CCH = 16
DLOC_ROWS = 768
LOC_ROWS = 2 * TM + N_EXPERTS * CCH
XW = D_MODEL // 2 + LANES
EXP_LANE0 = 4


def _nt_dot(a, b):
    return lax.dot_general(a, b, (((1,), (1,)), ((), ())), preferred_element_type=F32)


def _tn_dot(a, b):
    return lax.dot_general(a, b, (((0,), (0,)), ((), ())), preferred_element_type=F32)


def _dot(a, b):
    return jnp.dot(a, b, preferred_element_type=F32)


def _split3(x):
    hi = x.astype(BF16)
    r = x - hi.astype(F32)
    mid = r.astype(BF16)
    lo = (r - mid.astype(F32)).astype(BF16)
    return hi, mid, lo


def _rms(x, g):
    return x * lax.rsqrt(jnp.mean(x * x, axis=-1, keepdims=True) + RMS_EPS) * g


def _inproj_kernel(x_ref, g_ref, w_ref, wg_ref, wgt_ref, proj_ref, gcol_ref, grow_ref):
    h = _rms(x_ref[...], g_ref[...]).astype(BF16)
    ct = 512
    for c in range(N_MAIN // ct):
        proj_ref[:, c * ct:(c + 1) * ct] = _dot(h, w_ref[:, c * ct:(c + 1) * ct]).astype(BF16)
    gcol_ref[...] = _dot(h, wg_ref[...])
    grow_ref[...] = _nt_dot(wgt_ref[...], h)


def _inproj(x2d, g_mix, w_main, wg, wgt):
    T = x2d.shape[0]
    tm = TM_PROJ
    return pl.pallas_call(
        _inproj_kernel,
        grid=(T // tm,),
        in_specs=[
            pl.BlockSpec((tm, D_MODEL), lambda i: (i, 0)),
            pl.BlockSpec((1, D_MODEL), lambda i: (0, 0)),
            pl.BlockSpec((D_MODEL, N_MAIN), lambda i: (0, 0)),
            pl.BlockSpec((D_MODEL, LANES), lambda i: (0, 0)),
            pl.BlockSpec((8, D_MODEL), lambda i: (0, 0)),
        ],
        out_specs=[
            pl.BlockSpec((tm, N_MAIN), lambda i: (i, 0)),
            pl.BlockSpec((tm, LANES), lambda i: (i, 0)),
            pl.BlockSpec((8, tm), lambda i: (0, i)),
        ],
        out_shape=[
            jax.ShapeDtypeStruct((T, N_MAIN), BF16),
            jax.ShapeDtypeStruct((T, LANES), F32),
            jax.ShapeDtypeStruct((8, T), F32),
        ],
        compiler_params=pltpu.CompilerParams(
            dimension_semantics=("arbitrary",), vmem_limit_bytes=56 * 1024 * 1024),
        name="inproj",
    )(x2d, g_mix, w_main, wg, wgt)


def _attn_kernel(q_ref, k_ref, v_ref, c_ref, sa_ref, sb_ref, lam_ref, g_ref, o_ref,
                 q1_s, q2_s, k_s, vt_s, acc_s):
    S = q_ref.shape[0]
    lane = lax.broadcasted_iota(I32, (TQ, LANES), 1)
    qscale = DA_HEAD_DIM ** -0.5 * math.log2(math.e)

    def rope(x, rows):
        return (x * c_ref[rows, :] + pltpu.roll(x, 96, 1) * sa_ref[rows, :]
                + pltpu.roll(x, 32, 1) * sb_ref[rows, :])

    def prep(r, carry):
        rows = pl.ds(pl.multiple_of(r * TQ, TQ), TQ)
        for h in range(DA_HEADS):
            hs = slice(h * LANES, (h + 1) * LANES)
            qr = rope(q_ref[rows, hs].astype(F32), rows) * qscale
            q1_s[rows, hs] = jnp.where(lane < DA_HEAD_DIM, qr, 0.0).astype(BF16)
            q2_s[rows, hs] = jnp.where(lane >= DA_HEAD_DIM, qr, 0.0).astype(BF16)
            k_s[rows, hs] = rope(k_ref[rows, hs].astype(F32), rows).astype(BF16)
            vt_s[h, :DA_V_DIM, rows] = v_ref[rows, hs].astype(F32).T.astype(BF16)
        return carry

    lax.fori_loop(0, S // TQ, prep, 0)
    vt_s[:, DA_V_DIM:, :] = jnp.ones((DA_HEADS, VT_ROWS - DA_V_DIM, S), BF16)

    lq = lam_ref[...]
    lam = (jnp.exp(jnp.sum(lq[0:1] * lq[1:2], axis=-1, keepdims=True))
           - jnp.exp(jnp.sum(lq[2:3] * lq[3:4], axis=-1, keepdims=True)) + LAMBDA_INIT)

    key = lax.broadcasted_iota(I32, (TK, 2 * TQ), 0)
    qry = lax.broadcasted_iota(I32, (TK, 2 * TQ), 1)
    causal = key <= jnp.where(qry >= TQ, qry - TQ, qry)

    def qtile(i, carry):
        qo = pl.multiple_of(i * TQ, TQ)
        acc_s[...] = jnp.zeros(acc_s.shape, F32)

        def kvstep(j, ms, masked):
            ko = pl.multiple_of(j * TK, TK)
            out = []
            for h in range(DA_HEADS):
                hs = slice(h * LANES, (h + 1) * LANES)
                qq = jnp.concatenate([q1_s[pl.ds(qo, TQ), hs], q2_s[pl.ds(qo, TQ), hs]], axis=0)
                st = _nt_dot(k_s[pl.ds(ko, TK), hs], qq)
                if masked:
                    st = jnp.where(causal, st, NEG)
                m_new = jnp.maximum(ms[h], jnp.max(st, axis=0, keepdims=True))
                alpha = jnp.exp2(ms[h] - m_new)
                p = jnp.exp2(st - m_new).astype(BF16)
                acc_s[h] = alpha * acc_s[h] + _dot(vt_s[h, :, pl.ds(ko, TK)], p)
                out.append(m_new)
            return tuple(out)

        m0 = tuple(jnp.full((1, 2 * TQ), NEG, F32) for _ in range(DA_HEADS))
        m1 = lax.fori_loop(0, i, lambda j, ms: kvstep(j, ms, False), m0)
        kvstep(i, m1, True)

        for h in range(DA_HEADS):
            hs = slice(h * LANES, (h + 1) * LANES)
            acc = acc_s[h]
            on = acc[:DA_V_DIM] * (1.0 / acc[DA_V_DIM:DA_V_DIM + 1])
            ot = on[:, :TQ] - lam * on[:, TQ:]
            o = _rms(ot.T, g_ref[...]) * (1.0 - LAMBDA_INIT)
            o_ref[pl.ds(qo, TQ), hs] = o.astype(BF16)
        return carry

    lax.fori_loop(0, S // TQ, qtile, 0)


def _attention(proj, rope_c, rope_sa, rope_sb, lam_qk, subln_g, B, S):
    T = B * S
    full = lambda shape: pl.BlockSpec(shape, lambda b: (0, 0))
    cb = lambda j: pl.BlockSpec((S, DA_WIDTH), lambda b, j=j: (b, j))
    return pl.pallas_call(
        _attn_kernel,
        grid=(B,),
        in_specs=[
            cb(0), cb(1), cb(2),
            full((S, LANES)), full((S, LANES)), full((S, LANES)),
            full((4, DA_HEAD_DIM)), full((1, LANES)),
        ],
        out_specs=pl.BlockSpec((S, DA_WIDTH), lambda b: (b, 0)),
        out_shape=jax.ShapeDtypeStruct((T, DA_WIDTH), BF16),
        scratch_shapes=[
            pltpu.VMEM((S, DA_WIDTH), BF16), pltpu.VMEM((S, DA_WIDTH), BF16),
            pltpu.VMEM((S, DA_WIDTH), BF16),
            pltpu.VMEM((DA_HEADS, VT_ROWS, S), BF16),
            pltpu.VMEM((DA_HEADS, VT_ROWS, 2 * TQ), F32),
        ],
        compiler_params=pltpu.CompilerParams(
            dimension_semantics=("arbitrary",), vmem_limit_bytes=48 * 1024 * 1024),
        name="attn",
    )(proj, proj, proj, rope_c, rope_sa, rope_sb, lam_qk, subln_g)


def _log_sigmoid(x):
    return jnp.minimum(x, 0.0) - jnp.log(1.0 + jnp.exp(-jnp.abs(x)))


def _mlstm_kernel(q_ref, k_ref, v_ref, o_ref, gcol_ref, grow_ref, cw_ref, cb_ref, gbc_ref, gbr_ref,
                  mg_ref, out_ref, qc_s, kc_s, ct_s, m_s):
    S = q_ref.shape[0]
    L = ML_CHUNK
    nc = S // L
    row16 = lax.broadcasted_iota(I32, (16, ML_WIDTH), 0)

    def conv_silu(x_ref, c, ro, w, b):
        cur = x_ref[pl.ds(ro, L), :].astype(F32)
        po = pl.multiple_of(jnp.maximum(ro - 16, 0), 16)
        tail = jnp.where(c > 0, x_ref[pl.ds(po, 16), :].astype(F32), 0.0)
        y = b + cur * w[3:4]
        for s in (1, 2, 3):
            sh = pltpu.roll(cur, s, 0)
            top = jnp.where(row16 < s, pltpu.roll(tail, s, 0), sh[:16])
            sh = jnp.concatenate([top, sh[16:]], axis=0)
            y = y + sh * w[3 - s:4 - s]
        return y * jax.nn.sigmoid(y)

    def conv_body(c, carry):
        ro = pl.multiple_of(c * L, L)
        qc_s[pl.ds(ro, L), :] = conv_silu(
            q_ref, c, ro, cw_ref[:, :ML_WIDTH], cb_ref[:, :ML_WIDTH]).astype(BF16)
        kc = conv_silu(k_ref, c, ro, cw_ref[:, ML_WIDTH:], cb_ref[:, ML_WIDTH:])
        kc_s[pl.ds(ro, L), :] = (kc * (ML_HEAD_DIM ** -0.5)).astype(BF16)
        return carry

    lax.fori_loop(0, nc, conv_body, 0)

    ct_s[...] = jnp.zeros(ct_s.shape, F32)
    m_s[...] = jnp.zeros(m_s.shape, F32)

    ri = lax.broadcasted_iota(I32, (L, L), 0)
    ci = lax.broadcasted_iota(I32, (L, L), 1)
    causal = ci <= ri
    tril = jnp.where(causal, 1.0, 0.0).astype(BF16)
    triu = jnp.where(ri <= ci, 1.0, 0.0).astype(BF16)
    ones_v = jnp.ones((L, ML_HEAD_DIM), BF16)

    def chunk(c, carry):
        ro = pl.multiple_of(c * L, L)
        gc = gcol_ref[pl.ds(ro, L), :] + gbc_ref[...]
        gr = grow_ref[:, pl.ds(ro, L)] + gbr_ref[...]
        lf_c = _log_sigmoid(gc)
        lf_r = _log_sigmoid(gr)
        b_c = sum(_dot(tril, p) for p in _split3(lf_c))
        b_r = sum(_dot(p, triu) for p in _split3(lf_r))
        for h in range(ML_HEADS):
            hs = slice(h * ML_HEAD_DIM, (h + 1) * ML_HEAD_DIM)
            bc = b_c[:, 4 + h:5 + h]
            lic = gc[:, h:h + 1]
            br = b_r[4 + h:5 + h, :]
            lir = gr[h:h + 1, :]
            m = m_s[h:h + 1, 0:1]
            q = qc_s[pl.ds(ro, L), hs]
            k = kc_s[pl.ds(ro, L), hs]
            vaug = jnp.concatenate([v_ref[pl.ds(ro, L), hs], ones_v], axis=1)
            dm = jnp.where(causal, bc - br + lir, NEG)
            inter = bc + m
            m_row = jnp.maximum(inter, jnp.max(dm, axis=-1, keepdims=True))
            sm = jnp.exp(dm - m_row) * _nt_dot(q, k)
            sc_in = jnp.exp(inter - m_row)
            ca = ct_s[h]
            a = _dot(sm.astype(BF16), vaug) + sc_in * _dot(q, ca.astype(BF16))
            num = a[:, :ML_HEAD_DIM]
            den = a[:, ML_HEAD_DIM:]
            hh = num / jnp.maximum(jnp.abs(den), jnp.exp(-m_row))
            bl = br[:, L - 1:L]
            dec = bl - bc + lic
            m_new = jnp.maximum(bl + m, jnp.max(dec, axis=0, keepdims=True))
            ws = jnp.exp(dec - m_new)
            sc = jnp.exp(bl + m - m_new)
            kw = (k.astype(F32) * ws).astype(BF16)
            ct_s[h] = sc * ca + _tn_dot(kw, vaug)
            m_s[h:h + 1, :] = jnp.broadcast_to(m_new, (1, LANES))
            hg = hh * jax.nn.sigmoid(o_ref[pl.ds(ro, L), hs].astype(F32))
            out_ref[pl.ds(ro, L), hs] = _rms(hg, mg_ref[:, hs]).astype(BF16)
        return carry

    lax.fori_loop(0, nc, chunk, 0)


def _mlstm(proj, gcol, grow, conv_w, conv_b, gb_col, gb_row, mh_g, B, S):
    T = B * S
    full = lambda shape: pl.BlockSpec(shape, lambda b: (0, 0))
    cb = lambda j: pl.BlockSpec((S, ML_WIDTH), lambda b, j=j: (b, j))
    return pl.pallas_call(
        _mlstm_kernel,
        grid=(B,),
        in_specs=[
            cb(3), cb(4), cb(5), cb(6),
            pl.BlockSpec((S, LANES), lambda b: (b, 0)),
            pl.BlockSpec((8, S), lambda b: (0, b)),
            full((CONV_K, 2 * ML_WIDTH)), full((1, 2 * ML_WIDTH)),
            full((1, LANES)), full((8, LANES)), full((1, ML_WIDTH)),
        ],
        out_specs=pl.BlockSpec((S, ML_WIDTH), lambda b: (b, 0)),
        out_shape=jax.ShapeDtypeStruct((T, ML_WIDTH), BF16),
        scratch_shapes=[
            pltpu.VMEM((S, ML_WIDTH), BF16), pltpu.VMEM((S, ML_WIDTH), BF16),
            pltpu.VMEM((ML_HEADS, ML_HEAD_DIM, 2 * ML_HEAD_DIM), F32),
            pltpu.VMEM((8, LANES), F32),
        ],
        compiler_params=pltpu.CompilerParams(
            dimension_semantics=("arbitrary",), vmem_limit_bytes=56 * 1024 * 1024),
        name="mlstm",
    )(proj, proj, proj, proj, gcol, grow, conv_w, conv_b, gb_col, gb_row, mh_g)


def _router_kernel(oa_ref, hm_ref, x_ref, wo_ref, g_ref, wrh_ref, wrl_ref, br_ref,
                   x2_ref, hn_ref, ri_ref, rw_ref, cnt_ref):
    mixo = _dot(oa_ref[...], wo_ref[:DA_WIDTH, :]) + _dot(hm_ref[...], wo_ref[DA_WIDTH:, :])
    x2 = x_ref[...] + mixo
    x2_ref[...] = x2
    hn = _rms(x2, g_ref[...])
    hn_hi = hn.astype(BF16)
    hn_ref[...] = hn_hi
    hn_lo = (hn - hn_hi.astype(F32)).astype(BF16)
    logits = (_dot(hn_hi, wrh_ref[...]) + _dot(hn_lo, wrh_ref[...]) + _dot(hn_hi, wrl_ref[...])
              + br_ref[...])
    lane = lax.broadcasted_iota(I32, (TM, LANES), 1)
    lane_f = lane.astype(F32)

    def lmax(v):
        return jnp.max(v, axis=-1, keepdims=True)

    def first_idx(mask):
        return jnp.min(jnp.where(mask, lane_f, 1e6), axis=-1, keepdims=True).astype(I32)

    gl = jnp.where(lane < N_GROUPS, logits, NEG)
    gmax = lmax(gl)
    gsel = first_idx(gl == gmax)
    g_w = 1.0 / jnp.sum(jnp.exp(gl - gmax), axis=-1, keepdims=True)
    elo = EXP_LANE0 + gsel * EXPERTS_PER_GROUP
    el = jnp.where((lane >= elo) & (lane < elo + EXPERTS_PER_GROUP), logits, NEG)
    v1 = lmax(el)
    i1 = first_idx(el == v1)
    el2 = jnp.where(lane == i1, NEG, el)
    v2 = lmax(el2)
    i2 = first_idx(el2 == v2)
    t = jnp.exp(v2 - v1)
    w0 = g_w / (1.0 + t)
    w1 = g_w * t / (1.0 + t)

    oh0 = jnp.where(lane == i1, 1.0, 0.0)
    oh1 = jnp.where(lane == i2, 1.0, 0.0)
    mh = oh0 + oh1
    r_i = lax.broadcasted_iota(I32, (TM, TM), 0)
    c_i = lax.broadcasted_iota(I32, (TM, TM), 1)
    lstrict = jnp.where(c_i < r_i, 1.0, 0.0).astype(BF16)
    pre = _dot(lstrict, mh.astype(BF16))
    cnt = jnp.sum(mh, axis=0, keepdims=True)
    cnt8 = jnp.floor((cnt + (SUB - 1)) * (1.0 / SUB)) * SUB
    cnt16 = jnp.floor((cnt + (CCH - 1)) * (1.0 / CCH)) * CCH
    u_r = lax.broadcasted_iota(I32, (LANES, LANES), 0)
    u_c = lax.broadcasted_iota(I32, (LANES, LANES), 1)
    ustrict = jnp.where(u_r < u_c, 1.0, 0.0).astype(BF16)
    lo = _dot(jnp.broadcast_to(cnt8, (8, LANES)).astype(BF16), ustrict)[0:1]
    lo16 = _dot(jnp.broadcast_to(cnt16, (8, LANES)).astype(BF16), ustrict)[0:1]

    def lsum(v):
        return jnp.sum(v, axis=-1, keepdims=True)

    pos0 = lsum(oh0 * (pre + lo))
    pos1 = lsum(oh1 * (pre + lo))
    pad0 = lsum(oh0 * (pre + lo16))
    pad1 = lsum(oh1 * (pre + lo16))

    def put(vals):
        out = jnp.zeros((TM, LANES), vals[0].dtype)
        for j, v in enumerate(vals):
            out = jnp.where(lane == j, v, out)
        return out

    ri_ref[...] = put([i1 - EXP_LANE0, i2 - EXP_LANE0, pos0.astype(I32), pos1.astype(I32),
                       pad0.astype(I32), pad1.astype(I32)])
    a0 = jnp.floor(pos0 * (1.0 / 16.0))
    a1 = jnp.floor(pos1 * (1.0 / 16.0))
    rw_ref[...] = put([w0, w1, a0, pos0 - 16.0 * a0, a1, pos1 - 16.0 * a1])
    cnt_ref[0] = cnt.astype(I32)


def _router(oa, hm, x2d, w_out, g_ffn, wr_hi, wr_lo, b_r):
    T = x2d.shape[0]
    nt = T // TM
    full = lambda shape: pl.BlockSpec(shape, lambda i: (0, 0))
    return pl.pallas_call(
        _router_kernel,
        grid=(nt,),
        in_specs=[
            pl.BlockSpec((TM, DA_WIDTH), lambda i: (i, 0)),
            pl.BlockSpec((TM, ML_WIDTH), lambda i: (i, 0)),
            pl.BlockSpec((TM, D_MODEL), lambda i: (i, 0)),
            full((D_MODEL, D_MODEL)), full((1, D_MODEL)),
            full((D_MODEL, LANES)), full((D_MODEL, LANES)), full((1, LANES)),
        ],
        out_specs=[
            pl.BlockSpec((TM, D_MODEL), lambda i: (i, 0)),
            pl.BlockSpec((TM, D_MODEL), lambda i: (i, 0)),
            pl.BlockSpec((TM, LANES), lambda i: (i, 0)),
            pl.BlockSpec((TM, LANES), lambda i: (i, 0)),
            pl.BlockSpec((1, 1, LANES), lambda i: (i, 0, 0)),
        ],
        out_shape=[
            jax.ShapeDtypeStruct((T, D_MODEL), F32),
            jax.ShapeDtypeStruct((T, D_MODEL), BF16),
            jax.ShapeDtypeStruct((T, LANES), I32),
            jax.ShapeDtypeStruct((T, LANES), F32),
            jax.ShapeDtypeStruct((nt, 1, LANES), I32),
        ],
        compiler_params=pltpu.CompilerParams(
            dimension_semantics=("arbitrary",), vmem_limit_bytes=40 * 1024 * 1024),
        name="router",
    )(oa, hm, x2d, w_out, g_ffn, wr_hi, wr_lo, b_r)


def _pack_halves(x):
    lo = lax.bitcast_convert_type(x[:, :D_MODEL // 2], I32)
    hi = lax.bitcast_convert_type(x[:, D_MODEL // 2:], I32)
    return lax.shift_right_logical(lo, 16) | (hi & jnp.int32(-65536))


def _unpack_halves(w):
    lo = lax.bitcast_convert_type(lax.shift_left(w, 16), F32)
    hi = lax.bitcast_convert_type(w & jnp.int32(-65536), F32)
    return lo.astype(BF16), hi.astype(BF16)


def _dispatch_kernel(cnt_sm, lo_sm, dst_sm, hn_ref, rw_ref, xb_in, xb_ref, xs_s, nd_s, semx):
    del xb_in
    i = pl.program_id(0)
    nt = pl.num_programs(0)
    slot = i % 2
    half = D_MODEL // 2

    @pl.when(i == 0)
    def _():
        for s in range(2):
            xs_s[s, DLOC_ROWS:, :] = jnp.zeros((DCH, XW), I32)

    rw = rw_ref[...]
    lane8 = lax.broadcasted_iota(I32, (8, LANES), 1)
    sub8 = lax.broadcasted_iota(I32, (8, LANES), 0)
    sel = jnp.where((sub8 == 0) & (lane8 == 2), 16.0,
                    jnp.where((sub8 == 0) & (lane8 == 3), 1.0,
                              jnp.where((sub8 == 1) & (lane8 == 4), 16.0,
                                        jnp.where((sub8 == 1) & (lane8 == 5), 1.0, 0.0))))
    post = _nt_dot(sel.astype(BF16), rw.astype(BF16))
    r_i = lax.broadcasted_iota(I32, (DLOC_ROWS, TM), 0).astype(F32)
    perm0 = jnp.where(r_i == post[0:1, :], 1.0, 0.0).astype(BF16)
    perm1 = jnp.where(r_i == post[1:2, :], 1.0, 0.0).astype(BF16)
    xs = _dot(perm0 + perm1, hn_ref[...])
    xs_s[slot, :DLOC_ROWS, :half] = _pack_halves(xs)
    wsort = jnp.zeros((DLOC_ROWS, LANES), F32)
    for k, perm in enumerate((perm0, perm1)):
        wrep = jnp.broadcast_to(rw[:, k:k + 1], (TM, LANES))
        for part in _split3(wrep):
            wsort = wsort + _dot(perm, part)
    xs_s[slot, :DLOC_ROWS, half:] = lax.bitcast_convert_type(wsort, I32)

    def xcopy(s, src, dst):
        return pltpu.make_async_copy(xs_s.at[s, pl.ds(src, DCH), :], xb_ref.at[pl.ds(dst, DCH), :], semx)

    def wait_all(n):
        def wbody(j, c):
            xcopy(0, 0, 0).wait()
            return c
        lax.fori_loop(0, n, wbody, 0)

    @pl.when(i > 0)
    def _():
        wait_all(nd_s[0])

    def ebody(e, nd):
        n = cnt_sm[i * N_EXPERTS + e]
        lo = pl.multiple_of(lo_sm[i * N_EXPERTS + e], SUB)
        dst = pl.multiple_of(dst_sm[i * N_EXPERTS + e], SUB)

        def cbody(c, nd2):
            xcopy(slot, lo + c * DCH, dst + c * DCH).start()
            return nd2 + 1

        return lax.fori_loop(0, (n + DCH - 1) // DCH, cbody, nd)

    nd = lax.fori_loop(0, N_EXPERTS, ebody, jnp.int32(0))
    nd_s[0] = nd

    @pl.when(i == nt - 1)
    def _():
        wait_all(nd)


def _dispatch(cnt_f, lo_f, dst_f, hn, rw, xb0):
    T = hn.shape[0]
    nt = T // TM
    grid_spec = pltpu.PrefetchScalarGridSpec(
        num_scalar_prefetch=3,
        grid=(nt,),
        in_specs=[
            pl.BlockSpec((TM, D_MODEL), lambda i, *_: (i, 0)),
            pl.BlockSpec((TM, LANES), lambda i, *_: (i, 0)),
            pl.BlockSpec(memory_space=pl.ANY),
        ],
        out_specs=pl.BlockSpec(memory_space=pl.ANY),
        scratch_shapes=[
            pltpu.VMEM((2, DLOC_ROWS + DCH, XW), I32),
            pltpu.SMEM((1,), I32),
            pltpu.SemaphoreType.DMA(()),
        ],
    )
    return pl.pallas_call(
        _dispatch_kernel,
        grid_spec=grid_spec,
        out_shape=jax.ShapeDtypeStruct(xb0.shape, I32),
        input_output_aliases={5: 0},
        compiler_params=pltpu.CompilerParams(
            dimension_semantics=("arbitrary",), vmem_limit_bytes=40 * 1024 * 1024),
        name="dispatch",
    )(cnt_f, lo_f, dst_f, hn, rw, xb0)


def _expert_kernel(be_sm, nu_sm, xb_ref, w1_ref, w3_ref, w2_ref, yb_ref):
    p = pl.program_id(0)

    @pl.when(p < nu_sm[0])
    def _():
        half = D_MODEL // 2
        xlo, xhi = _unpack_halves(xb_ref[:, :half])
        h1 = _dot(xlo, w1_ref[:half, :]) + _dot(xhi, w1_ref[half:, :])
        h3 = _dot(xlo, w3_ref[:half, :]) + _dot(xhi, w3_ref[half:, :])
        wrep = lax.bitcast_convert_type(xb_ref[:, half:], F32)
        wfull = jnp.concatenate([wrep] * (D_EXPERT // LANES), axis=1)
        hdn = (h1 * jax.nn.sigmoid(h1) * h3 * wfull).astype(BF16)
        y = _dot(hdn, w2_ref[...]).astype(BF16).astype(F32)
        yb_ref[...] = _pack_halves(y)

    @pl.when(p >= nu_sm[0])
    def _():
        yb_ref[...] = jnp.zeros(yb_ref.shape, I32)


def _experts(blk_e, nused, xb, w1, w3, w2, nb):
    def rows(p, be, nu):
        return (jnp.minimum(p, nu[0] - 1), 0)

    def wsel(p, be, nu):
        return (be[jnp.minimum(p, nu[0] - 1)], 0, 0)

    grid_spec = pltpu.PrefetchScalarGridSpec(
        num_scalar_prefetch=2,
        grid=(nb,),
        in_specs=[
            pl.BlockSpec((BLK, XW), rows),
            pl.BlockSpec((None, D_MODEL, D_EXPERT), wsel),
            pl.BlockSpec((None, D_MODEL, D_EXPERT), wsel),
            pl.BlockSpec((None, D_EXPERT, D_MODEL), wsel),
        ],
        out_specs=pl.BlockSpec((BLK, D_MODEL // 2), lambda p, be, nu: (p, 0)),
    )
    return pl.pallas_call(
        _expert_kernel,
        grid_spec=grid_spec,
        out_shape=jax.ShapeDtypeStruct((nb * BLK, D_MODEL // 2), I32),
        compiler_params=pltpu.CompilerParams(
            dimension_semantics=("arbitrary",), vmem_limit_bytes=40 * 1024 * 1024),
        name="experts",
    )(blk_e, nused, xb, w1, w3, w2)


def _combine_kernel(cnt_sm, lo_sm, dst_sm, yb_ref, x2_ref, ri_ref, g_ref, out_ref, yl_s, nd_s, sem):
    i = pl.program_id(0)
    nt = pl.num_programs(0)
    slot = i % 2

    def ycopy(s, src, dst):
        return pltpu.make_async_copy(yb_ref.at[pl.ds(src, CCH), :], yl_s.at[s, pl.ds(dst, CCH), :],
                                     sem.at[s])

    def issue(tile, s):
        def ebody(e, nd):
            n = cnt_sm[tile * N_EXPERTS + e]
            lo = pl.multiple_of(lo_sm[tile * N_EXPERTS + e], CCH)
            src = pl.multiple_of(dst_sm[tile * N_EXPERTS + e], SUB)

            def cbody(c, nd2):
                ycopy(s, src + c * CCH, lo + c * CCH).start()
                return nd2 + 1

            return lax.fori_loop(0, (n + CCH - 1) // CCH, cbody, nd)

        nd_s[s] = lax.fori_loop(0, N_EXPERTS, ebody, jnp.int32(0))

    @pl.when(i == 0)
    def _():
        yl_s[...] = jnp.zeros(yl_s.shape, I32)
        issue(0, 0)

    @pl.when(i + 1 < nt)
    def _():
        issue(i + 1, 1 - slot)

    def wbody(j, c):
        ycopy(slot, 0, 0).wait()
        return c

    lax.fori_loop(0, nd_s[slot], wbody, 0)

    ri = ri_ref[...]
    pad0 = ri[:, 4:5]
    pad1 = ri[:, 5:6]
    lane = lax.broadcasted_iota(I32, (TM, BLK), 1)
    ylo = jnp.zeros((TM, D_MODEL // 2), F32)
    yhi = jnp.zeros((TM, D_MODEL // 2), F32)
    for c in range(LOC_ROWS // BLK):
        r = lane + c * BLK
        selm = (jnp.where(pad0 == r, 1.0, 0.0) + jnp.where(pad1 == r, 1.0, 0.0)).astype(BF16)
        lo_h, hi_h = _unpack_halves(yl_s[slot, c * BLK:(c + 1) * BLK, :])
        ylo = ylo + _dot(selm, lo_h)
        yhi = yhi + _dot(selm, hi_h)
    x3 = x2_ref[...] + jnp.concatenate([ylo, yhi], axis=1)
    out_ref[...] = _rms(x3, g_ref[...])


def _combine(cnt_f, lo16_f, dst_f, yb, x2, ri, g_final):
    T = x2.shape[0]
    nt = T // TM
    grid_spec = pltpu.PrefetchScalarGridSpec(
        num_scalar_prefetch=3,
        grid=(nt,),
        in_specs=[
            pl.BlockSpec(memory_space=pl.ANY),
            pl.BlockSpec((TM, D_MODEL), lambda i, *_: (i, 0)),
            pl.BlockSpec((TM, LANES), lambda i, *_: (i, 0)),
            pl.BlockSpec((1, D_MODEL), lambda i, *_: (0, 0)),
        ],
        out_specs=pl.BlockSpec((TM, D_MODEL), lambda i, *_: (i, 0)),
        scratch_shapes=[
            pltpu.VMEM((2, LOC_ROWS, D_MODEL // 2), I32),
            pltpu.SMEM((2,), I32),
            pltpu.SemaphoreType.DMA((2,)),
        ],
    )
    return pl.pallas_call(
        _combine_kernel,
        grid_spec=grid_spec,
        out_shape=jax.ShapeDtypeStruct((T, D_MODEL), F32),
        compiler_params=pltpu.CompilerParams(
            dimension_semantics=("arbitrary",), vmem_limit_bytes=40 * 1024 * 1024),
        name="combine",
    )(cnt_f, lo16_f, dst_f, yb, x2, ri, g_final)


def _rope_tables(S):
    half = DA_HEAD_DIM // 2
    inv = 1.0 / (ROPE_THETA ** (jnp.arange(0, DA_HEAD_DIM, 2, dtype=F32) / DA_HEAD_DIM))
    ang = jnp.arange(S, dtype=F32)[:, None] * inv[None, :]
    cos, sin = jnp.cos(ang), jnp.sin(ang)
    lane = jnp.arange(LANES)
    idx = lane % half
    lower = (lane % DA_HEAD_DIM) < half
    c = cos[:, idx]
    s = sin[:, idx]
    return c, jnp.where(lower[None, :], -s, 0.0), jnp.where(lower[None, :], 0.0, s)


def kernel(x, w_in, conv_w, conv_b, gate_b, lam_qk, subln_g, mhnorm_g, w_out, g_mix, g_ffn, w_grp,
           b_grp, w_erouter, b_erouter, w1, w3, w2, g_final):
    B, S, D = x.shape
    T = B * S
    nt = T // TM
    x2d = x.reshape(T, D)
    l = 0

    w_main = w_in[l, :, :N_MAIN].astype(BF16)
    wg8 = w_in[l, :, N_MAIN:]
    wg = jnp.pad(wg8, ((0, 0), (0, LANES - 8))).astype(BF16)
    wgt = wg8.T.astype(BF16)
    gb8 = gate_b[l].reshape(8)
    gb_col = jnp.pad(gb8, (0, LANES - 8)).reshape(1, LANES)
    gb_row = jnp.broadcast_to(gb8[:, None], (8, LANES))
    rope_c, rope_sa, rope_sb = _rope_tables(S)
    w_r = jnp.concatenate(
        [w_grp[l], w_erouter[l].transpose(1, 0, 2).reshape(D, N_EXPERTS)], axis=1)
    w_r = jnp.pad(w_r, ((0, 0), (0, LANES - w_r.shape[1])))
    wr_hi = w_r.astype(BF16)
    wr_lo = (w_r - wr_hi.astype(F32)).astype(BF16)
    b_r = jnp.pad(jnp.concatenate([b_grp[l], b_erouter[l].reshape(N_EXPERTS)]),
                  (0, LANES - N_GROUPS - N_EXPERTS)).reshape(1, LANES)

    proj, gcol, grow = _inproj(x2d, g_mix[l].reshape(1, D), w_main, wg, wgt)
    oa = _attention(proj, rope_c, rope_sa, rope_sb, lam_qk[l], subln_g[l].reshape(1, LANES), B, S)
    hm = _mlstm(proj, gcol, grow, conv_w[l], conv_b[l].reshape(1, -1), gb_col, gb_row,
                mhnorm_g[l].reshape(1, ML_WIDTH), B, S)
    x2, hn, ri, rw, cnt3 = _router(oa, hm, x2d, w_out[l].astype(BF16), g_ffn[l].reshape(1, D),
                                   wr_hi, wr_lo, b_r)

    cnt = cnt3[:, 0, EXP_LANE0:EXP_LANE0 + N_EXPERTS]
    c8 = (cnt + SUB - 1) // SUB * SUB
    pcount = (jnp.sum(c8, axis=0) + DCH + BLK - 1) // BLK * BLK
    pend = jnp.cumsum(pcount)
    pstart = pend - pcount
    dst = pstart[None, :] + jnp.cumsum(c8, axis=0) - c8
    lo = jnp.cumsum(c8, axis=1) - c8
    c16 = (cnt + CCH - 1) // CCH * CCH
    lo16 = jnp.cumsum(c16, axis=1) - c16
    nb = (2 * T + nt * N_EXPERTS * (SUB - 1) + N_EXPERTS * (DCH + BLK - 1)) // BLK + 1
    blk_e = jnp.minimum(
        jnp.searchsorted(pend, jnp.arange(nb, dtype=I32) * BLK, side='right'), N_EXPERTS - 1
    ).astype(I32)
    nused = (pend[-1] // BLK).astype(I32).reshape(1)
    cnt_f = cnt.reshape(-1).astype(I32)
    dst_f = dst.reshape(-1).astype(I32)

    xb0 = jnp.zeros((nb * BLK, XW), I32)
    xb = _dispatch(cnt_f, lo.reshape(-1).astype(I32), dst_f, hn, rw, xb0)
    yb = _experts(blk_e, nused, xb, w1[l].astype(BF16), w3[l].astype(BF16), w2[l].astype(BF16), nb)
    out = _combine(cnt_f, lo16.reshape(-1).astype(I32), dst_f, yb, x2, ri, g_final.reshape(1, D))
    return out.reshape(B, S, D)
```

```python
import math

import numpy as np
import jax
import jax.numpy as jnp
from jax import lax
from jax.experimental import pallas as pl
from jax.experimental.pallas import tpu as pltpu

F32 = jnp.float32
BF16 = jnp.bfloat16
I32 = jnp.int32

D_MODEL = 1024
DA_HEADS = 4
DA_HEAD_DIM = 64
DA_V_DIM = 128
DA_WIDTH = 512
ML_HEADS = 4
ML_WIDTH = 512
ML_HEAD_DIM = 128
ML_CHUNK = 128
CONV_K = 4
ROPE_THETA = 10000.0
RMS_EPS = 1e-6
N_GROUPS = 4
EXPERTS_PER_GROUP = 8
N_EXPERTS = 32
D_EXPERT = 512
LAMBDA_INIT = 0.8 - 0.6 * math.exp(-0.3 * 0)

LANES = 128
SUB = 8
N_MAIN = 7 * 512
NEG = -1e30

TM_PROJ = 512
TQ = 256
TK = 512
VT_ROWS = DA_V_DIM + 16
TM = 256
BLK = 256
DCH = 32
CCH = 16
DLOC_ROWS = 768
LOC_ROWS = 2 * TM + N_EXPERTS * CCH
XW = D_MODEL // 2 + LANES
EXP_ROW0 = 4
RT_ROWS = 40
R_E0, R_E1, R_POS0, R_POS1, R_PAD0, R_PAD1, R_W0, R_W1 = range(8)


def _nt_dot(a, b):
    return lax.dot_general(a, b, (((1,), (1,)), ((), ())), preferred_element_type=F32)


def _tn_dot(a, b):
    return lax.dot_general(a, b, (((0,), (0,)), ((), ())), preferred_element_type=F32)


def _dot(a, b):
    return jnp.dot(a, b, preferred_element_type=F32)


def _split3(x):
    hi = x.astype(BF16)
    r = x - hi.astype(F32)
    mid = r.astype(BF16)
    lo = (r - mid.astype(F32)).astype(BF16)
    return hi, mid, lo


def _rms(x, g):
    return x * lax.rsqrt(jnp.mean(x * x, axis=-1, keepdims=True) + RMS_EPS) * g


def _inproj_kernel(x_ref, g_ref, w_ref, wg_ref, wgt_ref, proj_ref, gcol_ref, grow_ref):
    h = _rms(x_ref[...], g_ref[...]).astype(BF16)
    ct = 512
    for c in range(N_MAIN // ct):
        proj_ref[:, c * ct:(c + 1) * ct] = _dot(h, w_ref[:, c * ct:(c + 1) * ct]).astype(BF16)
    gcol_ref[...] = _dot(h, wg_ref[...])
    grow_ref[...] = _nt_dot(wgt_ref[...], h)


def _inproj(x2d, g_mix, w_main, wg, wgt):
    T = x2d.shape[0]
    tm = TM_PROJ
    return pl.pallas_call(
        _inproj_kernel,
        grid=(T // tm,),
        in_specs=[
            pl.BlockSpec((tm, D_MODEL), lambda i: (i, 0)),
            pl.BlockSpec((1, D_MODEL), lambda i: (0, 0)),
            pl.BlockSpec((D_MODEL, N_MAIN), lambda i: (0, 0)),
            pl.BlockSpec((D_MODEL, LANES), lambda i: (0, 0)),
            pl.BlockSpec((8, D_MODEL), lambda i: (0, 0)),
        ],
        out_specs=[
            pl.BlockSpec((tm, N_MAIN), lambda i: (i, 0)),
            pl.BlockSpec((tm, LANES), lambda i: (i, 0)),
            pl.BlockSpec((8, tm), lambda i: (0, i)),
        ],
        out_shape=[
            jax.ShapeDtypeStruct((T, N_MAIN), BF16),
            jax.ShapeDtypeStruct((T, LANES), F32),
            jax.ShapeDtypeStruct((8, T), F32),
        ],
        compiler_params=pltpu.CompilerParams(
            dimension_semantics=("arbitrary",), vmem_limit_bytes=56 * 1024 * 1024),
        name="inproj",
    )(x2d, g_mix, w_main, wg, wgt)


def _attn_kernel(q_ref, k_ref, v_ref, c_ref, sa_ref, sb_ref, lam_ref, g_ref, o_ref,
                 q1_s, q2_s, k_s, vt_s, acc_s, st_s):
    S = q_ref.shape[0]
    lane = lax.broadcasted_iota(I32, (TQ, LANES), 1)
    qscale = DA_HEAD_DIM ** -0.5 * math.log2(math.e)

    def rope(x, rows):
        return (x * c_ref[rows, :] + pltpu.roll(x, 96, 1) * sa_ref[rows, :]
                + pltpu.roll(x, 32, 1) * sb_ref[rows, :])

    def prep(r, carry):
        rows = pl.ds(pl.multiple_of(r * TQ, TQ), TQ)
        for h in range(DA_HEADS):
            hs = slice(h * LANES, (h + 1) * LANES)
            qr = rope(q_ref[rows, hs].astype(F32), rows) * qscale
            q1_s[rows, hs] = jnp.where(lane < DA_HEAD_DIM, qr, 0.0).astype(BF16)
            q2_s[rows, hs] = jnp.where(lane >= DA_HEAD_DIM, qr, 0.0).astype(BF16)
            k_s[rows, hs] = rope(k_ref[rows, hs].astype(F32), rows).astype(BF16)
            vt_s[h, :DA_V_DIM, rows] = v_ref[rows, hs].astype(F32).T.astype(BF16)
        return carry

    lax.fori_loop(0, S // TQ, prep, 0)
    vt_s[:, DA_V_DIM:, :] = jnp.ones((DA_HEADS, VT_ROWS - DA_V_DIM, S), BF16)

    lq = lam_ref[...]
    lam = (jnp.exp(jnp.sum(lq[0:1] * lq[1:2], axis=-1, keepdims=True))
           - jnp.exp(jnp.sum(lq[2:3] * lq[3:4], axis=-1, keepdims=True)) + LAMBDA_INIT)

    key = lax.broadcasted_iota(I32, (TK, 2 * TQ), 0)
    qry = lax.broadcasted_iota(I32, (TK, 2 * TQ), 1)
    kq = key - jnp.where(qry >= TQ, qry - TQ, qry)
    qpk = TK // TQ

    def qtile(i, carry):
        qo = pl.multiple_of(i * TQ, TQ)
        nfull = i // qpk
        causal = kq <= (i - nfull * qpk) * TQ
        acc_s[...] = jnp.zeros(acc_s.shape, F32)

        def kvstep(j, ms, masked):
            ko = pl.multiple_of(j * TK, TK)
            out = []

            def scores(h):
                hs = slice(h * LANES, (h + 1) * LANES)
                qq = jnp.concatenate([q1_s[pl.ds(qo, TQ), hs], q2_s[pl.ds(qo, TQ), hs]], axis=0)
                st = _nt_dot(k_s[pl.ds(ko, TK), hs], qq)
                st_s[h] = jnp.where(causal, st, NEG) if masked else st

            scores(0)
            for h in range(DA_HEADS):
                if h + 1 < DA_HEADS:
                    scores(h + 1)
                st = st_s[h]
                m_new = jnp.maximum(ms[h], jnp.max(st, axis=0, keepdims=True))
                alpha = jnp.exp2(ms[h] - m_new)
                p = jnp.exp2(st - m_new).astype(BF16)
                acc_s[h] = alpha * acc_s[h] + _dot(vt_s[h, :, pl.ds(ko, TK)], p)
                out.append(m_new)
            return tuple(out)

        m0 = tuple(jnp.full((1, 2 * TQ), NEG, F32) for _ in range(DA_HEADS))
        m1 = lax.fori_loop(0, nfull, lambda j, ms: kvstep(j, ms, False), m0)
        kvstep(nfull, m1, True)

        for h in range(DA_HEADS):
            hs = slice(h * LANES, (h + 1) * LANES)
            acc = acc_s[h]
            on = acc[:DA_V_DIM] * (1.0 / acc[DA_V_DIM:DA_V_DIM + 1])
            ot = on[:, :TQ] - lam * on[:, TQ:]
            o = _rms(ot.T, g_ref[...]) * (1.0 - LAMBDA_INIT)
            o_ref[pl.ds(qo, TQ), hs] = o.astype(BF16)
        return carry

    lax.fori_loop(0, S // TQ, qtile, 0)


def _attention(proj, rope_c, rope_sa, rope_sb, lam_qk, subln_g, B, S):
    T = B * S
    full = lambda shape: pl.BlockSpec(shape, lambda b: (0, 0))
    cb = lambda j: pl.BlockSpec((S, DA_WIDTH), lambda b, j=j: (b, j))
    return pl.pallas_call(
        _attn_kernel,
        grid=(B,),
        in_specs=[
            cb(0), cb(1), cb(2),
            full((S, LANES)), full((S, LANES)), full((S, LANES)),
            full((4, DA_HEAD_DIM)), full((1, LANES)),
        ],
        out_specs=pl.BlockSpec((S, DA_WIDTH), lambda b: (b, 0)),
        out_shape=jax.ShapeDtypeStruct((T, DA_WIDTH), BF16),
        scratch_shapes=[
            pltpu.VMEM((S, DA_WIDTH), BF16), pltpu.VMEM((S, DA_WIDTH), BF16),
            pltpu.VMEM((S, DA_WIDTH), BF16),
            pltpu.VMEM((DA_HEADS, VT_ROWS, S), BF16),
            pltpu.VMEM((DA_HEADS, VT_ROWS, 2 * TQ), F32),
            pltpu.VMEM((DA_HEADS, TK, 2 * TQ), F32),
        ],
        compiler_params=pltpu.CompilerParams(
            dimension_semantics=("arbitrary",), vmem_limit_bytes=48 * 1024 * 1024),
        name="attn",
    )(proj, proj, proj, rope_c, rope_sa, rope_sb, lam_qk, subln_g)


def _log_sigmoid(x):
    return jnp.minimum(x, 0.0) - jnp.log(1.0 + jnp.exp(-jnp.abs(x)))


def _mlstm_kernel(q_ref, k_ref, v_ref, o_ref, gcol_ref, grow_ref, cw_ref, cb_ref, gbc_ref, gbr_ref,
                  mg_ref, out_ref, qc_s, kc_s, ct_s, m_s):
    S = q_ref.shape[0]
    L = ML_CHUNK
    nc = S // L
    row16 = lax.broadcasted_iota(I32, (16, ML_WIDTH), 0)

    def conv_silu(x_ref, c, ro, w, b):
        cur = x_ref[pl.ds(ro, L), :].astype(F32)
        po = pl.multiple_of(jnp.maximum(ro - 16, 0), 16)
        tail = jnp.where(c > 0, x_ref[pl.ds(po, 16), :].astype(F32), 0.0)
        y = b + cur * w[3:4]
        for s in (1, 2, 3):
            sh = pltpu.roll(cur, s, 0)
            top = jnp.where(row16 < s, pltpu.roll(tail, s, 0), sh[:16])
            sh = jnp.concatenate([top, sh[16:]], axis=0)
            y = y + sh * w[3 - s:4 - s]
        return y * jax.nn.sigmoid(y)

    def conv_body(c, carry):
        ro = pl.multiple_of(c * L, L)
        qc_s[pl.ds(ro, L), :] = conv_silu(
            q_ref, c, ro, cw_ref[:, :ML_WIDTH], cb_ref[:, :ML_WIDTH]).astype(BF16)
        kc = conv_silu(k_ref, c, ro, cw_ref[:, ML_WIDTH:], cb_ref[:, ML_WIDTH:])
        kc_s[pl.ds(ro, L), :] = (kc * (ML_HEAD_DIM ** -0.5)).astype(BF16)
        return carry

    lax.fori_loop(0, nc, conv_body, 0)

    ct_s[...] = jnp.zeros(ct_s.shape, F32)
    m_s[...] = jnp.zeros(m_s.shape, F32)

    ri = lax.broadcasted_iota(I32, (L, L), 0)
    ci = lax.broadcasted_iota(I32, (L, L), 1)
    causal = ci <= ri
    tril = jnp.where(causal, 1.0, 0.0).astype(BF16)
    triu = jnp.where(ri <= ci, 1.0, 0.0).astype(BF16)
    ones_v = jnp.ones((L, ML_HEAD_DIM), BF16)

    def chunk(c, carry):
        ro = pl.multiple_of(c * L, L)
        gc = gcol_ref[pl.ds(ro, L), :] + gbc_ref[...]
        gr = grow_ref[:, pl.ds(ro, L)] + gbr_ref[...]
        lf_c = _log_sigmoid(gc)
        lf_r = _log_sigmoid(gr)
        b_c = sum(_dot(tril, p) for p in _split3(lf_c))
        b_r = sum(_dot(p, triu) for p in _split3(lf_r))
        for h in range(ML_HEADS):
            hs = slice(h * ML_HEAD_DIM, (h + 1) * ML_HEAD_DIM)
            bc = b_c[:, 4 + h:5 + h]
            lic = gc[:, h:h + 1]
            br = b_r[4 + h:5 + h, :]
            lir = gr[h:h + 1, :]
            m = m_s[h:h + 1, 0:1]
            q = qc_s[pl.ds(ro, L), hs]
            k = kc_s[pl.ds(ro, L), hs]
            vaug = jnp.concatenate([v_ref[pl.ds(ro, L), hs], ones_v], axis=1)
            dm = jnp.where(causal, bc - br + lir, NEG)
            inter = bc + m
            m_row = jnp.maximum(inter, jnp.max(dm, axis=-1, keepdims=True))
            sm = jnp.exp(dm - m_row) * _nt_dot(q, k)
            sc_in = jnp.exp(inter - m_row)
            ca = ct_s[h]
            a = _dot(sm.astype(BF16), vaug) + sc_in * _dot(q, ca.astype(BF16))
            num = a[:, :ML_HEAD_DIM]
            den = a[:, ML_HEAD_DIM:]
            hh = num / jnp.maximum(jnp.abs(den), jnp.exp(-m_row))
            bl = br[:, L - 1:L]
            dec = bl - bc + lic
            m_new = jnp.maximum(bl + m, jnp.max(dec, axis=0, keepdims=True))
            ws = jnp.exp(dec - m_new)
            sc = jnp.exp(bl + m - m_new)
            kw = (k.astype(F32) * ws).astype(BF16)
            ct_s[h] = sc * ca + _tn_dot(kw, vaug)
            m_s[h:h + 1, :] = jnp.broadcast_to(m_new, (1, LANES))
            hg = hh * jax.nn.sigmoid(o_ref[pl.ds(ro, L), hs].astype(F32))
            out_ref[pl.ds(ro, L), hs] = _rms(hg, mg_ref[:, hs]).astype(BF16)
        return carry

    lax.fori_loop(0, nc, chunk, 0)


def _mlstm(proj, gcol, grow, conv_w, conv_b, gb_col, gb_row, mh_g, B, S):
    T = B * S
    full = lambda shape: pl.BlockSpec(shape, lambda b: (0, 0))
    cb = lambda j: pl.BlockSpec((S, ML_WIDTH), lambda b, j=j: (b, j))
    return pl.pallas_call(
        _mlstm_kernel,
        grid=(B,),
        in_specs=[
            cb(3), cb(4), cb(5), cb(6),
            pl.BlockSpec((S, LANES), lambda b: (b, 0)),
            pl.BlockSpec((8, S), lambda b: (0, b)),
            full((CONV_K, 2 * ML_WIDTH)), full((1, 2 * ML_WIDTH)),
            full((1, LANES)), full((8, LANES)), full((1, ML_WIDTH)),
        ],
        out_specs=pl.BlockSpec((S, ML_WIDTH), lambda b: (b, 0)),
        out_shape=jax.ShapeDtypeStruct((T, ML_WIDTH), BF16),
        scratch_shapes=[
            pltpu.VMEM((S, ML_WIDTH), BF16), pltpu.VMEM((S, ML_WIDTH), BF16),
            pltpu.VMEM((ML_HEADS, ML_HEAD_DIM, 2 * ML_HEAD_DIM), F32),
            pltpu.VMEM((8, LANES), F32),
        ],
        compiler_params=pltpu.CompilerParams(
            dimension_semantics=("arbitrary",), vmem_limit_bytes=56 * 1024 * 1024),
        name="mlstm",
    )(proj, proj, proj, proj, gcol, grow, conv_w, conv_b, gb_col, gb_row, mh_g)


def _router_kernel(oa_ref, hm_ref, x_ref, wo_ref, g_ref, wr_ref, brt_ref,
                   x2_ref, hn_ref, rrow_ref, rcol_ref, cnt_ref):
    mixo = _dot(oa_ref[...], wo_ref[:DA_WIDTH, :]) + _dot(hm_ref[...], wo_ref[DA_WIDTH:, :])
    x2 = x_ref[...] + mixo
    x2_ref[...] = x2
    hn = _rms(x2, g_ref[...])
    hn_hi = hn.astype(BF16)
    hn_ref[...] = hn_hi
    hn_lo = (hn - hn_hi.astype(F32)).astype(BF16)
    a = _dot(hn_hi, wr_ref[...])
    b = _dot(hn_lo, wr_ref[:, :LANES])
    lt = (a[:, :LANES] + a[:, LANES:] + b).T[:RT_ROWS] + brt_ref[:RT_ROWS, :]
    sub = lax.broadcasted_iota(I32, (RT_ROWS, TM), 0)
    sub_f = sub.astype(F32)

    def cmax(v):
        return jnp.max(v, axis=0, keepdims=True)

    def first_idx(mask):
        return jnp.min(jnp.where(mask, sub_f, 1e6), axis=0, keepdims=True).astype(I32)

    gl = jnp.where(sub < N_GROUPS, lt, NEG)
    gmax = cmax(gl)
    gsel = first_idx(gl == gmax)
    g_w = 1.0 / jnp.sum(jnp.exp(gl - gmax), axis=0, keepdims=True)
    elo = EXP_ROW0 + gsel * EXPERTS_PER_GROUP
    el = jnp.where((sub >= elo) & (sub < elo + EXPERTS_PER_GROUP), lt, NEG)
    v1 = cmax(el)
    i1 = first_idx(el == v1)
    el2 = jnp.where(sub == i1, NEG, el)
    v2 = cmax(el2)
    i2 = first_idx(el2 == v2)
    t = jnp.exp(v2 - v1)
    w0 = g_w / (1.0 + t)
    w1 = g_w * t / (1.0 + t)

    oh0 = jnp.where(sub == i1, 1.0, 0.0)
    oh1 = jnp.where(sub == i2, 1.0, 0.0)
    mh = oh0 + oh1
    r_i = lax.broadcasted_iota(I32, (TM, TM), 0)
    c_i = lax.broadcasted_iota(I32, (TM, TM), 1)
    before = jnp.where(r_i < c_i, 1.0, 0.0).astype(BF16)
    mh_f = jnp.concatenate([mh, jnp.zeros((LANES - RT_ROWS, TM), F32)], axis=0)
    mh_b = mh_f.astype(BF16)
    pre = _dot(mh_b, before)[:RT_ROWS]
    cnt = jnp.sum(mh_f, axis=1, keepdims=True)
    cnt8 = jnp.floor((cnt + (SUB - 1)) * (1.0 / SUB)) * SUB
    cnt16 = jnp.floor((cnt + (CCH - 1)) * (1.0 / CCH)) * CCH
    e_r = lax.broadcasted_iota(I32, (LANES, LANES), 0)
    e_c = lax.broadcasted_iota(I32, (LANES, LANES), 1)
    below = jnp.where(e_c < e_r, 1.0, 0.0).astype(BF16)
    lo8 = _dot(below, jnp.broadcast_to(cnt8, (LANES, LANES)).astype(BF16))[:RT_ROWS, 0:1]
    lo16 = _dot(below, jnp.broadcast_to(cnt16, (LANES, LANES)).astype(BF16))[:RT_ROWS, 0:1]

    def csum(v):
        return jnp.sum(v, axis=0, keepdims=True)

    rows = [None] * 8
    rows[R_E0] = (i1 - EXP_ROW0).astype(F32)
    rows[R_E1] = (i2 - EXP_ROW0).astype(F32)
    rows[R_POS0] = csum(oh0 * (pre + lo8))
    rows[R_POS1] = csum(oh1 * (pre + lo8))
    rows[R_PAD0] = csum(oh0 * (pre + lo16))
    rows[R_PAD1] = csum(oh1 * (pre + lo16))
    rows[R_W0] = w0
    rows[R_W1] = w1
    sub128 = lax.broadcasted_iota(I32, (LANES, TM), 0)
    r128 = jnp.zeros((LANES, TM), F32)
    for j, v in enumerate(rows):
        r128 = jnp.where(sub128 == j, v, r128)
    rrow_ref[...] = r128[:8]
    rcol_ref[...] = r128.T
    cnt_ref[0] = _nt_dot(jnp.ones((8, TM), BF16), mh_b)[0:1].astype(I32)


def _router(oa, hm, x2d, w_out, g_ffn, wr_t, b_rt):
    T = x2d.shape[0]
    nt = T // TM
    full = lambda shape: pl.BlockSpec(shape, lambda i: (0, 0))
    return pl.pallas_call(
        _router_kernel,
        grid=(nt,),
        in_specs=[
            pl.BlockSpec((TM, DA_WIDTH), lambda i: (i, 0)),
            pl.BlockSpec((TM, ML_WIDTH), lambda i: (i, 0)),
            pl.BlockSpec((TM, D_MODEL), lambda i: (i, 0)),
            full((D_MODEL, D_MODEL)), full((1, D_MODEL)),
            full((D_MODEL, 2 * LANES)), full((LANES, TM)),
        ],
        out_specs=[
            pl.BlockSpec((TM, D_MODEL), lambda i: (i, 0)),
            pl.BlockSpec((TM, D_MODEL), lambda i: (i, 0)),
            pl.BlockSpec((8, TM), lambda i: (0, i)),
            pl.BlockSpec((TM, LANES), lambda i: (i, 0)),
            pl.BlockSpec((1, 1, LANES), lambda i: (i, 0, 0)),
        ],
        out_shape=[
            jax.ShapeDtypeStruct((T, D_MODEL), F32),
            jax.ShapeDtypeStruct((T, D_MODEL), BF16),
            jax.ShapeDtypeStruct((8, T), F32),
            jax.ShapeDtypeStruct((T, LANES), F32),
            jax.ShapeDtypeStruct((nt, 1, LANES), I32),
        ],
        compiler_params=pltpu.CompilerParams(
            dimension_semantics=("arbitrary",), vmem_limit_bytes=40 * 1024 * 1024),
        name="router",
    )(oa, hm, x2d, w_out, g_ffn, wr_t, b_rt)


def _pack_halves(x):
    lo = lax.bitcast_convert_type(x[:, :D_MODEL // 2], I32)
    hi = lax.bitcast_convert_type(x[:, D_MODEL // 2:], I32)
    return lax.shift_right_logical(lo, 16) | (hi & jnp.int32(-65536))


def _unpack_halves(w):
    lo = lax.bitcast_convert_type(lax.shift_left(w, 16), F32)
    hi = lax.bitcast_convert_type(w & jnp.int32(-65536), F32)
    return lo.astype(BF16), hi.astype(BF16)


def _dispatch_kernel(cnt_sm, lo_sm, dst_sm, gs_sm, gn_sm, hn_ref, rrow_ref, rcol_ref, xb_ref,
                     xs_s, z_s, nd_s, semx, semz):
    i = pl.program_id(0)
    nt = pl.num_programs(0)
    slot = i % 2
    half = D_MODEL // 2

    def zcopy(rows, dst):
        return pltpu.make_async_copy(z_s.at[pl.ds(0, rows), :], xb_ref.at[pl.ds(dst, rows), :], semz)

    def gap_fill(wait):
        def ebody(e, carry):
            start = pl.multiple_of(gs_sm[e], SUB)
            left = gn_sm[e]
            for rows in (BLK, DCH, SUB):
                n = left // rows

                def body(c, cc, rows=rows, start=start):
                    cp = zcopy(rows, pl.multiple_of(start + c * rows, SUB))
                    cp.wait() if wait else cp.start()
                    return cc

                lax.fori_loop(0, n, body, 0)
                start = start + n * rows
                left = left - n * rows
            return carry

        lax.fori_loop(0, N_EXPERTS + 1, ebody, 0)

    @pl.when(i == 0)
    def _():
        for s in range(2):
            xs_s[s, DLOC_ROWS:, :] = jnp.zeros((DCH, XW), I32)
        z_s[...] = jnp.zeros(z_s.shape, I32)
        gap_fill(False)
        gap_fill(True)

    rr = rrow_ref[...]
    rc = rcol_ref[...]
    r_i = lax.broadcasted_iota(I32, (DLOC_ROWS, TM), 0).astype(F32)
    perm0 = jnp.where(r_i == rr[R_POS0:R_POS0 + 1, :], 1.0, 0.0).astype(BF16)
    perm1 = jnp.where(r_i == rr[R_POS1:R_POS1 + 1, :], 1.0, 0.0).astype(BF16)
    xs = _dot(perm0 + perm1, hn_ref[...])
    xs_s[slot, :DLOC_ROWS, :half] = _pack_halves(xs)
    lane = lax.broadcasted_iota(I32, (TM, LANES), 1)

    def parts(col):
        hi, mid, lo = (p.astype(F32) for p in _split3(jnp.broadcast_to(rc[:, col:col + 1], (TM, LANES))))
        sel = jnp.where(lane == 0, hi, jnp.where(lane == 1, mid, jnp.where(lane == 2, lo, 0.0)))
        return sel.astype(BF16)

    wparts = _dot(perm0, parts(R_W0)) + _dot(perm1, parts(R_W1))
    l_r = lax.broadcasted_iota(I32, (LANES, LANES), 0)
    ones3 = jnp.where(l_r < 3, 1.0, 0.0).astype(BF16)
    wsort = _dot(wparts.astype(BF16), ones3)
    xs_s[slot, :DLOC_ROWS, half:] = lax.bitcast_convert_type(wsort, I32)

    def xcopy(s, src, dst):
        return pltpu.make_async_copy(xs_s.at[s, pl.ds(src, DCH), :], xb_ref.at[pl.ds(dst, DCH), :], semx)

    def wait_all(n):
        def wbody(j, c):
            xcopy(0, 0, 0).wait()
            return c
        lax.fori_loop(0, n, wbody, 0)

    @pl.when(i > 0)
    def _():
        wait_all(nd_s[0])

    def ebody(e, nd):
        n = cnt_sm[i * N_EXPERTS + e]
        lo = pl.multiple_of(lo_sm[i * N_EXPERTS + e], SUB)
        dst = pl.multiple_of(dst_sm[i * N_EXPERTS + e], SUB)

        def cbody(c, nd2):
            xcopy(slot, lo + c * DCH, dst + c * DCH).start()
            return nd2 + 1

        return lax.fori_loop(0, (n + DCH - 1) // DCH, cbody, nd)

    nd = lax.fori_loop(0, N_EXPERTS, ebody, jnp.int32(0))
    nd_s[0] = nd

    @pl.when(i == nt - 1)
    def _():
        wait_all(nd)


def _dispatch(cnt_f, lo_f, dst_f, gap_start, gap_rows, hn, rrow, rcol, nb):
    T = hn.shape[0]
    nt = T // TM
    grid_spec = pltpu.PrefetchScalarGridSpec(
        num_scalar_prefetch=5,
        grid=(nt,),
        in_specs=[
            pl.BlockSpec((TM, D_MODEL), lambda i, *_: (i, 0)),
            pl.BlockSpec((8, TM), lambda i, *_: (0, i)),
            pl.BlockSpec((TM, LANES), lambda i, *_: (i, 0)),
        ],
        out_specs=pl.BlockSpec(memory_space=pl.ANY),
        scratch_shapes=[
            pltpu.VMEM((2, DLOC_ROWS + DCH, XW), I32),
            pltpu.VMEM((BLK, XW), I32),
            pltpu.SMEM((1,), I32),
            pltpu.SemaphoreType.DMA(()), pltpu.SemaphoreType.DMA(()),
        ],
    )
    return pl.pallas_call(
        _dispatch_kernel,
        grid_spec=grid_spec,
        out_shape=jax.ShapeDtypeStruct((nb * BLK, XW), I32),
        compiler_params=pltpu.CompilerParams(
            dimension_semantics=("arbitrary",), vmem_limit_bytes=40 * 1024 * 1024),
        name="dispatch",
    )(cnt_f, lo_f, dst_f, gap_start, gap_rows, hn, rrow, rcol)


def _expert_kernel(be_sm, nu_sm, xb_ref, w1_ref, w3_ref, w2_ref, yb_ref, w1_s, w3_s, w2_s):
    p = pl.program_id(0)
    used = p < nu_sm[0]

    @pl.when(used & ((p == 0) | (be_sm[p] != be_sm[jnp.maximum(p - 1, 0)])))
    def _():
        w1_s[...] = w1_ref[...].astype(BF16)
        w3_s[...] = w3_ref[...].astype(BF16)
        w2_s[...] = w2_ref[...].astype(BF16)

    @pl.when(used)
    def _():
        half = D_MODEL // 2
        xlo, xhi = _unpack_halves(xb_ref[:, :half])
        h1 = _dot(xlo, w1_s[:half, :]) + _dot(xhi, w1_s[half:, :])
        h3 = _dot(xlo, w3_s[:half, :]) + _dot(xhi, w3_s[half:, :])
        wrep = lax.bitcast_convert_type(xb_ref[:, half:], F32)
        wfull = jnp.concatenate([wrep] * (D_EXPERT // LANES), axis=1)
        hdn = (h1 * jax.nn.sigmoid(h1) * h3 * wfull).astype(BF16)
        y = _dot(hdn, w2_s[...]).astype(BF16).astype(F32)
        yb_ref[...] = _pack_halves(y)

    @pl.when(jnp.logical_not(used))
    def _():
        yb_ref[...] = jnp.zeros(yb_ref.shape, I32)


def _experts(blk_e, nused, xb, w1, w3, w2, nb):
    def rows(p, be, nu):
        return (jnp.minimum(p, nu[0] - 1), 0)

    def wsel(p, be, nu):
        return (be[jnp.minimum(p, nu[0] - 1)], 0, 0)

    grid_spec = pltpu.PrefetchScalarGridSpec(
        num_scalar_prefetch=2,
        grid=(nb,),
        in_specs=[
            pl.BlockSpec((BLK, XW), rows),
            pl.BlockSpec((None, D_MODEL, D_EXPERT), wsel),
            pl.BlockSpec((None, D_MODEL, D_EXPERT), wsel),
            pl.BlockSpec((None, D_EXPERT, D_MODEL), wsel),
        ],
        out_specs=pl.BlockSpec((BLK, D_MODEL // 2), lambda p, be, nu: (p, 0)),
        scratch_shapes=[
            pltpu.VMEM((D_MODEL, D_EXPERT), BF16), pltpu.VMEM((D_MODEL, D_EXPERT), BF16),
            pltpu.VMEM((D_EXPERT, D_MODEL), BF16),
        ],
    )
    return pl.pallas_call(
        _expert_kernel,
        grid_spec=grid_spec,
        out_shape=jax.ShapeDtypeStruct((nb * BLK, D_MODEL // 2), I32),
        compiler_params=pltpu.CompilerParams(
            dimension_semantics=("arbitrary",), vmem_limit_bytes=48 * 1024 * 1024),
        name="experts",
    )(blk_e, nused, xb, w1, w3, w2)


def _combine_kernel(cnt_sm, lo_sm, dst_sm, yb_ref, x2_ref, rcol_ref, g_ref, out_ref, yl_s, nd_s, sem):
    i = pl.program_id(0)
    nt = pl.num_programs(0)
    slot = i % 2

    def ycopy(s, src, dst):
        return pltpu.make_async_copy(yb_ref.at[pl.ds(src, CCH), :], yl_s.at[s, pl.ds(dst, CCH), :],
                                     sem.at[s])

    def issue(tile, s):
        def ebody(e, nd):
            n = cnt_sm[tile * N_EXPERTS + e]
            lo = pl.multiple_of(lo_sm[tile * N_EXPERTS + e], CCH)
            src = pl.multiple_of(dst_sm[tile * N_EXPERTS + e], SUB)

            def cbody(c, nd2):
                ycopy(s, src + c * CCH, lo + c * CCH).start()
                return nd2 + 1

            return lax.fori_loop(0, (n + CCH - 1) // CCH, cbody, nd)

        nd_s[s] = lax.fori_loop(0, N_EXPERTS, ebody, jnp.int32(0))

    @pl.when(i == 0)
    def _():
        yl_s[...] = jnp.zeros(yl_s.shape, I32)
        issue(0, 0)

    @pl.when(i + 1 < nt)
    def _():
        issue(i + 1, 1 - slot)

    def wbody(j, c):
        ycopy(slot, 0, 0).wait()
        return c

    lax.fori_loop(0, nd_s[slot], wbody, 0)

    rc = rcol_ref[...]
    pad0 = rc[:, R_PAD0:R_PAD0 + 1]
    pad1 = rc[:, R_PAD1:R_PAD1 + 1]
    lane = lax.broadcasted_iota(I32, (TM, BLK), 1).astype(F32)
    ylo = jnp.zeros((TM, D_MODEL // 2), F32)
    yhi = jnp.zeros((TM, D_MODEL // 2), F32)
    for c in range(LOC_ROWS // BLK):
        r = lane + float(c * BLK)
        selm = (jnp.where(pad0 == r, 1.0, 0.0) + jnp.where(pad1 == r, 1.0, 0.0)).astype(BF16)
        lo_h, hi_h = _unpack_halves(yl_s[slot, c * BLK:(c + 1) * BLK, :])
        ylo = ylo + _dot(selm, lo_h)
        yhi = yhi + _dot(selm, hi_h)
    x3 = x2_ref[...] + jnp.concatenate([ylo, yhi], axis=1)
    out_ref[...] = _rms(x3, g_ref[...])


def _combine(cnt_f, lo16_f, dst_f, yb, x2, rcol, g_final):
    T = x2.shape[0]
    nt = T // TM
    grid_spec = pltpu.PrefetchScalarGridSpec(
        num_scalar_prefetch=3,
        grid=(nt,),
        in_specs=[
            pl.BlockSpec(memory_space=pl.ANY),
            pl.BlockSpec((TM, D_MODEL), lambda i, *_: (i, 0)),
            pl.BlockSpec((TM, LANES), lambda i, *_: (i, 0)),
            pl.BlockSpec((1, D_MODEL), lambda i, *_: (0, 0)),
        ],
        out_specs=pl.BlockSpec((TM, D_MODEL), lambda i, *_: (i, 0)),
        scratch_shapes=[
            pltpu.VMEM((2, LOC_ROWS, D_MODEL // 2), I32),
            pltpu.SMEM((2,), I32),
            pltpu.SemaphoreType.DMA((2,)),
        ],
    )
    return pl.pallas_call(
        _combine_kernel,
        grid_spec=grid_spec,
        out_shape=jax.ShapeDtypeStruct((T, D_MODEL), F32),
        compiler_params=pltpu.CompilerParams(
            dimension_semantics=("arbitrary",), vmem_limit_bytes=40 * 1024 * 1024),
        name="combine",
    )(cnt_f, lo16_f, dst_f, yb, x2, rcol, g_final)


def _rope_tables(S):
    half = DA_HEAD_DIM // 2
    inv = (1.0 / (np.float32(ROPE_THETA) ** (np.arange(0, DA_HEAD_DIM, 2, dtype=np.float32)
                                             / np.float32(DA_HEAD_DIM)))).astype(np.float32)
    ang = (np.arange(S, dtype=np.float32)[:, None] * inv[None, :]).astype(np.float32)
    cos, sin = np.cos(ang), np.sin(ang)
    lane = np.arange(LANES)
    idx = lane % half
    lower = (lane % DA_HEAD_DIM) < half
    c = cos[:, idx]
    s = sin[:, idx]
    z = np.zeros_like(s)
    return (jnp.asarray(c, F32), jnp.asarray(np.where(lower[None, :], -s, z), F32),
            jnp.asarray(np.where(lower[None, :], z, s), F32))


def _tri_dot(a, b):
    return jnp.dot(a.astype(F32), b.astype(F32), precision=lax.Precision.HIGHEST).astype(I32)


def kernel(x, w_in, conv_w, conv_b, gate_b, lam_qk, subln_g, mhnorm_g, w_out, g_mix, g_ffn, w_grp,
           b_grp, w_erouter, b_erouter, w1, w3, w2, g_final):
    B, S, D = x.shape
    T = B * S
    nt = T // TM
    x2d = x.reshape(T, D)
    l = 0

    w_main = w_in[l, :, :N_MAIN].astype(BF16)
    wg8 = w_in[l, :, N_MAIN:]
    wg = jnp.pad(wg8, ((0, 0), (0, LANES - 8))).astype(BF16)
    wgt = wg8.T.astype(BF16)
    gb8 = gate_b[l].reshape(8)
    gb_col = jnp.pad(gb8, (0, LANES - 8)).reshape(1, LANES)
    gb_row = jnp.broadcast_to(gb8[:, None], (8, LANES))
    rope_c, rope_sa, rope_sb = _rope_tables(S)
    w_r = jnp.concatenate(
        [w_grp[l], w_erouter[l].transpose(1, 0, 2).reshape(D, N_EXPERTS)], axis=1)
    w_r = jnp.pad(w_r, ((0, 0), (0, LANES - w_r.shape[1])))
    wr_hi = w_r.astype(BF16)
    wr_t = jnp.concatenate([wr_hi, (w_r - wr_hi.astype(F32)).astype(BF16)], axis=1)
    b_r = jnp.pad(jnp.concatenate([b_grp[l], b_erouter[l].reshape(N_EXPERTS)]),
                  (0, LANES - N_GROUPS - N_EXPERTS))
    b_rt = jnp.broadcast_to(b_r[:, None], (LANES, TM))

    proj, gcol, grow = _inproj(x2d, g_mix[l].reshape(1, D), w_main, wg, wgt)
    oa = _attention(proj, rope_c, rope_sa, rope_sb, lam_qk[l], subln_g[l].reshape(1, LANES), B, S)
    hm = _mlstm(proj, gcol, grow, conv_w[l], conv_b[l].reshape(1, -1), gb_col, gb_row,
                mhnorm_g[l].reshape(1, ML_WIDTH), B, S)
    x2, hn, rrow, rcol, cnt3 = _router(oa, hm, x2d, w_out[l].astype(BF16), g_ffn[l].reshape(1, D),
                                       wr_t, b_rt)

    cnt = cnt3[:, 0, EXP_ROW0:EXP_ROW0 + N_EXPERTS]
    c8 = (cnt + SUB - 1) // SUB * SUB
    c16 = (cnt + CCH - 1) // CCH * CCH
    count8 = jnp.sum(c8, axis=0)
    pcount = (count8 + DCH + BLK - 1) // BLK * BLK
    up_e = np.triu(np.ones((N_EXPERTS, N_EXPERTS), np.float32), 1)
    lo_t = np.tril(np.ones((nt, nt), np.float32), -1)
    pstart = _tri_dot(pcount[None, :], up_e)[0]
    pend = pstart + pcount
    dst = pstart[None, :] + _tri_dot(lo_t, c8)
    lo8 = _tri_dot(c8, up_e)
    lo16 = _tri_dot(c16, up_e)
    nb = (2 * T + nt * N_EXPERTS * (SUB - 1) + N_EXPERTS * (DCH + BLK - 1)) // BLK + 1
    blk_row = jnp.arange(nb, dtype=I32) * BLK
    blk_e = jnp.minimum(jnp.sum((pend[None, :] <= blk_row[:, None]).astype(I32), axis=1),
                        N_EXPERTS - 1).astype(I32)
    nused = (pend[-1] // BLK).astype(I32).reshape(1)
    cnt_f = cnt.reshape(-1).astype(I32)
    dst_f = dst.reshape(-1).astype(I32)

    gap_start = jnp.concatenate([pstart + count8, pend[-1:]]).astype(I32)
    gap_rows = jnp.concatenate([pcount - count8, nb * BLK - pend[-1:]]).astype(I32)
    xb = _dispatch(cnt_f, lo8.reshape(-1), dst_f, gap_start, gap_rows, hn, rrow, rcol, nb)
    yb = _experts(blk_e, nused, xb, w1[l], w3[l], w2[l], nb)
    out = _combine(cnt_f, lo16.reshape(-1), dst_f, yb, x2, rcol, g_final.reshape(1, D))
    return out.reshape(B, S, D)
```

```python
import math

import numpy as np
import jax
import jax.numpy as jnp
from jax import lax
from jax.experimental import pallas as pl
from jax.experimental.pallas import tpu as pltpu

F32 = jnp.float32
BF16 = jnp.bfloat16
I32 = jnp.int32

D_MODEL = 1024
DA_HEADS = 4
DA_HEAD_DIM = 64
DA_V_DIM = 128
DA_WIDTH = 512
ML_HEADS = 4
ML_WIDTH = 512
ML_HEAD_DIM = 128
ML_CHUNK = 128
CONV_K = 4
ROPE_THETA = 10000.0
RMS_EPS = 1e-6
N_GROUPS = 4
EXPERTS_PER_GROUP = 8
N_EXPERTS = 32
D_EXPERT = 512
LAMBDA_INIT = 0.8 - 0.6 * math.exp(-0.3 * 0)

LANES = 128
SUB = 8
N_MAIN = 7 * 512
NEG = -1e30

TM_PROJ = 512
TQ = 256
TK = 512
VT_ROWS = DA_V_DIM + 16
TM = 256
BLK = 512
ESUB = 256
CBLK = 256
DCH = 32
CCH = 16
DLOC_ROWS = 768
LOC_ROWS = 2 * TM + N_EXPERTS * CCH
XW = D_MODEL // 2 + LANES
EXP_ROW0 = 4
RT_ROWS = 40
R_E0, R_E1, R_POS0, R_POS1, R_PAD0, R_PAD1, R_W0, R_W1 = range(8)


def _nt_dot(a, b):
    return lax.dot_general(a, b, (((1,), (1,)), ((), ())), preferred_element_type=F32)


def _tn_dot(a, b):
    return lax.dot_general(a, b, (((0,), (0,)), ((), ())), preferred_element_type=F32)


def _dot(a, b):
    return jnp.dot(a, b, preferred_element_type=F32)


def _split3(x):
    hi = x.astype(BF16)
    r = x - hi.astype(F32)
    mid = r.astype(BF16)
    lo = (r - mid.astype(F32)).astype(BF16)
    return hi, mid, lo


def _rms(x, g):
    return x * lax.rsqrt(jnp.mean(x * x, axis=-1, keepdims=True) + RMS_EPS) * g


def _inproj_kernel(x_ref, g_ref, w_ref, wg_ref, wgt_ref, proj_ref, gcol_ref, grow_ref):
    h = _rms(x_ref[...], g_ref[...]).astype(BF16)
    ct = 512
    for c in range(N_MAIN // ct):
        proj_ref[:, c * ct:(c + 1) * ct] = _dot(h, w_ref[:, c * ct:(c + 1) * ct]).astype(BF16)
    gcol_ref[...] = _dot(h, wg_ref[...])
    grow_ref[...] = _nt_dot(wgt_ref[...], h)


def _inproj(x2d, g_mix, w_main, wg, wgt):
    T = x2d.shape[0]
    tm = TM_PROJ
    return pl.pallas_call(
        _inproj_kernel,
        grid=(T // tm,),
        in_specs=[
            pl.BlockSpec((tm, D_MODEL), lambda i: (i, 0)),
            pl.BlockSpec((1, D_MODEL), lambda i: (0, 0)),
            pl.BlockSpec((D_MODEL, N_MAIN), lambda i: (0, 0)),
            pl.BlockSpec((D_MODEL, LANES), lambda i: (0, 0)),
            pl.BlockSpec((8, D_MODEL), lambda i: (0, 0)),
        ],
        out_specs=[
            pl.BlockSpec((tm, N_MAIN), lambda i: (i, 0)),
            pl.BlockSpec((tm, LANES), lambda i: (i, 0)),
            pl.BlockSpec((8, tm), lambda i: (0, i)),
        ],
        out_shape=[
            jax.ShapeDtypeStruct((T, N_MAIN), BF16),
            jax.ShapeDtypeStruct((T, LANES), F32),
            jax.ShapeDtypeStruct((8, T), F32),
        ],
        compiler_params=pltpu.CompilerParams(
            dimension_semantics=("arbitrary",), vmem_limit_bytes=56 * 1024 * 1024),
        name="inproj",
    )(x2d, g_mix, w_main, wg, wgt)


def _attn_kernel(q_ref, k_ref, v_ref, c_ref, sa_ref, sb_ref, lam_ref, g_ref, o_ref,
                 q1_s, q2_s, k_s, vt_s, acc_s, st_s):
    S = q_ref.shape[0]
    lane = lax.broadcasted_iota(I32, (TQ, LANES), 1)
    qscale = DA_HEAD_DIM ** -0.5 * math.log2(math.e)

    def rope(x, rows):
        return (x * c_ref[rows, :] + pltpu.roll(x, 96, 1) * sa_ref[rows, :]
                + pltpu.roll(x, 32, 1) * sb_ref[rows, :])

    def prep(r, carry):
        rows = pl.ds(pl.multiple_of(r * TQ, TQ), TQ)
        for h in range(DA_HEADS):
            hs = slice(h * LANES, (h + 1) * LANES)
            qr = rope(q_ref[rows, hs].astype(F32), rows) * qscale
            q1_s[rows, hs] = jnp.where(lane < DA_HEAD_DIM, qr, 0.0).astype(BF16)
            q2_s[rows, hs] = jnp.where(lane >= DA_HEAD_DIM, qr, 0.0).astype(BF16)
            k_s[rows, hs] = rope(k_ref[rows, hs].astype(F32), rows).astype(BF16)
            vt_s[h, :DA_V_DIM, rows] = v_ref[rows, hs].astype(F32).T.astype(BF16)
        return carry

    lax.fori_loop(0, S // TQ, prep, 0)
    vt_s[:, DA_V_DIM:, :] = jnp.ones((DA_HEADS, VT_ROWS - DA_V_DIM, S), BF16)

    lq = lam_ref[...]
    lam = (jnp.exp(jnp.sum(lq[0:1] * lq[1:2], axis=-1, keepdims=True))
           - jnp.exp(jnp.sum(lq[2:3] * lq[3:4], axis=-1, keepdims=True)) + LAMBDA_INIT)

    key = lax.broadcasted_iota(I32, (TK, 2 * TQ), 0)
    qry = lax.broadcasted_iota(I32, (TK, 2 * TQ), 1)
    kq = key - jnp.where(qry >= TQ, qry - TQ, qry)
    qpk = TK // TQ

    def qtile(i, carry):
        qo = pl.multiple_of(i * TQ, TQ)
        nfull = i // qpk
        causal = kq <= (i - nfull * qpk) * TQ
        acc_s[...] = jnp.zeros(acc_s.shape, F32)

        def kvstep(j, ms, masked):
            ko = pl.multiple_of(j * TK, TK)
            out = []

            def scores(h):
                hs = slice(h * LANES, (h + 1) * LANES)
                qq = jnp.concatenate([q1_s[pl.ds(qo, TQ), hs], q2_s[pl.ds(qo, TQ), hs]], axis=0)
                st = _nt_dot(k_s[pl.ds(ko, TK), hs], qq)
                st_s[h] = jnp.where(causal, st, NEG) if masked else st

            scores(0)
            for h in range(DA_HEADS):
                if h + 1 < DA_HEADS:
                    scores(h + 1)
                st = st_s[h]
                m_new = jnp.maximum(ms[h], jnp.max(st, axis=0, keepdims=True))
                alpha = jnp.exp2(ms[h] - m_new)
                p = jnp.exp2(st - m_new).astype(BF16)
                acc_s[h] = alpha * acc_s[h] + _dot(vt_s[h, :, pl.ds(ko, TK)], p)
                out.append(m_new)
            return tuple(out)

        m0 = tuple(jnp.full((1, 2 * TQ), NEG, F32) for _ in range(DA_HEADS))
        m1 = lax.fori_loop(0, nfull, lambda j, ms: kvstep(j, ms, False), m0)
        kvstep(nfull, m1, True)

        for h in range(DA_HEADS):
            hs = slice(h * LANES, (h + 1) * LANES)
            acc = acc_s[h]
            on = acc[:DA_V_DIM] * (1.0 / acc[DA_V_DIM:DA_V_DIM + 1])
            ot = on[:, :TQ] - lam * on[:, TQ:]
            o = _rms(ot.T, g_ref[...]) * (1.0 - LAMBDA_INIT)
            o_ref[pl.ds(qo, TQ), hs] = o.astype(BF16)
        return carry

    lax.fori_loop(0, S // TQ, qtile, 0)


def _attention(proj, rope_c, rope_sa, rope_sb, lam_qk, subln_g, B, S):
    T = B * S
    full = lambda shape: pl.BlockSpec(shape, lambda b: (0, 0))
    cb = lambda j: pl.BlockSpec((S, DA_WIDTH), lambda b, j=j: (b, j))
    return pl.pallas_call(
        _attn_kernel,
        grid=(B,),
        in_specs=[
            cb(0), cb(1), cb(2),
            full((S, LANES)), full((S, LANES)), full((S, LANES)),
            full((4, DA_HEAD_DIM)), full((1, LANES)),
        ],
        out_specs=pl.BlockSpec((S, DA_WIDTH), lambda b: (b, 0)),
        out_shape=jax.ShapeDtypeStruct((T, DA_WIDTH), BF16),
        scratch_shapes=[
            pltpu.VMEM((S, DA_WIDTH), BF16), pltpu.VMEM((S, DA_WIDTH), BF16),
            pltpu.VMEM((S, DA_WIDTH), BF16),
            pltpu.VMEM((DA_HEADS, VT_ROWS, S), BF16),
            pltpu.VMEM((DA_HEADS, VT_ROWS, 2 * TQ), F32),
            pltpu.VMEM((DA_HEADS, TK, 2 * TQ), F32),
        ],
        compiler_params=pltpu.CompilerParams(
            dimension_semantics=("arbitrary",), vmem_limit_bytes=48 * 1024 * 1024),
        name="attn",
    )(proj, proj, proj, rope_c, rope_sa, rope_sb, lam_qk, subln_g)


def _log_sigmoid(x):
    return jnp.minimum(x, 0.0) - jnp.log(1.0 + jnp.exp(-jnp.abs(x)))


def _mlstm_kernel(q_ref, k_ref, v_ref, o_ref, gcol_ref, grow_ref, cw_ref, cb_ref, gbc_ref, gbr_ref,
                  mg_ref, out_ref, qc_s, kc_s, vt_s, ct_s, m_s):
    S = q_ref.shape[0]
    L = ML_CHUNK
    nc = S // L
    row16 = lax.broadcasted_iota(I32, (16, ML_HEAD_DIM), 0)

    def conv_silu(x_ref, c, ro, hs, wcol):
        w = cw_ref[:, wcol]
        cur = x_ref[pl.ds(ro, L), hs].astype(F32)
        po = pl.multiple_of(jnp.maximum(ro - 16, 0), 16)
        tail = jnp.where(c > 0, x_ref[pl.ds(po, 16), hs].astype(F32), 0.0)
        y = cb_ref[:, wcol] + cur * w[3:4]
        for s in (1, 2, 3):
            sh = pltpu.roll(cur, s, 0)
            top = jnp.where(row16 < s, pltpu.roll(tail, s, 0), sh[:16])
            sh = jnp.concatenate([top, sh[16:]], axis=0)
            y = y + sh * w[3 - s:4 - s]
        return y * jax.nn.sigmoid(y)

    def conv_body(c, carry):
        ro = pl.multiple_of(c * L, L)
        for h in range(ML_HEADS):
            hs = slice(h * ML_HEAD_DIM, (h + 1) * ML_HEAD_DIM)
            ks_ = slice(ML_WIDTH + h * ML_HEAD_DIM, ML_WIDTH + (h + 1) * ML_HEAD_DIM)
            qc_s[pl.ds(ro, L), hs] = conv_silu(q_ref, c, ro, hs, hs).astype(BF16)
            kc = conv_silu(k_ref, c, ro, hs, ks_)
            kc_s[pl.ds(ro, L), hs] = (kc * (ML_HEAD_DIM ** -0.5)).astype(BF16)
            vt_s[h, :ML_HEAD_DIM, pl.ds(ro, L)] = v_ref[pl.ds(ro, L), hs].astype(F32).T.astype(BF16)
        return carry

    lax.fori_loop(0, nc, conv_body, 0)
    vt_s[:, ML_HEAD_DIM:, :] = jnp.ones((ML_HEADS, ML_HEAD_DIM, S), BF16)

    ct_s[...] = jnp.zeros(ct_s.shape, F32)
    m_s[...] = jnp.zeros(m_s.shape, F32)

    ri = lax.broadcasted_iota(I32, (L, L), 0)
    ci = lax.broadcasted_iota(I32, (L, L), 1)
    causal_t = ri <= ci
    tril = jnp.where(ci <= ri, 1.0, 0.0).astype(BF16)
    triu = jnp.where(causal_t, 1.0, 0.0).astype(BF16)

    def chunk(c, carry):
        ro = pl.multiple_of(c * L, L)
        gc = gcol_ref[pl.ds(ro, L), :] + gbc_ref[...]
        gr = grow_ref[:, pl.ds(ro, L)] + gbr_ref[...]
        lf_c = _log_sigmoid(gc)
        lf_r = _log_sigmoid(gr)
        b_c = sum(_dot(tril, p) for p in _split3(lf_c))
        b_r = sum(_dot(p, triu) for p in _split3(lf_r))
        heads = [slice(h * ML_HEAD_DIM, (h + 1) * ML_HEAD_DIM) for h in range(ML_HEADS)]
        ks, vts, kq, cq, ms = [], [], [], [], []
        for h, hs in enumerate(heads):
            q = qc_s[pl.ds(ro, L), hs]
            k = kc_s[pl.ds(ro, L), hs]
            ks.append(k)
            vts.append(vt_s[h, :, pl.ds(ro, L)])
            kq.append(_nt_dot(k, q))
            cq.append(_nt_dot(ct_s[h].astype(BF16), q))
            ms.append(m_s[h:h + 1, 0:1])
        for h in range(ML_HEADS):
            br = b_r[4 + h:5 + h, :]
            bl = br[:, L - 1:L]
            dec = bl - br + gr[h:h + 1, :]
            m_new = jnp.maximum(bl + ms[h], jnp.max(dec, axis=1, keepdims=True))
            ws = jnp.exp(dec - m_new)
            sc = jnp.exp(bl + ms[h] - m_new)
            vw = (vts[h].astype(F32) * ws).astype(BF16)
            ct_s[h] = sc * ct_s[h] + _dot(vw, ks[h])
            m_s[h:h + 1, :] = jnp.broadcast_to(m_new, (1, LANES))
        for h, hs in enumerate(heads):
            br = b_r[4 + h:5 + h, :]
            a_col = b_c[:, 4 + h:5 + h] - gc[:, h:h + 1]
            dm = jnp.where(causal_t, br - a_col, NEG)
            inter = br + ms[h]
            m_row = jnp.maximum(inter, jnp.max(dm, axis=0, keepdims=True))
            sm = jnp.exp(dm - m_row) * kq[h]
            sc_in = jnp.exp(inter - m_row)
            a = _dot(vts[h], sm.astype(BF16)) + sc_in * cq[h]
            den = a[ML_HEAD_DIM:ML_HEAD_DIM + 1, :]
            hh = (a[:ML_HEAD_DIM] / jnp.maximum(jnp.abs(den), jnp.exp(-m_row))).T
            hg = hh * jax.nn.sigmoid(o_ref[pl.ds(ro, L), hs].astype(F32))
            out_ref[pl.ds(ro, L), hs] = _rms(hg, mg_ref[:, hs]).astype(BF16)
        return carry

    lax.fori_loop(0, nc, chunk, 0)


def _mlstm(proj, gcol, grow, conv_w, conv_b, gb_col, gb_row, mh_g, B, S):
    T = B * S
    full = lambda shape: pl.BlockSpec(shape, lambda b: (0, 0))
    cb = lambda j: pl.BlockSpec((S, ML_WIDTH), lambda b, j=j: (b, j))
    return pl.pallas_call(
        _mlstm_kernel,
        grid=(B,),
        in_specs=[
            cb(3), cb(4), cb(5), cb(6),
            pl.BlockSpec((S, LANES), lambda b: (b, 0)),
            pl.BlockSpec((8, S), lambda b: (0, b)),
            full((CONV_K, 2 * ML_WIDTH)), full((1, 2 * ML_WIDTH)),
            full((1, LANES)), full((8, LANES)), full((1, ML_WIDTH)),
        ],
        out_specs=pl.BlockSpec((S, ML_WIDTH), lambda b: (b, 0)),
        out_shape=jax.ShapeDtypeStruct((T, ML_WIDTH), BF16),
        scratch_shapes=[
            pltpu.VMEM((S, ML_WIDTH), BF16), pltpu.VMEM((S, ML_WIDTH), BF16),
            pltpu.VMEM((ML_HEADS, 2 * ML_HEAD_DIM, S), BF16),
            pltpu.VMEM((ML_HEADS, 2 * ML_HEAD_DIM, ML_HEAD_DIM), F32),
            pltpu.VMEM((8, LANES), F32),
        ],
        compiler_params=pltpu.CompilerParams(
            dimension_semantics=("arbitrary",), vmem_limit_bytes=56 * 1024 * 1024),
        name="mlstm",
    )(proj, proj, proj, proj, gcol, grow, conv_w, conv_b, gb_col, gb_row, mh_g)


def _router_kernel(oa_ref, hm_ref, x_ref, wo_ref, g_ref, wr_ref, brt_ref,
                   x2_ref, hn_ref, rrow_ref, rcol_ref, cnt_ref):
    mixo = _dot(oa_ref[...], wo_ref[:DA_WIDTH, :]) + _dot(hm_ref[...], wo_ref[DA_WIDTH:, :])
    x2 = x_ref[...] + mixo
    x2_ref[...] = x2
    hn = _rms(x2, g_ref[...])
    hn_hi = hn.astype(BF16)
    hn_ref[...] = hn_hi
    hn_lo = (hn - hn_hi.astype(F32)).astype(BF16)
    a = _dot(hn_hi, wr_ref[...])
    b = _dot(hn_lo, wr_ref[:, :LANES])
    lt = (a[:, :LANES] + a[:, LANES:] + b).T[:RT_ROWS] + brt_ref[:RT_ROWS, :]
    sub = lax.broadcasted_iota(I32, (RT_ROWS, TM), 0)
    sub_f = sub.astype(F32)

    def cmax(v):
        return jnp.max(v, axis=0, keepdims=True)

    def first_idx(mask):
        return jnp.min(jnp.where(mask, sub_f, 1e6), axis=0, keepdims=True).astype(I32)

    gl = jnp.where(sub < N_GROUPS, lt, NEG)
    gmax = cmax(gl)
    gsel = first_idx(gl == gmax)
    g_w = 1.0 / jnp.sum(jnp.exp(gl - gmax), axis=0, keepdims=True)
    elo = EXP_ROW0 + gsel * EXPERTS_PER_GROUP
    el = jnp.where((sub >= elo) & (sub < elo + EXPERTS_PER_GROUP), lt, NEG)
    v1 = cmax(el)
    i1 = first_idx(el == v1)
    el2 = jnp.where(sub == i1, NEG, el)
    v2 = cmax(el2)
    i2 = first_idx(el2 == v2)
    t = jnp.exp(v2 - v1)
    w0 = g_w / (1.0 + t)
    w1 = g_w * t / (1.0 + t)

    oh0 = jnp.where(sub == i1, 1.0, 0.0)
    oh1 = jnp.where(sub == i2, 1.0, 0.0)
    mh = oh0 + oh1
    r_i = lax.broadcasted_iota(I32, (TM, TM), 0)
    c_i = lax.broadcasted_iota(I32, (TM, TM), 1)
    before = jnp.where(r_i < c_i, 1.0, 0.0).astype(BF16)
    mh_f = jnp.concatenate([mh, jnp.zeros((LANES - RT_ROWS, TM), F32)], axis=0)
    mh_b = mh_f.astype(BF16)
    pre = _dot(mh_b, before)[:RT_ROWS]
    cnt = jnp.sum(mh_f, axis=1, keepdims=True)
    cnt8 = jnp.floor((cnt + (SUB - 1)) * (1.0 / SUB)) * SUB
    cnt16 = jnp.floor((cnt + (CCH - 1)) * (1.0 / CCH)) * CCH
    e_r = lax.broadcasted_iota(I32, (LANES, LANES), 0)
    e_c = lax.broadcasted_iota(I32, (LANES, LANES), 1)
    below = jnp.where(e_c < e_r, 1.0, 0.0).astype(BF16)
    lo8 = _dot(below, jnp.broadcast_to(cnt8, (LANES, LANES)).astype(BF16))[:RT_ROWS, 0:1]
    lo16 = _dot(below, jnp.broadcast_to(cnt16, (LANES, LANES)).astype(BF16))[:RT_ROWS, 0:1]

    def csum(v):
        return jnp.sum(v, axis=0, keepdims=True)

    rows = [None] * 8
    rows[R_E0] = (i1 - EXP_ROW0).astype(F32)
    rows[R_E1] = (i2 - EXP_ROW0).astype(F32)
    rows[R_POS0] = csum(oh0 * (pre + lo8))
    rows[R_POS1] = csum(oh1 * (pre + lo8))
    rows[R_PAD0] = csum(oh0 * (pre + lo16))
    rows[R_PAD1] = csum(oh1 * (pre + lo16))
    rows[R_W0] = w0
    rows[R_W1] = w1
    sub128 = lax.broadcasted_iota(I32, (LANES, TM), 0)
    r128 = jnp.zeros((LANES, TM), F32)
    for j, v in enumerate(rows):
        r128 = jnp.where(sub128 == j, v, r128)
    rrow_ref[...] = r128[:8]
    rcol_ref[...] = r128.T
    cnt_ref[0] = _nt_dot(jnp.ones((8, TM), BF16), mh_b)[0:1].astype(I32)


def _router(oa, hm, x2d, w_out, g_ffn, wr_t, b_rt):
    T = x2d.shape[0]
    nt = T // TM
    full = lambda shape: pl.BlockSpec(shape, lambda i: (0, 0))
    return pl.pallas_call(
        _router_kernel,
        grid=(nt,),
        in_specs=[
            pl.BlockSpec((TM, DA_WIDTH), lambda i: (i, 0)),
            pl.BlockSpec((TM, ML_WIDTH), lambda i: (i, 0)),
            pl.BlockSpec((TM, D_MODEL), lambda i: (i, 0)),
            full((D_MODEL, D_MODEL)), full((1, D_MODEL)),
            full((D_MODEL, 2 * LANES)), full((LANES, TM)),
        ],
        out_specs=[
            pl.BlockSpec((TM, D_MODEL), lambda i: (i, 0)),
            pl.BlockSpec((TM, D_MODEL), lambda i: (i, 0)),
            pl.BlockSpec((8, TM), lambda i: (0, i)),
            pl.BlockSpec((TM, LANES), lambda i: (i, 0)),
            pl.BlockSpec((1, 1, LANES), lambda i: (i, 0, 0)),
        ],
        out_shape=[
            jax.ShapeDtypeStruct((T, D_MODEL), F32),
            jax.ShapeDtypeStruct((T, D_MODEL), BF16),
            jax.ShapeDtypeStruct((8, T), F32),
            jax.ShapeDtypeStruct((T, LANES), F32),
            jax.ShapeDtypeStruct((nt, 1, LANES), I32),
        ],
        compiler_params=pltpu.CompilerParams(
            dimension_semantics=("arbitrary",), vmem_limit_bytes=40 * 1024 * 1024),
        name="router",
    )(oa, hm, x2d, w_out, g_ffn, wr_t, b_rt)


def _pack_halves(x):
    lo = lax.bitcast_convert_type(x[:, :D_MODEL // 2], I32)
    hi = lax.bitcast_convert_type(x[:, D_MODEL // 2:], I32)
    return lax.shift_right_logical(lo, 16) | (hi & jnp.int32(-65536))


def _unpack_halves(w):
    lo = lax.bitcast_convert_type(lax.shift_left(w, 16), F32)
    hi = lax.bitcast_convert_type(w & jnp.int32(-65536), F32)
    return lo.astype(BF16), hi.astype(BF16)


def _dispatch_kernel(cnt_sm, lo_sm, dst_sm, gs_sm, gn_sm, hn_ref, rrow_ref, rcol_ref, xb_ref,
                     xs_s, z_s, nd_s, semx, semz):
    i = pl.program_id(0)
    nt = pl.num_programs(0)
    slot = i % 2
    half = D_MODEL // 2

    def zcopy(rows, dst):
        return pltpu.make_async_copy(z_s.at[pl.ds(0, rows), :], xb_ref.at[pl.ds(dst, rows), :], semz)

    def gap_fill(wait):
        def ebody(e, carry):
            start = pl.multiple_of(gs_sm[e], SUB)
            left = gn_sm[e]
            for rows in (BLK, DCH, SUB):
                n = left // rows

                def body(c, cc, rows=rows, start=start):
                    cp = zcopy(rows, pl.multiple_of(start + c * rows, SUB))
                    cp.wait() if wait else cp.start()
                    return cc

                lax.fori_loop(0, n, body, 0)
                start = start + n * rows
                left = left - n * rows
            return carry

        lax.fori_loop(0, N_EXPERTS + 1, ebody, 0)

    @pl.when(i == 0)
    def _():
        for s in range(2):
            xs_s[s, DLOC_ROWS:, :] = jnp.zeros((DCH, XW), I32)
        z_s[...] = jnp.zeros(z_s.shape, I32)
        gap_fill(False)
        gap_fill(True)

    rr = rrow_ref[...]
    rc = rcol_ref[...]
    r_i = lax.broadcasted_iota(I32, (DLOC_ROWS, TM), 0).astype(F32)
    perm0 = jnp.where(r_i == rr[R_POS0:R_POS0 + 1, :], 1.0, 0.0).astype(BF16)
    perm1 = jnp.where(r_i == rr[R_POS1:R_POS1 + 1, :], 1.0, 0.0).astype(BF16)
    xs = _dot(perm0 + perm1, hn_ref[...])
    xs_s[slot, :DLOC_ROWS, :half] = _pack_halves(xs)
    lane = lax.broadcasted_iota(I32, (TM, LANES), 1)

    def parts(col):
        hi, mid, lo = (p.astype(F32) for p in _split3(jnp.broadcast_to(rc[:, col:col + 1], (TM, LANES))))
        sel = jnp.where(lane == 0, hi, jnp.where(lane == 1, mid, jnp.where(lane == 2, lo, 0.0)))
        return sel.astype(BF16)

    wparts = _dot(perm0, parts(R_W0)) + _dot(perm1, parts(R_W1))
    l_r = lax.broadcasted_iota(I32, (LANES, LANES), 0)
    ones3 = jnp.where(l_r < 3, 1.0, 0.0).astype(BF16)
    wsort = _dot(wparts.astype(BF16), ones3)
    xs_s[slot, :DLOC_ROWS, half:] = lax.bitcast_convert_type(wsort, I32)

    def xcopy(s, src, dst):
        return pltpu.make_async_copy(xs_s.at[s, pl.ds(src, DCH), :], xb_ref.at[pl.ds(dst, DCH), :], semx)

    def wait_all(n):
        def wbody(j, c):
            xcopy(0, 0, 0).wait()
            return c
        lax.fori_loop(0, n, wbody, 0)

    @pl.when(i > 0)
    def _():
        wait_all(nd_s[0])

    def ebody(e, nd):
        n = cnt_sm[i * N_EXPERTS + e]
        lo = pl.multiple_of(lo_sm[i * N_EXPERTS + e], SUB)
        dst = pl.multiple_of(dst_sm[i * N_EXPERTS + e], SUB)

        def cbody(c, nd2):
            xcopy(slot, lo + c * DCH, dst + c * DCH).start()
            return nd2 + 1

        return lax.fori_loop(0, (n + DCH - 1) // DCH, cbody, nd)

    nd = lax.fori_loop(0, N_EXPERTS, ebody, jnp.int32(0))
    nd_s[0] = nd

    @pl.when(i == nt - 1)
    def _():
        wait_all(nd)


def _dispatch(cnt_f, lo_f, dst_f, gap_start, gap_rows, hn, rrow, rcol, nb):
    T = hn.shape[0]
    nt = T // TM
    grid_spec = pltpu.PrefetchScalarGridSpec(
        num_scalar_prefetch=5,
        grid=(nt,),
        in_specs=[
            pl.BlockSpec((TM, D_MODEL), lambda i, *_: (i, 0)),
            pl.BlockSpec((8, TM), lambda i, *_: (0, i)),
            pl.BlockSpec((TM, LANES), lambda i, *_: (i, 0)),
        ],
        out_specs=pl.BlockSpec(memory_space=pl.ANY),
        scratch_shapes=[
            pltpu.VMEM((2, DLOC_ROWS + DCH, XW), I32),
            pltpu.VMEM((BLK, XW), I32),
            pltpu.SMEM((1,), I32),
            pltpu.SemaphoreType.DMA(()), pltpu.SemaphoreType.DMA(()),
        ],
    )
    return pl.pallas_call(
        _dispatch_kernel,
        grid_spec=grid_spec,
        out_shape=jax.ShapeDtypeStruct((nb * BLK, XW), I32),
        compiler_params=pltpu.CompilerParams(
            dimension_semantics=("arbitrary",), vmem_limit_bytes=40 * 1024 * 1024),
        name="dispatch",
    )(cnt_f, lo_f, dst_f, gap_start, gap_rows, hn, rrow, rcol)


def _expert_kernel(be_sm, nu_sm, xb_ref, w1_ref, w3_ref, w2_ref, yb_ref, w1_s, w3_s, w2_s):
    p = pl.program_id(0)
    used = p < nu_sm[0]

    @pl.when(used & ((p == 0) | (be_sm[p] != be_sm[jnp.maximum(p - 1, 0)])))
    def _():
        w1_s[...] = w1_ref[...].astype(BF16)
        w3_s[...] = w3_ref[...].astype(BF16)
        w2_s[...] = w2_ref[...].astype(BF16)

    @pl.when(used)
    def _():
        half = D_MODEL // 2
        for r in range(BLK // ESUB):
            rs = slice(r * ESUB, (r + 1) * ESUB)
            xlo, xhi = _unpack_halves(xb_ref[rs, :half])
            h1 = _dot(xlo, w1_s[:half, :]) + _dot(xhi, w1_s[half:, :])
            h3 = _dot(xlo, w3_s[:half, :]) + _dot(xhi, w3_s[half:, :])
            wrep = lax.bitcast_convert_type(xb_ref[rs, half:], F32)
            wfull = jnp.concatenate([wrep] * (D_EXPERT // LANES), axis=1)
            hdn = (h1 * jax.nn.sigmoid(h1) * h3 * wfull).astype(BF16)
            y = _dot(hdn, w2_s[...]).astype(BF16).astype(F32)
            yb_ref[rs, :] = _pack_halves(y)

    @pl.when(jnp.logical_not(used))
    def _():
        yb_ref[...] = jnp.zeros(yb_ref.shape, I32)


def _experts(blk_e, nused, xb, w1, w3, w2, nb):
    def rows(p, be, nu):
        return (jnp.minimum(p, nu[0] - 1), 0)

    def wsel(p, be, nu):
        return (be[jnp.minimum(p, nu[0] - 1)], 0, 0)

    grid_spec = pltpu.PrefetchScalarGridSpec(
        num_scalar_prefetch=2,
        grid=(nb,),
        in_specs=[
            pl.BlockSpec((BLK, XW), rows),
            pl.BlockSpec((None, D_MODEL, D_EXPERT), wsel),
            pl.BlockSpec((None, D_MODEL, D_EXPERT), wsel),
            pl.BlockSpec((None, D_EXPERT, D_MODEL), wsel),
        ],
        out_specs=pl.BlockSpec((BLK, D_MODEL // 2), lambda p, be, nu: (p, 0)),
        scratch_shapes=[
            pltpu.VMEM((D_MODEL, D_EXPERT), BF16), pltpu.VMEM((D_MODEL, D_EXPERT), BF16),
            pltpu.VMEM((D_EXPERT, D_MODEL), BF16),
        ],
    )
    return pl.pallas_call(
        _expert_kernel,
        grid_spec=grid_spec,
        out_shape=jax.ShapeDtypeStruct((nb * BLK, D_MODEL // 2), I32),
        compiler_params=pltpu.CompilerParams(
            dimension_semantics=("arbitrary",), vmem_limit_bytes=48 * 1024 * 1024),
        name="experts",
    )(blk_e, nused, xb, w1, w3, w2)


def _combine_kernel(cnt_sm, lo_sm, dst_sm, yb_ref, x2_ref, rcol_ref, g_ref, out_ref, yl_s, nd_s, sem):
    i = pl.program_id(0)
    nt = pl.num_programs(0)
    slot = i % 2

    def ycopy(s, src, dst):
        return pltpu.make_async_copy(yb_ref.at[pl.ds(src, CCH), :], yl_s.at[s, pl.ds(dst, CCH), :],
                                     sem.at[s])

    def issue(tile, s):
        def ebody(e, nd):
            n = cnt_sm[tile * N_EXPERTS + e]
            lo = pl.multiple_of(lo_sm[tile * N_EXPERTS + e], CCH)
            src = pl.multiple_of(dst_sm[tile * N_EXPERTS + e], SUB)

            def cbody(c, nd2):
                ycopy(s, src + c * CCH, lo + c * CCH).start()
                return nd2 + 1

            return lax.fori_loop(0, (n + CCH - 1) // CCH, cbody, nd)

        nd_s[s] = lax.fori_loop(0, N_EXPERTS, ebody, jnp.int32(0))

    @pl.when(i == 0)
    def _():
        yl_s[...] = jnp.zeros(yl_s.shape, I32)
        issue(0, 0)

    @pl.when(i + 1 < nt)
    def _():
        issue(i + 1, 1 - slot)

    def wbody(j, c):
        ycopy(slot, 0, 0).wait()
        return c

    lax.fori_loop(0, nd_s[slot], wbody, 0)

    rc = rcol_ref[...]
    pad0 = rc[:, R_PAD0:R_PAD0 + 1]
    pad1 = rc[:, R_PAD1:R_PAD1 + 1]
    lane = lax.broadcasted_iota(I32, (TM, CBLK), 1).astype(F32)
    ylo = jnp.zeros((TM, D_MODEL // 2), F32)
    yhi = jnp.zeros((TM, D_MODEL // 2), F32)
    for c in range(LOC_ROWS // CBLK):
        r = lane + float(c * CBLK)
        selm = (jnp.where(pad0 == r, 1.0, 0.0) + jnp.where(pad1 == r, 1.0, 0.0)).astype(BF16)
        lo_h, hi_h = _unpack_halves(yl_s[slot, c * CBLK:(c + 1) * CBLK, :])
        ylo = ylo + _dot(selm, lo_h)
        yhi = yhi + _dot(selm, hi_h)
    x3 = x2_ref[...] + jnp.concatenate([ylo, yhi], axis=1)
    out_ref[...] = _rms(x3, g_ref[...])


def _combine(cnt_f, lo16_f, dst_f, yb, x2, rcol, g_final):
    T = x2.shape[0]
    nt = T // TM
    grid_spec = pltpu.PrefetchScalarGridSpec(
        num_scalar_prefetch=3,
        grid=(nt,),
        in_specs=[
            pl.BlockSpec(memory_space=pl.ANY),
            pl.BlockSpec((TM, D_MODEL), lambda i, *_: (i, 0)),
            pl.BlockSpec((TM, LANES), lambda i, *_: (i, 0)),
            pl.BlockSpec((1, D_MODEL), lambda i, *_: (0, 0)),
        ],
        out_specs=pl.BlockSpec((TM, D_MODEL), lambda i, *_: (i, 0)),
        scratch_shapes=[
            pltpu.VMEM((2, LOC_ROWS, D_MODEL // 2), I32),
            pltpu.SMEM((2,), I32),
            pltpu.SemaphoreType.DMA((2,)),
        ],
    )
    return pl.pallas_call(
        _combine_kernel,
        grid_spec=grid_spec,
        out_shape=jax.ShapeDtypeStruct((T, D_MODEL), F32),
        compiler_params=pltpu.CompilerParams(
            dimension_semantics=("arbitrary",), vmem_limit_bytes=40 * 1024 * 1024),
        name="combine",
    )(cnt_f, lo16_f, dst_f, yb, x2, rcol, g_final)


def _rope_tables(S):
    half = DA_HEAD_DIM // 2
    inv = (1.0 / (np.float32(ROPE_THETA) ** (np.arange(0, DA_HEAD_DIM, 2, dtype=np.float32)
                                             / np.float32(DA_HEAD_DIM)))).astype(np.float32)
    ang = (np.arange(S, dtype=np.float32)[:, None] * inv[None, :]).astype(np.float32)
    cos, sin = np.cos(ang), np.sin(ang)
    lane = np.arange(LANES)
    idx = lane % half
    lower = (lane % DA_HEAD_DIM) < half
    c = cos[:, idx]
    s = sin[:, idx]
    z = np.zeros_like(s)
    return (jnp.asarray(c, F32), jnp.asarray(np.where(lower[None, :], -s, z), F32),
            jnp.asarray(np.where(lower[None, :], z, s), F32))


def _tri_dot(a, b):
    return jnp.dot(a.astype(F32), b.astype(F32), precision=lax.Precision.HIGHEST).astype(I32)


def kernel(x, w_in, conv_w, conv_b, gate_b, lam_qk, subln_g, mhnorm_g, w_out, g_mix, g_ffn, w_grp,
           b_grp, w_erouter, b_erouter, w1, w3, w2, g_final):
    B, S, D = x.shape
    T = B * S
    nt = T // TM
    x2d = x.reshape(T, D)
    l = 0

    w_main = w_in[l, :, :N_MAIN].astype(BF16)
    wg8 = w_in[l, :, N_MAIN:]
    wg = jnp.pad(wg8, ((0, 0), (0, LANES - 8))).astype(BF16)
    wgt = wg8.T.astype(BF16)
    gb8 = gate_b[l].reshape(8)
    gb_col = jnp.pad(gb8, (0, LANES - 8)).reshape(1, LANES)
    gb_row = jnp.broadcast_to(gb8[:, None], (8, LANES))
    rope_c, rope_sa, rope_sb = _rope_tables(S)
    w_r = jnp.concatenate(
        [w_grp[l], w_erouter[l].transpose(1, 0, 2).reshape(D, N_EXPERTS)], axis=1)
    w_r = jnp.pad(w_r, ((0, 0), (0, LANES - w_r.shape[1])))
    wr_hi = w_r.astype(BF16)
    wr_t = jnp.concatenate([wr_hi, (w_r - wr_hi.astype(F32)).astype(BF16)], axis=1)
    b_r = jnp.pad(jnp.concatenate([b_grp[l], b_erouter[l].reshape(N_EXPERTS)]),
                  (0, LANES - N_GROUPS - N_EXPERTS))
    b_rt = jnp.broadcast_to(b_r[:, None], (LANES, TM))

    proj, gcol, grow = _inproj(x2d, g_mix[l].reshape(1, D), w_main, wg, wgt)
    oa = _attention(proj, rope_c, rope_sa, rope_sb, lam_qk[l], subln_g[l].reshape(1, LANES), B, S)
    hm = _mlstm(proj, gcol, grow, conv_w[l], conv_b[l].reshape(1, -1), gb_col, gb_row,
                mhnorm_g[l].reshape(1, ML_WIDTH), B, S)
    x2, hn, rrow, rcol, cnt3 = _router(oa, hm, x2d, w_out[l].astype(BF16), g_ffn[l].reshape(1, D),
                                       wr_t, b_rt)

    cnt = cnt3[:, 0, EXP_ROW0:EXP_ROW0 + N_EXPERTS]
    c8 = (cnt + SUB - 1) // SUB * SUB
    c16 = (cnt + CCH - 1) // CCH * CCH
    count8 = jnp.sum(c8, axis=0)
    pcount = (count8 + DCH + BLK - 1) // BLK * BLK
    up_e = np.triu(np.ones((N_EXPERTS, N_EXPERTS), np.float32), 1)
    lo_t = np.tril(np.ones((nt, nt), np.float32), -1)
    pstart = _tri_dot(pcount[None, :], up_e)[0]
    pend = pstart + pcount
    dst = pstart[None, :] + _tri_dot(lo_t, c8)
    lo8 = _tri_dot(c8, up_e)
    lo16 = _tri_dot(c16, up_e)
    nb = (2 * T + nt * N_EXPERTS * (SUB - 1) + N_EXPERTS * (DCH + BLK - 1)) // BLK + 1
    blk_row = jnp.arange(nb, dtype=I32) * BLK
    blk_e = jnp.minimum(jnp.sum((pend[None, :] <= blk_row[:, None]).astype(I32), axis=1),
                        N_EXPERTS - 1).astype(I32)
    nused = (pend[-1] // BLK).astype(I32).reshape(1)
    cnt_f = cnt.reshape(-1).astype(I32)
    dst_f = dst.reshape(-1).astype(I32)

    gap_start = jnp.concatenate([pstart + count8, pend[-1:]]).astype(I32)
    gap_rows = jnp.concatenate([pcount - count8, nb * BLK - pend[-1:]]).astype(I32)
    xb = _dispatch(cnt_f, lo8.reshape(-1), dst_f, gap_start, gap_rows, hn, rrow, rcol, nb)
    yb = _experts(blk_e, nused, xb, w1[l], w3[l], w2[l], nb)
    out = _combine(cnt_f, lo16.reshape(-1), dst_f, yb, x2, rcol, g_final.reshape(1, D))
    return out.reshape(B, S, D)
```

```python
import math

import numpy as np
import jax
import jax.numpy as jnp
from jax import lax
from jax.experimental import pallas as pl
from jax.experimental.pallas import tpu as pltpu

F32 = jnp.float32
BF16 = jnp.bfloat16
I32 = jnp.int32

D_MODEL = 1024
DA_HEADS = 4
DA_HEAD_DIM = 64
DA_V_DIM = 128
DA_WIDTH = 512
ML_HEADS = 4
ML_WIDTH = 512
ML_HEAD_DIM = 128
ML_CHUNK = 128
CONV_K = 4
ROPE_THETA = 10000.0
RMS_EPS = 1e-6
N_GROUPS = 4
EXPERTS_PER_GROUP = 8
N_EXPERTS = 32
D_EXPERT = 512
LAMBDA_INIT = 0.8 - 0.6 * math.exp(-0.3 * 0)

LANES = 128
SUB = 8
N_MAIN = 7 * 512
NEG = -1e30

TM_PROJ = 1024
TQ = 256
TK = 512
VT_ROWS = DA_V_DIM + 16
TM = 256
BLK = 512
ESUB = 256
CBLK = 256
DCH = 32
CCH = 32
DLOC_ROWS = 768
LOC_COMMON = N_EXPERTS * CCH
LOC_ROWS = 2 * TM + N_EXPERTS * CCH
XW = D_MODEL // 2 + LANES
EXP_ROW0 = 4
RT_ROWS = 40
R_E0, R_E1, R_POS0, R_POS1, R_PAD0, R_PAD1, R_W0, R_W1 = range(8)


def _nt_dot(a, b):
    return lax.dot_general(a, b, (((1,), (1,)), ((), ())), preferred_element_type=F32)


def _tn_dot(a, b):
    return lax.dot_general(a, b, (((0,), (0,)), ((), ())), preferred_element_type=F32)


def _dot(a, b):
    return jnp.dot(a, b, preferred_element_type=F32)


def _split3(x):
    hi = x.astype(BF16)
    r = x - hi.astype(F32)
    mid = r.astype(BF16)
    lo = (r - mid.astype(F32)).astype(BF16)
    return hi, mid, lo


def _rms(x, g):
    return x * lax.rsqrt(jnp.mean(x * x, axis=-1, keepdims=True) + RMS_EPS) * g


def _inproj_kernel(x_ref, g_ref, w_ref, wg_ref, wgt_ref, proj_ref, gcol_ref, grow_ref):
    h = _rms(x_ref[...], g_ref[...]).astype(BF16)
    ct = 512
    for c in range(N_MAIN // ct):
        proj_ref[:, c * ct:(c + 1) * ct] = _dot(h, w_ref[:, c * ct:(c + 1) * ct]).astype(BF16)
    gcol_ref[...] = _dot(h, wg_ref[...])
    grow_ref[...] = _nt_dot(wgt_ref[...], h)


def _inproj(x2d, g_mix, w_main, wg, wgt):
    T = x2d.shape[0]
    tm = TM_PROJ
    return pl.pallas_call(
        _inproj_kernel,
        grid=(T // tm,),
        in_specs=[
            pl.BlockSpec((tm, D_MODEL), lambda i: (i, 0)),
            pl.BlockSpec((1, D_MODEL), lambda i: (0, 0)),
            pl.BlockSpec((D_MODEL, N_MAIN), lambda i: (0, 0)),
            pl.BlockSpec((D_MODEL, LANES), lambda i: (0, 0)),
            pl.BlockSpec((8, D_MODEL), lambda i: (0, 0)),
        ],
        out_specs=[
            pl.BlockSpec((tm, N_MAIN), lambda i: (i, 0)),
            pl.BlockSpec((tm, LANES), lambda i: (i, 0)),
            pl.BlockSpec((8, tm), lambda i: (0, i)),
        ],
        out_shape=[
            jax.ShapeDtypeStruct((T, N_MAIN), BF16),
            jax.ShapeDtypeStruct((T, LANES), F32),
            jax.ShapeDtypeStruct((8, T), F32),
        ],
        compiler_params=pltpu.CompilerParams(
            dimension_semantics=("arbitrary",), vmem_limit_bytes=56 * 1024 * 1024),
        name="inproj",
    )(x2d, g_mix, w_main, wg, wgt)


def _attn_kernel(q_ref, k_ref, v_ref, c_ref, sa_ref, sb_ref, lam_ref, g_ref, o_ref,
                 q1_s, q2_s, k_s, vt_s, acc_s, st_s):
    S = q_ref.shape[0]
    lane = lax.broadcasted_iota(I32, (TQ, LANES), 1)
    qscale = DA_HEAD_DIM ** -0.5 * math.log2(math.e)

    def rope(x, rows):
        return (x * c_ref[rows, :] + pltpu.roll(x, 96, 1) * sa_ref[rows, :]
                + pltpu.roll(x, 32, 1) * sb_ref[rows, :])

    def prep(r, carry):
        rows = pl.ds(pl.multiple_of(r * TQ, TQ), TQ)
        for h in range(DA_HEADS):
            hs = slice(h * LANES, (h + 1) * LANES)
            qr = rope(q_ref[rows, hs].astype(F32), rows) * qscale
            q1_s[rows, hs] = jnp.where(lane < DA_HEAD_DIM, qr, 0.0).astype(BF16)
            q2_s[rows, hs] = jnp.where(lane >= DA_HEAD_DIM, qr, 0.0).astype(BF16)
            k_s[rows, hs] = rope(k_ref[rows, hs].astype(F32), rows).astype(BF16)
            vt_s[h, :DA_V_DIM, rows] = v_ref[rows, hs].astype(F32).T.astype(BF16)
        return carry

    lax.fori_loop(0, S // TQ, prep, 0)
    vt_s[:, DA_V_DIM:, :] = jnp.ones((DA_HEADS, VT_ROWS - DA_V_DIM, S), BF16)

    lq = lam_ref[...]
    lam = (jnp.exp(jnp.sum(lq[0:1] * lq[1:2], axis=-1, keepdims=True))
           - jnp.exp(jnp.sum(lq[2:3] * lq[3:4], axis=-1, keepdims=True)) + LAMBDA_INIT)

    key = lax.broadcasted_iota(I32, (TK, 2 * TQ), 0)
    qry = lax.broadcasted_iota(I32, (TK, 2 * TQ), 1)
    kq = key - jnp.where(qry >= TQ, qry - TQ, qry)
    qpk = TK // TQ

    def qtile(i, carry):
        qo = pl.multiple_of(i * TQ, TQ)
        nfull = i // qpk
        causal = kq <= (i - nfull * qpk) * TQ
        acc_s[...] = jnp.zeros(acc_s.shape, F32)

        def kvstep(j, ms, masked):
            ko = pl.multiple_of(j * TK, TK)
            out = []

            def scores(h):
                hs = slice(h * LANES, (h + 1) * LANES)
                qq = jnp.concatenate([q1_s[pl.ds(qo, TQ), hs], q2_s[pl.ds(qo, TQ), hs]], axis=0)
                st = _nt_dot(k_s[pl.ds(ko, TK), hs], qq)
                st_s[h] = jnp.where(causal, st, NEG) if masked else st

            scores(0)
            for h in range(DA_HEADS):
                if h + 1 < DA_HEADS:
                    scores(h + 1)
                st = st_s[h]
                m_new = jnp.maximum(ms[h], jnp.max(st, axis=0, keepdims=True))
                alpha = jnp.exp2(ms[h] - m_new)
                p = jnp.exp2(st - m_new).astype(BF16)
                acc_s[h] = alpha * acc_s[h] + _dot(vt_s[h, :, pl.ds(ko, TK)], p)
                out.append(m_new)
            return tuple(out)

        m0 = tuple(jnp.full((1, 2 * TQ), NEG, F32) for _ in range(DA_HEADS))
        m1 = lax.fori_loop(0, nfull, lambda j, ms: kvstep(j, ms, False), m0)
        kvstep(nfull, m1, True)

        for h in range(DA_HEADS):
            hs = slice(h * LANES, (h + 1) * LANES)
            acc = acc_s[h]
            on = acc[:DA_V_DIM] * (1.0 / acc[DA_V_DIM:DA_V_DIM + 1])
            ot = on[:, :TQ] - lam * on[:, TQ:]
            o = _rms(ot.T, g_ref[...]) * (1.0 - LAMBDA_INIT)
            o_ref[pl.ds(qo, TQ), hs] = o.astype(BF16)
        return carry

    lax.fori_loop(0, S // TQ, qtile, 0)


def _attention(proj, rope_c, rope_sa, rope_sb, lam_qk, subln_g, B, S):
    T = B * S
    full = lambda shape: pl.BlockSpec(shape, lambda b: (0, 0))
    cb = lambda j: pl.BlockSpec((S, DA_WIDTH), lambda b, j=j: (b, j))
    return pl.pallas_call(
        _attn_kernel,
        grid=(B,),
        in_specs=[
            cb(0), cb(1), cb(2),
            full((S, LANES)), full((S, LANES)), full((S, LANES)),
            full((4, DA_HEAD_DIM)), full((1, LANES)),
        ],
        out_specs=pl.BlockSpec((S, DA_WIDTH), lambda b: (b, 0)),
        out_shape=jax.ShapeDtypeStruct((T, DA_WIDTH), BF16),
        scratch_shapes=[
            pltpu.VMEM((S, DA_WIDTH), BF16), pltpu.VMEM((S, DA_WIDTH), BF16),
            pltpu.VMEM((S, DA_WIDTH), BF16),
            pltpu.VMEM((DA_HEADS, VT_ROWS, S), BF16),
            pltpu.VMEM((DA_HEADS, VT_ROWS, 2 * TQ), F32),
            pltpu.VMEM((DA_HEADS, TK, 2 * TQ), F32),
        ],
        compiler_params=pltpu.CompilerParams(
            dimension_semantics=("arbitrary",), vmem_limit_bytes=48 * 1024 * 1024),
        name="attn",
    )(proj, proj, proj, rope_c, rope_sa, rope_sb, lam_qk, subln_g)


def _log_sigmoid(x):
    return jnp.minimum(x, 0.0) - jnp.log(1.0 + jnp.exp(-jnp.abs(x)))


def _mlstm_kernel(q_ref, k_ref, v_ref, o_ref, gcol_ref, grow_ref, cw_ref, cb_ref, gbc_ref, gbr_ref,
                  mg_ref, out_ref, qc_s, kc_s, vt_s, ct_s, m_s):
    S = q_ref.shape[0]
    L = ML_CHUNK
    nc = S // L
    row16 = lax.broadcasted_iota(I32, (16, ML_HEAD_DIM), 0)

    def conv_silu(x_ref, c, ro, hs, wcol):
        w = cw_ref[:, wcol]
        cur = x_ref[pl.ds(ro, L), hs].astype(F32)
        po = pl.multiple_of(jnp.maximum(ro - 16, 0), 16)
        tail = jnp.where(c > 0, x_ref[pl.ds(po, 16), hs].astype(F32), 0.0)
        y = cb_ref[:, wcol] + cur * w[3:4]
        for s in (1, 2, 3):
            sh = pltpu.roll(cur, s, 0)
            top = jnp.where(row16 < s, pltpu.roll(tail, s, 0), sh[:16])
            sh = jnp.concatenate([top, sh[16:]], axis=0)
            y = y + sh * w[3 - s:4 - s]
        return y * jax.nn.sigmoid(y)

    def conv_body(c, carry):
        ro = pl.multiple_of(c * L, L)
        for h in range(ML_HEADS):
            hs = slice(h * ML_HEAD_DIM, (h + 1) * ML_HEAD_DIM)
            ks_ = slice(ML_WIDTH + h * ML_HEAD_DIM, ML_WIDTH + (h + 1) * ML_HEAD_DIM)
            qc_s[pl.ds(ro, L), hs] = conv_silu(q_ref, c, ro, hs, hs).astype(BF16)
            kc = conv_silu(k_ref, c, ro, hs, ks_)
            kc_s[pl.ds(ro, L), hs] = (kc * (ML_HEAD_DIM ** -0.5)).astype(BF16)
            vt_s[h, :ML_HEAD_DIM, pl.ds(ro, L)] = v_ref[pl.ds(ro, L), hs].astype(F32).T.astype(BF16)
        return carry

    lax.fori_loop(0, nc, conv_body, 0)
    vt_s[:, ML_HEAD_DIM:, :] = jnp.ones((ML_HEADS, ML_HEAD_DIM, S), BF16)

    ct_s[...] = jnp.zeros(ct_s.shape, F32)
    m_s[...] = jnp.zeros(m_s.shape, F32)

    ri = lax.broadcasted_iota(I32, (L, L), 0)
    ci = lax.broadcasted_iota(I32, (L, L), 1)
    causal_t = ri <= ci
    tril = jnp.where(ci <= ri, 1.0, 0.0).astype(BF16)
    triu = jnp.where(causal_t, 1.0, 0.0).astype(BF16)

    def chunk(c, carry):
        ro = pl.multiple_of(c * L, L)
        gc = gcol_ref[pl.ds(ro, L), :] + gbc_ref[...]
        gr = grow_ref[:, pl.ds(ro, L)] + gbr_ref[...]
        lf_c = _log_sigmoid(gc)
        lf_r = _log_sigmoid(gr)
        b_c = sum(_dot(tril, p) for p in _split3(lf_c))
        b_r = sum(_dot(p, triu) for p in _split3(lf_r))
        heads = [slice(h * ML_HEAD_DIM, (h + 1) * ML_HEAD_DIM) for h in range(ML_HEADS)]
        ks, vts, kq, cq, ms = [], [], [], [], []
        for h, hs in enumerate(heads):
            q = qc_s[pl.ds(ro, L), hs]
            k = kc_s[pl.ds(ro, L), hs]
            ks.append(k)
            vts.append(vt_s[h, :, pl.ds(ro, L)])
            kq.append(_nt_dot(k, q))
            cq.append(_nt_dot(ct_s[h].astype(BF16), q))
            ms.append(m_s[h:h + 1, 0:1])
        for h in range(ML_HEADS):
            br = b_r[4 + h:5 + h, :]
            bl = br[:, L - 1:L]
            dec = bl - br + gr[h:h + 1, :]
            m_new = jnp.maximum(bl + ms[h], jnp.max(dec, axis=1, keepdims=True))
            ws = jnp.exp(dec - m_new)
            sc = jnp.exp(bl + ms[h] - m_new)
            vw = (vts[h].astype(F32) * ws).astype(BF16)
            ct_s[h] = sc * ct_s[h] + _dot(vw, ks[h])
            m_s[h:h + 1, :] = jnp.broadcast_to(m_new, (1, LANES))
        for h, hs in enumerate(heads):
            br = b_r[4 + h:5 + h, :]
            a_col = b_c[:, 4 + h:5 + h] - gc[:, h:h + 1]
            dm = jnp.where(causal_t, br - a_col, NEG)
            inter = br + ms[h]
            m_row = jnp.maximum(inter, jnp.max(dm, axis=0, keepdims=True))
            sm = jnp.exp(dm - m_row) * kq[h]
            sc_in = jnp.exp(inter - m_row)
            a = _dot(vts[h], sm.astype(BF16)) + sc_in * cq[h]
            den = a[ML_HEAD_DIM:ML_HEAD_DIM + 1, :]
            hh = (a[:ML_HEAD_DIM] / jnp.maximum(jnp.abs(den), jnp.exp(-m_row))).T
            hg = hh * jax.nn.sigmoid(o_ref[pl.ds(ro, L), hs].astype(F32))
            out_ref[pl.ds(ro, L), hs] = _rms(hg, mg_ref[:, hs]).astype(BF16)
        return carry

    lax.fori_loop(0, nc, chunk, 0)


def _mlstm(proj, gcol, grow, conv_w, conv_b, gb_col, gb_row, mh_g, B, S):
    T = B * S
    full = lambda shape: pl.BlockSpec(shape, lambda b: (0, 0))
    cb = lambda j: pl.BlockSpec((S, ML_WIDTH), lambda b, j=j: (b, j))
    return pl.pallas_call(
        _mlstm_kernel,
        grid=(B,),
        in_specs=[
            cb(3), cb(4), cb(5), cb(6),
            pl.BlockSpec((S, LANES), lambda b: (b, 0)),
            pl.BlockSpec((8, S), lambda b: (0, b)),
            full((CONV_K, 2 * ML_WIDTH)), full((1, 2 * ML_WIDTH)),
            full((1, LANES)), full((8, LANES)), full((1, ML_WIDTH)),
        ],
        out_specs=pl.BlockSpec((S, ML_WIDTH), lambda b: (b, 0)),
        out_shape=jax.ShapeDtypeStruct((T, ML_WIDTH), BF16),
        scratch_shapes=[
            pltpu.VMEM((S, ML_WIDTH), BF16), pltpu.VMEM((S, ML_WIDTH), BF16),
            pltpu.VMEM((ML_HEADS, 2 * ML_HEAD_DIM, S), BF16),
            pltpu.VMEM((ML_HEADS, 2 * ML_HEAD_DIM, ML_HEAD_DIM), F32),
            pltpu.VMEM((8, LANES), F32),
        ],
        compiler_params=pltpu.CompilerParams(
            dimension_semantics=("arbitrary",), vmem_limit_bytes=56 * 1024 * 1024),
        name="mlstm",
    )(proj, proj, proj, proj, gcol, grow, conv_w, conv_b, gb_col, gb_row, mh_g)


def _router_kernel(oa_ref, hm_ref, x_ref, wo_ref, g_ref, wr_ref, brt_ref,
                   x2_ref, hn_ref, rrow_ref, rcol_ref, cnt_ref):
    mixo = _dot(oa_ref[...], wo_ref[:DA_WIDTH, :]) + _dot(hm_ref[...], wo_ref[DA_WIDTH:, :])
    x2 = x_ref[...] + mixo
    x2_ref[...] = x2
    hn = _rms(x2, g_ref[...])
    hn_hi = hn.astype(BF16)
    hn_ref[...] = hn_hi
    hn_lo = (hn - hn_hi.astype(F32)).astype(BF16)
    a = _dot(hn_hi, wr_ref[...])
    b = _dot(hn_lo, wr_ref[:, :LANES])
    lt = (a[:, :LANES] + a[:, LANES:] + b).T[:RT_ROWS] + brt_ref[:RT_ROWS, :]
    sub = lax.broadcasted_iota(I32, (RT_ROWS, TM), 0)
    sub_f = sub.astype(F32)

    def cmax(v):
        return jnp.max(v, axis=0, keepdims=True)

    def first_idx(mask):
        return jnp.min(jnp.where(mask, sub_f, 1e6), axis=0, keepdims=True).astype(I32)

    gl = jnp.where(sub < N_GROUPS, lt, NEG)
    gmax = cmax(gl)
    gsel = first_idx(gl == gmax)
    g_w = 1.0 / jnp.sum(jnp.exp(gl - gmax), axis=0, keepdims=True)
    elo = EXP_ROW0 + gsel * EXPERTS_PER_GROUP
    el = jnp.where((sub >= elo) & (sub < elo + EXPERTS_PER_GROUP), lt, NEG)
    v1 = cmax(el)
    i1 = first_idx(el == v1)
    el2 = jnp.where(sub == i1, NEG, el)
    v2 = cmax(el2)
    i2 = first_idx(el2 == v2)
    t = jnp.exp(v2 - v1)
    w0 = g_w / (1.0 + t)
    w1 = g_w * t / (1.0 + t)

    oh0 = jnp.where(sub == i1, 1.0, 0.0)
    oh1 = jnp.where(sub == i2, 1.0, 0.0)
    mh = oh0 + oh1
    r_i = lax.broadcasted_iota(I32, (TM, TM), 0)
    c_i = lax.broadcasted_iota(I32, (TM, TM), 1)
    before = jnp.where(r_i < c_i, 1.0, 0.0).astype(BF16)
    mh_f = jnp.concatenate([mh, jnp.zeros((LANES - RT_ROWS, TM), F32)], axis=0)
    mh_b = mh_f.astype(BF16)
    pre = _dot(mh_b, before)[:RT_ROWS]
    cnt = jnp.sum(mh_f, axis=1, keepdims=True)
    cnt8 = jnp.floor((cnt + (SUB - 1)) * (1.0 / SUB)) * SUB
    erow = lax.broadcasted_iota(I32, (LANES, 1), 0)
    is_exp = (erow >= EXP_ROW0) & (erow < EXP_ROW0 + N_EXPERTS)
    cnt16 = jnp.where(is_exp, jnp.maximum(jnp.floor((cnt + (CCH - 1)) * (1.0 / CCH)), 1.0) * CCH, 0.0)
    e_r = lax.broadcasted_iota(I32, (LANES, LANES), 0)
    e_c = lax.broadcasted_iota(I32, (LANES, LANES), 1)
    below = jnp.where(e_c < e_r, 1.0, 0.0).astype(BF16)
    lo8 = _dot(below, jnp.broadcast_to(cnt8, (LANES, LANES)).astype(BF16))[:RT_ROWS, 0:1]
    lo16 = _dot(below, jnp.broadcast_to(cnt16, (LANES, LANES)).astype(BF16))[:RT_ROWS, 0:1]

    def csum(v):
        return jnp.sum(v, axis=0, keepdims=True)

    rows = [None] * 8
    rows[R_E0] = (i1 - EXP_ROW0).astype(F32)
    rows[R_E1] = (i2 - EXP_ROW0).astype(F32)
    rows[R_POS0] = csum(oh0 * (pre + lo8))
    rows[R_POS1] = csum(oh1 * (pre + lo8))
    rows[R_PAD0] = csum(oh0 * (pre + lo16))
    rows[R_PAD1] = csum(oh1 * (pre + lo16))
    rows[R_W0] = w0
    rows[R_W1] = w1
    sub128 = lax.broadcasted_iota(I32, (LANES, TM), 0)
    r128 = jnp.zeros((LANES, TM), F32)
    for j, v in enumerate(rows):
        r128 = jnp.where(sub128 == j, v, r128)
    rrow_ref[...] = r128[:8]
    rcol_ref[...] = r128.T
    cnt_ref[0] = _nt_dot(jnp.ones((8, TM), BF16), mh_b)[0:1].astype(I32)


def _router(oa, hm, x2d, w_out, g_ffn, wr_t, b_rt):
    T = x2d.shape[0]
    nt = T // TM
    full = lambda shape: pl.BlockSpec(shape, lambda i: (0, 0))
    return pl.pallas_call(
        _router_kernel,
        grid=(nt,),
        in_specs=[
            pl.BlockSpec((TM, DA_WIDTH), lambda i: (i, 0)),
            pl.BlockSpec((TM, ML_WIDTH), lambda i: (i, 0)),
            pl.BlockSpec((TM, D_MODEL), lambda i: (i, 0)),
            full((D_MODEL, D_MODEL)), full((1, D_MODEL)),
            full((D_MODEL, 2 * LANES)), full((LANES, TM)),
        ],
        out_specs=[
            pl.BlockSpec((TM, D_MODEL), lambda i: (i, 0)),
            pl.BlockSpec((TM, D_MODEL), lambda i: (i, 0)),
            pl.BlockSpec((8, TM), lambda i: (0, i)),
            pl.BlockSpec((TM, LANES), lambda i: (i, 0)),
            pl.BlockSpec((1, 1, LANES), lambda i: (i, 0, 0)),
        ],
        out_shape=[
            jax.ShapeDtypeStruct((T, D_MODEL), F32),
            jax.ShapeDtypeStruct((T, D_MODEL), BF16),
            jax.ShapeDtypeStruct((8, T), F32),
            jax.ShapeDtypeStruct((T, LANES), F32),
            jax.ShapeDtypeStruct((nt, 1, LANES), I32),
        ],
        compiler_params=pltpu.CompilerParams(
            dimension_semantics=("arbitrary",), vmem_limit_bytes=40 * 1024 * 1024),
        name="router",
    )(oa, hm, x2d, w_out, g_ffn, wr_t, b_rt)


def _pack_halves(x):
    lo = lax.bitcast_convert_type(x[:, :D_MODEL // 2], I32)
    hi = lax.bitcast_convert_type(x[:, D_MODEL // 2:], I32)
    return lax.shift_right_logical(lo, 16) | (hi & jnp.int32(-65536))


def _unpack_halves(w):
    lo = lax.bitcast_convert_type(lax.shift_left(w, 16), F32)
    hi = lax.bitcast_convert_type(w & jnp.int32(-65536), F32)
    return lo.astype(BF16), hi.astype(BF16)


def _dispatch_kernel(cnt_sm, lo_sm, dst_sm, extra_sm, gs_sm, gn_sm, hn_ref, rrow_ref, rcol_ref, xb_ref,
                     xs_s, z_s, semx, semx2, semz):
    i = pl.program_id(0)
    nt = pl.num_programs(0)
    slot = i % 2
    half = D_MODEL // 2

    def zcopy(rows, dst):
        return pltpu.make_async_copy(z_s.at[pl.ds(0, rows), :], xb_ref.at[pl.ds(dst, rows), :], semz)

    def gap_fill(wait):
        def ebody(e, carry):
            start = pl.multiple_of(gs_sm[e], SUB)
            left = gn_sm[e]
            for rows in (BLK, DCH, SUB):
                n = left // rows

                def body(c, cc, rows=rows, start=start):
                    cp = zcopy(rows, pl.multiple_of(start + c * rows, SUB))
                    cp.wait() if wait else cp.start()
                    return cc

                lax.fori_loop(0, n, body, 0)
                start = start + n * rows
                left = left - n * rows
            return carry

        lax.fori_loop(0, N_EXPERTS + 1, ebody, 0)

    @pl.when(i == 0)
    def _():
        for s in range(2):
            xs_s[s, DLOC_ROWS:, :] = jnp.zeros((DCH, XW), I32)
        z_s[...] = jnp.zeros(z_s.shape, I32)
        gap_fill(False)
        gap_fill(True)

    rr = rrow_ref[...]
    rc = rcol_ref[...]
    r_i = lax.broadcasted_iota(I32, (DLOC_ROWS, TM), 0).astype(F32)
    perm0 = jnp.where(r_i == rr[R_POS0:R_POS0 + 1, :], 1.0, 0.0).astype(BF16)
    perm1 = jnp.where(r_i == rr[R_POS1:R_POS1 + 1, :], 1.0, 0.0).astype(BF16)
    xs = _dot(perm0 + perm1, hn_ref[...])
    xs_s[slot, :DLOC_ROWS, :half] = _pack_halves(xs)
    lane = lax.broadcasted_iota(I32, (TM, LANES), 1)

    def parts(col):
        hi, mid, lo = (p.astype(F32) for p in _split3(jnp.broadcast_to(rc[:, col:col + 1], (TM, LANES))))
        sel = jnp.where(lane == 0, hi, jnp.where(lane == 1, mid, jnp.where(lane == 2, lo, 0.0)))
        return sel.astype(BF16)

    wparts = _dot(perm0, parts(R_W0)) + _dot(perm1, parts(R_W1))
    l_r = lax.broadcasted_iota(I32, (LANES, LANES), 0)
    ones3 = jnp.where(l_r < 3, 1.0, 0.0).astype(BF16)
    wsort = _dot(wparts.astype(BF16), ones3)
    xs_s[slot, :DLOC_ROWS, half:] = lax.bitcast_convert_type(wsort, I32)

    def xcopy(s, src, dst, sem):
        return pltpu.make_async_copy(xs_s.at[s, pl.ds(src, DCH), :], xb_ref.at[pl.ds(dst, DCH), :], sem)

    def wait_all(n_extra):
        for _ in range(N_EXPERTS):
            xcopy(0, 0, 0, semx).wait()

        def wbody(j, c):
            xcopy(0, 0, 0, semx2).wait()
            return c
        lax.fori_loop(0, n_extra, wbody, 0)

    @pl.when(i > 0)
    def _():
        wait_all(extra_sm[jnp.maximum(i - 1, 0)])

    for e in range(N_EXPERTS):
        lo = pl.multiple_of(lo_sm[i * N_EXPERTS + e], SUB)
        dst = pl.multiple_of(dst_sm[i * N_EXPERTS + e], SUB)
        xcopy(slot, lo, dst, semx).start()

    @pl.when(extra_sm[i] > 0)
    def _():
        def ebody(e, carry):
            n = cnt_sm[i * N_EXPERTS + e]
            lo = pl.multiple_of(lo_sm[i * N_EXPERTS + e], SUB)
            dst = pl.multiple_of(dst_sm[i * N_EXPERTS + e], SUB)

            def cbody(c, cc):
                xcopy(slot, lo + c * DCH, dst + c * DCH, semx2).start()
                return cc

            return lax.fori_loop(1, (n + DCH - 1) // DCH, cbody, carry)

        lax.fori_loop(0, N_EXPERTS, ebody, 0)

    @pl.when(i == nt - 1)
    def _():
        wait_all(extra_sm[i])


def _dispatch(cnt_f, lo_f, dst_f, extra, gap_start, gap_rows, hn, rrow, rcol, nb):
    T = hn.shape[0]
    nt = T // TM
    grid_spec = pltpu.PrefetchScalarGridSpec(
        num_scalar_prefetch=6,
        grid=(nt,),
        in_specs=[
            pl.BlockSpec((TM, D_MODEL), lambda i, *_: (i, 0)),
            pl.BlockSpec((8, TM), lambda i, *_: (0, i)),
            pl.BlockSpec((TM, LANES), lambda i, *_: (i, 0)),
        ],
        out_specs=pl.BlockSpec(memory_space=pl.ANY),
        scratch_shapes=[
            pltpu.VMEM((2, DLOC_ROWS + DCH, XW), I32),
            pltpu.VMEM((BLK, XW), I32),
            pltpu.SemaphoreType.DMA(()), pltpu.SemaphoreType.DMA(()), pltpu.SemaphoreType.DMA(()),
        ],
    )
    return pl.pallas_call(
        _dispatch_kernel,
        grid_spec=grid_spec,
        out_shape=jax.ShapeDtypeStruct((nb * BLK, XW), I32),
        compiler_params=pltpu.CompilerParams(
            dimension_semantics=("arbitrary",), vmem_limit_bytes=40 * 1024 * 1024),
        name="dispatch",
    )(cnt_f, lo_f, dst_f, extra, gap_start, gap_rows, hn, rrow, rcol)


def _expert_kernel(be_sm, nu_sm, xb_ref, w1_ref, w3_ref, w2_ref, yb_ref, w1_s, w3_s, w2_s):
    p = pl.program_id(0)
    used = p < nu_sm[0]

    @pl.when(used & ((p == 0) | (be_sm[p] != be_sm[jnp.maximum(p - 1, 0)])))
    def _():
        w1_s[...] = w1_ref[...].astype(BF16)
        w3_s[...] = w3_ref[...].astype(BF16)
        w2_s[...] = w2_ref[...].astype(BF16)

    @pl.when(used)
    def _():
        half = D_MODEL // 2
        for r in range(BLK // ESUB):
            rs = slice(r * ESUB, (r + 1) * ESUB)
            xlo, xhi = _unpack_halves(xb_ref[rs, :half])
            h1 = _dot(xlo, w1_s[:half, :]) + _dot(xhi, w1_s[half:, :])
            h3 = _dot(xlo, w3_s[:half, :]) + _dot(xhi, w3_s[half:, :])
            wrep = lax.bitcast_convert_type(xb_ref[rs, half:], F32)
            wfull = jnp.concatenate([wrep] * (D_EXPERT // LANES), axis=1)
            hdn = (h1 * jax.nn.sigmoid(h1) * h3 * wfull).astype(BF16)
            y = _dot(hdn, w2_s[...]).astype(BF16).astype(F32)
            yb_ref[rs, :] = _pack_halves(y)

    @pl.when(jnp.logical_not(used))
    def _():
        yb_ref[...] = jnp.zeros(yb_ref.shape, I32)


def _experts(blk_e, nused, xb, w1, w3, w2, nb):
    def rows(p, be, nu):
        return (jnp.minimum(p, nu[0] - 1), 0)

    def wsel(p, be, nu):
        return (be[jnp.minimum(p, nu[0] - 1)], 0, 0)

    grid_spec = pltpu.PrefetchScalarGridSpec(
        num_scalar_prefetch=2,
        grid=(nb,),
        in_specs=[
            pl.BlockSpec((BLK, XW), rows),
            pl.BlockSpec((None, D_MODEL, D_EXPERT), wsel),
            pl.BlockSpec((None, D_MODEL, D_EXPERT), wsel),
            pl.BlockSpec((None, D_EXPERT, D_MODEL), wsel),
        ],
        out_specs=pl.BlockSpec((BLK, D_MODEL // 2), lambda p, be, nu: (p, 0)),
        scratch_shapes=[
            pltpu.VMEM((D_MODEL, D_EXPERT), BF16), pltpu.VMEM((D_MODEL, D_EXPERT), BF16),
            pltpu.VMEM((D_EXPERT, D_MODEL), BF16),
        ],
    )
    return pl.pallas_call(
        _expert_kernel,
        grid_spec=grid_spec,
        out_shape=jax.ShapeDtypeStruct((nb * BLK, D_MODEL // 2), I32),
        compiler_params=pltpu.CompilerParams(
            dimension_semantics=("arbitrary",), vmem_limit_bytes=48 * 1024 * 1024),
        name="experts",
    )(blk_e, nused, xb, w1, w3, w2)


def _combine_kernel(cnt_sm, lo_sm, dst_sm, used_sm, extra_sm, yb_ref, x2_ref, rcol_ref, g_ref, out_ref,
                    yl_s, y_s, sem, sem2):
    i = pl.program_id(0)
    nt = pl.num_programs(0)
    slot = i % 2

    def ycopy(s, src, dst, sm):
        return pltpu.make_async_copy(yb_ref.at[pl.ds(src, CCH), :], yl_s.at[s, pl.ds(dst, CCH), :],
                                     sm.at[s])

    def issue(tile, s):
        for e in range(N_EXPERTS):
            lo = pl.multiple_of(lo_sm[tile * N_EXPERTS + e], CCH)
            src = pl.multiple_of(dst_sm[tile * N_EXPERTS + e], SUB)
            ycopy(s, src, lo, sem).start()

        @pl.when(extra_sm[tile] > 0)
        def _():
            def ebody(e, carry):
                n = cnt_sm[tile * N_EXPERTS + e]
                lo = pl.multiple_of(lo_sm[tile * N_EXPERTS + e], CCH)
                src = pl.multiple_of(dst_sm[tile * N_EXPERTS + e], SUB)

                def cbody(c, cc):
                    ycopy(s, src + c * CCH, lo + c * CCH, sem2).start()
                    return cc

                return lax.fori_loop(1, (n + CCH - 1) // CCH, cbody, carry)

            lax.fori_loop(0, N_EXPERTS, ebody, 0)

    @pl.when(i == 0)
    def _():
        yl_s[...] = jnp.zeros(yl_s.shape, I32)
        issue(0, 0)

    @pl.when(i + 1 < nt)
    def _():
        issue(i + 1, 1 - slot)

    for _ in range(N_EXPERTS):
        ycopy(slot, 0, 0, sem).wait()

    def wbody(j, c):
        ycopy(slot, 0, 0, sem2).wait()
        return c

    lax.fori_loop(0, extra_sm[i], wbody, 0)

    rc = rcol_ref[...]
    pad0 = rc[:, R_PAD0:R_PAD0 + 1]
    pad1 = rc[:, R_PAD1:R_PAD1 + 1]
    lane = lax.broadcasted_iota(I32, (TM, CBLK), 1).astype(F32)

    def chunk(c):
        r = lane + float(c * CBLK)
        selm = (jnp.where(pad0 == r, 1.0, 0.0) + jnp.where(pad1 == r, 1.0, 0.0)).astype(BF16)
        lo_h, hi_h = _unpack_halves(yl_s[slot, c * CBLK:(c + 1) * CBLK, :])
        return _dot(selm, lo_h), _dot(selm, hi_h)

    half = D_MODEL // 2
    ylo = jnp.zeros((TM, half), F32)
    yhi = jnp.zeros((TM, half), F32)
    for c in range(LOC_COMMON // CBLK):
        dlo, dhi = chunk(c)
        ylo = ylo + dlo
        yhi = yhi + dhi
    y_s[:, :half] = ylo
    y_s[:, half:] = yhi
    for c in range(LOC_COMMON // CBLK, LOC_ROWS // CBLK):
        @pl.when(used_sm[i] > c * CBLK)
        def _(c=c):
            dlo, dhi = chunk(c)
            y_s[:, :half] += dlo
            y_s[:, half:] += dhi
    out_ref[...] = _rms(x2_ref[...] + y_s[...], g_ref[...])


def _combine(cnt_f, lo16_f, dst_f, used, extra, yb, x2, rcol, g_final):
    T = x2.shape[0]
    nt = T // TM
    grid_spec = pltpu.PrefetchScalarGridSpec(
        num_scalar_prefetch=5,
        grid=(nt,),
        in_specs=[
            pl.BlockSpec(memory_space=pl.ANY),
            pl.BlockSpec((TM, D_MODEL), lambda i, *_: (i, 0)),
            pl.BlockSpec((TM, LANES), lambda i, *_: (i, 0)),
            pl.BlockSpec((1, D_MODEL), lambda i, *_: (0, 0)),
        ],
        out_specs=pl.BlockSpec((TM, D_MODEL), lambda i, *_: (i, 0)),
        scratch_shapes=[
            pltpu.VMEM((2, LOC_ROWS, D_MODEL // 2), I32),
            pltpu.VMEM((TM, D_MODEL), F32),
            pltpu.SemaphoreType.DMA((2,)), pltpu.SemaphoreType.DMA((2,)),
        ],
    )
    return pl.pallas_call(
        _combine_kernel,
        grid_spec=grid_spec,
        out_shape=jax.ShapeDtypeStruct((T, D_MODEL), F32),
        compiler_params=pltpu.CompilerParams(
            dimension_semantics=("arbitrary",), vmem_limit_bytes=40 * 1024 * 1024),
        name="combine",
    )(cnt_f, lo16_f, dst_f, used, extra, yb, x2, rcol, g_final)


def _rope_tables(S):
    half = DA_HEAD_DIM // 2
    inv = (1.0 / (np.float32(ROPE_THETA) ** (np.arange(0, DA_HEAD_DIM, 2, dtype=np.float32)
                                             / np.float32(DA_HEAD_DIM)))).astype(np.float32)
    ang = (np.arange(S, dtype=np.float32)[:, None] * inv[None, :]).astype(np.float32)
    cos, sin = np.cos(ang), np.sin(ang)
    lane = np.arange(LANES)
    idx = lane % half
    lower = (lane % DA_HEAD_DIM) < half
    c = cos[:, idx]
    s = sin[:, idx]
    z = np.zeros_like(s)
    return (jnp.asarray(c, F32), jnp.asarray(np.where(lower[None, :], -s, z), F32),
            jnp.asarray(np.where(lower[None, :], z, s), F32))


def _tri_dot(a, b):
    return jnp.dot(a.astype(F32), b.astype(F32), precision=lax.Precision.HIGHEST).astype(I32)


def kernel(x, w_in, conv_w, conv_b, gate_b, lam_qk, subln_g, mhnorm_g, w_out, g_mix, g_ffn, w_grp,
           b_grp, w_erouter, b_erouter, w1, w3, w2, g_final):
    B, S, D = x.shape
    T = B * S
    nt = T // TM
    x2d = x.reshape(T, D)
    l = 0

    w_main = w_in[l, :, :N_MAIN].astype(BF16)
    wg8 = w_in[l, :, N_MAIN:]
    wg = jnp.pad(wg8, ((0, 0), (0, LANES - 8))).astype(BF16)
    wgt = wg8.T.astype(BF16)
    gb8 = gate_b[l].reshape(8)
    gb_col = jnp.pad(gb8, (0, LANES - 8)).reshape(1, LANES)
    gb_row = jnp.broadcast_to(gb8[:, None], (8, LANES))
    rope_c, rope_sa, rope_sb = _rope_tables(S)
    w_r = jnp.concatenate(
        [w_grp[l], w_erouter[l].transpose(1, 0, 2).reshape(D, N_EXPERTS)], axis=1)
    w_r = jnp.pad(w_r, ((0, 0), (0, LANES - w_r.shape[1])))
    wr_hi = w_r.astype(BF16)
    wr_t = jnp.concatenate([wr_hi, (w_r - wr_hi.astype(F32)).astype(BF16)], axis=1)
    b_r = jnp.pad(jnp.concatenate([b_grp[l], b_erouter[l].reshape(N_EXPERTS)]),
                  (0, LANES - N_GROUPS - N_EXPERTS))
    b_rt = jnp.broadcast_to(b_r[:, None], (LANES, TM))

    proj, gcol, grow = _inproj(x2d, g_mix[l].reshape(1, D), w_main, wg, wgt)
    oa = _attention(proj, rope_c, rope_sa, rope_sb, lam_qk[l], subln_g[l].reshape(1, LANES), B, S)
    hm = _mlstm(proj, gcol, grow, conv_w[l], conv_b[l].reshape(1, -1), gb_col, gb_row,
                mhnorm_g[l].reshape(1, ML_WIDTH), B, S)
    x2, hn, rrow, rcol, cnt3 = _router(oa, hm, x2d, w_out[l].astype(BF16), g_ffn[l].reshape(1, D),
                                       wr_t, b_rt)

    cnt = cnt3[:, 0, EXP_ROW0:EXP_ROW0 + N_EXPERTS]
    c8 = (cnt + SUB - 1) // SUB * SUB
    c16 = jnp.maximum((cnt + CCH - 1) // CCH, 1) * CCH
    count8 = jnp.sum(c8, axis=0)
    pcount = (count8 + DCH + BLK - 1) // BLK * BLK
    up_e = np.triu(np.ones((N_EXPERTS, N_EXPERTS), np.float32), 1)
    lo_t = np.tril(np.ones((nt, nt), np.float32), -1)
    pstart = _tri_dot(pcount[None, :], up_e)[0]
    pend = pstart + pcount
    dst = pstart[None, :] + _tri_dot(lo_t, c8)
    lo8 = _tri_dot(c8, up_e)
    lo16 = _tri_dot(c16, up_e)
    nb = (2 * T + nt * N_EXPERTS * (SUB - 1) + N_EXPERTS * (DCH + BLK - 1)) // BLK + 1
    blk_row = jnp.arange(nb, dtype=I32) * BLK
    blk_e = jnp.minimum(jnp.sum((pend[None, :] <= blk_row[:, None]).astype(I32), axis=1),
                        N_EXPERTS - 1).astype(I32)
    nused = (pend[-1] // BLK).astype(I32).reshape(1)
    cnt_f = cnt.reshape(-1).astype(I32)
    dst_f = dst.reshape(-1).astype(I32)

    gap_start = jnp.concatenate([pstart + count8, pend[-1:]]).astype(I32)
    gap_rows = jnp.concatenate([pcount - count8, nb * BLK - pend[-1:]]).astype(I32)
    extra = jnp.sum(jnp.maximum((cnt + DCH - 1) // DCH - 1, 0), axis=1).astype(I32)
    xb = _dispatch(cnt_f, lo8.reshape(-1), dst_f, extra, gap_start, gap_rows, hn, rrow, rcol, nb)
    yb = _experts(blk_e, nused, xb, w1[l], w3[l], w2[l], nb)
    used = jnp.sum(c16, axis=1).astype(I32)
    out = _combine(cnt_f, lo16.reshape(-1), dst_f, used, extra, yb, x2, rcol, g_final.reshape(1, D))
    return out.reshape(B, S, D)
```

```python
import math

import numpy as np
import jax
import jax.numpy as jnp
from jax import lax
from jax.experimental import pallas as pl
from jax.experimental.pallas import tpu as pltpu

F32 = jnp.float32
BF16 = jnp.bfloat16
I32 = jnp.int32

D_MODEL = 1024
DA_HEADS = 4
DA_HEAD_DIM = 64
DA_V_DIM = 128
DA_WIDTH = 512
ML_HEADS = 4
ML_WIDTH = 512
ML_HEAD_DIM = 128
ML_CHUNK = 128
ML_GROUP = 4
CONV_K = 4
ROPE_THETA = 10000.0
RMS_EPS = 1e-6
N_GROUPS = 4
EXPERTS_PER_GROUP = 8
N_EXPERTS = 32
D_EXPERT = 512
LAMBDA_INIT = 0.8 - 0.6 * math.exp(-0.3 * 0)

LANES = 128
SUB = 8
N_MAIN = 7 * 512
NEG = -1e30

TM_PROJ = 1024
TQ = 256
TK = 512
VT_ROWS = DA_V_DIM + 16
TM = 256
BLK = 512
ESUB = 256
CBLK = 256
DCH = 32
CCH = 32
DLOC_ROWS = 768
LOC_COMMON = N_EXPERTS * CCH
LOC_ROWS = 2 * TM + N_EXPERTS * CCH
XW = D_MODEL // 2 + LANES
EXP_ROW0 = 4
RT_ROWS = 40
RT_GROUP = 4
R_E0, R_E1, R_POS0, R_POS1, R_PAD0, R_PAD1, R_W0, R_W1 = range(8)


def _nt_dot(a, b):
    return lax.dot_general(a, b, (((1,), (1,)), ((), ())), preferred_element_type=F32)


def _tn_dot(a, b):
    return lax.dot_general(a, b, (((0,), (0,)), ((), ())), preferred_element_type=F32)


def _dot(a, b):
    return jnp.dot(a, b, preferred_element_type=F32)


def _split3(x):
    hi = x.astype(BF16)
    r = x - hi.astype(F32)
    mid = r.astype(BF16)
    lo = (r - mid.astype(F32)).astype(BF16)
    return hi, mid, lo


def _rms(x, g):
    return x * lax.rsqrt(jnp.mean(x * x, axis=-1, keepdims=True) + RMS_EPS) * g


def _inproj_kernel(x_ref, g_ref, w_ref, wg_ref, wgt_ref, proj_ref, gcol_ref, grow_ref):
    h = _rms(x_ref[...], g_ref[...]).astype(BF16)
    ct = 512
    for c in range(N_MAIN // ct):
        proj_ref[:, c * ct:(c + 1) * ct] = _dot(h, w_ref[:, c * ct:(c + 1) * ct]).astype(BF16)
    gcol_ref[...] = _dot(h, wg_ref[...])
    grow_ref[...] = _nt_dot(wgt_ref[...], h)


def _inproj(x2d, g_mix, w_main, wg, wgt):
    T = x2d.shape[0]
    tm = TM_PROJ
    return pl.pallas_call(
        _inproj_kernel,
        grid=(T // tm,),
        in_specs=[
            pl.BlockSpec((tm, D_MODEL), lambda i: (i, 0)),
            pl.BlockSpec((1, D_MODEL), lambda i: (0, 0)),
            pl.BlockSpec((D_MODEL, N_MAIN), lambda i: (0, 0)),
            pl.BlockSpec((D_MODEL, LANES), lambda i: (0, 0)),
            pl.BlockSpec((8, D_MODEL), lambda i: (0, 0)),
        ],
        out_specs=[
            pl.BlockSpec((tm, N_MAIN), lambda i: (i, 0)),
            pl.BlockSpec((tm, LANES), lambda i: (i, 0)),
            pl.BlockSpec((8, tm), lambda i: (0, i)),
        ],
        out_shape=[
            jax.ShapeDtypeStruct((T, N_MAIN), BF16),
            jax.ShapeDtypeStruct((T, LANES), F32),
            jax.ShapeDtypeStruct((8, T), F32),
        ],
        compiler_params=pltpu.CompilerParams(
            dimension_semantics=("arbitrary",), vmem_limit_bytes=56 * 1024 * 1024),
        name="inproj",
    )(x2d, g_mix, w_main, wg, wgt)


def _attn_kernel(q_ref, k_ref, v_ref, c_ref, sa_ref, sb_ref, lam_ref, g_ref, o_ref,
                 q1_s, q2_s, k_s, vt_s, acc_s, st_s):
    S = q_ref.shape[0]
    lane = lax.broadcasted_iota(I32, (TQ, LANES), 1)
    qscale = DA_HEAD_DIM ** -0.5 * math.log2(math.e)

    def rope(x, rows):
        return (x * c_ref[rows, :] + pltpu.roll(x, 96, 1) * sa_ref[rows, :]
                + pltpu.roll(x, 32, 1) * sb_ref[rows, :])

    def prep(r, carry):
        rows = pl.ds(pl.multiple_of(r * TQ, TQ), TQ)
        for h in range(DA_HEADS):
            hs = slice(h * LANES, (h + 1) * LANES)
            qr = rope(q_ref[rows, hs].astype(F32), rows) * qscale
            q1_s[rows, hs] = jnp.where(lane < DA_HEAD_DIM, qr, 0.0).astype(BF16)
            q2_s[rows, hs] = jnp.where(lane >= DA_HEAD_DIM, qr, 0.0).astype(BF16)
            k_s[rows, hs] = rope(k_ref[rows, hs].astype(F32), rows).astype(BF16)
            vt_s[h, :DA_V_DIM, rows] = v_ref[rows, hs].astype(F32).T.astype(BF16)
        return carry

    lax.fori_loop(0, S // TQ, prep, 0)
    vt_s[:, DA_V_DIM:, :] = jnp.ones((DA_HEADS, VT_ROWS - DA_V_DIM, S), BF16)

    lq = lam_ref[...]
    lam = (jnp.exp(jnp.sum(lq[0:1] * lq[1:2], axis=-1, keepdims=True))
           - jnp.exp(jnp.sum(lq[2:3] * lq[3:4], axis=-1, keepdims=True)) + LAMBDA_INIT)

    key = lax.broadcasted_iota(I32, (TK, 2 * TQ), 0)
    qry = lax.broadcasted_iota(I32, (TK, 2 * TQ), 1)
    kq = key - jnp.where(qry >= TQ, qry - TQ, qry)
    qpk = TK // TQ

    def qtile(i, carry):
        qo = pl.multiple_of(i * TQ, TQ)
        nfull = i // qpk
        causal = kq <= (i - nfull * qpk) * TQ
        acc_s[...] = jnp.zeros(acc_s.shape, F32)

        def kvstep(j, ms, masked):
            ko = pl.multiple_of(j * TK, TK)
            out = []

            def scores(h):
                hs = slice(h * LANES, (h + 1) * LANES)
                qq = jnp.concatenate([q1_s[pl.ds(qo, TQ), hs], q2_s[pl.ds(qo, TQ), hs]], axis=0)
                st = _nt_dot(k_s[pl.ds(ko, TK), hs], qq)
                st_s[h] = jnp.where(causal, st, NEG) if masked else st

            scores(0)
            for h in range(DA_HEADS):
                if h + 1 < DA_HEADS:
                    scores(h + 1)
                st = st_s[h]
                m_new = jnp.maximum(ms[h], jnp.max(st, axis=0, keepdims=True))
                alpha = jnp.exp2(ms[h] - m_new)
                p = jnp.exp2(st - m_new).astype(BF16)
                acc_s[h] = alpha * acc_s[h] + _dot(vt_s[h, :, pl.ds(ko, TK)], p)
                out.append(m_new)
            return tuple(out)

        m0 = tuple(jnp.full((1, 2 * TQ), NEG, F32) for _ in range(DA_HEADS))
        m1 = lax.fori_loop(0, nfull, lambda j, ms: kvstep(j, ms, False), m0)
        kvstep(nfull, m1, True)

        for h in range(DA_HEADS):
            hs = slice(h * LANES, (h + 1) * LANES)
            acc = acc_s[h]
            on = acc[:DA_V_DIM] * (1.0 / acc[DA_V_DIM:DA_V_DIM + 1])
            ot = on[:, :TQ] - lam * on[:, TQ:]
            o = _rms(ot.T, g_ref[...]) * (1.0 - LAMBDA_INIT)
            o_ref[pl.ds(qo, TQ), hs] = o.astype(BF16)
        return carry

    lax.fori_loop(0, S // TQ, qtile, 0)


def _attention(proj, rope_c, rope_sa, rope_sb, lam_qk, subln_g, B, S):
    T = B * S
    full = lambda shape: pl.BlockSpec(shape, lambda b: (0, 0))
    cb = lambda j: pl.BlockSpec((S, DA_WIDTH), lambda b, j=j: (b, j))
    return pl.pallas_call(
        _attn_kernel,
        grid=(B,),
        in_specs=[
            cb(0), cb(1), cb(2),
            full((S, LANES)), full((S, LANES)), full((S, LANES)),
            full((4, DA_HEAD_DIM)), full((1, LANES)),
        ],
        out_specs=pl.BlockSpec((S, DA_WIDTH), lambda b: (b, 0)),
        out_shape=jax.ShapeDtypeStruct((T, DA_WIDTH), BF16),
        scratch_shapes=[
            pltpu.VMEM((S, DA_WIDTH), BF16), pltpu.VMEM((S, DA_WIDTH), BF16),
            pltpu.VMEM((S, DA_WIDTH), BF16),
            pltpu.VMEM((DA_HEADS, VT_ROWS, S), BF16),
            pltpu.VMEM((DA_HEADS, VT_ROWS, 2 * TQ), F32),
            pltpu.VMEM((DA_HEADS, TK, 2 * TQ), F32),
        ],
        compiler_params=pltpu.CompilerParams(
            dimension_semantics=("arbitrary",), vmem_limit_bytes=48 * 1024 * 1024),
        name="attn",
    )(proj, proj, proj, rope_c, rope_sa, rope_sb, lam_qk, subln_g)


def _log_sigmoid(x):
    return jnp.minimum(x, 0.0) - jnp.log(1.0 + jnp.exp(-jnp.abs(x)))


def _mlstm_kernel(q_ref, k_ref, v_ref, o_ref, gcol_ref, grow_ref, cw_ref, cb_ref, gbc_ref, gbr_ref,
                  mg_ref, out_ref, qc_s, kc_s, vt_s, ct_s, m_s):
    S = q_ref.shape[0]
    L = ML_CHUNK
    nc = S // L
    row16 = lax.broadcasted_iota(I32, (16, ML_HEAD_DIM), 0)

    def conv_silu(x_ref, c, ro, hs, wcol):
        w = cw_ref[:, wcol]
        cur = x_ref[pl.ds(ro, L), hs].astype(F32)
        po = pl.multiple_of(jnp.maximum(ro - 16, 0), 16)
        tail = jnp.where(c > 0, x_ref[pl.ds(po, 16), hs].astype(F32), 0.0)
        y = cb_ref[:, wcol] + cur * w[3:4]
        for s in (1, 2, 3):
            sh = pltpu.roll(cur, s, 0)
            top = jnp.where(row16 < s, pltpu.roll(tail, s, 0), sh[:16])
            sh = jnp.concatenate([top, sh[16:]], axis=0)
            y = y + sh * w[3 - s:4 - s]
        return y * jax.nn.sigmoid(y)

    def conv_body(c, carry):
        ro = pl.multiple_of(c * L, L)
        for h in range(ML_HEADS):
            hs = slice(h * ML_HEAD_DIM, (h + 1) * ML_HEAD_DIM)
            ks_ = slice(ML_WIDTH + h * ML_HEAD_DIM, ML_WIDTH + (h + 1) * ML_HEAD_DIM)
            qc_s[pl.ds(ro, L), hs] = conv_silu(q_ref, c, ro, hs, hs).astype(BF16)
            kc = conv_silu(k_ref, c, ro, hs, ks_)
            kc_s[pl.ds(ro, L), hs] = (kc * (ML_HEAD_DIM ** -0.5)).astype(BF16)
            vt_s[h, :ML_HEAD_DIM, pl.ds(ro, L)] = v_ref[pl.ds(ro, L), hs].astype(F32).T.astype(BF16)
        return carry

    lax.fori_loop(0, nc, conv_body, 0)
    vt_s[:, ML_HEAD_DIM:, :] = jnp.ones((ML_HEADS, ML_HEAD_DIM, S), BF16)

    ct_s[...] = jnp.zeros(ct_s.shape, F32)
    m_s[...] = jnp.zeros(m_s.shape, F32)

    ri = lax.broadcasted_iota(I32, (L, L), 0)
    ci = lax.broadcasted_iota(I32, (L, L), 1)
    causal_t = ri <= ci
    tril = jnp.where(ci <= ri, 1.0, 0.0).astype(BF16)
    triu = jnp.where(causal_t, 1.0, 0.0).astype(BF16)

    heads = [slice(h * ML_HEAD_DIM, (h + 1) * ML_HEAD_DIM) for h in range(ML_HEADS)]

    def gates(c):
        ro = pl.multiple_of(c * L, L)
        gc = gcol_ref[pl.ds(ro, L), :] + gbc_ref[...]
        gr = grow_ref[:, pl.ds(ro, L)] + gbr_ref[...]
        b_c = sum(_dot(tril, p) for p in _split3(_log_sigmoid(gc)))
        b_r = sum(_dot(p, triu) for p in _split3(_log_sigmoid(gr)))
        return ro, gc, gr, b_c, b_r

    def read_state(g):
        ro = g[0]
        ks, vts, kq, cq, ms = [], [], [], [], []
        for h, hs in enumerate(heads):
            q = qc_s[pl.ds(ro, L), hs]
            k = kc_s[pl.ds(ro, L), hs]
            ks.append(k)
            vts.append(vt_s[h, :, pl.ds(ro, L)])
            kq.append(_nt_dot(k, q))
            cq.append(_nt_dot(ct_s[h].astype(BF16), q))
            ms.append(m_s[h:h + 1, 0:1])
        return ks, vts, kq, cq, ms

    def update_state(g, d):
        _, _, gr, _, b_r = g
        ks, vts, _, _, ms = d
        for h in range(ML_HEADS):
            br = b_r[4 + h:5 + h, :]
            bl = br[:, L - 1:L]
            dec = bl - br + gr[h:h + 1, :]
            m_new = jnp.maximum(bl + ms[h], jnp.max(dec, axis=1, keepdims=True))
            ws = jnp.exp(dec - m_new)
            sc = jnp.exp(bl + ms[h] - m_new)
            vw = (vts[h].astype(F32) * ws).astype(BF16)
            ct_s[h] = sc * ct_s[h] + _dot(vw, ks[h])
            m_s[h:h + 1, :] = jnp.broadcast_to(m_new, (1, LANES))

    def outputs(g, d):
        ro, gc, _, b_c, b_r = g
        _, vts, kq, cq, ms = d
        for h, hs in enumerate(heads):
            br = b_r[4 + h:5 + h, :]
            a_col = b_c[:, 4 + h:5 + h] - gc[:, h:h + 1]
            dm = jnp.where(causal_t, br - a_col, NEG)
            inter = br + ms[h]
            m_row = jnp.maximum(inter, jnp.max(dm, axis=0, keepdims=True))
            sm = jnp.exp(dm - m_row) * kq[h]
            sc_in = jnp.exp(inter - m_row)
            a = _dot(vts[h], sm.astype(BF16)) + sc_in * cq[h]
            den = a[ML_HEAD_DIM:ML_HEAD_DIM + 1, :]
            hh = (a[:ML_HEAD_DIM] / jnp.maximum(jnp.abs(den), jnp.exp(-m_row))).T
            hg = hh * jax.nn.sigmoid(o_ref[pl.ds(ro, L), hs].astype(F32))
            out_ref[pl.ds(ro, L), hs] = _rms(hg, mg_ref[:, hs]).astype(BF16)

    def chunk_group(cg, carry):
        gs = [gates(ML_GROUP * cg + j) for j in range(ML_GROUP)]
        ds = []
        for g in gs:
            ds.append(read_state(g))
            update_state(g, ds[-1])
        for g, d in zip(gs, ds):
            outputs(g, d)
        return carry

    lax.fori_loop(0, nc // ML_GROUP, chunk_group, 0)


def _mlstm(proj, gcol, grow, conv_w, conv_b, gb_col, gb_row, mh_g, B, S):
    T = B * S
    full = lambda shape: pl.BlockSpec(shape, lambda b: (0, 0))
    cb = lambda j: pl.BlockSpec((S, ML_WIDTH), lambda b, j=j: (b, j))
    return pl.pallas_call(
        _mlstm_kernel,
        grid=(B,),
        in_specs=[
            cb(3), cb(4), cb(5), cb(6),
            pl.BlockSpec((S, LANES), lambda b: (b, 0)),
            pl.BlockSpec((8, S), lambda b: (0, b)),
            full((CONV_K, 2 * ML_WIDTH)), full((1, 2 * ML_WIDTH)),
            full((1, LANES)), full((8, LANES)), full((1, ML_WIDTH)),
        ],
        out_specs=pl.BlockSpec((S, ML_WIDTH), lambda b: (b, 0)),
        out_shape=jax.ShapeDtypeStruct((T, ML_WIDTH), BF16),
        scratch_shapes=[
            pltpu.VMEM((S, ML_WIDTH), BF16), pltpu.VMEM((S, ML_WIDTH), BF16),
            pltpu.VMEM((ML_HEADS, 2 * ML_HEAD_DIM, S), BF16),
            pltpu.VMEM((ML_HEADS, 2 * ML_HEAD_DIM, ML_HEAD_DIM), F32),
            pltpu.VMEM((8, LANES), F32),
        ],
        compiler_params=pltpu.CompilerParams(
            dimension_semantics=("arbitrary",), vmem_limit_bytes=56 * 1024 * 1024),
        name="mlstm",
    )(proj, proj, proj, proj, gcol, grow, conv_w, conv_b, gb_col, gb_row, mh_g)


def _router_kernel(oa_ref, hm_ref, x_ref, wo_ref, g_ref, wr_ref, brt_ref,
                   x2_ref, hn_ref, rrow_ref, rcol_ref, cnt_ref):
    def project(s):
        rs = slice(s * TM, (s + 1) * TM)
        mixo = _dot(oa_ref[rs, :], wo_ref[:DA_WIDTH, :]) + _dot(hm_ref[rs, :], wo_ref[DA_WIDTH:, :])
        x2 = x_ref[rs, :] + mixo
        x2_ref[rs, :] = x2
        hn = _rms(x2, g_ref[...])
        hn_hi = hn.astype(BF16)
        hn_ref[rs, :] = hn_hi
        hn_lo = (hn - hn_hi.astype(F32)).astype(BF16)
        a = _dot(hn_hi, wr_ref[...])
        b = _dot(hn_lo, wr_ref[:, :LANES])
        return (a[:, :LANES] + a[:, LANES:] + b).T[:RT_ROWS] + brt_ref[:RT_ROWS, :]

    lts = [project(0)]
    for s in range(RT_GROUP):
        if s + 1 < RT_GROUP:
            lts.append(project(s + 1))
        _route(s, lts[s], rrow_ref, rcol_ref, cnt_ref)


def _route(s, lt, rrow_ref, rcol_ref, cnt_ref):
    sub = lax.broadcasted_iota(I32, (RT_ROWS, TM), 0)
    sub_f = sub.astype(F32)

    def cmax(v):
        return jnp.max(v, axis=0, keepdims=True)

    def first_idx(mask):
        return jnp.min(jnp.where(mask, sub_f, 1e6), axis=0, keepdims=True).astype(I32)

    gl = jnp.where(sub < N_GROUPS, lt, NEG)
    gmax = cmax(gl)
    gsel = first_idx(gl == gmax)
    g_w = 1.0 / jnp.sum(jnp.exp(gl - gmax), axis=0, keepdims=True)
    elo = EXP_ROW0 + gsel * EXPERTS_PER_GROUP
    el = jnp.where((sub >= elo) & (sub < elo + EXPERTS_PER_GROUP), lt, NEG)
    v1 = cmax(el)
    i1 = first_idx(el == v1)
    el2 = jnp.where(sub == i1, NEG, el)
    v2 = cmax(el2)
    i2 = first_idx(el2 == v2)
    t = jnp.exp(v2 - v1)
    w0 = g_w / (1.0 + t)
    w1 = g_w * t / (1.0 + t)

    oh0 = jnp.where(sub == i1, 1.0, 0.0)
    oh1 = jnp.where(sub == i2, 1.0, 0.0)
    mh = oh0 + oh1
    r_i = lax.broadcasted_iota(I32, (TM, TM), 0)
    c_i = lax.broadcasted_iota(I32, (TM, TM), 1)
    before = jnp.where(r_i < c_i, 1.0, 0.0).astype(BF16)
    mh_f = jnp.concatenate([mh, jnp.zeros((LANES - RT_ROWS, TM), F32)], axis=0)
    mh_b = mh_f.astype(BF16)
    pre = _dot(mh_b, before)[:RT_ROWS]
    cnt = jnp.sum(mh_f, axis=1, keepdims=True)
    cnt8 = jnp.floor((cnt + (SUB - 1)) * (1.0 / SUB)) * SUB
    erow = lax.broadcasted_iota(I32, (LANES, 1), 0)
    is_exp = (erow >= EXP_ROW0) & (erow < EXP_ROW0 + N_EXPERTS)
    cnt16 = jnp.where(is_exp, jnp.maximum(jnp.floor((cnt + (CCH - 1)) * (1.0 / CCH)), 1.0) * CCH, 0.0)
    e_r = lax.broadcasted_iota(I32, (LANES, LANES), 0)
    e_c = lax.broadcasted_iota(I32, (LANES, LANES), 1)
    below = jnp.where(e_c < e_r, 1.0, 0.0).astype(BF16)
    lo8 = _dot(below, jnp.broadcast_to(cnt8, (LANES, LANES)).astype(BF16))[:RT_ROWS, 0:1]
    lo16 = _dot(below, jnp.broadcast_to(cnt16, (LANES, LANES)).astype(BF16))[:RT_ROWS, 0:1]

    def csum(v):
        return jnp.sum(v, axis=0, keepdims=True)

    rows = [None] * 8
    rows[R_E0] = (i1 - EXP_ROW0).astype(F32)
    rows[R_E1] = (i2 - EXP_ROW0).astype(F32)
    rows[R_POS0] = csum(oh0 * (pre + lo8))
    rows[R_POS1] = csum(oh1 * (pre + lo8))
    rows[R_PAD0] = csum(oh0 * (pre + lo16))
    rows[R_PAD1] = csum(oh1 * (pre + lo16))
    rows[R_W0] = w0
    rows[R_W1] = w1
    sub128 = lax.broadcasted_iota(I32, (LANES, TM), 0)
    r128 = jnp.zeros((LANES, TM), F32)
    for j, v in enumerate(rows):
        r128 = jnp.where(sub128 == j, v, r128)
    rrow_ref[:, s * TM:(s + 1) * TM] = r128[:8]
    rcol_ref[s * TM:(s + 1) * TM, :] = r128.T
    cnt_ref[s] = _nt_dot(jnp.ones((8, TM), BF16), mh_b)[0:1].astype(I32)


def _router(oa, hm, x2d, w_out, g_ffn, wr_t, b_rt):
    T = x2d.shape[0]
    nt = T // TM
    full = lambda shape: pl.BlockSpec(shape, lambda i: (0, 0))
    tg = RT_GROUP * TM
    return pl.pallas_call(
        _router_kernel,
        grid=(T // tg,),
        in_specs=[
            pl.BlockSpec((tg, DA_WIDTH), lambda i: (i, 0)),
            pl.BlockSpec((tg, ML_WIDTH), lambda i: (i, 0)),
            pl.BlockSpec((tg, D_MODEL), lambda i: (i, 0)),
            full((D_MODEL, D_MODEL)), full((1, D_MODEL)),
            full((D_MODEL, 2 * LANES)), full((LANES, TM)),
        ],
        out_specs=[
            pl.BlockSpec((tg, D_MODEL), lambda i: (i, 0)),
            pl.BlockSpec((tg, D_MODEL), lambda i: (i, 0)),
            pl.BlockSpec((8, tg), lambda i: (0, i)),
            pl.BlockSpec((tg, LANES), lambda i: (i, 0)),
            pl.BlockSpec((RT_GROUP, 1, LANES), lambda i: (i, 0, 0)),
        ],
        out_shape=[
            jax.ShapeDtypeStruct((T, D_MODEL), F32),
            jax.ShapeDtypeStruct((T, D_MODEL), BF16),
            jax.ShapeDtypeStruct((8, T), F32),
            jax.ShapeDtypeStruct((T, LANES), F32),
            jax.ShapeDtypeStruct((nt, 1, LANES), I32),
        ],
        compiler_params=pltpu.CompilerParams(
            dimension_semantics=("arbitrary",), vmem_limit_bytes=40 * 1024 * 1024),
        name="router",
    )(oa, hm, x2d, w_out, g_ffn, wr_t, b_rt)


def _pack_halves(x):
    lo = lax.bitcast_convert_type(x[:, :D_MODEL // 2], I32)
    hi = lax.bitcast_convert_type(x[:, D_MODEL // 2:], I32)
    return lax.shift_right_logical(lo, 16) | (hi & jnp.int32(-65536))


def _unpack_halves(w):
    lo = lax.bitcast_convert_type(lax.shift_left(w, 16), F32)
    hi = lax.bitcast_convert_type(w & jnp.int32(-65536), F32)
    return lo.astype(BF16), hi.astype(BF16)


def _dispatch_kernel(cnt_sm, lo_sm, dst_sm, extra_sm, gs_sm, gn_sm, hn_ref, rrow_ref, rcol_ref, xb_ref,
                     xs_s, z_s, semx, semx2, semz):
    i = pl.program_id(0)
    nt = pl.num_programs(0)
    slot = i % 2
    half = D_MODEL // 2

    def zcopy(rows, dst):
        return pltpu.make_async_copy(z_s.at[pl.ds(0, rows), :], xb_ref.at[pl.ds(dst, rows), :], semz)

    def gap_fill(wait):
        def ebody(e, carry):
            start = pl.multiple_of(gs_sm[e], SUB)
            left = gn_sm[e]
            for rows in (BLK, DCH, SUB):
                n = left // rows

                def body(c, cc, rows=rows, start=start):
                    cp = zcopy(rows, pl.multiple_of(start + c * rows, SUB))
                    cp.wait() if wait else cp.start()
                    return cc

                lax.fori_loop(0, n, body, 0)
                start = start + n * rows
                left = left - n * rows
            return carry

        lax.fori_loop(0, N_EXPERTS + 1, ebody, 0)

    @pl.when(i == 0)
    def _():
        for s in range(2):
            xs_s[s, DLOC_ROWS:, :] = jnp.zeros((DCH, XW), I32)
        z_s[...] = jnp.zeros(z_s.shape, I32)
        gap_fill(False)
        gap_fill(True)

    rr = rrow_ref[...]
    rc = rcol_ref[...]
    r_i = lax.broadcasted_iota(I32, (DLOC_ROWS, TM), 0).astype(F32)
    perm0 = jnp.where(r_i == rr[R_POS0:R_POS0 + 1, :], 1.0, 0.0).astype(BF16)
    perm1 = jnp.where(r_i == rr[R_POS1:R_POS1 + 1, :], 1.0, 0.0).astype(BF16)
    xs = _dot(perm0 + perm1, hn_ref[...])
    xs_s[slot, :DLOC_ROWS, :half] = _pack_halves(xs)
    lane = lax.broadcasted_iota(I32, (TM, LANES), 1)

    def parts(col):
        hi, mid, lo = (p.astype(F32) for p in _split3(jnp.broadcast_to(rc[:, col:col + 1], (TM, LANES))))
        sel = jnp.where(lane == 0, hi, jnp.where(lane == 1, mid, jnp.where(lane == 2, lo, 0.0)))
        return sel.astype(BF16)

    wparts = _dot(perm0, parts(R_W0)) + _dot(perm1, parts(R_W1))
    l_r = lax.broadcasted_iota(I32, (LANES, LANES), 0)
    ones3 = jnp.where(l_r < 3, 1.0, 0.0).astype(BF16)
    wsort = _dot(wparts.astype(BF16), ones3)
    xs_s[slot, :DLOC_ROWS, half:] = lax.bitcast_convert_type(wsort, I32)

    def xcopy(s, src, dst, sem):
        return pltpu.make_async_copy(xs_s.at[s, pl.ds(src, DCH), :], xb_ref.at[pl.ds(dst, DCH), :], sem)

    def wait_all(n_extra):
        for _ in range(N_EXPERTS):
            xcopy(0, 0, 0, semx).wait()

        def wbody(j, c):
            xcopy(0, 0, 0, semx2).wait()
            return c
        lax.fori_loop(0, n_extra, wbody, 0)

    @pl.when(i > 0)
    def _():
        wait_all(extra_sm[jnp.maximum(i - 1, 0)])

    for e in range(N_EXPERTS):
        lo = pl.multiple_of(lo_sm[i * N_EXPERTS + e], SUB)
        dst = pl.multiple_of(dst_sm[i * N_EXPERTS + e], SUB)
        xcopy(slot, lo, dst, semx).start()

    @pl.when(extra_sm[i] > 0)
    def _():
        def ebody(e, carry):
            n = cnt_sm[i * N_EXPERTS + e]
            lo = pl.multiple_of(lo_sm[i * N_EXPERTS + e], SUB)
            dst = pl.multiple_of(dst_sm[i * N_EXPERTS + e], SUB)

            def cbody(c, cc):
                xcopy(slot, lo + c * DCH, dst + c * DCH, semx2).start()
                return cc

            return lax.fori_loop(1, (n + DCH - 1) // DCH, cbody, carry)

        lax.fori_loop(0, N_EXPERTS, ebody, 0)

    @pl.when(i == nt - 1)
    def _():
        wait_all(extra_sm[i])


def _dispatch(cnt_f, lo_f, dst_f, extra, gap_start, gap_rows, hn, rrow, rcol, nb):
    T = hn.shape[0]
    nt = T // TM
    grid_spec = pltpu.PrefetchScalarGridSpec(
        num_scalar_prefetch=6,
        grid=(nt,),
        in_specs=[
            pl.BlockSpec((TM, D_MODEL), lambda i, *_: (i, 0)),
            pl.BlockSpec((8, TM), lambda i, *_: (0, i)),
            pl.BlockSpec((TM, LANES), lambda i, *_: (i, 0)),
        ],
        out_specs=pl.BlockSpec(memory_space=pl.ANY),
        scratch_shapes=[
            pltpu.VMEM((2, DLOC_ROWS + DCH, XW), I32),
            pltpu.VMEM((BLK, XW), I32),
            pltpu.SemaphoreType.DMA(()), pltpu.SemaphoreType.DMA(()), pltpu.SemaphoreType.DMA(()),
        ],
    )
    return pl.pallas_call(
        _dispatch_kernel,
        grid_spec=grid_spec,
        out_shape=jax.ShapeDtypeStruct((nb * BLK, XW), I32),
        compiler_params=pltpu.CompilerParams(
            dimension_semantics=("arbitrary",), vmem_limit_bytes=40 * 1024 * 1024),
        name="dispatch",
    )(cnt_f, lo_f, dst_f, extra, gap_start, gap_rows, hn, rrow, rcol)


def _expert_kernel(be_sm, nu_sm, xb_ref, w1_ref, w3_ref, w2_ref, yb_ref, w1_s, w3_s, w2_s):
    p = pl.program_id(0)
    used = p < nu_sm[0]

    @pl.when(used & ((p == 0) | (be_sm[p] != be_sm[jnp.maximum(p - 1, 0)])))
    def _():
        w1_s[...] = w1_ref[...].astype(BF16)
        w3_s[...] = w3_ref[...].astype(BF16)
        w2_s[...] = w2_ref[...].astype(BF16)

    @pl.when(used)
    def _():
        half = D_MODEL // 2
        for r in range(BLK // ESUB):
            rs = slice(r * ESUB, (r + 1) * ESUB)
            xlo, xhi = _unpack_halves(xb_ref[rs, :half])
            h1 = _dot(xlo, w1_s[:half, :]) + _dot(xhi, w1_s[half:, :])
            h3 = _dot(xlo, w3_s[:half, :]) + _dot(xhi, w3_s[half:, :])
            wrep = lax.bitcast_convert_type(xb_ref[rs, half:], F32)
            wfull = jnp.concatenate([wrep] * (D_EXPERT // LANES), axis=1)
            hdn = (h1 * jax.nn.sigmoid(h1) * h3 * wfull).astype(BF16)
            y = _dot(hdn, w2_s[...]).astype(BF16).astype(F32)
            yb_ref[rs, :] = _pack_halves(y)

    @pl.when(jnp.logical_not(used))
    def _():
        yb_ref[...] = jnp.zeros(yb_ref.shape, I32)


def _experts(blk_e, nused, xb, w1, w3, w2, nb):
    def rows(p, be, nu):
        return (jnp.minimum(p, nu[0] - 1), 0)

    def wsel(p, be, nu):
        return (be[jnp.minimum(p, nu[0] - 1)], 0, 0)

    grid_spec = pltpu.PrefetchScalarGridSpec(
        num_scalar_prefetch=2,
        grid=(nb,),
        in_specs=[
            pl.BlockSpec((BLK, XW), rows),
            pl.BlockSpec((None, D_MODEL, D_EXPERT), wsel),
            pl.BlockSpec((None, D_MODEL, D_EXPERT), wsel),
            pl.BlockSpec((None, D_EXPERT, D_MODEL), wsel),
        ],
        out_specs=pl.BlockSpec((BLK, D_MODEL // 2), lambda p, be, nu: (p, 0)),
        scratch_shapes=[
            pltpu.VMEM((D_MODEL, D_EXPERT), BF16), pltpu.VMEM((D_MODEL, D_EXPERT), BF16),
            pltpu.VMEM((D_EXPERT, D_MODEL), BF16),
        ],
    )
    return pl.pallas_call(
        _expert_kernel,
        grid_spec=grid_spec,
        out_shape=jax.ShapeDtypeStruct((nb * BLK, D_MODEL // 2), I32),
        compiler_params=pltpu.CompilerParams(
            dimension_semantics=("arbitrary",), vmem_limit_bytes=48 * 1024 * 1024),
        name="experts",
    )(blk_e, nused, xb, w1, w3, w2)


def _combine_kernel(cnt_sm, lo_sm, dst_sm, used_sm, extra_sm, yb_ref, x2_ref, rcol_ref, g_ref, out_ref,
                    yl_s, y_s, sem, sem2):
    i = pl.program_id(0)
    nt = pl.num_programs(0)
    slot = i % 2

    def ycopy(s, src, dst, sm):
        return pltpu.make_async_copy(yb_ref.at[pl.ds(src, CCH), :], yl_s.at[s, pl.ds(dst, CCH), :],
                                     sm.at[s])

    def issue(tile, s):
        for e in range(N_EXPERTS):
            lo = pl.multiple_of(lo_sm[tile * N_EXPERTS + e], CCH)
            src = pl.multiple_of(dst_sm[tile * N_EXPERTS + e], SUB)
            ycopy(s, src, lo, sem).start()

        @pl.when(extra_sm[tile] > 0)
        def _():
            def ebody(e, carry):
                n = cnt_sm[tile * N_EXPERTS + e]
                lo = pl.multiple_of(lo_sm[tile * N_EXPERTS + e], CCH)
                src = pl.multiple_of(dst_sm[tile * N_EXPERTS + e], SUB)

                def cbody(c, cc):
                    ycopy(s, src + c * CCH, lo + c * CCH, sem2).start()
                    return cc

                return lax.fori_loop(1, (n + CCH - 1) // CCH, cbody, carry)

            lax.fori_loop(0, N_EXPERTS, ebody, 0)

    @pl.when(i == 0)
    def _():
        yl_s[...] = jnp.zeros(yl_s.shape, I32)
        issue(0, 0)

    @pl.when(i + 1 < nt)
    def _():
        issue(i + 1, 1 - slot)

    for _ in range(N_EXPERTS):
        ycopy(slot, 0, 0, sem).wait()

    def wbody(j, c):
        ycopy(slot, 0, 0, sem2).wait()
        return c

    lax.fori_loop(0, extra_sm[i], wbody, 0)

    rc = rcol_ref[...]
    pad0 = rc[:, R_PAD0:R_PAD0 + 1]
    pad1 = rc[:, R_PAD1:R_PAD1 + 1]
    lane = lax.broadcasted_iota(I32, (TM, CBLK), 1).astype(F32)

    def chunk(c):
        r = lane + float(c * CBLK)
        selm = (jnp.where(pad0 == r, 1.0, 0.0) + jnp.where(pad1 == r, 1.0, 0.0)).astype(BF16)
        lo_h, hi_h = _unpack_halves(yl_s[slot, c * CBLK:(c + 1) * CBLK, :])
        return _dot(selm, lo_h), _dot(selm, hi_h)

    half = D_MODEL // 2
    ylo = jnp.zeros((TM, half), F32)
    yhi = jnp.zeros((TM, half), F32)
    for c in range(LOC_COMMON // CBLK):
        dlo, dhi = chunk(c)
        ylo = ylo + dlo
        yhi = yhi + dhi
    y_s[:, :half] = ylo
    y_s[:, half:] = yhi
    for c in range(LOC_COMMON // CBLK, LOC_ROWS // CBLK):
        @pl.when(used_sm[i] > c * CBLK)
        def _(c=c):
            dlo, dhi = chunk(c)
            y_s[:, :half] += dlo
            y_s[:, half:] += dhi
    out_ref[...] = _rms(x2_ref[...] + y_s[...], g_ref[...])


def _combine(cnt_f, lo16_f, dst_f, used, extra, yb, x2, rcol, g_final):
    T = x2.shape[0]
    nt = T // TM
    grid_spec = pltpu.PrefetchScalarGridSpec(
        num_scalar_prefetch=5,
        grid=(nt,),
        in_specs=[
            pl.BlockSpec(memory_space=pl.ANY),
            pl.BlockSpec((TM, D_MODEL), lambda i, *_: (i, 0)),
            pl.BlockSpec((TM, LANES), lambda i, *_: (i, 0)),
            pl.BlockSpec((1, D_MODEL), lambda i, *_: (0, 0)),
        ],
        out_specs=pl.BlockSpec((TM, D_MODEL), lambda i, *_: (i, 0)),
        scratch_shapes=[
            pltpu.VMEM((2, LOC_ROWS, D_MODEL // 2), I32),
            pltpu.VMEM((TM, D_MODEL), F32),
            pltpu.SemaphoreType.DMA((2,)), pltpu.SemaphoreType.DMA((2,)),
        ],
    )
    return pl.pallas_call(
        _combine_kernel,
        grid_spec=grid_spec,
        out_shape=jax.ShapeDtypeStruct((T, D_MODEL), F32),
        compiler_params=pltpu.CompilerParams(
            dimension_semantics=("arbitrary",), vmem_limit_bytes=40 * 1024 * 1024),
        name="combine",
    )(cnt_f, lo16_f, dst_f, used, extra, yb, x2, rcol, g_final)


def _rope_tables(S):
    half = DA_HEAD_DIM // 2
    inv = (1.0 / (np.float32(ROPE_THETA) ** (np.arange(0, DA_HEAD_DIM, 2, dtype=np.float32)
                                             / np.float32(DA_HEAD_DIM)))).astype(np.float32)
    ang = (np.arange(S, dtype=np.float32)[:, None] * inv[None, :]).astype(np.float32)
    cos, sin = np.cos(ang), np.sin(ang)
    lane = np.arange(LANES)
    idx = lane % half
    lower = (lane % DA_HEAD_DIM) < half
    c = cos[:, idx]
    s = sin[:, idx]
    z = np.zeros_like(s)
    return (jnp.asarray(c, F32), jnp.asarray(np.where(lower[None, :], -s, z), F32),
            jnp.asarray(np.where(lower[None, :], z, s), F32))


def _tri_dot(a, b):
    return jnp.dot(a.astype(F32), b.astype(F32), precision=lax.Precision.HIGHEST).astype(I32)


def kernel(x, w_in, conv_w, conv_b, gate_b, lam_qk, subln_g, mhnorm_g, w_out, g_mix, g_ffn, w_grp,
           b_grp, w_erouter, b_erouter, w1, w3, w2, g_final):
    B, S, D = x.shape
    T = B * S
    nt = T // TM
    x2d = x.reshape(T, D)
    l = 0

    w_main = w_in[l, :, :N_MAIN].astype(BF16)
    wg8 = w_in[l, :, N_MAIN:]
    wg = jnp.pad(wg8, ((0, 0), (0, LANES - 8))).astype(BF16)
    wgt = wg8.T.astype(BF16)
    gb8 = gate_b[l].reshape(8)
    gb_col = jnp.pad(gb8, (0, LANES - 8)).reshape(1, LANES)
    gb_row = jnp.broadcast_to(gb8[:, None], (8, LANES))
    rope_c, rope_sa, rope_sb = _rope_tables(S)
    w_r = jnp.concatenate(
        [w_grp[l], w_erouter[l].transpose(1, 0, 2).reshape(D, N_EXPERTS)], axis=1)
    w_r = jnp.pad(w_r, ((0, 0), (0, LANES - w_r.shape[1])))
    wr_hi = w_r.astype(BF16)
    wr_t = jnp.concatenate([wr_hi, (w_r - wr_hi.astype(F32)).astype(BF16)], axis=1)
    b_r = jnp.pad(jnp.concatenate([b_grp[l], b_erouter[l].reshape(N_EXPERTS)]),
                  (0, LANES - N_GROUPS - N_EXPERTS))
    b_rt = jnp.broadcast_to(b_r[:, None], (LANES, TM))

    proj, gcol, grow = _inproj(x2d, g_mix[l].reshape(1, D), w_main, wg, wgt)
    oa = _attention(proj, rope_c, rope_sa, rope_sb, lam_qk[l], subln_g[l].reshape(1, LANES), B, S)
    hm = _mlstm(proj, gcol, grow, conv_w[l], conv_b[l].reshape(1, -1), gb_col, gb_row,
                mhnorm_g[l].reshape(1, ML_WIDTH), B, S)
    x2, hn, rrow, rcol, cnt3 = _router(oa, hm, x2d, w_out[l].astype(BF16), g_ffn[l].reshape(1, D),
                                       wr_t, b_rt)

    cnt = cnt3[:, 0, EXP_ROW0:EXP_ROW0 + N_EXPERTS]
    c8 = (cnt + SUB - 1) // SUB * SUB
    c16 = jnp.maximum((cnt + CCH - 1) // CCH, 1) * CCH
    count8 = jnp.sum(c8, axis=0)
    pcount = (count8 + DCH + BLK - 1) // BLK * BLK
    up_e = np.triu(np.ones((N_EXPERTS, N_EXPERTS), np.float32), 1)
    lo_t = np.tril(np.ones((nt, nt), np.float32), -1)
    pstart = _tri_dot(pcount[None, :], up_e)[0]
    pend = pstart + pcount
    dst = pstart[None, :] + _tri_dot(lo_t, c8)
    lo8 = _tri_dot(c8, up_e)
    lo16 = _tri_dot(c16, up_e)
    nb = (2 * T + nt * N_EXPERTS * (SUB - 1) + N_EXPERTS * (DCH + BLK - 1)) // BLK + 1
    blk_row = jnp.arange(nb, dtype=I32) * BLK
    blk_e = jnp.minimum(jnp.sum((pend[None, :] <= blk_row[:, None]).astype(I32), axis=1),
                        N_EXPERTS - 1).astype(I32)
    nused = (pend[-1] // BLK).astype(I32).reshape(1)
    cnt_f = cnt.reshape(-1).astype(I32)
    dst_f = dst.reshape(-1).astype(I32)

    gap_start = jnp.concatenate([pstart + count8, pend[-1:]]).astype(I32)
    gap_rows = jnp.concatenate([pcount - count8, nb * BLK - pend[-1:]]).astype(I32)
    extra = jnp.sum(jnp.maximum((cnt + DCH - 1) // DCH - 1, 0), axis=1).astype(I32)
    xb = _dispatch(cnt_f, lo8.reshape(-1), dst_f, extra, gap_start, gap_rows, hn, rrow, rcol, nb)
    yb = _experts(blk_e, nused, xb, w1[l], w3[l], w2[l], nb)
    used = jnp.sum(c16, axis=1).astype(I32)
    out = _combine(cnt_f, lo16.reshape(-1), dst_f, used, extra, yb, x2, rcol, g_final.reshape(1, D))
    return out.reshape(B, S, D)
```

```python
import functools
import math

import numpy as np
import jax
import jax.numpy as jnp
from jax import lax
from jax.experimental import pallas as pl
from jax.experimental.pallas import tpu as pltpu

F32 = jnp.float32
BF16 = jnp.bfloat16
I32 = jnp.int32

D_MODEL = 1024
DA_HEADS = 4
DA_HEAD_DIM = 64
DA_V_DIM = 128
DA_WIDTH = 512
ML_HEADS = 4
ML_WIDTH = 512
ML_HEAD_DIM = 128
ML_CHUNK = 128
ML_GROUP = 4
CONV_K = 4
ROPE_THETA = 10000.0
RMS_EPS = 1e-6
N_GROUPS = 4
EXPERTS_PER_GROUP = 8
N_EXPERTS = 32
D_EXPERT = 512
LAMBDA_INIT = 0.8 - 0.6 * math.exp(-0.3 * 0)

LANES = 128
SUB = 8
N_MAIN = 7 * 512
NEG = -1e30

TM_PROJ = 1024
TQ = 256
TK = 512
VT_ROWS = DA_V_DIM + 16
TM = 256
BLK = 512
ESUB = 256
CBLK = 256
DCH = 32
CCH = 32
DLOC_ROWS = 768
LOC_COMMON = N_EXPERTS * CCH
LOC_ROWS = 2 * TM + N_EXPERTS * CCH
XW = D_MODEL // 2 + LANES
EXP_ROW0 = 4
RT_ROWS = 40
RT_GROUP = 4
R_E0, R_E1, R_POS0, R_POS1, R_PAD0, R_PAD1, R_W0, R_W1 = range(8)


def _nt_dot(a, b):
    return lax.dot_general(a, b, (((1,), (1,)), ((), ())), preferred_element_type=F32)


def _tn_dot(a, b):
    return lax.dot_general(a, b, (((0,), (0,)), ((), ())), preferred_element_type=F32)


def _dot(a, b):
    return jnp.dot(a, b, preferred_element_type=F32)


def _split3(x):
    hi = x.astype(BF16)
    r = x - hi.astype(F32)
    mid = r.astype(BF16)
    lo = (r - mid.astype(F32)).astype(BF16)
    return hi, mid, lo


def _rms(x, g):
    return x * lax.rsqrt(jnp.mean(x * x, axis=-1, keepdims=True) + RMS_EPS) * g


def _inproj_kernel(x_ref, g_ref, w_ref, wg_ref, wgt_ref, c_ref, sa_ref, sb_ref, cw_ref, cb_ref,
                   q1_ref, q2_ref, kr_ref, va_ref, qc_ref, kc_ref, vm_ref, om_ref, gcol_ref, grow_ref,
                   conv_s, *, tiles_per_seq):
    tm = x_ref.shape[0]
    cw = 512
    first = (pl.program_id(0) % tiles_per_seq) == 0

    @pl.when(first)
    def _():
        conv_s[0:8, :] = jnp.zeros((8, 2 * ML_WIDTH), F32)

    @pl.when(jnp.logical_not(first))
    def _():
        conv_s[0:8, :] = conv_s[tm:tm + 8, :]

    h = _rms(x_ref[...], g_ref[...]).astype(BF16)

    def cols(c):
        return _dot(h, w_ref[:, c * cw:(c + 1) * cw])

    rb = min(256, tm)
    lane = lax.broadcasted_iota(I32, (rb, LANES), 1)
    qscale = DA_HEAD_DIM ** -0.5 * math.log2(math.e)

    def rope(x, rows):
        return (x * c_ref[rows, :] + pltpu.roll(x, 96, 1) * sa_ref[rows, :]
                + pltpu.roll(x, 32, 1) * sb_ref[rows, :])

    def blocks():
        for r in range(tm // rb):
            for hh in range(cw // LANES):
                yield slice(r * rb, (r + 1) * rb), slice(hh * LANES, (hh + 1) * LANES)

    def conv_silu(c0, out_ref, scale):
        for rows, hs in blocks():
            cs = slice(c0 + hs.start, c0 + hs.stop)
            w = cw_ref[:, cs]
            y = cb_ref[:, cs] + conv_s[8 + rows.start:8 + rows.stop, cs] * w[3:4]
            for s in (1, 2, 3):
                y = y + conv_s[8 + rows.start - s:8 + rows.stop - s, cs] * w[3 - s:4 - s]
            y = y * jax.nn.sigmoid(y)
            out_ref[rows, hs] = (y if scale is None else y * scale).astype(BF16)

    qa = cols(0)
    ka = cols(1)
    for rows, hs in blocks():
        qr = rope(qa[rows, hs], rows) * qscale
        q1_ref[rows, hs] = jnp.where(lane < DA_HEAD_DIM, qr, 0.0).astype(BF16)
        q2_ref[rows, hs] = jnp.where(lane >= DA_HEAD_DIM, qr, 0.0).astype(BF16)
    va_ref[...] = cols(2).astype(BF16)
    for rows, hs in blocks():
        kr_ref[rows, hs] = rope(ka[rows, hs], rows).astype(BF16)
    conv_s[8:, :cw] = cols(3)
    vm_ref[...] = cols(5).astype(BF16)
    conv_silu(0, qc_ref, None)
    conv_s[8:, cw:] = cols(4)
    om_ref[...] = cols(6).astype(BF16)
    gcol_ref[...] = _dot(h, wg_ref[...])
    grow_ref[...] = _nt_dot(wgt_ref[...], h)
    conv_silu(ML_WIDTH, kc_ref, ML_HEAD_DIM ** -0.5)


def _inproj(x2d, g_mix, w_main, wg, wgt, rope_c, rope_sa, rope_sb, conv_w, conv_b, S):
    T = x2d.shape[0]
    tm = min(TM_PROJ, S)
    assert S % tm == 0
    tps = S // tm
    full = lambda shape: pl.BlockSpec(shape, lambda i: (0, 0))
    rope_spec = pl.BlockSpec((tm, LANES), lambda i: (i % tps, 0))
    col_spec = pl.BlockSpec((tm, 512), lambda i: (i, 0))
    return pl.pallas_call(
        functools.partial(_inproj_kernel, tiles_per_seq=tps),
        grid=(T // tm,),
        in_specs=[
            pl.BlockSpec((tm, D_MODEL), lambda i: (i, 0)),
            full((1, D_MODEL)), full((D_MODEL, N_MAIN)), full((D_MODEL, LANES)), full((8, D_MODEL)),
            rope_spec, rope_spec, rope_spec,
            full((CONV_K, 2 * ML_WIDTH)), full((1, 2 * ML_WIDTH)),
        ],
        out_specs=[col_spec] * 8 + [
            pl.BlockSpec((tm, LANES), lambda i: (i, 0)),
            pl.BlockSpec((8, tm), lambda i: (0, i)),
        ],
        out_shape=[jax.ShapeDtypeStruct((T, 512), BF16)] * 8 + [
            jax.ShapeDtypeStruct((T, LANES), F32),
            jax.ShapeDtypeStruct((8, T), F32),
        ],
        scratch_shapes=[pltpu.VMEM((tm + 8, 2 * ML_WIDTH), F32)],
        compiler_params=pltpu.CompilerParams(
            dimension_semantics=("arbitrary",), vmem_limit_bytes=56 * 1024 * 1024),
        name="inproj",
    )(x2d, g_mix, w_main, wg, wgt, rope_c, rope_sa, rope_sb, conv_w, conv_b)


def _attn_kernel(q1_s, q2_s, k_s, v_ref, lam_ref, g_ref, o_ref, vt_s, acc_s, st_s):
    S = v_ref.shape[0]

    def prep(r, carry):
        rows = pl.ds(pl.multiple_of(r * TQ, TQ), TQ)
        for h in range(DA_HEADS):
            hs = slice(h * LANES, (h + 1) * LANES)
            vt_s[h, :DA_V_DIM, rows] = v_ref[rows, hs].astype(F32).T.astype(BF16)
        return carry

    lax.fori_loop(0, S // TQ, prep, 0)
    vt_s[:, DA_V_DIM:, :] = jnp.ones((DA_HEADS, VT_ROWS - DA_V_DIM, S), BF16)

    lq = lam_ref[...]
    lam = (jnp.exp(jnp.sum(lq[0:1] * lq[1:2], axis=-1, keepdims=True))
           - jnp.exp(jnp.sum(lq[2:3] * lq[3:4], axis=-1, keepdims=True)) + LAMBDA_INIT)

    key = lax.broadcasted_iota(I32, (TK, 2 * TQ), 0)
    qry = lax.broadcasted_iota(I32, (TK, 2 * TQ), 1)
    kq = key - jnp.where(qry >= TQ, qry - TQ, qry)
    qpk = TK // TQ

    def kvstep(qo, j, ms, causal, slot):
        ko = pl.multiple_of(j * TK, TK)
        out = []

        def scores(h):
            hs = slice(h * LANES, (h + 1) * LANES)
            qq = jnp.concatenate([q1_s[pl.ds(qo, TQ), hs], q2_s[pl.ds(qo, TQ), hs]], axis=0)
            st = _nt_dot(k_s[pl.ds(ko, TK), hs], qq)
            st_s[h] = st if causal is None else jnp.where(causal, st, NEG)

        scores(0)
        for h in range(DA_HEADS):
            if h + 1 < DA_HEADS:
                scores(h + 1)
            st = st_s[h]
            m_new = jnp.maximum(ms[h], jnp.max(st, axis=0, keepdims=True))
            alpha = jnp.exp2(ms[h] - m_new)
            p = jnp.exp2(st - m_new).astype(BF16)
            acc_s[slot, h] = alpha * acc_s[slot, h] + _dot(vt_s[h, :, pl.ds(ko, TK)], p)
            out.append(m_new)
        return tuple(out)

    def diagonal(i, slot):
        qo = pl.multiple_of(i * TQ, TQ)
        nfull = i // qpk
        acc_s[slot] = jnp.zeros(acc_s.shape[1:], F32)
        m0 = tuple(jnp.full((1, 2 * TQ), NEG, F32) for _ in range(DA_HEADS))
        return kvstep(qo, nfull, m0, kq <= (i - nfull * qpk) * TQ, slot)

    def finalize(i, slot):
        qo = pl.multiple_of(i * TQ, TQ)
        for h in range(DA_HEADS):
            hs = slice(h * LANES, (h + 1) * LANES)
            acc = acc_s[slot, h]
            on = acc[:DA_V_DIM] * (1.0 / acc[DA_V_DIM:DA_V_DIM + 1])
            ot = on[:, :TQ] - lam * on[:, TQ:]
            o = _rms(ot.T, g_ref[...]) * (1.0 - LAMBDA_INIT)
            o_ref[pl.ds(qo, TQ), hs] = o.astype(BF16)

    def qtile(i, carry):
        slot = i % 2
        finalize(i - 1, 1 - slot)
        ms = diagonal(i, slot)
        qo = pl.multiple_of(i * TQ, TQ)
        lax.fori_loop(0, i // qpk, lambda j, m: kvstep(qo, j, m, None, slot), ms)
        return carry

    diagonal(0, 0)
    nq = S // TQ
    lax.fori_loop(1, nq, qtile, 0)
    finalize(nq - 1, (nq - 1) % 2)


def _attention(q1, q2, kr, va, lam_qk, subln_g, B, S):
    T = B * S
    full = lambda shape: pl.BlockSpec(shape, lambda b: (0, 0))
    seq = pl.BlockSpec((S, DA_WIDTH), lambda b: (b, 0))
    return pl.pallas_call(
        _attn_kernel,
        grid=(B,),
        in_specs=[seq, seq, seq, seq, full((4, DA_HEAD_DIM)), full((1, LANES))],
        out_specs=seq,
        out_shape=jax.ShapeDtypeStruct((T, DA_WIDTH), BF16),
        scratch_shapes=[
            pltpu.VMEM((DA_HEADS, VT_ROWS, S), BF16),
            pltpu.VMEM((2, DA_HEADS, VT_ROWS, 2 * TQ), F32),
            pltpu.VMEM((DA_HEADS, TK, 2 * TQ), F32),
        ],
        compiler_params=pltpu.CompilerParams(
            dimension_semantics=("arbitrary",), vmem_limit_bytes=48 * 1024 * 1024),
        name="attn",
    )(q1, q2, kr, va, lam_qk, subln_g)


def _log_sigmoid(x):
    return jnp.minimum(x, 0.0) - jnp.log(1.0 + jnp.exp(-jnp.abs(x)))


def _mlstm_kernel(qc_s, kc_s, v_ref, o_ref, gcol_ref, grow_ref, gbc_ref, gbr_ref,
                  mg_ref, out_ref, vt_s, ct_s, m_s):
    S = v_ref.shape[0]
    L = ML_CHUNK
    nc = S // L

    def transpose_v(c, carry):
        ro = pl.multiple_of(c * L, L)
        for h in range(ML_HEADS):
            hs = slice(h * ML_HEAD_DIM, (h + 1) * ML_HEAD_DIM)
            vt_s[h, :ML_HEAD_DIM, pl.ds(ro, L)] = v_ref[pl.ds(ro, L), hs].astype(F32).T.astype(BF16)
        return carry

    lax.fori_loop(0, nc, transpose_v, 0)
    vt_s[:, ML_HEAD_DIM:, :] = jnp.ones((ML_HEADS, ML_HEAD_DIM, S), BF16)

    ct_s[...] = jnp.zeros(ct_s.shape, F32)
    m_s[...] = jnp.zeros(m_s.shape, F32)

    ri = lax.broadcasted_iota(I32, (L, L), 0)
    ci = lax.broadcasted_iota(I32, (L, L), 1)
    causal_t = ri <= ci
    tril = jnp.where(ci <= ri, 1.0, 0.0).astype(BF16)
    triu = jnp.where(causal_t, 1.0, 0.0).astype(BF16)

    heads = [slice(h * ML_HEAD_DIM, (h + 1) * ML_HEAD_DIM) for h in range(ML_HEADS)]

    def gates(c):
        ro = pl.multiple_of(c * L, L)
        gc = gcol_ref[pl.ds(ro, L), :] + gbc_ref[...]
        gr = grow_ref[:, pl.ds(ro, L)] + gbr_ref[...]
        b_c = sum(_dot(tril, p) for p in _split3(_log_sigmoid(gc)))
        b_r = sum(_dot(p, triu) for p in _split3(_log_sigmoid(gr)))
        return ro, gc, gr, b_c, b_r

    def read_state(g):
        ro = g[0]
        ks, vts, kq, cq, ms = [], [], [], [], []
        for h, hs in enumerate(heads):
            q = qc_s[pl.ds(ro, L), hs]
            k = kc_s[pl.ds(ro, L), hs]
            ks.append(k)
            vts.append(vt_s[h, :, pl.ds(ro, L)])
            kq.append(_nt_dot(k, q))
            cq.append(_nt_dot(ct_s[h].astype(BF16), q))
            ms.append(m_s[h:h + 1, 0:1])
        return ks, vts, kq, cq, ms

    def update_state(g, d):
        _, _, gr, _, b_r = g
        ks, vts, _, _, ms = d
        for h in range(ML_HEADS):
            br = b_r[4 + h:5 + h, :]
            bl = br[:, L - 1:L]
            dec = bl - br + gr[h:h + 1, :]
            m_new = jnp.maximum(bl + ms[h], jnp.max(dec, axis=1, keepdims=True))
            ws = jnp.exp(dec - m_new)
            sc = jnp.exp(bl + ms[h] - m_new)
            vw = (vts[h].astype(F32) * ws).astype(BF16)
            ct_s[h] = sc * ct_s[h] + _dot(vw, ks[h])
            m_s[h:h + 1, :] = jnp.broadcast_to(m_new, (1, LANES))

    def outputs(g, d):
        ro, gc, _, b_c, b_r = g
        _, vts, kq, cq, ms = d
        for h, hs in enumerate(heads):
            br = b_r[4 + h:5 + h, :]
            a_col = b_c[:, 4 + h:5 + h] - gc[:, h:h + 1]
            dm = jnp.where(causal_t, br - a_col, NEG)
            inter = br + ms[h]
            m_row = jnp.maximum(inter, jnp.max(dm, axis=0, keepdims=True))
            sm = jnp.exp(dm - m_row) * kq[h]
            sc_in = jnp.exp(inter - m_row)
            a = _dot(vts[h], sm.astype(BF16)) + sc_in * cq[h]
            den = a[ML_HEAD_DIM:ML_HEAD_DIM + 1, :]
            hh = (a[:ML_HEAD_DIM] / jnp.maximum(jnp.abs(den), jnp.exp(-m_row))).T
            hg = hh * jax.nn.sigmoid(o_ref[pl.ds(ro, L), hs].astype(F32))
            out_ref[pl.ds(ro, L), hs] = _rms(hg, mg_ref[:, hs]).astype(BF16)

    def chunk_group(cg, carry):
        gs = [gates(ML_GROUP * cg + j) for j in range(ML_GROUP)]
        ds = []
        for g in gs:
            ds.append(read_state(g))
            update_state(g, ds[-1])
        for g, d in zip(gs, ds):
            outputs(g, d)
        return carry

    lax.fori_loop(0, nc // ML_GROUP, chunk_group, 0)


def _mlstm(qc, kc, vm, om, gcol, grow, gb_col, gb_row, mh_g, B, S):
    T = B * S
    full = lambda shape: pl.BlockSpec(shape, lambda b: (0, 0))
    seq = pl.BlockSpec((S, ML_WIDTH), lambda b: (b, 0))
    return pl.pallas_call(
        _mlstm_kernel,
        grid=(B,),
        in_specs=[
            seq, seq, seq, seq,
            pl.BlockSpec((S, LANES), lambda b: (b, 0)),
            pl.BlockSpec((8, S), lambda b: (0, b)),
            full((1, LANES)), full((8, LANES)), full((1, ML_WIDTH)),
        ],
        out_specs=seq,
        out_shape=jax.ShapeDtypeStruct((T, ML_WIDTH), BF16),
        scratch_shapes=[
            pltpu.VMEM((ML_HEADS, 2 * ML_HEAD_DIM, S), BF16),
            pltpu.VMEM((ML_HEADS, 2 * ML_HEAD_DIM, ML_HEAD_DIM), F32),
            pltpu.VMEM((8, LANES), F32),
        ],
        compiler_params=pltpu.CompilerParams(
            dimension_semantics=("arbitrary",), vmem_limit_bytes=56 * 1024 * 1024),
        name="mlstm",
    )(qc, kc, vm, om, gcol, grow, gb_col, gb_row, mh_g)


def _router_kernel(oa_ref, hm_ref, x_ref, wo_ref, g_ref, wr_ref, brt_ref,
                   x2_ref, hn_ref, rrow_ref, rcol_ref, cnt_ref):
    def project(s):
        rs = slice(s * TM, (s + 1) * TM)
        mixo = _dot(oa_ref[rs, :], wo_ref[:DA_WIDTH, :]) + _dot(hm_ref[rs, :], wo_ref[DA_WIDTH:, :])
        x2 = x_ref[rs, :] + mixo
        x2_ref[rs, :] = x2
        hn = _rms(x2, g_ref[...])
        hn_hi = hn.astype(BF16)
        hn_ref[rs, :] = hn_hi
        hn_lo = (hn - hn_hi.astype(F32)).astype(BF16)
        a = _dot(hn_hi, wr_ref[...])
        b = _dot(hn_lo, wr_ref[:, :LANES])
        return (a[:, :LANES] + a[:, LANES:] + b).T[:RT_ROWS] + brt_ref[:RT_ROWS, :]

    lts = [project(0)]
    for s in range(RT_GROUP):
        if s + 1 < RT_GROUP:
            lts.append(project(s + 1))
        _route(s, lts[s], rrow_ref, rcol_ref, cnt_ref)


def _route(s, lt, rrow_ref, rcol_ref, cnt_ref):
    sub = lax.broadcasted_iota(I32, (RT_ROWS, TM), 0)
    sub_f = sub.astype(F32)

    def cmax(v):
        return jnp.max(v, axis=0, keepdims=True)

    def first_idx(mask):
        return jnp.min(jnp.where(mask, sub_f, 1e6), axis=0, keepdims=True).astype(I32)

    gl = jnp.where(sub < N_GROUPS, lt, NEG)
    gmax = cmax(gl)
    gsel = first_idx(gl == gmax)
    g_w = 1.0 / jnp.sum(jnp.exp(gl - gmax), axis=0, keepdims=True)
    elo = EXP_ROW0 + gsel * EXPERTS_PER_GROUP
    el = jnp.where((sub >= elo) & (sub < elo + EXPERTS_PER_GROUP), lt, NEG)
    v1 = cmax(el)
    i1 = first_idx(el == v1)
    el2 = jnp.where(sub == i1, NEG, el)
    v2 = cmax(el2)
    i2 = first_idx(el2 == v2)
    t = jnp.exp(v2 - v1)
    w0 = g_w / (1.0 + t)
    w1 = g_w * t / (1.0 + t)

    oh0 = jnp.where(sub == i1, 1.0, 0.0)
    oh1 = jnp.where(sub == i2, 1.0, 0.0)
    mh = oh0 + oh1
    r_i = lax.broadcasted_iota(I32, (TM, TM), 0)
    c_i = lax.broadcasted_iota(I32, (TM, TM), 1)
    before = jnp.where(r_i < c_i, 1.0, 0.0).astype(BF16)
    mh_f = jnp.concatenate([mh, jnp.zeros((LANES - RT_ROWS, TM), F32)], axis=0)
    mh_b = mh_f.astype(BF16)
    pre = _dot(mh_b, before)[:RT_ROWS]
    cnt = jnp.sum(mh_f, axis=1, keepdims=True)
    cnt8 = jnp.floor((cnt + (SUB - 1)) * (1.0 / SUB)) * SUB
    erow = lax.broadcasted_iota(I32, (LANES, 1), 0)
    is_exp = (erow >= EXP_ROW0) & (erow < EXP_ROW0 + N_EXPERTS)
    cnt16 = jnp.where(is_exp, jnp.maximum(jnp.floor((cnt + (CCH - 1)) * (1.0 / CCH)), 1.0) * CCH, 0.0)
    e_r = lax.broadcasted_iota(I32, (LANES, LANES), 0)
    e_c = lax.broadcasted_iota(I32, (LANES, LANES), 1)
    below = jnp.where(e_c < e_r, 1.0, 0.0).astype(BF16)
    lo8 = _dot(below, jnp.broadcast_to(cnt8, (LANES, LANES)).astype(BF16))[:RT_ROWS, 0:1]
    lo16 = _dot(below, jnp.broadcast_to(cnt16, (LANES, LANES)).astype(BF16))[:RT_ROWS, 0:1]

    def csum(v):
        return jnp.sum(v, axis=0, keepdims=True)

    rows = [None] * 8
    rows[R_E0] = (i1 - EXP_ROW0).astype(F32)
    rows[R_E1] = (i2 - EXP_ROW0).astype(F32)
    rows[R_POS0] = csum(oh0 * (pre + lo8))
    rows[R_POS1] = csum(oh1 * (pre + lo8))
    rows[R_PAD0] = csum(oh0 * (pre + lo16))
    rows[R_PAD1] = csum(oh1 * (pre + lo16))
    rows[R_W0] = w0
    rows[R_W1] = w1
    sub128 = lax.broadcasted_iota(I32, (LANES, TM), 0)
    r128 = jnp.zeros((LANES, TM), F32)
    for j, v in enumerate(rows):
        r128 = jnp.where(sub128 == j, v, r128)
    rrow_ref[:, s * TM:(s + 1) * TM] = r128[:8]
    rcol_ref[s * TM:(s + 1) * TM, :] = r128.T
    cnt_ref[s] = _nt_dot(jnp.ones((8, TM), BF16), mh_b)[0:1].astype(I32)


def _router(oa, hm, x2d, w_out, g_ffn, wr_t, b_rt):
    T = x2d.shape[0]
    nt = T // TM
    full = lambda shape: pl.BlockSpec(shape, lambda i: (0, 0))
    tg = RT_GROUP * TM
    return pl.pallas_call(
        _router_kernel,
        grid=(T // tg,),
        in_specs=[
            pl.BlockSpec((tg, DA_WIDTH), lambda i: (i, 0)),
            pl.BlockSpec((tg, ML_WIDTH), lambda i: (i, 0)),
            pl.BlockSpec((tg, D_MODEL), lambda i: (i, 0)),
            full((D_MODEL, D_MODEL)), full((1, D_MODEL)),
            full((D_MODEL, 2 * LANES)), full((LANES, TM)),
        ],
        out_specs=[
            pl.BlockSpec((tg, D_MODEL), lambda i: (i, 0)),
            pl.BlockSpec((tg, D_MODEL), lambda i: (i, 0)),
            pl.BlockSpec((8, tg), lambda i: (0, i)),
            pl.BlockSpec((tg, LANES), lambda i: (i, 0)),
            pl.BlockSpec((RT_GROUP, 1, LANES), lambda i: (i, 0, 0)),
        ],
        out_shape=[
            jax.ShapeDtypeStruct((T, D_MODEL), F32),
            jax.ShapeDtypeStruct((T, D_MODEL), BF16),
            jax.ShapeDtypeStruct((8, T), F32),
            jax.ShapeDtypeStruct((T, LANES), F32),
            jax.ShapeDtypeStruct((nt, 1, LANES), I32),
        ],
        compiler_params=pltpu.CompilerParams(
            dimension_semantics=("arbitrary",), vmem_limit_bytes=40 * 1024 * 1024),
        name="router",
    )(oa, hm, x2d, w_out, g_ffn, wr_t, b_rt)


def _pack_halves(x):
    lo = lax.bitcast_convert_type(x[:, :D_MODEL // 2], I32)
    hi = lax.bitcast_convert_type(x[:, D_MODEL // 2:], I32)
    return lax.shift_right_logical(lo, 16) | (hi & jnp.int32(-65536))


def _unpack_halves(w):
    lo = lax.bitcast_convert_type(lax.shift_left(w, 16), F32)
    hi = lax.bitcast_convert_type(w & jnp.int32(-65536), F32)
    return lo.astype(BF16), hi.astype(BF16)


def _dispatch_kernel(cnt_sm, lo_sm, dst_sm, extra_sm, gs_sm, gn_sm, hn_ref, rrow_ref, rcol_ref, xb_ref,
                     xs_s, z_s, semx, semx2, semz):
    i = pl.program_id(0)
    nt = pl.num_programs(0)
    slot = i % 2
    half = D_MODEL // 2

    def zcopy(rows, dst):
        return pltpu.make_async_copy(z_s.at[pl.ds(0, rows), :], xb_ref.at[pl.ds(dst, rows), :], semz)

    def gap_fill(wait):
        def ebody(e, carry):
            start = pl.multiple_of(gs_sm[e], SUB)
            left = gn_sm[e]
            for rows in (BLK, DCH, SUB):
                n = left // rows

                def body(c, cc, rows=rows, start=start):
                    cp = zcopy(rows, pl.multiple_of(start + c * rows, SUB))
                    cp.wait() if wait else cp.start()
                    return cc

                lax.fori_loop(0, n, body, 0)
                start = start + n * rows
                left = left - n * rows
            return carry

        lax.fori_loop(0, N_EXPERTS + 1, ebody, 0)

    @pl.when(i == 0)
    def _():
        for s in range(2):
            xs_s[s, DLOC_ROWS:, :] = jnp.zeros((DCH, XW), I32)
        z_s[...] = jnp.zeros(z_s.shape, I32)
        gap_fill(False)
        gap_fill(True)

    rr = rrow_ref[...]
    rc = rcol_ref[...]
    r_i = lax.broadcasted_iota(I32, (DLOC_ROWS, TM), 0).astype(F32)
    perm0 = jnp.where(r_i == rr[R_POS0:R_POS0 + 1, :], 1.0, 0.0).astype(BF16)
    perm1 = jnp.where(r_i == rr[R_POS1:R_POS1 + 1, :], 1.0, 0.0).astype(BF16)
    xs = _dot(perm0 + perm1, hn_ref[...])
    xs_s[slot, :DLOC_ROWS, :half] = _pack_halves(xs)
    lane = lax.broadcasted_iota(I32, (TM, LANES), 1)

    def parts(col):
        hi, mid, lo = (p.astype(F32) for p in _split3(jnp.broadcast_to(rc[:, col:col + 1], (TM, LANES))))
        sel = jnp.where(lane == 0, hi, jnp.where(lane == 1, mid, jnp.where(lane == 2, lo, 0.0)))
        return sel.astype(BF16)

    wparts = _dot(perm0, parts(R_W0)) + _dot(perm1, parts(R_W1))
    l_r = lax.broadcasted_iota(I32, (LANES, LANES), 0)
    ones3 = jnp.where(l_r < 3, 1.0, 0.0).astype(BF16)
    wsort = _dot(wparts.astype(BF16), ones3)
    xs_s[slot, :DLOC_ROWS, half:] = lax.bitcast_convert_type(wsort, I32)

    def xcopy(s, src, dst, sem):
        return pltpu.make_async_copy(xs_s.at[s, pl.ds(src, DCH), :], xb_ref.at[pl.ds(dst, DCH), :], sem)

    def wait_all(n_extra):
        for _ in range(N_EXPERTS):
            xcopy(0, 0, 0, semx).wait()

        def wbody(j, c):
            xcopy(0, 0, 0, semx2).wait()
            return c
        lax.fori_loop(0, n_extra, wbody, 0)

    @pl.when(i > 0)
    def _():
        wait_all(extra_sm[jnp.maximum(i - 1, 0)])

    for e in range(N_EXPERTS):
        lo = pl.multiple_of(lo_sm[i * N_EXPERTS + e], SUB)
        dst = pl.multiple_of(dst_sm[i * N_EXPERTS + e], SUB)
        xcopy(slot, lo, dst, semx).start()

    @pl.when(extra_sm[i] > 0)
    def _():
        def ebody(e, carry):
            n = cnt_sm[i * N_EXPERTS + e]
            lo = pl.multiple_of(lo_sm[i * N_EXPERTS + e], SUB)
            dst = pl.multiple_of(dst_sm[i * N_EXPERTS + e], SUB)

            def cbody(c, cc):
                xcopy(slot, lo + c * DCH, dst + c * DCH, semx2).start()
                return cc

            return lax.fori_loop(1, (n + DCH - 1) // DCH, cbody, carry)

        lax.fori_loop(0, N_EXPERTS, ebody, 0)

    @pl.when(i == nt - 1)
    def _():
        wait_all(extra_sm[i])


def _dispatch(cnt_f, lo_f, dst_f, extra, gap_start, gap_rows, hn, rrow, rcol, nb):
    T = hn.shape[0]
    nt = T // TM
    grid_spec = pltpu.PrefetchScalarGridSpec(
        num_scalar_prefetch=6,
        grid=(nt,),
        in_specs=[
            pl.BlockSpec((TM, D_MODEL), lambda i, *_: (i, 0)),
            pl.BlockSpec((8, TM), lambda i, *_: (0, i)),
            pl.BlockSpec((TM, LANES), lambda i, *_: (i, 0)),
        ],
        out_specs=pl.BlockSpec(memory_space=pl.ANY),
        scratch_shapes=[
            pltpu.VMEM((2, DLOC_ROWS + DCH, XW), I32),
            pltpu.VMEM((BLK, XW), I32),
            pltpu.SemaphoreType.DMA(()), pltpu.SemaphoreType.DMA(()), pltpu.SemaphoreType.DMA(()),
        ],
    )
    return pl.pallas_call(
        _dispatch_kernel,
        grid_spec=grid_spec,
        out_shape=jax.ShapeDtypeStruct((nb * BLK, XW), I32),
        compiler_params=pltpu.CompilerParams(
            dimension_semantics=("arbitrary",), vmem_limit_bytes=40 * 1024 * 1024),
        name="dispatch",
    )(cnt_f, lo_f, dst_f, extra, gap_start, gap_rows, hn, rrow, rcol)


def _expert_kernel(be_sm, nu_sm, xb_ref, w1_ref, w3_ref, w2_ref, yb_ref, w1_s, w3_s, w2_s):
    p = pl.program_id(0)
    used = p < nu_sm[0]

    @pl.when(used & ((p == 0) | (be_sm[p] != be_sm[jnp.maximum(p - 1, 0)])))
    def _():
        w1_s[...] = w1_ref[...].astype(BF16)
        w3_s[...] = w3_ref[...].astype(BF16)
        w2_s[...] = w2_ref[...].astype(BF16)

    @pl.when(used)
    def _():
        half = D_MODEL // 2
        for r in range(BLK // ESUB):
            rs = slice(r * ESUB, (r + 1) * ESUB)
            xlo, xhi = _unpack_halves(xb_ref[rs, :half])
            h1 = _dot(xlo, w1_s[:half, :]) + _dot(xhi, w1_s[half:, :])
            h3 = _dot(xlo, w3_s[:half, :]) + _dot(xhi, w3_s[half:, :])
            wrep = lax.bitcast_convert_type(xb_ref[rs, half:], F32)
            wfull = jnp.concatenate([wrep] * (D_EXPERT // LANES), axis=1)
            hdn = (h1 * jax.nn.sigmoid(h1) * h3 * wfull).astype(BF16)
            y = _dot(hdn, w2_s[...]).astype(BF16).astype(F32)
            yb_ref[rs, :] = _pack_halves(y)

    @pl.when(jnp.logical_not(used))
    def _():
        yb_ref[...] = jnp.zeros(yb_ref.shape, I32)


def _experts(blk_e, nused, xb, w1, w3, w2, nb):
    def rows(p, be, nu):
        return (jnp.minimum(p, nu[0] - 1), 0)

    def wsel(p, be, nu):
        return (be[jnp.minimum(p, nu[0] - 1)], 0, 0)

    grid_spec = pltpu.PrefetchScalarGridSpec(
        num_scalar_prefetch=2,
        grid=(nb,),
        in_specs=[
            pl.BlockSpec((BLK, XW), rows),
            pl.BlockSpec((None, D_MODEL, D_EXPERT), wsel),
            pl.BlockSpec((None, D_MODEL, D_EXPERT), wsel),
            pl.BlockSpec((None, D_EXPERT, D_MODEL), wsel),
        ],
        out_specs=pl.BlockSpec((BLK, D_MODEL // 2), lambda p, be, nu: (p, 0)),
        scratch_shapes=[
            pltpu.VMEM((D_MODEL, D_EXPERT), BF16), pltpu.VMEM((D_MODEL, D_EXPERT), BF16),
            pltpu.VMEM((D_EXPERT, D_MODEL), BF16),
        ],
    )
    return pl.pallas_call(
        _expert_kernel,
        grid_spec=grid_spec,
        out_shape=jax.ShapeDtypeStruct((nb * BLK, D_MODEL // 2), I32),
        compiler_params=pltpu.CompilerParams(
            dimension_semantics=("arbitrary",), vmem_limit_bytes=48 * 1024 * 1024),
        name="experts",
    )(blk_e, nused, xb, w1, w3, w2)


def _combine_kernel(cnt_sm, lo_sm, dst_sm, used_sm, extra_sm, yb_ref, x2_ref, rcol_ref, g_ref, out_ref,
                    yl_s, y_s, sem, sem2):
    i = pl.program_id(0)
    nt = pl.num_programs(0)
    slot = i % 2

    def ycopy(s, src, dst, sm):
        return pltpu.make_async_copy(yb_ref.at[pl.ds(src, CCH), :], yl_s.at[s, pl.ds(dst, CCH), :],
                                     sm.at[s])

    def issue(tile, s):
        for e in range(N_EXPERTS):
            lo = pl.multiple_of(lo_sm[tile * N_EXPERTS + e], CCH)
            src = pl.multiple_of(dst_sm[tile * N_EXPERTS + e], SUB)
            ycopy(s, src, lo, sem).start()

        @pl.when(extra_sm[tile] > 0)
        def _():
            def ebody(e, carry):
                n = cnt_sm[tile * N_EXPERTS + e]
                lo = pl.multiple_of(lo_sm[tile * N_EXPERTS + e], CCH)
                src = pl.multiple_of(dst_sm[tile * N_EXPERTS + e], SUB)

                def cbody(c, cc):
                    ycopy(s, src + c * CCH, lo + c * CCH, sem2).start()
                    return cc

                return lax.fori_loop(1, (n + CCH - 1) // CCH, cbody, carry)

            lax.fori_loop(0, N_EXPERTS, ebody, 0)

    @pl.when(i == 0)
    def _():
        yl_s[...] = jnp.zeros(yl_s.shape, I32)
        issue(0, 0)

    @pl.when(i + 1 < nt)
    def _():
        issue(i + 1, 1 - slot)

    for _ in range(N_EXPERTS):
        ycopy(slot, 0, 0, sem).wait()

    def wbody(j, c):
        ycopy(slot, 0, 0, sem2).wait()
        return c

    lax.fori_loop(0, extra_sm[i], wbody, 0)

    rc = rcol_ref[...]
    pad0 = rc[:, R_PAD0:R_PAD0 + 1]
    pad1 = rc[:, R_PAD1:R_PAD1 + 1]
    lane = lax.broadcasted_iota(I32, (TM, CBLK), 1).astype(F32)

    def chunk(c):
        r = lane + float(c * CBLK)
        selm = (jnp.where(pad0 == r, 1.0, 0.0) + jnp.where(pad1 == r, 1.0, 0.0)).astype(BF16)
        lo_h, hi_h = _unpack_halves(yl_s[slot, c * CBLK:(c + 1) * CBLK, :])
        return _dot(selm, lo_h), _dot(selm, hi_h)

    half = D_MODEL // 2
    ylo = jnp.zeros((TM, half), F32)
    yhi = jnp.zeros((TM, half), F32)
    for c in range(LOC_COMMON // CBLK):
        dlo, dhi = chunk(c)
        ylo = ylo + dlo
        yhi = yhi + dhi
    y_s[:, :half] = ylo
    y_s[:, half:] = yhi
    for c in range(LOC_COMMON // CBLK, LOC_ROWS // CBLK):
        @pl.when(used_sm[i] > c * CBLK)
        def _(c=c):
            dlo, dhi = chunk(c)
            y_s[:, :half] += dlo
            y_s[:, half:] += dhi
    out_ref[...] = _rms(x2_ref[...] + y_s[...], g_ref[...])


def _combine(cnt_f, lo16_f, dst_f, used, extra, yb, x2, rcol, g_final):
    T = x2.shape[0]
    nt = T // TM
    grid_spec = pltpu.PrefetchScalarGridSpec(
        num_scalar_prefetch=5,
        grid=(nt,),
        in_specs=[
            pl.BlockSpec(memory_space=pl.ANY),
            pl.BlockSpec((TM, D_MODEL), lambda i, *_: (i, 0)),
            pl.BlockSpec((TM, LANES), lambda i, *_: (i, 0)),
            pl.BlockSpec((1, D_MODEL), lambda i, *_: (0, 0)),
        ],
        out_specs=pl.BlockSpec((TM, D_MODEL), lambda i, *_: (i, 0)),
        scratch_shapes=[
            pltpu.VMEM((2, LOC_ROWS, D_MODEL // 2), I32),
            pltpu.VMEM((TM, D_MODEL), F32),
            pltpu.SemaphoreType.DMA((2,)), pltpu.SemaphoreType.DMA((2,)),
        ],
    )
    return pl.pallas_call(
        _combine_kernel,
        grid_spec=grid_spec,
        out_shape=jax.ShapeDtypeStruct((T, D_MODEL), F32),
        compiler_params=pltpu.CompilerParams(
            dimension_semantics=("arbitrary",), vmem_limit_bytes=40 * 1024 * 1024),
        name="combine",
    )(cnt_f, lo16_f, dst_f, used, extra, yb, x2, rcol, g_final)


def _rope_tables(S):
    half = DA_HEAD_DIM // 2
    inv = (1.0 / (np.float32(ROPE_THETA) ** (np.arange(0, DA_HEAD_DIM, 2, dtype=np.float32)
                                             / np.float32(DA_HEAD_DIM)))).astype(np.float32)
    ang = (np.arange(S, dtype=np.float32)[:, None] * inv[None, :]).astype(np.float32)
    cos, sin = np.cos(ang), np.sin(ang)
    lane = np.arange(LANES)
    idx = lane % half
    lower = (lane % DA_HEAD_DIM) < half
    c = cos[:, idx]
    s = sin[:, idx]
    z = np.zeros_like(s)
    return (jnp.asarray(c, F32), jnp.asarray(np.where(lower[None, :], -s, z), F32),
            jnp.asarray(np.where(lower[None, :], z, s), F32))


def _tri_dot(a, b):
    return jnp.dot(a.astype(F32), b.astype(F32), precision=lax.Precision.HIGHEST).astype(I32)


def kernel(x, w_in, conv_w, conv_b, gate_b, lam_qk, subln_g, mhnorm_g, w_out, g_mix, g_ffn, w_grp,
           b_grp, w_erouter, b_erouter, w1, w3, w2, g_final):
    B, S, D = x.shape
    T = B * S
    nt = T // TM
    x2d = x.reshape(T, D)
    l = 0

    w_main = w_in[l, :, :N_MAIN].astype(BF16)
    wg8 = w_in[l, :, N_MAIN:]
    wg = jnp.pad(wg8, ((0, 0), (0, LANES - 8))).astype(BF16)
    wgt = wg8.T.astype(BF16)
    gb8 = gate_b[l].reshape(8)
    gb_col = jnp.pad(gb8, (0, LANES - 8)).reshape(1, LANES)
    gb_row = jnp.broadcast_to(gb8[:, None], (8, LANES))
    rope_c, rope_sa, rope_sb = _rope_tables(S)
    w_r = jnp.concatenate(
        [w_grp[l], w_erouter[l].transpose(1, 0, 2).reshape(D, N_EXPERTS)], axis=1)
    w_r = jnp.pad(w_r, ((0, 0), (0, LANES - w_r.shape[1])))
    wr_hi = w_r.astype(BF16)
    wr_t = jnp.concatenate([wr_hi, (w_r - wr_hi.astype(F32)).astype(BF16)], axis=1)
    b_r = jnp.pad(jnp.concatenate([b_grp[l], b_erouter[l].reshape(N_EXPERTS)]),
                  (0, LANES - N_GROUPS - N_EXPERTS))
    b_rt = jnp.broadcast_to(b_r[:, None], (LANES, TM))

    q1, q2, kr, va, qc, kc, vm, om, gcol, grow = _inproj(
        x2d, g_mix[l].reshape(1, D), w_main, wg, wgt, rope_c, rope_sa, rope_sb,
        conv_w[l], conv_b[l].reshape(1, -1), S)
    oa = _attention(q1, q2, kr, va, lam_qk[l], subln_g[l].reshape(1, LANES), B, S)
    hm = _mlstm(qc, kc, vm, om, gcol, grow, gb_col, gb_row, mhnorm_g[l].reshape(1, ML_WIDTH), B, S)
    x2, hn, rrow, rcol, cnt3 = _router(oa, hm, x2d, w_out[l].astype(BF16), g_ffn[l].reshape(1, D),
                                       wr_t, b_rt)

    cnt = cnt3[:, 0, EXP_ROW0:EXP_ROW0 + N_EXPERTS]
    c8 = (cnt + SUB - 1) // SUB * SUB
    c16 = jnp.maximum((cnt + CCH - 1) // CCH, 1) * CCH
    count8 = jnp.sum(c8, axis=0)
    pcount = (count8 + DCH + BLK - 1) // BLK * BLK
    up_e = np.triu(np.ones((N_EXPERTS, N_EXPERTS), np.float32), 1)
    lo_t = np.tril(np.ones((nt, nt), np.float32), -1)
    pstart = _tri_dot(pcount[None, :], up_e)[0]
    pend = pstart + pcount
    dst = pstart[None, :] + _tri_dot(lo_t, c8)
    lo8 = _tri_dot(c8, up_e)
    lo16 = _tri_dot(c16, up_e)
    nb = (2 * T + nt * N_EXPERTS * (SUB - 1) + N_EXPERTS * (DCH + BLK - 1)) // BLK + 1
    blk_row = jnp.arange(nb, dtype=I32) * BLK
    blk_e = jnp.minimum(jnp.sum((pend[None, :] <= blk_row[:, None]).astype(I32), axis=1),
                        N_EXPERTS - 1).astype(I32)
    nused = (pend[-1] // BLK).astype(I32).reshape(1)
    cnt_f = cnt.reshape(-1).astype(I32)
    dst_f = dst.reshape(-1).astype(I32)

    gap_start = jnp.concatenate([pstart + count8, pend[-1:]]).astype(I32)
    gap_rows = jnp.concatenate([pcount - count8, nb * BLK - pend[-1:]]).astype(I32)
    extra = jnp.sum(jnp.maximum((cnt + DCH - 1) // DCH - 1, 0), axis=1).astype(I32)
    xb = _dispatch(cnt_f, lo8.reshape(-1), dst_f, extra, gap_start, gap_rows, hn, rrow, rcol, nb)
    yb = _experts(blk_e, nused, xb, w1[l], w3[l], w2[l], nb)
    used = jnp.sum(c16, axis=1).astype(I32)
    out = _combine(cnt_f, lo16.reshape(-1), dst_f, used, extra, yb, x2, rcol, g_final.reshape(1, D))
    return out.reshape(B, S, D)
```

```python
import functools
import math

import numpy as np
import jax
import jax.numpy as jnp
from jax import lax
from jax.experimental import pallas as pl
from jax.experimental.pallas import tpu as pltpu

F32 = jnp.float32
BF16 = jnp.bfloat16
I32 = jnp.int32

D_MODEL = 1024
DA_HEADS = 4
DA_HEAD_DIM = 64
DA_V_DIM = 128
DA_WIDTH = 512
ML_HEADS = 4
ML_WIDTH = 512
ML_HEAD_DIM = 128
ML_CHUNK = 128
ML_GROUP = 4
CONV_K = 4
ROPE_THETA = 10000.0
RMS_EPS = 1e-6
N_GROUPS = 4
EXPERTS_PER_GROUP = 8
N_EXPERTS = 32
D_EXPERT = 512
LAMBDA_INIT = 0.8 - 0.6 * math.exp(-0.3 * 0)

LANES = 128
SUB = 8
N_MAIN = 7 * 512
NEG = -1e30

TM_PROJ = 1024
TQ = 256
TK = 512
VT_ROWS = DA_V_DIM + 16
TM = 256
BLK = 512
ESUB = 256
CBLK = 256
DCH = 24
CCH = 24
DLOC_ROWS = 768
LOC_COMMON = N_EXPERTS * CCH
LOC_ROWS = 2 * TM + N_EXPERTS * CCH
XW = D_MODEL // 2 + LANES
EXP_ROW0 = 4
RT_ROWS = 40
RT_GROUP = 4
R_E0, R_E1, R_POS0, R_POS1, R_PAD0, R_PAD1, R_W0, R_W1 = range(8)


def _nt_dot(a, b):
    return lax.dot_general(a, b, (((1,), (1,)), ((), ())), preferred_element_type=F32)


def _tn_dot(a, b):
    return lax.dot_general(a, b, (((0,), (0,)), ((), ())), preferred_element_type=F32)


def _dot(a, b):
    return jnp.dot(a, b, preferred_element_type=F32)


def _split3(x):
    hi = x.astype(BF16)
    r = x - hi.astype(F32)
    mid = r.astype(BF16)
    lo = (r - mid.astype(F32)).astype(BF16)
    return hi, mid, lo


def _rms(x, g):
    return x * lax.rsqrt(jnp.mean(x * x, axis=-1, keepdims=True) + RMS_EPS) * g


def _inproj_kernel(x_ref, g_ref, w_ref, wg_ref, wgt_ref, c_ref, sa_ref, sb_ref, cw_ref, cb_ref,
                   q1_ref, q2_ref, kr_ref, va_ref, qc_ref, kc_ref, vm_ref, om_ref, gcol_ref, grow_ref,
                   h_s, cq_s, ck_s, *, tiles_per_seq):
    tm = x_ref.shape[0]
    cw = 512
    first = (pl.program_id(0) % tiles_per_seq) == 0

    @pl.when(first)
    def _():
        cq_s[0:8, :] = jnp.zeros((8, cw), F32)
        ck_s[0:8, :] = jnp.zeros((8, cw), F32)

    @pl.when(jnp.logical_not(first))
    def _():
        cq_s[0:8, :] = cq_s[tm:tm + 8, :]
        ck_s[0:8, :] = ck_s[tm:tm + 8, :]

    h_s[...] = _rms(x_ref[...], g_ref[...]).astype(BF16)
    nhalf = 2 if tm >= 512 else 1
    th = tm // nhalf
    halves = [slice(p * th, (p + 1) * th) for p in range(nhalf)]

    def cols(c, hv):
        return _dot(h_s[hv, :], w_ref[:, c * cw:(c + 1) * cw])

    rb = min(128, th)
    lane = lax.broadcasted_iota(I32, (rb, LANES), 1)
    qscale = DA_HEAD_DIM ** -0.5 * math.log2(math.e)

    def rope(x, rows):
        return (x * c_ref[rows, :] + pltpu.roll(x, 96, 1) * sa_ref[rows, :]
                + pltpu.roll(x, 32, 1) * sb_ref[rows, :])

    def blocks(hv):
        for r in range(th // rb):
            for hh in range(cw // LANES):
                yield (slice(hv.start + r * rb, hv.start + (r + 1) * rb), slice(r * rb, (r + 1) * rb),
                       slice(hh * LANES, (hh + 1) * LANES))

    def rope_q(qa, hv):
        for rows, loc, hs in blocks(hv):
            qr = rope(qa[loc, hs], rows) * qscale
            q1_ref[rows, hs] = jnp.where(lane < DA_HEAD_DIM, qr, 0.0).astype(BF16)
            q2_ref[rows, hs] = jnp.where(lane >= DA_HEAD_DIM, qr, 0.0).astype(BF16)

    def rope_k(ka, hv):
        for rows, loc, hs in blocks(hv):
            kr_ref[rows, hs] = rope(ka[loc, hs], rows).astype(BF16)

    def conv_silu(src, c0, hv, out_ref, scale):
        for rows, _, hs in blocks(hv):
            cs = slice(c0 + hs.start, c0 + hs.stop)
            w = cw_ref[:, cs]
            y = cb_ref[:, cs] + src[8 + rows.start:8 + rows.stop, hs] * w[3:4]
            for s in (1, 2, 3):
                y = y + src[8 + rows.start - s:8 + rows.stop - s, hs] * w[3 - s:4 - s]
            y = y * jax.nn.sigmoid(y)
            out_ref[rows, hs] = (y if scale is None else y * scale).astype(BF16)

    def to_conv(c, dst, hv):
        dst[8 + hv.start:8 + hv.stop, :] = cols(c, hv)

    qa = [cols(0, hv) for hv in halves]
    ka = []
    for p, hv in enumerate(halves):
        ka.append(cols(1, hv))
        rope_q(qa[p], hv)
    for p, hv in enumerate(halves):
        va_ref[hv, :] = cols(2, hv).astype(BF16)
        rope_k(ka[p], hv)
    for hv in halves:
        to_conv(3, cq_s, hv)
    for hv in halves:
        to_conv(4, ck_s, hv)
        conv_silu(cq_s, 0, hv, qc_ref, None)
    for hv in halves:
        vm_ref[hv, :] = cols(5, hv).astype(BF16)
        conv_silu(ck_s, ML_WIDTH, hv, kc_ref, ML_HEAD_DIM ** -0.5)
    om_ref[...] = _dot(h_s[...], w_ref[:, 6 * cw:7 * cw]).astype(BF16)
    gcol_ref[...] = _dot(h_s[...], wg_ref[...])
    grow_ref[...] = _nt_dot(wgt_ref[...], h_s[...])


def _inproj(x2d, g_mix, w_main, wg, wgt, rope_c, rope_sa, rope_sb, conv_w, conv_b, S):
    T = x2d.shape[0]
    tm = min(TM_PROJ, S)
    assert S % tm == 0
    tps = S // tm
    full = lambda shape: pl.BlockSpec(shape, lambda i: (0, 0))
    rope_spec = pl.BlockSpec((tm, LANES), lambda i: (i % tps, 0))
    col_spec = pl.BlockSpec((tm, 512), lambda i: (i, 0))
    return pl.pallas_call(
        functools.partial(_inproj_kernel, tiles_per_seq=tps),
        grid=(T // tm,),
        in_specs=[
            pl.BlockSpec((tm, D_MODEL), lambda i: (i, 0)),
            full((1, D_MODEL)), full((D_MODEL, N_MAIN)), full((D_MODEL, LANES)), full((8, D_MODEL)),
            rope_spec, rope_spec, rope_spec,
            full((CONV_K, 2 * ML_WIDTH)), full((1, 2 * ML_WIDTH)),
        ],
        out_specs=[col_spec] * 8 + [
            pl.BlockSpec((tm, LANES), lambda i: (i, 0)),
            pl.BlockSpec((8, tm), lambda i: (0, i)),
        ],
        out_shape=[jax.ShapeDtypeStruct((T, 512), BF16)] * 8 + [
            jax.ShapeDtypeStruct((T, LANES), F32),
            jax.ShapeDtypeStruct((8, T), F32),
        ],
        scratch_shapes=[pltpu.VMEM((tm, D_MODEL), BF16),
                        pltpu.VMEM((tm + 8, ML_WIDTH), F32), pltpu.VMEM((tm + 8, ML_WIDTH), F32)],
        compiler_params=pltpu.CompilerParams(
            dimension_semantics=("arbitrary",), vmem_limit_bytes=56 * 1024 * 1024),
        name="inproj",
    )(x2d, g_mix, w_main, wg, wgt, rope_c, rope_sa, rope_sb, conv_w, conv_b)


def _attn_kernel(q1_s, q2_s, k_s, v_ref, lam_ref, g_ref, o_ref, vt_s, acc_s, st_s):
    S = v_ref.shape[0]

    def prep(r, carry):
        rows = pl.ds(pl.multiple_of(r * TQ, TQ), TQ)
        for h in range(DA_HEADS):
            hs = slice(h * LANES, (h + 1) * LANES)
            vt_s[h, :DA_V_DIM, rows] = v_ref[rows, hs].astype(F32).T.astype(BF16)
        return carry

    lax.fori_loop(0, S // TQ, prep, 0)
    vt_s[:, DA_V_DIM:, :] = jnp.ones((DA_HEADS, VT_ROWS - DA_V_DIM, S), BF16)

    lq = lam_ref[...]
    lam = (jnp.exp(jnp.sum(lq[0:1] * lq[1:2], axis=-1, keepdims=True))
           - jnp.exp(jnp.sum(lq[2:3] * lq[3:4], axis=-1, keepdims=True)) + LAMBDA_INIT)

    key = lax.broadcasted_iota(I32, (TK, 2 * TQ), 0)
    qry = lax.broadcasted_iota(I32, (TK, 2 * TQ), 1)
    kq = key - jnp.where(qry >= TQ, qry - TQ, qry)
    qpk = TK // TQ

    def kvstep(qo, j, ms, causal, slot):
        ko = pl.multiple_of(j * TK, TK)
        out = []

        def scores(h):
            hs = slice(h * LANES, (h + 1) * LANES)
            qq = jnp.concatenate([q1_s[pl.ds(qo, TQ), hs], q2_s[pl.ds(qo, TQ), hs]], axis=0)
            st = _nt_dot(k_s[pl.ds(ko, TK), hs], qq)
            st_s[h] = st if causal is None else jnp.where(causal, st, NEG)

        scores(0)
        for h in range(DA_HEADS):
            if h + 1 < DA_HEADS:
                scores(h + 1)
            st = st_s[h]
            m_new = jnp.maximum(ms[h], jnp.max(st, axis=0, keepdims=True))
            alpha = jnp.exp2(ms[h] - m_new)
            p = jnp.exp2(st - m_new).astype(BF16)
            acc_s[slot, h] = alpha * acc_s[slot, h] + _dot(vt_s[h, :, pl.ds(ko, TK)], p)
            out.append(m_new)
        return tuple(out)

    def diagonal(i, slot):
        qo = pl.multiple_of(i * TQ, TQ)
        nfull = i // qpk
        acc_s[slot] = jnp.zeros(acc_s.shape[1:], F32)
        m0 = tuple(jnp.full((1, 2 * TQ), NEG, F32) for _ in range(DA_HEADS))
        return kvstep(qo, nfull, m0, kq <= (i - nfull * qpk) * TQ, slot)

    def finalize(i, slot):
        qo = pl.multiple_of(i * TQ, TQ)
        for h in range(DA_HEADS):
            hs = slice(h * LANES, (h + 1) * LANES)
            acc = acc_s[slot, h]
            on = acc[:DA_V_DIM] * (1.0 / acc[DA_V_DIM:DA_V_DIM + 1])
            ot = on[:, :TQ] - lam * on[:, TQ:]
            o = _rms(ot.T, g_ref[...]) * (1.0 - LAMBDA_INIT)
            o_ref[pl.ds(qo, TQ), hs] = o.astype(BF16)

    def qtile(i, carry):
        slot = i % 2
        finalize(i - 1, 1 - slot)
        ms = diagonal(i, slot)
        qo = pl.multiple_of(i * TQ, TQ)
        lax.fori_loop(0, i // qpk, lambda j, m: kvstep(qo, j, m, None, slot), ms)
        return carry

    diagonal(0, 0)
    nq = S // TQ
    lax.fori_loop(1, nq, qtile, 0)
    finalize(nq - 1, (nq - 1) % 2)


def _attention(q1, q2, kr, va, lam_qk, subln_g, B, S):
    T = B * S
    full = lambda shape: pl.BlockSpec(shape, lambda b: (0, 0))
    seq = pl.BlockSpec((S, DA_WIDTH), lambda b: (b, 0))
    return pl.pallas_call(
        _attn_kernel,
        grid=(B,),
        in_specs=[seq, seq, seq, seq, full((4, DA_HEAD_DIM)), full((1, LANES))],
        out_specs=seq,
        out_shape=jax.ShapeDtypeStruct((T, DA_WIDTH), BF16),
        scratch_shapes=[
            pltpu.VMEM((DA_HEADS, VT_ROWS, S), BF16),
            pltpu.VMEM((2, DA_HEADS, VT_ROWS, 2 * TQ), F32),
            pltpu.VMEM((DA_HEADS, TK, 2 * TQ), F32),
        ],
        compiler_params=pltpu.CompilerParams(
            dimension_semantics=("arbitrary",), vmem_limit_bytes=48 * 1024 * 1024),
        name="attn",
    )(q1, q2, kr, va, lam_qk, subln_g)


def _log_sigmoid(x):
    return jnp.minimum(x, 0.0) - jnp.log(1.0 + jnp.exp(-jnp.abs(x)))


def _mlstm_kernel(qc_s, kc_s, v_ref, o_ref, gcol_ref, grow_ref, gbc_ref, gbr_ref,
                  mg_ref, out_ref, vt_s, ct_s, m_s):
    S = v_ref.shape[0]
    L = ML_CHUNK
    nc = S // L

    def transpose_v(c, carry):
        ro = pl.multiple_of(c * L, L)
        for h in range(ML_HEADS):
            hs = slice(h * ML_HEAD_DIM, (h + 1) * ML_HEAD_DIM)
            vt_s[h, :ML_HEAD_DIM, pl.ds(ro, L)] = v_ref[pl.ds(ro, L), hs].astype(F32).T.astype(BF16)
        return carry

    lax.fori_loop(0, nc, transpose_v, 0)
    vt_s[:, ML_HEAD_DIM:, :] = jnp.ones((ML_HEADS, ML_HEAD_DIM, S), BF16)

    ct_s[...] = jnp.zeros(ct_s.shape, F32)
    m_s[...] = jnp.zeros(m_s.shape, F32)

    ri = lax.broadcasted_iota(I32, (L, L), 0)
    ci = lax.broadcasted_iota(I32, (L, L), 1)
    causal_t = ri <= ci
    tril = jnp.where(ci <= ri, 1.0, 0.0).astype(BF16)
    triu = jnp.where(causal_t, 1.0, 0.0).astype(BF16)

    heads = [slice(h * ML_HEAD_DIM, (h + 1) * ML_HEAD_DIM) for h in range(ML_HEADS)]

    def gates(c):
        ro = pl.multiple_of(c * L, L)
        gc = gcol_ref[pl.ds(ro, L), :] + gbc_ref[...]
        gr = grow_ref[:, pl.ds(ro, L)] + gbr_ref[...]
        b_c = sum(_dot(tril, p) for p in _split3(_log_sigmoid(gc)))
        b_r = sum(_dot(p, triu) for p in _split3(_log_sigmoid(gr)))
        return ro, gc, gr, b_c, b_r

    def read_state(g):
        ro = g[0]
        ks, vts, kq, cq, ms = [], [], [], [], []
        for h, hs in enumerate(heads):
            q = qc_s[pl.ds(ro, L), hs]
            k = kc_s[pl.ds(ro, L), hs]
            ks.append(k)
            vts.append(vt_s[h, :, pl.ds(ro, L)])
            kq.append(_nt_dot(k, q))
            cq.append(_nt_dot(ct_s[h].astype(BF16), q))
            ms.append(m_s[h:h + 1, 0:1])
        return ks, vts, kq, cq, ms

    def update_state(g, d):
        _, _, gr, _, b_r = g
        ks, vts, _, _, ms = d
        for h in range(ML_HEADS):
            br = b_r[4 + h:5 + h, :]
            bl = br[:, L - 1:L]
            dec = bl - br + gr[h:h + 1, :]
            m_new = jnp.maximum(bl + ms[h], jnp.max(dec, axis=1, keepdims=True))
            ws = jnp.exp(dec - m_new)
            sc = jnp.exp(bl + ms[h] - m_new)
            vw = (vts[h].astype(F32) * ws).astype(BF16)
            ct_s[h] = sc * ct_s[h] + _dot(vw, ks[h])
            m_s[h:h + 1, :] = jnp.broadcast_to(m_new, (1, LANES))

    def outputs(g, d):
        ro, gc, _, b_c, b_r = g
        _, vts, kq, cq, ms = d
        for h, hs in enumerate(heads):
            br = b_r[4 + h:5 + h, :]
            a_col = b_c[:, 4 + h:5 + h] - gc[:, h:h + 1]
            dm = jnp.where(causal_t, br - a_col, NEG)
            inter = br + ms[h]
            m_row = jnp.maximum(inter, jnp.max(dm, axis=0, keepdims=True))
            sm = jnp.exp(dm - m_row) * kq[h]
            sc_in = jnp.exp(inter - m_row)
            a = _dot(vts[h], sm.astype(BF16)) + sc_in * cq[h]
            den = a[ML_HEAD_DIM:ML_HEAD_DIM + 1, :]
            hh = (a[:ML_HEAD_DIM] / jnp.maximum(jnp.abs(den), jnp.exp(-m_row))).T
            hg = hh * jax.nn.sigmoid(o_ref[pl.ds(ro, L), hs].astype(F32))
            out_ref[pl.ds(ro, L), hs] = _rms(hg, mg_ref[:, hs]).astype(BF16)

    def chunk_group(cg, carry):
        gs = [gates(ML_GROUP * cg + j) for j in range(ML_GROUP)]
        ds = []
        for g in gs:
            ds.append(read_state(g))
            update_state(g, ds[-1])
        for g, d in zip(gs, ds):
            outputs(g, d)
        return carry

    lax.fori_loop(0, nc // ML_GROUP, chunk_group, 0)


def _mlstm(qc, kc, vm, om, gcol, grow, gb_col, gb_row, mh_g, B, S):
    T = B * S
    full = lambda shape: pl.BlockSpec(shape, lambda b: (0, 0))
    seq = pl.BlockSpec((S, ML_WIDTH), lambda b: (b, 0))
    return pl.pallas_call(
        _mlstm_kernel,
        grid=(B,),
        in_specs=[
            seq, seq, seq, seq,
            pl.BlockSpec((S, LANES), lambda b: (b, 0)),
            pl.BlockSpec((8, S), lambda b: (0, b)),
            full((1, LANES)), full((8, LANES)), full((1, ML_WIDTH)),
        ],
        out_specs=seq,
        out_shape=jax.ShapeDtypeStruct((T, ML_WIDTH), BF16),
        scratch_shapes=[
            pltpu.VMEM((ML_HEADS, 2 * ML_HEAD_DIM, S), BF16),
            pltpu.VMEM((ML_HEADS, 2 * ML_HEAD_DIM, ML_HEAD_DIM), F32),
            pltpu.VMEM((8, LANES), F32),
        ],
        compiler_params=pltpu.CompilerParams(
            dimension_semantics=("arbitrary",), vmem_limit_bytes=56 * 1024 * 1024),
        name="mlstm",
    )(qc, kc, vm, om, gcol, grow, gb_col, gb_row, mh_g)


def _router_kernel(oa_ref, hm_ref, x_ref, wo_ref, g_ref, wr_ref, brt_ref,
                   x2_ref, hn_ref, rrow_ref, rcol_ref, cnt_ref):
    def project(s):
        rs = slice(s * TM, (s + 1) * TM)
        mixo = _dot(oa_ref[rs, :], wo_ref[:DA_WIDTH, :]) + _dot(hm_ref[rs, :], wo_ref[DA_WIDTH:, :])
        x2 = x_ref[rs, :] + mixo
        x2_ref[rs, :] = x2
        hn = _rms(x2, g_ref[...])
        hn_hi = hn.astype(BF16)
        hn_ref[rs, :] = hn_hi
        hn_lo = (hn - hn_hi.astype(F32)).astype(BF16)
        a = _dot(hn_hi, wr_ref[...])
        b = _dot(hn_lo, wr_ref[:, :LANES])
        return (a[:, :LANES] + a[:, LANES:] + b).T[:RT_ROWS] + brt_ref[:RT_ROWS, :]

    lts = [project(0)]
    for s in range(RT_GROUP):
        if s + 1 < RT_GROUP:
            lts.append(project(s + 1))
        _route(s, lts[s], rrow_ref, rcol_ref, cnt_ref)


def _route(s, lt, rrow_ref, rcol_ref, cnt_ref):
    sub = lax.broadcasted_iota(I32, (RT_ROWS, TM), 0)
    sub_f = sub.astype(F32)

    def cmax(v):
        return jnp.max(v, axis=0, keepdims=True)

    def first_idx(mask):
        return jnp.min(jnp.where(mask, sub_f, 1e6), axis=0, keepdims=True).astype(I32)

    gl = jnp.where(sub < N_GROUPS, lt, NEG)
    gmax = cmax(gl)
    gsel = first_idx(gl == gmax)
    g_w = 1.0 / jnp.sum(jnp.exp(gl - gmax), axis=0, keepdims=True)
    elo = EXP_ROW0 + gsel * EXPERTS_PER_GROUP
    el = jnp.where((sub >= elo) & (sub < elo + EXPERTS_PER_GROUP), lt, NEG)
    v1 = cmax(el)
    i1 = first_idx(el == v1)
    el2 = jnp.where(sub == i1, NEG, el)
    v2 = cmax(el2)
    i2 = first_idx(el2 == v2)
    t = jnp.exp(v2 - v1)
    w0 = g_w / (1.0 + t)
    w1 = g_w * t / (1.0 + t)

    oh0 = jnp.where(sub == i1, 1.0, 0.0)
    oh1 = jnp.where(sub == i2, 1.0, 0.0)
    mh = oh0 + oh1
    r_i = lax.broadcasted_iota(I32, (TM, TM), 0)
    c_i = lax.broadcasted_iota(I32, (TM, TM), 1)
    before = jnp.where(r_i < c_i, 1.0, 0.0).astype(BF16)
    mh_f = jnp.concatenate([mh, jnp.zeros((LANES - RT_ROWS, TM), F32)], axis=0)
    mh_b = mh_f.astype(BF16)
    pre = _dot(mh_b, before)[:RT_ROWS]
    cnt = jnp.sum(mh_f, axis=1, keepdims=True)
    cnt8 = jnp.floor((cnt + (SUB - 1)) * (1.0 / SUB)) * SUB
    erow = lax.broadcasted_iota(I32, (LANES, 1), 0)
    is_exp = (erow >= EXP_ROW0) & (erow < EXP_ROW0 + N_EXPERTS)
    cnt16 = jnp.where(is_exp, jnp.maximum(jnp.floor((cnt + (CCH - 1)) * (1.0 / CCH)), 1.0) * CCH, 0.0)
    e_r = lax.broadcasted_iota(I32, (LANES, LANES), 0)
    e_c = lax.broadcasted_iota(I32, (LANES, LANES), 1)
    below = jnp.where(e_c < e_r, 1.0, 0.0).astype(BF16)
    lo8 = _dot(below, jnp.broadcast_to(cnt8, (LANES, LANES)).astype(BF16))[:RT_ROWS, 0:1]
    lo16 = _dot(below, jnp.broadcast_to(cnt16, (LANES, LANES)).astype(BF16))[:RT_ROWS, 0:1]

    def csum(v):
        return jnp.sum(v, axis=0, keepdims=True)

    rows = [None] * 8
    rows[R_E0] = (i1 - EXP_ROW0).astype(F32)
    rows[R_E1] = (i2 - EXP_ROW0).astype(F32)
    rows[R_POS0] = csum(oh0 * (pre + lo8))
    rows[R_POS1] = csum(oh1 * (pre + lo8))
    rows[R_PAD0] = csum(oh0 * (pre + lo16))
    rows[R_PAD1] = csum(oh1 * (pre + lo16))
    rows[R_W0] = w0
    rows[R_W1] = w1
    sub128 = lax.broadcasted_iota(I32, (LANES, TM), 0)
    r128 = jnp.zeros((LANES, TM), F32)
    for j, v in enumerate(rows):
        r128 = jnp.where(sub128 == j, v, r128)
    rrow_ref[:, s * TM:(s + 1) * TM] = r128[:8]
    rcol_ref[s * TM:(s + 1) * TM, :] = r128.T
    cnt_ref[s] = _nt_dot(jnp.ones((8, TM), BF16), mh_b)[0:1].astype(I32)


def _router(oa, hm, x2d, w_out, g_ffn, wr_t, b_rt):
    T = x2d.shape[0]
    nt = T // TM
    full = lambda shape: pl.BlockSpec(shape, lambda i: (0, 0))
    tg = RT_GROUP * TM
    return pl.pallas_call(
        _router_kernel,
        grid=(T // tg,),
        in_specs=[
            pl.BlockSpec((tg, DA_WIDTH), lambda i: (i, 0)),
            pl.BlockSpec((tg, ML_WIDTH), lambda i: (i, 0)),
            pl.BlockSpec((tg, D_MODEL), lambda i: (i, 0)),
            full((D_MODEL, D_MODEL)), full((1, D_MODEL)),
            full((D_MODEL, 2 * LANES)), full((LANES, TM)),
        ],
        out_specs=[
            pl.BlockSpec((tg, D_MODEL), lambda i: (i, 0)),
            pl.BlockSpec((tg, D_MODEL), lambda i: (i, 0)),
            pl.BlockSpec((8, tg), lambda i: (0, i)),
            pl.BlockSpec((tg, LANES), lambda i: (i, 0)),
            pl.BlockSpec((RT_GROUP, 1, LANES), lambda i: (i, 0, 0)),
        ],
        out_shape=[
            jax.ShapeDtypeStruct((T, D_MODEL), F32),
            jax.ShapeDtypeStruct((T, D_MODEL), BF16),
            jax.ShapeDtypeStruct((8, T), F32),
            jax.ShapeDtypeStruct((T, LANES), F32),
            jax.ShapeDtypeStruct((nt, 1, LANES), I32),
        ],
        compiler_params=pltpu.CompilerParams(
            dimension_semantics=("arbitrary",), vmem_limit_bytes=40 * 1024 * 1024),
        name="router",
    )(oa, hm, x2d, w_out, g_ffn, wr_t, b_rt)


def _pack_halves(x):
    lo = lax.bitcast_convert_type(x[:, :D_MODEL // 2], I32)
    hi = lax.bitcast_convert_type(x[:, D_MODEL // 2:], I32)
    return lax.shift_right_logical(lo, 16) | (hi & jnp.int32(-65536))


def _unpack_halves(w):
    lo = lax.bitcast_convert_type(lax.shift_left(w, 16), F32)
    hi = lax.bitcast_convert_type(w & jnp.int32(-65536), F32)
    return lo.astype(BF16), hi.astype(BF16)


def _dispatch_kernel(cnt_sm, lo_sm, dst_sm, extra_sm, gs_sm, gn_sm, hn_ref, rrow_ref, rcol_ref, xb_ref,
                     xs_s, z_s, semx, semx2, semz):
    i = pl.program_id(0)
    nt = pl.num_programs(0)
    slot = i % 2
    half = D_MODEL // 2

    def zcopy(rows, dst):
        return pltpu.make_async_copy(z_s.at[pl.ds(0, rows), :], xb_ref.at[pl.ds(dst, rows), :], semz)

    def gap_fill(wait):
        def ebody(e, carry):
            start = pl.multiple_of(gs_sm[e], SUB)
            left = gn_sm[e]
            for rows in (BLK, DCH, SUB):
                n = left // rows

                def body(c, cc, rows=rows, start=start):
                    cp = zcopy(rows, pl.multiple_of(start + c * rows, SUB))
                    cp.wait() if wait else cp.start()
                    return cc

                lax.fori_loop(0, n, body, 0)
                start = start + n * rows
                left = left - n * rows
            return carry

        lax.fori_loop(0, N_EXPERTS + 1, ebody, 0)

    @pl.when(i == 0)
    def _():
        for s in range(2):
            xs_s[s, DLOC_ROWS:, :] = jnp.zeros((DCH, XW), I32)
        z_s[...] = jnp.zeros(z_s.shape, I32)
        gap_fill(False)
        gap_fill(True)

    rr = rrow_ref[...]
    rc = rcol_ref[...]
    r_i = lax.broadcasted_iota(I32, (DLOC_ROWS, TM), 0).astype(F32)
    perm0 = jnp.where(r_i == rr[R_POS0:R_POS0 + 1, :], 1.0, 0.0).astype(BF16)
    perm1 = jnp.where(r_i == rr[R_POS1:R_POS1 + 1, :], 1.0, 0.0).astype(BF16)
    xs = _dot(perm0 + perm1, hn_ref[...])
    xs_s[slot, :DLOC_ROWS, :half] = _pack_halves(xs)
    lane = lax.broadcasted_iota(I32, (TM, LANES), 1)

    def parts(col):
        hi, mid, lo = (p.astype(F32) for p in _split3(jnp.broadcast_to(rc[:, col:col + 1], (TM, LANES))))
        sel = jnp.where(lane == 0, hi, jnp.where(lane == 1, mid, jnp.where(lane == 2, lo, 0.0)))
        return sel.astype(BF16)

    wparts = _dot(perm0, parts(R_W0)) + _dot(perm1, parts(R_W1))
    l_r = lax.broadcasted_iota(I32, (LANES, LANES), 0)
    ones3 = jnp.where(l_r < 3, 1.0, 0.0).astype(BF16)
    wsort = _dot(wparts.astype(BF16), ones3)
    xs_s[slot, :DLOC_ROWS, half:] = lax.bitcast_convert_type(wsort, I32)

    def xcopy(s, src, dst, sem):
        return pltpu.make_async_copy(xs_s.at[s, pl.ds(src, DCH), :], xb_ref.at[pl.ds(dst, DCH), :], sem)

    def wait_all(n_extra):
        for _ in range(N_EXPERTS):
            xcopy(0, 0, 0, semx).wait()

        def wbody(j, c):
            xcopy(0, 0, 0, semx2).wait()
            return c
        lax.fori_loop(0, n_extra, wbody, 0)

    @pl.when(i > 0)
    def _():
        wait_all(extra_sm[jnp.maximum(i - 1, 0)])

    for e in range(N_EXPERTS):
        lo = pl.multiple_of(lo_sm[i * N_EXPERTS + e], SUB)
        dst = pl.multiple_of(dst_sm[i * N_EXPERTS + e], SUB)
        xcopy(slot, lo, dst, semx).start()

    @pl.when(extra_sm[i] > 0)
    def _():
        def ebody(e, carry):
            n = cnt_sm[i * N_EXPERTS + e]
            lo = pl.multiple_of(lo_sm[i * N_EXPERTS + e], SUB)
            dst = pl.multiple_of(dst_sm[i * N_EXPERTS + e], SUB)

            def cbody(c, cc):
                xcopy(slot, lo + c * DCH, dst + c * DCH, semx2).start()
                return cc

            return lax.fori_loop(1, (n + DCH - 1) // DCH, cbody, carry)

        lax.fori_loop(0, N_EXPERTS, ebody, 0)

    @pl.when(i == nt - 1)
    def _():
        wait_all(extra_sm[i])


def _dispatch(cnt_f, lo_f, dst_f, extra, gap_start, gap_rows, hn, rrow, rcol, nb):
    T = hn.shape[0]
    nt = T // TM
    grid_spec = pltpu.PrefetchScalarGridSpec(
        num_scalar_prefetch=6,
        grid=(nt,),
        in_specs=[
            pl.BlockSpec((TM, D_MODEL), lambda i, *_: (i, 0)),
            pl.BlockSpec((8, TM), lambda i, *_: (0, i)),
            pl.BlockSpec((TM, LANES), lambda i, *_: (i, 0)),
        ],
        out_specs=pl.BlockSpec(memory_space=pl.ANY),
        scratch_shapes=[
            pltpu.VMEM((2, DLOC_ROWS + DCH, XW), I32),
            pltpu.VMEM((BLK, XW), I32),
            pltpu.SemaphoreType.DMA(()), pltpu.SemaphoreType.DMA(()), pltpu.SemaphoreType.DMA(()),
        ],
    )
    return pl.pallas_call(
        _dispatch_kernel,
        grid_spec=grid_spec,
        out_shape=jax.ShapeDtypeStruct((nb * BLK, XW), I32),
        compiler_params=pltpu.CompilerParams(
            dimension_semantics=("arbitrary",), vmem_limit_bytes=40 * 1024 * 1024),
        name="dispatch",
    )(cnt_f, lo_f, dst_f, extra, gap_start, gap_rows, hn, rrow, rcol)


def _expert_kernel(be_sm, nu_sm, xb_ref, w1_ref, w3_ref, w2_ref, yb_ref, w1_s, w3_s, w2_s):
    p = pl.program_id(0)
    used = p < nu_sm[0]

    @pl.when(used & ((p == 0) | (be_sm[p] != be_sm[jnp.maximum(p - 1, 0)])))
    def _():
        w1_s[...] = w1_ref[...].astype(BF16)
        w3_s[...] = w3_ref[...].astype(BF16)
        w2_s[...] = w2_ref[...].astype(BF16)

    @pl.when(used)
    def _():
        half = D_MODEL // 2
        for r in range(BLK // ESUB):
            rs = slice(r * ESUB, (r + 1) * ESUB)
            xlo, xhi = _unpack_halves(xb_ref[rs, :half])
            h1 = _dot(xlo, w1_s[:half, :]) + _dot(xhi, w1_s[half:, :])
            h3 = _dot(xlo, w3_s[:half, :]) + _dot(xhi, w3_s[half:, :])
            wrep = lax.bitcast_convert_type(xb_ref[rs, half:], F32)
            wfull = jnp.concatenate([wrep] * (D_EXPERT // LANES), axis=1)
            hdn = (h1 * jax.nn.sigmoid(h1) * h3 * wfull).astype(BF16)
            y = _dot(hdn, w2_s[...]).astype(BF16).astype(F32)
            yb_ref[rs, :] = _pack_halves(y)

    @pl.when(jnp.logical_not(used))
    def _():
        yb_ref[...] = jnp.zeros(yb_ref.shape, I32)


def _experts(blk_e, nused, xb, w1, w3, w2, nb):
    def rows(p, be, nu):
        return (jnp.minimum(p, nu[0] - 1), 0)

    def wsel(p, be, nu):
        return (be[jnp.minimum(p, nu[0] - 1)], 0, 0)

    grid_spec = pltpu.PrefetchScalarGridSpec(
        num_scalar_prefetch=2,
        grid=(nb,),
        in_specs=[
            pl.BlockSpec((BLK, XW), rows),
            pl.BlockSpec((None, D_MODEL, D_EXPERT), wsel),
            pl.BlockSpec((None, D_MODEL, D_EXPERT), wsel),
            pl.BlockSpec((None, D_EXPERT, D_MODEL), wsel),
        ],
        out_specs=pl.BlockSpec((BLK, D_MODEL // 2), lambda p, be, nu: (p, 0)),
        scratch_shapes=[
            pltpu.VMEM((D_MODEL, D_EXPERT), BF16), pltpu.VMEM((D_MODEL, D_EXPERT), BF16),
            pltpu.VMEM((D_EXPERT, D_MODEL), BF16),
        ],
    )
    return pl.pallas_call(
        _expert_kernel,
        grid_spec=grid_spec,
        out_shape=jax.ShapeDtypeStruct((nb * BLK, D_MODEL // 2), I32),
        compiler_params=pltpu.CompilerParams(
            dimension_semantics=("arbitrary",), vmem_limit_bytes=48 * 1024 * 1024),
        name="experts",
    )(blk_e, nused, xb, w1, w3, w2)


def _combine_kernel(cnt_sm, lo_sm, dst_sm, used_sm, extra_sm, yb_ref, x2_ref, rcol_ref, g_ref, out_ref,
                    yl_s, y_s, sem, sem2):
    i = pl.program_id(0)
    nt = pl.num_programs(0)
    slot = i % 2

    def ycopy(s, src, dst, sm):
        return pltpu.make_async_copy(yb_ref.at[pl.ds(src, CCH), :], yl_s.at[s, pl.ds(dst, CCH), :],
                                     sm.at[s])

    def issue(tile, s):
        for e in range(N_EXPERTS):
            lo = pl.multiple_of(lo_sm[tile * N_EXPERTS + e], CCH)
            src = pl.multiple_of(dst_sm[tile * N_EXPERTS + e], SUB)
            ycopy(s, src, lo, sem).start()

        @pl.when(extra_sm[tile] > 0)
        def _():
            def ebody(e, carry):
                n = cnt_sm[tile * N_EXPERTS + e]
                lo = pl.multiple_of(lo_sm[tile * N_EXPERTS + e], CCH)
                src = pl.multiple_of(dst_sm[tile * N_EXPERTS + e], SUB)

                def cbody(c, cc):
                    ycopy(s, src + c * CCH, lo + c * CCH, sem2).start()
                    return cc

                return lax.fori_loop(1, (n + CCH - 1) // CCH, cbody, carry)

            lax.fori_loop(0, N_EXPERTS, ebody, 0)

    @pl.when(i == 0)
    def _():
        yl_s[...] = jnp.zeros(yl_s.shape, I32)
        issue(0, 0)

    @pl.when(i + 1 < nt)
    def _():
        issue(i + 1, 1 - slot)

    for _ in range(N_EXPERTS):
        ycopy(slot, 0, 0, sem).wait()

    def wbody(j, c):
        ycopy(slot, 0, 0, sem2).wait()
        return c

    lax.fori_loop(0, extra_sm[i], wbody, 0)

    rc = rcol_ref[...]
    pad0 = rc[:, R_PAD0:R_PAD0 + 1]
    pad1 = rc[:, R_PAD1:R_PAD1 + 1]
    lane = lax.broadcasted_iota(I32, (TM, CBLK), 1).astype(F32)

    def chunk(c):
        r = lane + float(c * CBLK)
        selm = (jnp.where(pad0 == r, 1.0, 0.0) + jnp.where(pad1 == r, 1.0, 0.0)).astype(BF16)
        lo_h, hi_h = _unpack_halves(yl_s[slot, c * CBLK:(c + 1) * CBLK, :])
        return _dot(selm, lo_h), _dot(selm, hi_h)

    half = D_MODEL // 2
    ylo = jnp.zeros((TM, half), F32)
    yhi = jnp.zeros((TM, half), F32)
    for c in range(LOC_COMMON // CBLK):
        dlo, dhi = chunk(c)
        ylo = ylo + dlo
        yhi = yhi + dhi
    y_s[:, :half] = ylo
    y_s[:, half:] = yhi
    for c in range(LOC_COMMON // CBLK, LOC_ROWS // CBLK):
        @pl.when(used_sm[i] > c * CBLK)
        def _(c=c):
            dlo, dhi = chunk(c)
            y_s[:, :half] += dlo
            y_s[:, half:] += dhi
    out_ref[...] = _rms(x2_ref[...] + y_s[...], g_ref[...])


def _combine(cnt_f, lo16_f, dst_f, used, extra, yb, x2, rcol, g_final):
    T = x2.shape[0]
    nt = T // TM
    grid_spec = pltpu.PrefetchScalarGridSpec(
        num_scalar_prefetch=5,
        grid=(nt,),
        in_specs=[
            pl.BlockSpec(memory_space=pl.ANY),
            pl.BlockSpec((TM, D_MODEL), lambda i, *_: (i, 0)),
            pl.BlockSpec((TM, LANES), lambda i, *_: (i, 0)),
            pl.BlockSpec((1, D_MODEL), lambda i, *_: (0, 0)),
        ],
        out_specs=pl.BlockSpec((TM, D_MODEL), lambda i, *_: (i, 0)),
        scratch_shapes=[
            pltpu.VMEM((2, LOC_ROWS, D_MODEL // 2), I32),
            pltpu.VMEM((TM, D_MODEL), F32),
            pltpu.SemaphoreType.DMA((2,)), pltpu.SemaphoreType.DMA((2,)),
        ],
    )
    return pl.pallas_call(
        _combine_kernel,
        grid_spec=grid_spec,
        out_shape=jax.ShapeDtypeStruct((T, D_MODEL), F32),
        compiler_params=pltpu.CompilerParams(
            dimension_semantics=("arbitrary",), vmem_limit_bytes=40 * 1024 * 1024),
        name="combine",
    )(cnt_f, lo16_f, dst_f, used, extra, yb, x2, rcol, g_final)


def _rope_tables(S):
    half = DA_HEAD_DIM // 2
    inv = (1.0 / (np.float32(ROPE_THETA) ** (np.arange(0, DA_HEAD_DIM, 2, dtype=np.float32)
                                             / np.float32(DA_HEAD_DIM)))).astype(np.float32)
    ang = (np.arange(S, dtype=np.float32)[:, None] * inv[None, :]).astype(np.float32)
    cos, sin = np.cos(ang), np.sin(ang)
    lane = np.arange(LANES)
    idx = lane % half
    lower = (lane % DA_HEAD_DIM) < half
    c = cos[:, idx]
    s = sin[:, idx]
    z = np.zeros_like(s)
    return (jnp.asarray(c, F32), jnp.asarray(np.where(lower[None, :], -s, z), F32),
            jnp.asarray(np.where(lower[None, :], z, s), F32))


def _tri_dot(a, b):
    return jnp.dot(a.astype(F32), b.astype(F32), precision=lax.Precision.HIGHEST).astype(I32)


def kernel(x, w_in, conv_w, conv_b, gate_b, lam_qk, subln_g, mhnorm_g, w_out, g_mix, g_ffn, w_grp,
           b_grp, w_erouter, b_erouter, w1, w3, w2, g_final):
    B, S, D = x.shape
    T = B * S
    nt = T // TM
    x2d = x.reshape(T, D)
    l = 0

    w_main = w_in[l, :, :N_MAIN].astype(BF16)
    wg8 = w_in[l, :, N_MAIN:]
    wg = jnp.pad(wg8, ((0, 0), (0, LANES - 8))).astype(BF16)
    wgt = wg8.T.astype(BF16)
    gb8 = gate_b[l].reshape(8)
    gb_col = jnp.pad(gb8, (0, LANES - 8)).reshape(1, LANES)
    gb_row = jnp.broadcast_to(gb8[:, None], (8, LANES))
    rope_c, rope_sa, rope_sb = _rope_tables(S)
    w_r = jnp.concatenate(
        [w_grp[l], w_erouter[l].transpose(1, 0, 2).reshape(D, N_EXPERTS)], axis=1)
    w_r = jnp.pad(w_r, ((0, 0), (0, LANES - w_r.shape[1])))
    wr_hi = w_r.astype(BF16)
    wr_t = jnp.concatenate([wr_hi, (w_r - wr_hi.astype(F32)).astype(BF16)], axis=1)
    b_r = jnp.pad(jnp.concatenate([b_grp[l], b_erouter[l].reshape(N_EXPERTS)]),
                  (0, LANES - N_GROUPS - N_EXPERTS))
    b_rt = jnp.broadcast_to(b_r[:, None], (LANES, TM))

    q1, q2, kr, va, qc, kc, vm, om, gcol, grow = _inproj(
        x2d, g_mix[l].reshape(1, D), w_main, wg, wgt, rope_c, rope_sa, rope_sb,
        conv_w[l], conv_b[l].reshape(1, -1), S)
    oa = _attention(q1, q2, kr, va, lam_qk[l], subln_g[l].reshape(1, LANES), B, S)
    hm = _mlstm(qc, kc, vm, om, gcol, grow, gb_col, gb_row, mhnorm_g[l].reshape(1, ML_WIDTH), B, S)
    x2, hn, rrow, rcol, cnt3 = _router(oa, hm, x2d, w_out[l].astype(BF16), g_ffn[l].reshape(1, D),
                                       wr_t, b_rt)

    cnt = cnt3[:, 0, EXP_ROW0:EXP_ROW0 + N_EXPERTS]
    c8 = (cnt + SUB - 1) // SUB * SUB
    c16 = jnp.maximum((cnt + CCH - 1) // CCH, 1) * CCH
    count8 = jnp.sum(c8, axis=0)
    pcount = (count8 + DCH + BLK - 1) // BLK * BLK
    up_e = np.triu(np.ones((N_EXPERTS, N_EXPERTS), np.float32), 1)
    lo_t = np.tril(np.ones((nt, nt), np.float32), -1)
    pstart = _tri_dot(pcount[None, :], up_e)[0]
    pend = pstart + pcount
    dst = pstart[None, :] + _tri_dot(lo_t, c8)
    lo8 = _tri_dot(c8, up_e)
    lo16 = _tri_dot(c16, up_e)
    nb = (2 * T + nt * N_EXPERTS * (SUB - 1) + N_EXPERTS * (DCH + BLK - 1)) // BLK + 1
    blk_row = jnp.arange(nb, dtype=I32) * BLK
    blk_e = jnp.minimum(jnp.sum((pend[None, :] <= blk_row[:, None]).astype(I32), axis=1),
                        N_EXPERTS - 1).astype(I32)
    nused = (pend[-1] // BLK).astype(I32).reshape(1)
    cnt_f = cnt.reshape(-1).astype(I32)
    dst_f = dst.reshape(-1).astype(I32)

    gap_start = jnp.concatenate([pstart + count8, pend[-1:]]).astype(I32)
    gap_rows = jnp.concatenate([pcount - count8, nb * BLK - pend[-1:]]).astype(I32)
    extra = jnp.sum(jnp.maximum((cnt + DCH - 1) // DCH - 1, 0), axis=1).astype(I32)
    xb = _dispatch(cnt_f, lo8.reshape(-1), dst_f, extra, gap_start, gap_rows, hn, rrow, rcol, nb)
    yb = _experts(blk_e, nused, xb, w1[l], w3[l], w2[l], nb)
    used = jnp.sum(c16, axis=1).astype(I32)
    out = _combine(cnt_f, lo16.reshape(-1), dst_f, used, extra, yb, x2, rcol, g_final.reshape(1, D))
    return out.reshape(B, S, D)
```

```python
import functools
import math

import numpy as np
import jax
import jax.numpy as jnp
from jax import lax
from jax.experimental import pallas as pl
from jax.experimental.pallas import tpu as pltpu

F32 = jnp.float32
BF16 = jnp.bfloat16
I32 = jnp.int32

D_MODEL = 1024
DA_HEADS = 4
DA_HEAD_DIM = 64
DA_V_DIM = 128
DA_WIDTH = 512
ML_HEADS = 4
ML_WIDTH = 512
ML_HEAD_DIM = 128
ML_CHUNK = 128
ML_GROUP = 4
CONV_K = 4
ROPE_THETA = 10000.0
RMS_EPS = 1e-6
N_GROUPS = 4
EXPERTS_PER_GROUP = 8
N_EXPERTS = 32
D_EXPERT = 512
LAMBDA_INIT = 0.8 - 0.6 * math.exp(-0.3 * 0)

LANES = 128
SUB = 8
N_MAIN = 7 * 512
NEG = -1e30

TM_PROJ = 1024
TQ = 256
TK = 512
VT_ROWS = DA_V_DIM + 16
TM = 256
BLK = 512
ESUB = 256
CBLK = 256
DCH = 32
CCH = 32
DLOC_ROWS = 768
N_GAPS = 2 * N_EXPERTS + 1
LOC_COMMON = N_EXPERTS * CCH
LOC_ROWS = 2 * TM + N_EXPERTS * CCH
XW = D_MODEL // 2 + LANES
EXP_ROW0 = 4
RT_ROWS = 40
RT_GROUP = 4
R_E0, R_E1, R_POS0, R_POS1, R_PAD0, R_PAD1, R_W0, R_W1 = range(8)


def _nt_dot(a, b):
    return lax.dot_general(a, b, (((1,), (1,)), ((), ())), preferred_element_type=F32)


def _tn_dot(a, b):
    return lax.dot_general(a, b, (((0,), (0,)), ((), ())), preferred_element_type=F32)


def _dot(a, b):
    return jnp.dot(a, b, preferred_element_type=F32)


def _split3(x):
    hi = x.astype(BF16)
    r = x - hi.astype(F32)
    mid = r.astype(BF16)
    lo = (r - mid.astype(F32)).astype(BF16)
    return hi, mid, lo


def _rms(x, g):
    return x * lax.rsqrt(jnp.mean(x * x, axis=-1, keepdims=True) + RMS_EPS) * g


def _inproj_kernel(x_ref, g_ref, w_ref, wg_ref, wgt_ref, c_ref, sa_ref, sb_ref, cw_ref, cb_ref,
                   q1_ref, q2_ref, kr_ref, va_ref, qc_ref, kc_ref, vm_ref, om_ref, gcol_ref, grow_ref,
                   h_s, cq_s, ck_s, *, tiles_per_seq):
    tm = x_ref.shape[0]
    cw = 512
    first = (pl.program_id(0) % tiles_per_seq) == 0

    @pl.when(first)
    def _():
        cq_s[0:8, :] = jnp.zeros((8, cw), F32)
        ck_s[0:8, :] = jnp.zeros((8, cw), F32)

    @pl.when(jnp.logical_not(first))
    def _():
        cq_s[0:8, :] = cq_s[tm:tm + 8, :]
        ck_s[0:8, :] = ck_s[tm:tm + 8, :]

    h_s[...] = _rms(x_ref[...], g_ref[...]).astype(BF16)
    nhalf = 2 if tm >= 512 else 1
    th = tm // nhalf
    halves = [slice(p * th, (p + 1) * th) for p in range(nhalf)]

    def cols(c, hv):
        return _dot(h_s[hv, :], w_ref[:, c * cw:(c + 1) * cw])

    rb = min(128, th)
    lane = lax.broadcasted_iota(I32, (rb, LANES), 1)
    qscale = DA_HEAD_DIM ** -0.5 * math.log2(math.e)

    def rope(x, rows):
        return (x * c_ref[rows, :] + pltpu.roll(x, 96, 1) * sa_ref[rows, :]
                + pltpu.roll(x, 32, 1) * sb_ref[rows, :])

    def blocks(hv):
        for r in range(th // rb):
            for hh in range(cw // LANES):
                yield (slice(hv.start + r * rb, hv.start + (r + 1) * rb), slice(r * rb, (r + 1) * rb),
                       slice(hh * LANES, (hh + 1) * LANES))

    def rope_q(qa, hv):
        for rows, loc, hs in blocks(hv):
            qr = rope(qa[loc, hs], rows) * qscale
            q1_ref[rows, hs] = jnp.where(lane < DA_HEAD_DIM, qr, 0.0).astype(BF16)
            q2_ref[rows, hs] = jnp.where(lane >= DA_HEAD_DIM, qr, 0.0).astype(BF16)

    def rope_k(ka, hv):
        for rows, loc, hs in blocks(hv):
            kr_ref[rows, hs] = rope(ka[loc, hs], rows).astype(BF16)

    def conv_silu(src, c0, hv, out_ref, scale):
        for rows, _, hs in blocks(hv):
            cs = slice(c0 + hs.start, c0 + hs.stop)
            w = cw_ref[:, cs]
            y = cb_ref[:, cs] + src[8 + rows.start:8 + rows.stop, hs] * w[3:4]
            for s in (1, 2, 3):
                y = y + src[8 + rows.start - s:8 + rows.stop - s, hs] * w[3 - s:4 - s]
            y = y * jax.nn.sigmoid(y)
            out_ref[rows, hs] = (y if scale is None else y * scale).astype(BF16)

    def to_conv(c, dst, hv):
        dst[8 + hv.start:8 + hv.stop, :] = cols(c, hv)

    qa = [cols(0, hv) for hv in halves]
    ka = []
    for p, hv in enumerate(halves):
        ka.append(cols(1, hv))
        rope_q(qa[p], hv)
    for p, hv in enumerate(halves):
        va_ref[hv, :] = cols(2, hv).astype(BF16)
        rope_k(ka[p], hv)
    for hv in halves:
        to_conv(3, cq_s, hv)
    for hv in halves:
        to_conv(4, ck_s, hv)
        conv_silu(cq_s, 0, hv, qc_ref, None)
    for hv in halves:
        vm_ref[hv, :] = cols(5, hv).astype(BF16)
        conv_silu(ck_s, ML_WIDTH, hv, kc_ref, ML_HEAD_DIM ** -0.5)
    om_ref[...] = _dot(h_s[...], w_ref[:, 6 * cw:7 * cw]).astype(BF16)
    gcol_ref[...] = _dot(h_s[...], wg_ref[...])
    grow_ref[...] = _nt_dot(wgt_ref[...], h_s[...])


def _inproj(x2d, g_mix, w_main, wg, wgt, rope_c, rope_sa, rope_sb, conv_w, conv_b, S):
    T = x2d.shape[0]
    tm = min(TM_PROJ, S)
    assert S % tm == 0
    tps = S // tm
    full = lambda shape: pl.BlockSpec(shape, lambda i: (0, 0))
    rope_spec = pl.BlockSpec((tm, LANES), lambda i: (i % tps, 0))
    col_spec = pl.BlockSpec((tm, 512), lambda i: (i, 0))
    return pl.pallas_call(
        functools.partial(_inproj_kernel, tiles_per_seq=tps),
        grid=(T // tm,),
        in_specs=[
            pl.BlockSpec((tm, D_MODEL), lambda i: (i, 0)),
            full((1, D_MODEL)), full((D_MODEL, N_MAIN)), full((D_MODEL, LANES)), full((8, D_MODEL)),
            rope_spec, rope_spec, rope_spec,
            full((CONV_K, 2 * ML_WIDTH)), full((1, 2 * ML_WIDTH)),
        ],
        out_specs=[col_spec] * 8 + [
            pl.BlockSpec((tm, LANES), lambda i: (i, 0)),
            pl.BlockSpec((8, tm), lambda i: (0, i)),
        ],
        out_shape=[jax.ShapeDtypeStruct((T, 512), BF16)] * 8 + [
            jax.ShapeDtypeStruct((T, LANES), F32),
            jax.ShapeDtypeStruct((8, T), F32),
        ],
        scratch_shapes=[pltpu.VMEM((tm, D_MODEL), BF16),
                        pltpu.VMEM((tm + 8, ML_WIDTH), F32), pltpu.VMEM((tm + 8, ML_WIDTH), F32)],
        compiler_params=pltpu.CompilerParams(
            dimension_semantics=("arbitrary",), vmem_limit_bytes=56 * 1024 * 1024),
        name="inproj",
    )(x2d, g_mix, w_main, wg, wgt, rope_c, rope_sa, rope_sb, conv_w, conv_b)


def _attn_kernel(q1_s, q2_s, k_s, v_ref, lam_ref, g_ref, o_ref, vt_s, acc_s, st_s):
    S = v_ref.shape[0]

    def prep(r, carry):
        rows = pl.ds(pl.multiple_of(r * TQ, TQ), TQ)
        for h in range(DA_HEADS):
            hs = slice(h * LANES, (h + 1) * LANES)
            vt_s[h, :DA_V_DIM, rows] = v_ref[rows, hs].astype(F32).T.astype(BF16)
        return carry

    lax.fori_loop(0, S // TQ, prep, 0)
    vt_s[:, DA_V_DIM:, :] = jnp.ones((DA_HEADS, VT_ROWS - DA_V_DIM, S), BF16)

    lq = lam_ref[...]
    lam = (jnp.exp(jnp.sum(lq[0:1] * lq[1:2], axis=-1, keepdims=True))
           - jnp.exp(jnp.sum(lq[2:3] * lq[3:4], axis=-1, keepdims=True)) + LAMBDA_INIT)

    key = lax.broadcasted_iota(I32, (TK, 2 * TQ), 0)
    qry = lax.broadcasted_iota(I32, (TK, 2 * TQ), 1)
    kq = key - jnp.where(qry >= TQ, qry - TQ, qry)

    def kvstep(qo, ko, tk, ms, causal, slot):
        out = []

        def scores(h):
            hs = slice(h * LANES, (h + 1) * LANES)
            qq = jnp.concatenate([q1_s[pl.ds(qo, TQ), hs], q2_s[pl.ds(qo, TQ), hs]], axis=0)
            st = _nt_dot(k_s[pl.ds(ko, tk), hs], qq)
            st_s[h, :tk, :] = st if causal is None else jnp.where(causal, st, NEG)

        scores(0)
        for h in range(DA_HEADS):
            if h + 1 < DA_HEADS:
                scores(h + 1)
            st = st_s[h, :tk, :]
            m_new = jnp.maximum(ms[h], jnp.max(st, axis=0, keepdims=True))
            alpha = jnp.exp2(ms[h] - m_new)
            p = jnp.exp2(st - m_new).astype(BF16)
            acc_s[slot, h] = alpha * acc_s[slot, h] + _dot(vt_s[h, :, pl.ds(ko, tk)], p)
            out.append(m_new)
        return tuple(out)

    def start(i, ko, tk, off, slot):
        acc_s[slot] = jnp.zeros(acc_s.shape[1:], F32)
        m0 = tuple(jnp.full((1, 2 * TQ), NEG, F32) for _ in range(DA_HEADS))
        return kvstep(pl.multiple_of(i * TQ, TQ), ko, tk, m0, kq[:tk] <= off, slot)

    def full_steps(i, n, ms, slot):
        qo = pl.multiple_of(i * TQ, TQ)
        lax.fori_loop(0, n, lambda j, m: kvstep(qo, pl.multiple_of(j * TK, TK), TK, m, None, slot), ms)

    def finalize(i, slot):
        qo = pl.multiple_of(i * TQ, TQ)
        for h in range(DA_HEADS):
            hs = slice(h * LANES, (h + 1) * LANES)
            acc = acc_s[slot, h]
            on = acc[:DA_V_DIM] * (1.0 / acc[DA_V_DIM:DA_V_DIM + 1])
            ot = on[:, :TQ] - lam * on[:, TQ:]
            o = _rms(ot.T, g_ref[...]) * (1.0 - LAMBDA_INIT)
            o_ref[pl.ds(qo, TQ), hs] = o.astype(BF16)

    def tile_pair(u, carry):
        ko = pl.multiple_of(u * TK, TK)
        finalize(2 * u - 1, 1)
        full_steps(2 * u, u, start(2 * u, ko, TQ, 0, 0), 0)
        finalize(2 * u, 0)
        full_steps(2 * u + 1, u, start(2 * u + 1, ko, TK, TQ, 1), 1)
        return carry

    assert TK == 2 * TQ and (S // TQ) % 2 == 0
    start(0, 0, TQ, 0, 0)
    finalize(0, 0)
    start(1, 0, TK, TQ, 1)
    lax.fori_loop(1, S // TK, tile_pair, 0)
    finalize(S // TQ - 1, 1)


def _attention(q1, q2, kr, va, lam_qk, subln_g, B, S):
    T = B * S
    full = lambda shape: pl.BlockSpec(shape, lambda b: (0, 0))
    seq = pl.BlockSpec((S, DA_WIDTH), lambda b: (b, 0))
    return pl.pallas_call(
        _attn_kernel,
        grid=(B,),
        in_specs=[seq, seq, seq, seq, full((4, DA_HEAD_DIM)), full((1, LANES))],
        out_specs=seq,
        out_shape=jax.ShapeDtypeStruct((T, DA_WIDTH), BF16),
        scratch_shapes=[
            pltpu.VMEM((DA_HEADS, VT_ROWS, S), BF16),
            pltpu.VMEM((2, DA_HEADS, VT_ROWS, 2 * TQ), F32),
            pltpu.VMEM((DA_HEADS, TK, 2 * TQ), F32),
        ],
        compiler_params=pltpu.CompilerParams(
            dimension_semantics=("arbitrary",), vmem_limit_bytes=48 * 1024 * 1024),
        name="attn",
    )(q1, q2, kr, va, lam_qk, subln_g)


def _log_sigmoid(x):
    return jnp.minimum(x, 0.0) - jnp.log(1.0 + jnp.exp(-jnp.abs(x)))


def _mlstm_kernel(qc_s, kc_s, v_ref, o_ref, gcol_ref, grow_ref, gbc_ref, gbr_ref,
                  mg_ref, out_ref, vt_s, ct_s, m_s):
    S = v_ref.shape[0]
    L = ML_CHUNK
    nc = S // L

    def transpose_v(c, carry):
        ro = pl.multiple_of(c * L, L)
        for h in range(ML_HEADS):
            hs = slice(h * ML_HEAD_DIM, (h + 1) * ML_HEAD_DIM)
            vt_s[h, :ML_HEAD_DIM, pl.ds(ro, L)] = v_ref[pl.ds(ro, L), hs].astype(F32).T.astype(BF16)
        return carry

    lax.fori_loop(0, nc, transpose_v, 0)
    vt_s[:, ML_HEAD_DIM:, :] = jnp.ones((ML_HEADS, ML_HEAD_DIM, S), BF16)

    ct_s[...] = jnp.zeros(ct_s.shape, F32)
    m_s[...] = jnp.zeros(m_s.shape, F32)

    ri = lax.broadcasted_iota(I32, (L, L), 0)
    ci = lax.broadcasted_iota(I32, (L, L), 1)
    causal_t = ri <= ci
    tril = jnp.where(ci <= ri, 1.0, 0.0).astype(BF16)
    triu = jnp.where(causal_t, 1.0, 0.0).astype(BF16)

    heads = [slice(h * ML_HEAD_DIM, (h + 1) * ML_HEAD_DIM) for h in range(ML_HEADS)]

    def gates(c):
        ro = pl.multiple_of(c * L, L)
        gc = gcol_ref[pl.ds(ro, L), :] + gbc_ref[...]
        gr = grow_ref[:, pl.ds(ro, L)] + gbr_ref[...]
        b_c = sum(_dot(tril, p) for p in _split3(_log_sigmoid(gc)))
        b_r = sum(_dot(p, triu) for p in _split3(_log_sigmoid(gr)))
        return ro, gc, gr, b_c, b_r

    def read_state(g):
        ro = g[0]
        ks, vts, kq, cq, ms = [], [], [], [], []
        for h, hs in enumerate(heads):
            q = qc_s[pl.ds(ro, L), hs]
            k = kc_s[pl.ds(ro, L), hs]
            ks.append(k)
            vts.append(vt_s[h, :, pl.ds(ro, L)])
            kq.append(_nt_dot(k, q))
            cq.append(_nt_dot(ct_s[h].astype(BF16), q))
            ms.append(m_s[h:h + 1, 0:1])
        return ks, vts, kq, cq, ms

    def update_state(g, d):
        _, _, gr, _, b_r = g
        ks, vts, _, _, ms = d
        for h in range(ML_HEADS):
            br = b_r[4 + h:5 + h, :]
            bl = br[:, L - 1:L]
            dec = bl - br + gr[h:h + 1, :]
            m_new = jnp.maximum(bl + ms[h], jnp.max(dec, axis=1, keepdims=True))
            ws = jnp.exp(dec - m_new)
            sc = jnp.exp(bl + ms[h] - m_new)
            vw = (vts[h].astype(F32) * ws).astype(BF16)
            ct_s[h] = sc * ct_s[h] + _dot(vw, ks[h])
            m_s[h:h + 1, :] = jnp.broadcast_to(m_new, (1, LANES))

    def outputs(g, d):
        ro, gc, _, b_c, b_r = g
        _, vts, kq, cq, ms = d
        for h, hs in enumerate(heads):
            br = b_r[4 + h:5 + h, :]
            a_col = b_c[:, 4 + h:5 + h] - gc[:, h:h + 1]
            dm = jnp.where(causal_t, br - a_col, NEG)
            inter = br + ms[h]
            m_row = jnp.maximum(inter, jnp.max(dm, axis=0, keepdims=True))
            sm = jnp.exp(dm - m_row) * kq[h]
            sc_in = jnp.exp(inter - m_row)
            a = _dot(vts[h], sm.astype(BF16)) + sc_in * cq[h]
            den = a[ML_HEAD_DIM:ML_HEAD_DIM + 1, :]
            hh = (a[:ML_HEAD_DIM] / jnp.maximum(jnp.abs(den), jnp.exp(-m_row))).T
            hg = hh * jax.nn.sigmoid(o_ref[pl.ds(ro, L), hs].astype(F32))
            out_ref[pl.ds(ro, L), hs] = _rms(hg, mg_ref[:, hs]).astype(BF16)

    def chunk_group(cg, carry):
        gs = [gates(ML_GROUP * cg + j) for j in range(ML_GROUP)]
        ds = []
        for g in gs:
            ds.append(read_state(g))
            update_state(g, ds[-1])
        for g, d in zip(gs, ds):
            outputs(g, d)
        return carry

    lax.fori_loop(0, nc // ML_GROUP, chunk_group, 0)


def _mlstm(qc, kc, vm, om, gcol, grow, gb_col, gb_row, mh_g, B, S):
    T = B * S
    full = lambda shape: pl.BlockSpec(shape, lambda b: (0, 0))
    seq = pl.BlockSpec((S, ML_WIDTH), lambda b: (b, 0))
    return pl.pallas_call(
        _mlstm_kernel,
        grid=(B,),
        in_specs=[
            seq, seq, seq, seq,
            pl.BlockSpec((S, LANES), lambda b: (b, 0)),
            pl.BlockSpec((8, S), lambda b: (0, b)),
            full((1, LANES)), full((8, LANES)), full((1, ML_WIDTH)),
        ],
        out_specs=seq,
        out_shape=jax.ShapeDtypeStruct((T, ML_WIDTH), BF16),
        scratch_shapes=[
            pltpu.VMEM((ML_HEADS, 2 * ML_HEAD_DIM, S), BF16),
            pltpu.VMEM((ML_HEADS, 2 * ML_HEAD_DIM, ML_HEAD_DIM), F32),
            pltpu.VMEM((8, LANES), F32),
        ],
        compiler_params=pltpu.CompilerParams(
            dimension_semantics=("arbitrary",), vmem_limit_bytes=56 * 1024 * 1024),
        name="mlstm",
    )(qc, kc, vm, om, gcol, grow, gb_col, gb_row, mh_g)


def _router_kernel(oa_ref, hm_ref, x_ref, wo_ref, g_ref, wr_ref, brt_ref,
                   x2_ref, hn_ref, rrow_ref, rcol_ref, cnt_ref):
    def project(s):
        rs = slice(s * TM, (s + 1) * TM)
        mixo = _dot(oa_ref[rs, :], wo_ref[:DA_WIDTH, :]) + _dot(hm_ref[rs, :], wo_ref[DA_WIDTH:, :])
        x2 = x_ref[rs, :] + mixo
        x2_ref[rs, :] = x2
        hn = _rms(x2, g_ref[...])
        hn_hi = hn.astype(BF16)
        hn_ref[rs, :] = hn_hi
        hn_lo = (hn - hn_hi.astype(F32)).astype(BF16)
        a = _dot(hn_hi, wr_ref[...])
        b = _dot(hn_lo, wr_ref[:, :LANES])
        return (a[:, :LANES] + a[:, LANES:] + b).T[:RT_ROWS] + brt_ref[:RT_ROWS, :]

    lts = [project(0)]
    for s in range(RT_GROUP):
        if s + 1 < RT_GROUP:
            lts.append(project(s + 1))
        _route(s, lts[s], rrow_ref, rcol_ref, cnt_ref)


def _route(s, lt, rrow_ref, rcol_ref, cnt_ref):
    sub = lax.broadcasted_iota(I32, (RT_ROWS, TM), 0)
    sub_f = sub.astype(F32)

    def cmax(v):
        return jnp.max(v, axis=0, keepdims=True)

    def first_idx(mask):
        return jnp.min(jnp.where(mask, sub_f, 1e6), axis=0, keepdims=True).astype(I32)

    gl = jnp.where(sub < N_GROUPS, lt, NEG)
    gmax = cmax(gl)
    gsel = first_idx(gl == gmax)
    g_w = 1.0 / jnp.sum(jnp.exp(gl - gmax), axis=0, keepdims=True)
    elo = EXP_ROW0 + gsel * EXPERTS_PER_GROUP
    el = jnp.where((sub >= elo) & (sub < elo + EXPERTS_PER_GROUP), lt, NEG)
    v1 = cmax(el)
    i1 = first_idx(el == v1)
    el2 = jnp.where(sub == i1, NEG, el)
    v2 = cmax(el2)
    i2 = first_idx(el2 == v2)
    t = jnp.exp(v2 - v1)
    w0 = g_w / (1.0 + t)
    w1 = g_w * t / (1.0 + t)

    oh0 = jnp.where(sub == i1, 1.0, 0.0)
    oh1 = jnp.where(sub == i2, 1.0, 0.0)
    mh = oh0 + oh1
    r_i = lax.broadcasted_iota(I32, (TM, TM), 0)
    c_i = lax.broadcasted_iota(I32, (TM, TM), 1)
    before = jnp.where(r_i < c_i, 1.0, 0.0).astype(BF16)
    mh_f = jnp.concatenate([mh, jnp.zeros((LANES - RT_ROWS, TM), F32)], axis=0)
    mh_b = mh_f.astype(BF16)
    pre = _dot(mh_b, before)[:RT_ROWS]
    cnt = jnp.sum(mh_f, axis=1, keepdims=True)
    cnt8 = jnp.floor((cnt + (SUB - 1)) * (1.0 / SUB)) * SUB
    erow = lax.broadcasted_iota(I32, (LANES, 1), 0)
    is_exp = (erow >= EXP_ROW0) & (erow < EXP_ROW0 + N_EXPERTS)
    cnt16 = jnp.where(is_exp, jnp.maximum(jnp.floor((cnt + (CCH - 1)) * (1.0 / CCH)), 1.0) * CCH, 0.0)
    e_r = lax.broadcasted_iota(I32, (LANES, LANES), 0)
    e_c = lax.broadcasted_iota(I32, (LANES, LANES), 1)
    below = jnp.where(e_c < e_r, 1.0, 0.0).astype(BF16)
    lo8 = _dot(below, jnp.broadcast_to(cnt8, (LANES, LANES)).astype(BF16))[:RT_ROWS, 0:1]
    lo16 = _dot(below, jnp.broadcast_to(cnt16, (LANES, LANES)).astype(BF16))[:RT_ROWS, 0:1]

    def csum(v):
        return jnp.sum(v, axis=0, keepdims=True)

    rows = [None] * 8
    rows[R_E0] = (i1 - EXP_ROW0).astype(F32)
    rows[R_E1] = (i2 - EXP_ROW0).astype(F32)
    rows[R_POS0] = csum(oh0 * (pre + lo8))
    rows[R_POS1] = csum(oh1 * (pre + lo8))
    rows[R_PAD0] = csum(oh0 * (pre + lo16))
    rows[R_PAD1] = csum(oh1 * (pre + lo16))
    rows[R_W0] = w0
    rows[R_W1] = w1
    sub128 = lax.broadcasted_iota(I32, (LANES, TM), 0)
    r128 = jnp.zeros((LANES, TM), F32)
    for j, v in enumerate(rows):
        r128 = jnp.where(sub128 == j, v, r128)
    rrow_ref[:, s * TM:(s + 1) * TM] = r128[:8]
    rcol_ref[s * TM:(s + 1) * TM, :] = r128.T
    cnt_ref[s] = _nt_dot(jnp.ones((8, TM), BF16), mh_b)[0:1].astype(I32)


def _router(oa, hm, x2d, w_out, g_ffn, wr_t, b_rt):
    T = x2d.shape[0]
    nt = T // TM
    full = lambda shape: pl.BlockSpec(shape, lambda i: (0, 0))
    tg = RT_GROUP * TM
    return pl.pallas_call(
        _router_kernel,
        grid=(T // tg,),
        in_specs=[
            pl.BlockSpec((tg, DA_WIDTH), lambda i: (i, 0)),
            pl.BlockSpec((tg, ML_WIDTH), lambda i: (i, 0)),
            pl.BlockSpec((tg, D_MODEL), lambda i: (i, 0)),
            full((D_MODEL, D_MODEL)), full((1, D_MODEL)),
            full((D_MODEL, 2 * LANES)), full((LANES, TM)),
        ],
        out_specs=[
            pl.BlockSpec((tg, D_MODEL), lambda i: (i, 0)),
            pl.BlockSpec((tg, D_MODEL), lambda i: (i, 0)),
            pl.BlockSpec((8, tg), lambda i: (0, i)),
            pl.BlockSpec((tg, LANES), lambda i: (i, 0)),
            pl.BlockSpec((RT_GROUP, 1, LANES), lambda i: (i, 0, 0)),
        ],
        out_shape=[
            jax.ShapeDtypeStruct((T, D_MODEL), F32),
            jax.ShapeDtypeStruct((T, D_MODEL), BF16),
            jax.ShapeDtypeStruct((8, T), F32),
            jax.ShapeDtypeStruct((T, LANES), F32),
            jax.ShapeDtypeStruct((nt, 1, LANES), I32),
        ],
        compiler_params=pltpu.CompilerParams(
            dimension_semantics=("arbitrary",), vmem_limit_bytes=40 * 1024 * 1024),
        name="router",
    )(oa, hm, x2d, w_out, g_ffn, wr_t, b_rt)


def _pack_halves(x):
    lo = lax.bitcast_convert_type(x[:, :D_MODEL // 2], I32)
    hi = lax.bitcast_convert_type(x[:, D_MODEL // 2:], I32)
    return lax.shift_right_logical(lo, 16) | (hi & jnp.int32(-65536))


def _unpack_halves(w):
    lo = lax.bitcast_convert_type(lax.shift_left(w, 16), F32)
    hi = lax.bitcast_convert_type(w & jnp.int32(-65536), F32)
    return lo.astype(BF16), hi.astype(BF16)


def _dispatch_kernel(cnt_sm, lo_sm, dst_sm, extra_sm, gs_sm, gn_sm, hn_ref, rrow_ref, rcol_ref, xb_ref,
                     xs_s, z_s, semx, semx2, semz):
    i = pl.program_id(0)
    nt = pl.num_programs(0)
    slot = i % 2
    half = D_MODEL // 2

    def zcopy(rows, dst):
        return pltpu.make_async_copy(z_s.at[pl.ds(0, rows), :], xb_ref.at[pl.ds(dst, rows), :], semz)

    def gap_fill(wait):
        def ebody(e, carry):
            start = pl.multiple_of(gs_sm[e], SUB)
            left = gn_sm[e]
            for rows in (BLK, DCH, SUB):
                n = left // rows

                def body(c, cc, rows=rows, start=start):
                    cp = zcopy(rows, pl.multiple_of(start + c * rows, SUB))
                    cp.wait() if wait else cp.start()
                    return cc

                lax.fori_loop(0, n, body, 0)
                start = start + n * rows
                left = left - n * rows
            return carry

        lax.fori_loop(0, N_GAPS, ebody, 0)

    @pl.when(i == 0)
    def _():
        for s in range(2):
            xs_s[s, DLOC_ROWS:, :] = jnp.zeros((DCH, XW), I32)
        z_s[...] = jnp.zeros(z_s.shape, I32)
        gap_fill(False)
        gap_fill(True)

    def xcopy(s, src, dst, sem):
        return pltpu.make_async_copy(xs_s.at[s, pl.ds(src, DCH), :], xb_ref.at[pl.ds(dst, DCH), :],
                                     sem.at[s])

    def wait_tile(s, n_extra):
        for _ in range(N_EXPERTS):
            xcopy(s, 0, 0, semx).wait()

        def wbody(j, c):
            xcopy(s, 0, 0, semx2).wait()
            return c
        lax.fori_loop(0, n_extra, wbody, 0)

    @pl.when(i >= 2)
    def _():
        wait_tile(slot, extra_sm[jnp.maximum(i - 2, 0)])

    rr = rrow_ref[...]
    rc = rcol_ref[...]
    r_i = lax.broadcasted_iota(I32, (DLOC_ROWS, TM), 0).astype(F32)
    perm0 = jnp.where(r_i == rr[R_POS0:R_POS0 + 1, :], 1.0, 0.0).astype(BF16)
    perm1 = jnp.where(r_i == rr[R_POS1:R_POS1 + 1, :], 1.0, 0.0).astype(BF16)
    xs = _dot(perm0 + perm1, hn_ref[...])
    xs_s[slot, :DLOC_ROWS, :half] = _pack_halves(xs)
    lane = lax.broadcasted_iota(I32, (TM, LANES), 1)

    def parts(col):
        hi, mid, lo = (p.astype(F32) for p in _split3(jnp.broadcast_to(rc[:, col:col + 1], (TM, LANES))))
        sel = jnp.where(lane == 0, hi, jnp.where(lane == 1, mid, jnp.where(lane == 2, lo, 0.0)))
        return sel.astype(BF16)

    wparts = _dot(perm0, parts(R_W0)) + _dot(perm1, parts(R_W1))
    l_r = lax.broadcasted_iota(I32, (LANES, LANES), 0)
    ones3 = jnp.where(l_r < 3, 1.0, 0.0).astype(BF16)
    wsort = _dot(wparts.astype(BF16), ones3)
    xs_s[slot, :DLOC_ROWS, half:] = lax.bitcast_convert_type(wsort, I32)

    for e in range(N_EXPERTS):
        lo = pl.multiple_of(lo_sm[i * N_EXPERTS + e], SUB)
        dst = pl.multiple_of(dst_sm[i * N_EXPERTS + e], SUB)
        xcopy(slot, lo, dst, semx).start()

    @pl.when(extra_sm[i] > 0)
    def _():
        def ebody(e, carry):
            n = cnt_sm[i * N_EXPERTS + e]
            lo = pl.multiple_of(lo_sm[i * N_EXPERTS + e], SUB)
            dst = pl.multiple_of(dst_sm[i * N_EXPERTS + e], SUB)

            def cbody(c, cc):
                xcopy(slot, lo + c * DCH, dst + c * DCH, semx2).start()
                return cc

            return lax.fori_loop(1, (n + DCH - 1) // DCH, cbody, carry)

        lax.fori_loop(0, N_EXPERTS, ebody, 0)

    @pl.when(i == nt - 1)
    def _():
        @pl.when(nt >= 2)
        def _():
            wait_tile(1 - slot, extra_sm[jnp.maximum(i - 1, 0)])
        wait_tile(slot, extra_sm[i])


def _dispatch(cnt_f, lo_f, dst_f, extra, gap_start, gap_rows, hn, rrow, rcol, nb):
    T = hn.shape[0]
    nt = T // TM
    grid_spec = pltpu.PrefetchScalarGridSpec(
        num_scalar_prefetch=6,
        grid=(nt,),
        in_specs=[
            pl.BlockSpec((TM, D_MODEL), lambda i, *_: (i, 0)),
            pl.BlockSpec((8, TM), lambda i, *_: (0, i)),
            pl.BlockSpec((TM, LANES), lambda i, *_: (i, 0)),
        ],
        out_specs=pl.BlockSpec(memory_space=pl.ANY),
        scratch_shapes=[
            pltpu.VMEM((2, DLOC_ROWS + DCH, XW), I32),
            pltpu.VMEM((BLK, XW), I32),
            pltpu.SemaphoreType.DMA((2,)), pltpu.SemaphoreType.DMA((2,)), pltpu.SemaphoreType.DMA(()),
        ],
    )
    return pl.pallas_call(
        _dispatch_kernel,
        grid_spec=grid_spec,
        out_shape=jax.ShapeDtypeStruct((nb * BLK, XW), I32),
        compiler_params=pltpu.CompilerParams(
            dimension_semantics=("arbitrary",), vmem_limit_bytes=40 * 1024 * 1024),
        name="dispatch",
    )(cnt_f, lo_f, dst_f, extra, gap_start, gap_rows, hn, rrow, rcol)


def _expert_kernel(be_sm, nu_sm, xb_ref, w1_ref, w3_ref, w2_ref, yb_ref, w1_s, w3_s, w2_s):
    p = pl.program_id(0)
    used = p < nu_sm[0]

    @pl.when(used & ((p == 0) | (be_sm[p] != be_sm[jnp.maximum(p - 1, 0)])))
    def _():
        w1_s[...] = w1_ref[...].astype(BF16)
        w3_s[...] = w3_ref[...].astype(BF16)
        w2_s[...] = w2_ref[...].astype(BF16)

    @pl.when(used)
    def _():
        half = D_MODEL // 2
        for r in range(BLK // ESUB):
            rs = slice(r * ESUB, (r + 1) * ESUB)
            xlo, xhi = _unpack_halves(xb_ref[rs, :half])
            h1 = _dot(xlo, w1_s[:half, :]) + _dot(xhi, w1_s[half:, :])
            h3 = _dot(xlo, w3_s[:half, :]) + _dot(xhi, w3_s[half:, :])
            wrep = lax.bitcast_convert_type(xb_ref[rs, half:], F32)
            wfull = jnp.concatenate([wrep] * (D_EXPERT // LANES), axis=1)
            hdn = (h1 * jax.nn.sigmoid(h1) * h3 * wfull).astype(BF16)
            y = _dot(hdn, w2_s[...]).astype(BF16).astype(F32)
            yb_ref[rs, :] = _pack_halves(y)

    @pl.when(jnp.logical_not(used))
    def _():
        yb_ref[...] = jnp.zeros(yb_ref.shape, I32)


def _experts(blk_e, nused, xb, w1, w3, w2, nb):
    def rows(p, be, nu):
        return (jnp.minimum(p, nu[0] - 1), 0)

    def wsel(p, be, nu):
        return (be[jnp.minimum(p, nu[0] - 1)], 0, 0)

    grid_spec = pltpu.PrefetchScalarGridSpec(
        num_scalar_prefetch=2,
        grid=(nb,),
        in_specs=[
            pl.BlockSpec((BLK, XW), rows),
            pl.BlockSpec((None, D_MODEL, D_EXPERT), wsel),
            pl.BlockSpec((None, D_MODEL, D_EXPERT), wsel),
            pl.BlockSpec((None, D_EXPERT, D_MODEL), wsel),
        ],
        out_specs=pl.BlockSpec((BLK, D_MODEL // 2), lambda p, be, nu: (p, 0)),
        scratch_shapes=[
            pltpu.VMEM((D_MODEL, D_EXPERT), BF16), pltpu.VMEM((D_MODEL, D_EXPERT), BF16),
            pltpu.VMEM((D_EXPERT, D_MODEL), BF16),
        ],
    )
    return pl.pallas_call(
        _expert_kernel,
        grid_spec=grid_spec,
        out_shape=jax.ShapeDtypeStruct((nb * BLK, D_MODEL // 2), I32),
        compiler_params=pltpu.CompilerParams(
            dimension_semantics=("arbitrary",), vmem_limit_bytes=48 * 1024 * 1024),
        name="experts",
    )(blk_e, nused, xb, w1, w3, w2)


def _combine_kernel(cnt_sm, lo_sm, dst_sm, used_sm, extra_sm, yb_ref, x2_ref, rcol_ref, g_ref, out_ref,
                    yl_s, y_s, sem, sem2):
    i = pl.program_id(0)
    nt = pl.num_programs(0)
    slot = i % 2

    def ycopy(s, src, dst, sm):
        return pltpu.make_async_copy(yb_ref.at[pl.ds(src, CCH), :], yl_s.at[s, pl.ds(dst, CCH), :],
                                     sm.at[s])

    def issue(tile, s):
        for e in range(N_EXPERTS):
            lo = pl.multiple_of(lo_sm[tile * N_EXPERTS + e], CCH)
            src = pl.multiple_of(dst_sm[tile * N_EXPERTS + e], SUB)
            ycopy(s, src, lo, sem).start()

        @pl.when(extra_sm[tile] > 0)
        def _():
            def ebody(e, carry):
                n = cnt_sm[tile * N_EXPERTS + e]
                lo = pl.multiple_of(lo_sm[tile * N_EXPERTS + e], CCH)
                src = pl.multiple_of(dst_sm[tile * N_EXPERTS + e], SUB)

                def cbody(c, cc):
                    ycopy(s, src + c * CCH, lo + c * CCH, sem2).start()
                    return cc

                return lax.fori_loop(1, (n + CCH - 1) // CCH, cbody, carry)

            lax.fori_loop(0, N_EXPERTS, ebody, 0)

    @pl.when(i == 0)
    def _():
        yl_s[...] = jnp.zeros(yl_s.shape, I32)
        issue(0, 0)

    @pl.when(i + 1 < nt)
    def _():
        issue(i + 1, 1 - slot)

    for _ in range(N_EXPERTS):
        ycopy(slot, 0, 0, sem).wait()

    def wbody(j, c):
        ycopy(slot, 0, 0, sem2).wait()
        return c

    lax.fori_loop(0, extra_sm[i], wbody, 0)

    rc = rcol_ref[...]
    pad0 = rc[:, R_PAD0:R_PAD0 + 1]
    pad1 = rc[:, R_PAD1:R_PAD1 + 1]
    lane = lax.broadcasted_iota(I32, (TM, CBLK), 1).astype(F32)

    def chunk(c):
        r = lane + float(c * CBLK)
        selm = (jnp.where(pad0 == r, 1.0, 0.0) + jnp.where(pad1 == r, 1.0, 0.0)).astype(BF16)
        lo_h, hi_h = _unpack_halves(yl_s[slot, c * CBLK:(c + 1) * CBLK, :])
        return _dot(selm, lo_h), _dot(selm, hi_h)

    half = D_MODEL // 2
    ylo = jnp.zeros((TM, half), F32)
    yhi = jnp.zeros((TM, half), F32)
    for c in range(LOC_COMMON // CBLK):
        dlo, dhi = chunk(c)
        ylo = ylo + dlo
        yhi = yhi + dhi
    y_s[:, :half] = ylo
    y_s[:, half:] = yhi
    for c in range(LOC_COMMON // CBLK, LOC_ROWS // CBLK):
        @pl.when(used_sm[i] > c * CBLK)
        def _(c=c):
            dlo, dhi = chunk(c)
            y_s[:, :half] += dlo
            y_s[:, half:] += dhi
    out_ref[...] = _rms(x2_ref[...] + y_s[...], g_ref[...])


def _combine(cnt_f, lo16_f, dst_f, used, extra, yb, x2, rcol, g_final):
    T = x2.shape[0]
    nt = T // TM
    grid_spec = pltpu.PrefetchScalarGridSpec(
        num_scalar_prefetch=5,
        grid=(nt,),
        in_specs=[
            pl.BlockSpec(memory_space=pl.ANY),
            pl.BlockSpec((TM, D_MODEL), lambda i, *_: (i, 0)),
            pl.BlockSpec((TM, LANES), lambda i, *_: (i, 0)),
            pl.BlockSpec((1, D_MODEL), lambda i, *_: (0, 0)),
        ],
        out_specs=pl.BlockSpec((TM, D_MODEL), lambda i, *_: (i, 0)),
        scratch_shapes=[
            pltpu.VMEM((2, LOC_ROWS, D_MODEL // 2), I32),
            pltpu.VMEM((TM, D_MODEL), F32),
            pltpu.SemaphoreType.DMA((2,)), pltpu.SemaphoreType.DMA((2,)),
        ],
    )
    return pl.pallas_call(
        _combine_kernel,
        grid_spec=grid_spec,
        out_shape=jax.ShapeDtypeStruct((T, D_MODEL), F32),
        compiler_params=pltpu.CompilerParams(
            dimension_semantics=("arbitrary",), vmem_limit_bytes=40 * 1024 * 1024),
        name="combine",
    )(cnt_f, lo16_f, dst_f, used, extra, yb, x2, rcol, g_final)


def _rope_tables(S):
    half = DA_HEAD_DIM // 2
    inv = (1.0 / (np.float32(ROPE_THETA) ** (np.arange(0, DA_HEAD_DIM, 2, dtype=np.float32)
                                             / np.float32(DA_HEAD_DIM)))).astype(np.float32)
    ang = (np.arange(S, dtype=np.float32)[:, None] * inv[None, :]).astype(np.float32)
    cos, sin = np.cos(ang), np.sin(ang)
    lane = np.arange(LANES)
    idx = lane % half
    lower = (lane % DA_HEAD_DIM) < half
    c = cos[:, idx]
    s = sin[:, idx]
    z = np.zeros_like(s)
    return (jnp.asarray(c, F32), jnp.asarray(np.where(lower[None, :], -s, z), F32),
            jnp.asarray(np.where(lower[None, :], z, s), F32))


def _tri_dot(a, b):
    return jnp.dot(a.astype(F32), b.astype(F32), precision=lax.Precision.HIGHEST).astype(I32)


def kernel(x, w_in, conv_w, conv_b, gate_b, lam_qk, subln_g, mhnorm_g, w_out, g_mix, g_ffn, w_grp,
           b_grp, w_erouter, b_erouter, w1, w3, w2, g_final):
    B, S, D = x.shape
    T = B * S
    nt = T // TM
    x2d = x.reshape(T, D)
    l = 0

    w_main = w_in[l, :, :N_MAIN].astype(BF16)
    wg8 = w_in[l, :, N_MAIN:]
    wg = jnp.pad(wg8, ((0, 0), (0, LANES - 8))).astype(BF16)
    wgt = wg8.T.astype(BF16)
    gb8 = gate_b[l].reshape(8)
    gb_col = jnp.pad(gb8, (0, LANES - 8)).reshape(1, LANES)
    gb_row = jnp.broadcast_to(gb8[:, None], (8, LANES))
    rope_c, rope_sa, rope_sb = _rope_tables(S)
    w_r = jnp.concatenate(
        [w_grp[l], w_erouter[l].transpose(1, 0, 2).reshape(D, N_EXPERTS)], axis=1)
    w_r = jnp.pad(w_r, ((0, 0), (0, LANES - w_r.shape[1])))
    wr_hi = w_r.astype(BF16)
    wr_t = jnp.concatenate([wr_hi, (w_r - wr_hi.astype(F32)).astype(BF16)], axis=1)
    b_r = jnp.pad(jnp.concatenate([b_grp[l], b_erouter[l].reshape(N_EXPERTS)]),
                  (0, LANES - N_GROUPS - N_EXPERTS))
    b_rt = jnp.broadcast_to(b_r[:, None], (LANES, TM))

    q1, q2, kr, va, qc, kc, vm, om, gcol, grow = _inproj(
        x2d, g_mix[l].reshape(1, D), w_main, wg, wgt, rope_c, rope_sa, rope_sb,
        conv_w[l], conv_b[l].reshape(1, -1), S)
    oa = _attention(q1, q2, kr, va, lam_qk[l], subln_g[l].reshape(1, LANES), B, S)
    hm = _mlstm(qc, kc, vm, om, gcol, grow, gb_col, gb_row, mhnorm_g[l].reshape(1, ML_WIDTH), B, S)
    x2, hn, rrow, rcol, cnt3 = _router(oa, hm, x2d, w_out[l].astype(BF16), g_ffn[l].reshape(1, D),
                                       wr_t, b_rt)

    cnt = cnt3[:, 0, EXP_ROW0:EXP_ROW0 + N_EXPERTS]
    c8 = (cnt + SUB - 1) // SUB * SUB
    c16 = jnp.maximum((cnt + CCH - 1) // CCH, 1) * CCH
    odd = (np.arange(nt) % 2)[:, None]
    c8_even = c8 * (1 - odd)
    c8_odd = c8 * odd
    even_tot = jnp.sum(c8_even, axis=0)
    count8 = even_tot + DCH + jnp.sum(c8_odd, axis=0)
    pcount = (count8 + DCH + BLK - 1) // BLK * BLK
    up_e = np.triu(np.ones((N_EXPERTS, N_EXPERTS), np.float32), 1)
    lo_t = np.tril(np.ones((nt, nt), np.float32), -1)
    pstart = _tri_dot(pcount[None, :], up_e)[0]
    pend = pstart + pcount
    dst = pstart[None, :] + jnp.where(odd == 0, _tri_dot(lo_t, c8_even),
                                      (even_tot + DCH)[None, :] + _tri_dot(lo_t, c8_odd))
    lo8 = _tri_dot(c8, up_e)
    lo16 = _tri_dot(c16, up_e)
    nb = (2 * T + nt * N_EXPERTS * (SUB - 1) + N_EXPERTS * (2 * DCH + BLK - 1)) // BLK + 1
    blk_row = jnp.arange(nb, dtype=I32) * BLK
    blk_e = jnp.minimum(jnp.sum((pend[None, :] <= blk_row[:, None]).astype(I32), axis=1),
                        N_EXPERTS - 1).astype(I32)
    nused = (pend[-1] // BLK).astype(I32).reshape(1)
    cnt_f = cnt.reshape(-1).astype(I32)
    dst_f = dst.reshape(-1).astype(I32)

    gap_start = jnp.concatenate([pstart + even_tot, pstart + count8, pend[-1:]]).astype(I32)
    gap_rows = jnp.concatenate([jnp.full((N_EXPERTS,), DCH, I32), pcount - count8,
                                nb * BLK - pend[-1:]]).astype(I32)
    extra = jnp.sum(jnp.maximum((cnt + DCH - 1) // DCH - 1, 0), axis=1).astype(I32)
    xb = _dispatch(cnt_f, lo8.reshape(-1), dst_f, extra, gap_start, gap_rows, hn, rrow, rcol, nb)
    yb = _experts(blk_e, nused, xb, w1[l], w3[l], w2[l], nb)
    used = jnp.sum(c16, axis=1).astype(I32)
    out = _combine(cnt_f, lo16.reshape(-1), dst_f, used, extra, yb, x2, rcol, g_final.reshape(1, D))
    return out.reshape(B, S, D)
```

```python
import functools
import math

import numpy as np
import jax
import jax.numpy as jnp
from jax import lax
from jax.experimental import pallas as pl
from jax.experimental.pallas import tpu as pltpu

F32 = jnp.float32
BF16 = jnp.bfloat16
I32 = jnp.int32

D_MODEL = 1024
DA_HEADS = 4
DA_HEAD_DIM = 64
DA_V_DIM = 128
DA_WIDTH = 512
ML_HEADS = 4
ML_WIDTH = 512
ML_HEAD_DIM = 128
ML_CHUNK = 128
ML_GROUP = 4
CONV_K = 4
ROPE_THETA = 10000.0
RMS_EPS = 1e-6
N_GROUPS = 4
EXPERTS_PER_GROUP = 8
N_EXPERTS = 32
D_EXPERT = 512
LAMBDA_INIT = 0.8 - 0.6 * math.exp(-0.3 * 0)

LANES = 128
SUB = 8
N_MAIN = 7 * 512
NEG = -1e30

TM_PROJ = 1024
TQ = 256
TK = 512
VT_ROWS = DA_V_DIM + 16
TM = 256
BLK = 512
ESUB = 256
CBLK = 256
DCH = 32
CCH = 32
DLOC_ROWS = 768
N_GAPS = 2 * N_EXPERTS + 1
LOC_COMMON = N_EXPERTS * CCH
LOC_ROWS = 2 * TM + N_EXPERTS * CCH
XW = D_MODEL // 2 + LANES
EXP_ROW0 = 4
RT_ROWS = 40
RT_GROUP = 4
R_E0, R_E1, R_POS0, R_POS1, R_PAD0, R_PAD1, R_W0, R_W1 = range(8)


def _nt_dot(a, b):
    return lax.dot_general(a, b, (((1,), (1,)), ((), ())), preferred_element_type=F32)


def _tn_dot(a, b):
    return lax.dot_general(a, b, (((0,), (0,)), ((), ())), preferred_element_type=F32)


def _dot(a, b):
    return jnp.dot(a, b, preferred_element_type=F32)


def _split3(x):
    hi = x.astype(BF16)
    r = x - hi.astype(F32)
    mid = r.astype(BF16)
    lo = (r - mid.astype(F32)).astype(BF16)
    return hi, mid, lo


def _rms(x, g):
    return x * lax.rsqrt(jnp.mean(x * x, axis=-1, keepdims=True) + RMS_EPS) * g


def _inproj_kernel(x_ref, g_ref, w_ref, wg_ref, wgt_ref, c_ref, sa_ref, sb_ref, cw_ref, cb_ref,
                   q1_ref, q2_ref, kr_ref, va_ref, qc_ref, kc_ref, vm_ref, om_ref, gcol_ref, grow_ref,
                   h_s, cq_s, ck_s, *, tiles_per_seq):
    tm = x_ref.shape[0]
    cw = 512
    first = (pl.program_id(0) % tiles_per_seq) == 0

    @pl.when(first)
    def _():
        cq_s[0:8, :] = jnp.zeros((8, cw), F32)
        ck_s[0:8, :] = jnp.zeros((8, cw), F32)

    @pl.when(jnp.logical_not(first))
    def _():
        cq_s[0:8, :] = cq_s[tm:tm + 8, :]
        ck_s[0:8, :] = ck_s[tm:tm + 8, :]

    h_s[...] = _rms(x_ref[...], g_ref[...]).astype(BF16)
    nhalf = 2 if tm >= 512 else 1
    th = tm // nhalf
    halves = [slice(p * th, (p + 1) * th) for p in range(nhalf)]

    def cols(c, hv):
        return _dot(h_s[hv, :], w_ref[:, c * cw:(c + 1) * cw])

    rb = min(128, th)
    lane = lax.broadcasted_iota(I32, (rb, LANES), 1)
    qscale = DA_HEAD_DIM ** -0.5 * math.log2(math.e)

    def rope(x, rows):
        return (x * c_ref[rows, :] + pltpu.roll(x, 96, 1) * sa_ref[rows, :]
                + pltpu.roll(x, 32, 1) * sb_ref[rows, :])

    def blocks(hv):
        for r in range(th // rb):
            for hh in range(cw // LANES):
                yield (slice(hv.start + r * rb, hv.start + (r + 1) * rb), slice(r * rb, (r + 1) * rb),
                       slice(hh * LANES, (hh + 1) * LANES))

    def rope_q(qa, hv):
        for rows, loc, hs in blocks(hv):
            qr = rope(qa[loc, hs], rows) * qscale
            q1_ref[rows, hs] = jnp.where(lane < DA_HEAD_DIM, qr, 0.0).astype(BF16)
            q2_ref[rows, hs] = jnp.where(lane >= DA_HEAD_DIM, qr, 0.0).astype(BF16)

    def rope_k(ka, hv):
        for rows, loc, hs in blocks(hv):
            kr_ref[rows, hs] = rope(ka[loc, hs], rows).astype(BF16)

    def conv_silu(src, c0, hv, out_ref, scale):
        for rows, _, hs in blocks(hv):
            cs = slice(c0 + hs.start, c0 + hs.stop)
            w = cw_ref[:, cs]
            y = cb_ref[:, cs] + src[8 + rows.start:8 + rows.stop, hs] * w[3:4]
            for s in (1, 2, 3):
                y = y + src[8 + rows.start - s:8 + rows.stop - s, hs] * w[3 - s:4 - s]
            y = y * jax.nn.sigmoid(y)
            out_ref[rows, hs] = (y if scale is None else y * scale).astype(BF16)

    def to_conv(c, dst, hv):
        dst[8 + hv.start:8 + hv.stop, :] = cols(c, hv)

    qa = [cols(0, hv) for hv in halves]
    ka = []
    for p, hv in enumerate(halves):
        ka.append(cols(1, hv))
        rope_q(qa[p], hv)
    for p, hv in enumerate(halves):
        va_ref[hv, :] = cols(2, hv).astype(BF16)
        rope_k(ka[p], hv)
    for hv in halves:
        to_conv(3, cq_s, hv)
    for hv in halves:
        to_conv(4, ck_s, hv)
        conv_silu(cq_s, 0, hv, qc_ref, None)
    for hv in halves:
        vm_ref[hv, :] = cols(5, hv).astype(BF16)
        conv_silu(ck_s, ML_WIDTH, hv, kc_ref, ML_HEAD_DIM ** -0.5)
    om_ref[...] = _dot(h_s[...], w_ref[:, 6 * cw:7 * cw]).astype(BF16)
    gcol_ref[...] = _dot(h_s[...], wg_ref[...])
    grow_ref[...] = _nt_dot(wgt_ref[...], h_s[...])


def _inproj(x2d, g_mix, w_main, wg, wgt, rope_c, rope_sa, rope_sb, conv_w, conv_b, S):
    T = x2d.shape[0]
    tm = min(TM_PROJ, S)
    assert S % tm == 0
    tps = S // tm
    full = lambda shape: pl.BlockSpec(shape, lambda i: (0, 0))
    rope_spec = pl.BlockSpec((tm, LANES), lambda i: (i % tps, 0))
    col_spec = pl.BlockSpec((tm, 512), lambda i: (i, 0))
    return pl.pallas_call(
        functools.partial(_inproj_kernel, tiles_per_seq=tps),
        grid=(T // tm,),
        in_specs=[
            pl.BlockSpec((tm, D_MODEL), lambda i: (i, 0)),
            full((1, D_MODEL)), full((D_MODEL, N_MAIN)), full((D_MODEL, LANES)), full((8, D_MODEL)),
            rope_spec, rope_spec, rope_spec,
            full((CONV_K, 2 * ML_WIDTH)), full((1, 2 * ML_WIDTH)),
        ],
        out_specs=[col_spec] * 8 + [
            pl.BlockSpec((tm, LANES), lambda i: (i, 0)),
            pl.BlockSpec((8, tm), lambda i: (0, i)),
        ],
        out_shape=[jax.ShapeDtypeStruct((T, 512), BF16)] * 8 + [
            jax.ShapeDtypeStruct((T, LANES), F32),
            jax.ShapeDtypeStruct((8, T), F32),
        ],
        scratch_shapes=[pltpu.VMEM((tm, D_MODEL), BF16),
                        pltpu.VMEM((tm + 8, ML_WIDTH), F32), pltpu.VMEM((tm + 8, ML_WIDTH), F32)],
        compiler_params=pltpu.CompilerParams(
            dimension_semantics=("arbitrary",), vmem_limit_bytes=56 * 1024 * 1024),
        name="inproj",
    )(x2d, g_mix, w_main, wg, wgt, rope_c, rope_sa, rope_sb, conv_w, conv_b)


def _attn_kernel(q1_s, q2_s, k_s, v_ref, lam_ref, g_ref, o_ref, vt_s, acc_s, st_s):
    S = v_ref.shape[0]

    def prep(r, carry):
        rows = pl.ds(pl.multiple_of(r * TQ, TQ), TQ)
        for h in range(DA_HEADS):
            hs = slice(h * LANES, (h + 1) * LANES)
            vt_s[h, :DA_V_DIM, rows] = v_ref[rows, hs].astype(F32).T.astype(BF16)
        return carry

    lax.fori_loop(0, S // TQ, prep, 0)
    vt_s[:, DA_V_DIM:, :] = jnp.ones((DA_HEADS, VT_ROWS - DA_V_DIM, S), BF16)

    lq = lam_ref[...]
    lam = (jnp.exp(jnp.sum(lq[0:1] * lq[1:2], axis=-1, keepdims=True))
           - jnp.exp(jnp.sum(lq[2:3] * lq[3:4], axis=-1, keepdims=True)) + LAMBDA_INIT)

    key = lax.broadcasted_iota(I32, (TK, 2 * TQ), 0)
    qry = lax.broadcasted_iota(I32, (TK, 2 * TQ), 1)
    kq = key - jnp.where(qry >= TQ, qry - TQ, qry)

    def kvstep(qo, ko, tk, ms, causal, slot):
        out = []

        def scores(h):
            hs = slice(h * LANES, (h + 1) * LANES)
            qq = jnp.concatenate([q1_s[pl.ds(qo, TQ), hs], q2_s[pl.ds(qo, TQ), hs]], axis=0)
            st = _nt_dot(k_s[pl.ds(ko, tk), hs], qq)
            st_s[h, :tk, :] = st if causal is None else jnp.where(causal, st, NEG)

        scores(0)
        for h in range(DA_HEADS):
            if h + 1 < DA_HEADS:
                scores(h + 1)
            st = st_s[h, :tk, :]
            m_new = jnp.maximum(ms[h], jnp.max(st, axis=0, keepdims=True))
            alpha = jnp.exp2(ms[h] - m_new)
            p = jnp.exp2(st - m_new).astype(BF16)
            acc_s[slot, h] = alpha * acc_s[slot, h] + _dot(vt_s[h, :, pl.ds(ko, tk)], p)
            out.append(m_new)
        return tuple(out)

    def start(i, ko, tk, off, slot):
        acc_s[slot] = jnp.zeros(acc_s.shape[1:], F32)
        m0 = tuple(jnp.full((1, 2 * TQ), NEG, F32) for _ in range(DA_HEADS))
        return kvstep(pl.multiple_of(i * TQ, TQ), ko, tk, m0, kq[:tk] <= off, slot)

    def full_steps(i, n, ms, slot):
        qo = pl.multiple_of(i * TQ, TQ)
        lax.fori_loop(0, n, lambda j, m: kvstep(qo, pl.multiple_of(j * TK, TK), TK, m, None, slot), ms)

    def finalize(i, slot):
        qo = pl.multiple_of(i * TQ, TQ)
        for h in range(DA_HEADS):
            hs = slice(h * LANES, (h + 1) * LANES)
            acc = acc_s[slot, h]
            on = acc[:DA_V_DIM] * (1.0 / acc[DA_V_DIM:DA_V_DIM + 1])
            ot = on[:, :TQ] - lam * on[:, TQ:]
            o = _rms(ot.T, g_ref[...]) * (1.0 - LAMBDA_INIT)
            o_ref[pl.ds(qo, TQ), hs] = o.astype(BF16)

    def tile_pair(u, carry):
        ko = pl.multiple_of(u * TK, TK)
        finalize(2 * u - 1, 1)
        full_steps(2 * u, u, start(2 * u, ko, TQ, 0, 0), 0)
        finalize(2 * u, 0)
        full_steps(2 * u + 1, u, start(2 * u + 1, ko, TK, TQ, 1), 1)
        return carry

    assert TK == 2 * TQ and (S // TQ) % 2 == 0
    start(0, 0, TQ, 0, 0)
    finalize(0, 0)
    start(1, 0, TK, TQ, 1)
    lax.fori_loop(1, S // TK, tile_pair, 0)
    finalize(S // TQ - 1, 1)


def _attention(q1, q2, kr, va, lam_qk, subln_g, B, S):
    T = B * S
    full = lambda shape: pl.BlockSpec(shape, lambda b: (0, 0))
    seq = pl.BlockSpec((S, DA_WIDTH), lambda b: (b, 0))
    return pl.pallas_call(
        _attn_kernel,
        grid=(B,),
        in_specs=[seq, seq, seq, seq, full((4, DA_HEAD_DIM)), full((1, LANES))],
        out_specs=seq,
        out_shape=jax.ShapeDtypeStruct((T, DA_WIDTH), BF16),
        scratch_shapes=[
            pltpu.VMEM((DA_HEADS, VT_ROWS, S), BF16),
            pltpu.VMEM((2, DA_HEADS, VT_ROWS, 2 * TQ), F32),
            pltpu.VMEM((DA_HEADS, TK, 2 * TQ), F32),
        ],
        compiler_params=pltpu.CompilerParams(
            dimension_semantics=("arbitrary",), vmem_limit_bytes=48 * 1024 * 1024),
        name="attn",
    )(q1, q2, kr, va, lam_qk, subln_g)


def _log_sigmoid(x):
    return jnp.minimum(x, 0.0) - jnp.log(1.0 + jnp.exp(-jnp.abs(x)))


def _mlstm_kernel(qc_s, kc_s, v_ref, o_ref, gcol_ref, grow_ref, gbc_ref, gbr_ref,
                  mg_ref, out_ref, vt_s, ct_s, m_s):
    S = v_ref.shape[0]
    L = ML_CHUNK
    nc = S // L

    def transpose_v(c, carry):
        ro = pl.multiple_of(c * L, L)
        for h in range(ML_HEADS):
            hs = slice(h * ML_HEAD_DIM, (h + 1) * ML_HEAD_DIM)
            vt_s[h, :ML_HEAD_DIM, pl.ds(ro, L)] = v_ref[pl.ds(ro, L), hs].astype(F32).T.astype(BF16)
        return carry

    lax.fori_loop(0, nc, transpose_v, 0)
    vt_s[:, ML_HEAD_DIM:, :] = jnp.ones((ML_HEADS, ML_HEAD_DIM, S), BF16)

    ct_s[...] = jnp.zeros(ct_s.shape, F32)
    m_s[...] = jnp.zeros(m_s.shape, F32)

    ri = lax.broadcasted_iota(I32, (L, L), 0)
    ci = lax.broadcasted_iota(I32, (L, L), 1)
    causal_t = ri <= ci
    tril = jnp.where(ci <= ri, 1.0, 0.0).astype(BF16)
    triu = jnp.where(causal_t, 1.0, 0.0).astype(BF16)

    heads = [slice(h * ML_HEAD_DIM, (h + 1) * ML_HEAD_DIM) for h in range(ML_HEADS)]

    def gates(c):
        ro = pl.multiple_of(c * L, L)
        gc = gcol_ref[pl.ds(ro, L), :] + gbc_ref[...]
        gr = grow_ref[:, pl.ds(ro, L)] + gbr_ref[...]
        b_c = sum(_dot(tril, p) for p in _split3(_log_sigmoid(gc)))
        b_r = sum(_dot(p, triu) for p in _split3(_log_sigmoid(gr)))
        return ro, gc, gr, b_c, b_r

    def read_state(g):
        ro = g[0]
        ks, vts, kq, cq, ms = [], [], [], [], []
        for h, hs in enumerate(heads):
            q = qc_s[pl.ds(ro, L), hs]
            k = kc_s[pl.ds(ro, L), hs]
            ks.append(k)
            vts.append(vt_s[h, :, pl.ds(ro, L)])
            kq.append(_nt_dot(k, q))
            cq.append(_nt_dot(ct_s[h].astype(BF16), q))
            ms.append(m_s[h:h + 1, 0:1])
        return ks, vts, kq, cq, ms

    def update_state(g, d):
        _, _, gr, _, b_r = g
        ks, vts, _, _, ms = d
        for h in range(ML_HEADS):
            br = b_r[4 + h:5 + h, :]
            bl = br[:, L - 1:L]
            dec = bl - br + gr[h:h + 1, :]
            m_new = jnp.maximum(bl + ms[h], jnp.max(dec, axis=1, keepdims=True))
            ws = jnp.exp(dec - m_new)
            sc = jnp.exp(bl + ms[h] - m_new)
            vw = (vts[h].astype(F32) * ws).astype(BF16)
            ct_s[h] = sc * ct_s[h] + _dot(vw, ks[h])
            m_s[h:h + 1, :] = jnp.broadcast_to(m_new, (1, LANES))

    def outputs(g, d):
        ro, gc, _, b_c, b_r = g
        _, vts, kq, cq, ms = d
        for h, hs in enumerate(heads):
            br = b_r[4 + h:5 + h, :]
            a_col = b_c[:, 4 + h:5 + h] - gc[:, h:h + 1]
            dm = jnp.where(causal_t, br - a_col, NEG)
            inter = br + ms[h]
            m_row = jnp.maximum(inter, jnp.max(dm, axis=0, keepdims=True))
            sm = jnp.exp(dm - m_row) * kq[h]
            sc_in = jnp.exp(inter - m_row)
            a = _dot(vts[h], sm.astype(BF16)) + sc_in * cq[h]
            den = a[ML_HEAD_DIM:ML_HEAD_DIM + 1, :]
            hh = (a[:ML_HEAD_DIM] / jnp.maximum(jnp.abs(den), jnp.exp(-m_row))).T
            hg = hh * jax.nn.sigmoid(o_ref[pl.ds(ro, L), hs].astype(F32))
            out_ref[pl.ds(ro, L), hs] = _rms(hg, mg_ref[:, hs]).astype(BF16)

    def chunk_group(cg, carry):
        gs = [gates(ML_GROUP * cg + j) for j in range(ML_GROUP)]
        ds = []
        for g in gs:
            ds.append(read_state(g))
            update_state(g, ds[-1])
        for g, d in zip(gs, ds):
            outputs(g, d)
        return carry

    lax.fori_loop(0, nc // ML_GROUP, chunk_group, 0)


def _mlstm(qc, kc, vm, om, gcol, grow, gb_col, gb_row, mh_g, B, S):
    T = B * S
    full = lambda shape: pl.BlockSpec(shape, lambda b: (0, 0))
    seq = pl.BlockSpec((S, ML_WIDTH), lambda b: (b, 0))
    return pl.pallas_call(
        _mlstm_kernel,
        grid=(B,),
        in_specs=[
            seq, seq, seq, seq,
            pl.BlockSpec((S, LANES), lambda b: (b, 0)),
            pl.BlockSpec((8, S), lambda b: (0, b)),
            full((1, LANES)), full((8, LANES)), full((1, ML_WIDTH)),
        ],
        out_specs=seq,
        out_shape=jax.ShapeDtypeStruct((T, ML_WIDTH), BF16),
        scratch_shapes=[
            pltpu.VMEM((ML_HEADS, 2 * ML_HEAD_DIM, S), BF16),
            pltpu.VMEM((ML_HEADS, 2 * ML_HEAD_DIM, ML_HEAD_DIM), F32),
            pltpu.VMEM((8, LANES), F32),
        ],
        compiler_params=pltpu.CompilerParams(
            dimension_semantics=("arbitrary",), vmem_limit_bytes=56 * 1024 * 1024),
        name="mlstm",
    )(qc, kc, vm, om, gcol, grow, gb_col, gb_row, mh_g)


def _router_kernel(oa_ref, hm_ref, x_ref, wo_ref, g_ref, wr_ref, brt_ref,
                   x2_ref, hn_ref, rrow_ref, rcol_ref, cnt_ref):
    def project(s):
        rs = slice(s * TM, (s + 1) * TM)
        mixo = _dot(oa_ref[rs, :], wo_ref[:DA_WIDTH, :]) + _dot(hm_ref[rs, :], wo_ref[DA_WIDTH:, :])
        x2 = x_ref[rs, :] + mixo
        x2_ref[rs, :] = x2
        hn = _rms(x2, g_ref[...])
        hn_hi = hn.astype(BF16)
        hn_ref[rs, :] = hn_hi
        hn_lo = (hn - hn_hi.astype(F32)).astype(BF16)
        a = _dot(hn_hi, wr_ref[...])
        b = _dot(hn_lo, wr_ref[:, :LANES])
        return (a[:, :LANES] + a[:, LANES:] + b).T[:RT_ROWS] + brt_ref[:RT_ROWS, :]

    lts = [project(0)]
    for s in range(RT_GROUP):
        if s + 1 < RT_GROUP:
            lts.append(project(s + 1))
        _route(s, lts[s], rrow_ref, rcol_ref, cnt_ref)


def _route(s, lt, rrow_ref, rcol_ref, cnt_ref):
    sub = lax.broadcasted_iota(I32, (RT_ROWS, TM), 0)
    sub_f = sub.astype(F32)

    def cmax(v):
        return jnp.max(v, axis=0, keepdims=True)

    def first_idx(mask):
        return jnp.min(jnp.where(mask, sub_f, 1e6), axis=0, keepdims=True).astype(I32)

    gl = jnp.where(sub < N_GROUPS, lt, NEG)
    gmax = cmax(gl)
    gsel = first_idx(gl == gmax)
    g_w = 1.0 / jnp.sum(jnp.exp(gl - gmax), axis=0, keepdims=True)
    elo = EXP_ROW0 + gsel * EXPERTS_PER_GROUP
    el = jnp.where((sub >= elo) & (sub < elo + EXPERTS_PER_GROUP), lt, NEG)
    v1 = cmax(el)
    i1 = first_idx(el == v1)
    el2 = jnp.where(sub == i1, NEG, el)
    v2 = cmax(el2)
    i2 = first_idx(el2 == v2)
    t = jnp.exp(v2 - v1)
    w0 = g_w / (1.0 + t)
    w1 = g_w * t / (1.0 + t)

    oh0 = jnp.where(sub == i1, 1.0, 0.0)
    oh1 = jnp.where(sub == i2, 1.0, 0.0)
    mh = oh0 + oh1
    r_i = lax.broadcasted_iota(I32, (TM, TM), 0)
    c_i = lax.broadcasted_iota(I32, (TM, TM), 1)
    before = jnp.where(r_i < c_i, 1.0, 0.0).astype(BF16)
    mh_f = jnp.concatenate([mh, jnp.zeros((LANES - RT_ROWS, TM), F32)], axis=0)
    mh_b = mh_f.astype(BF16)
    pre = _dot(mh_b, before)[:RT_ROWS]
    cnt = jnp.sum(mh_f, axis=1, keepdims=True)
    cnt8 = jnp.floor((cnt + (SUB - 1)) * (1.0 / SUB)) * SUB
    erow = lax.broadcasted_iota(I32, (LANES, 1), 0)
    is_exp = (erow >= EXP_ROW0) & (erow < EXP_ROW0 + N_EXPERTS)
    cnt16 = jnp.where(is_exp, jnp.maximum(jnp.floor((cnt + (CCH - 1)) * (1.0 / CCH)), 1.0) * CCH, 0.0)
    e_r = lax.broadcasted_iota(I32, (LANES, LANES), 0)
    e_c = lax.broadcasted_iota(I32, (LANES, LANES), 1)
    below = jnp.where(e_c < e_r, 1.0, 0.0).astype(BF16)
    lo8 = _dot(below, jnp.broadcast_to(cnt8, (LANES, LANES)).astype(BF16))[:RT_ROWS, 0:1]
    lo16 = _dot(below, jnp.broadcast_to(cnt16, (LANES, LANES)).astype(BF16))[:RT_ROWS, 0:1]

    def csum(v):
        return jnp.sum(v, axis=0, keepdims=True)

    rows = [None] * 8
    rows[R_E0] = (i1 - EXP_ROW0).astype(F32)
    rows[R_E1] = (i2 - EXP_ROW0).astype(F32)
    rows[R_POS0] = csum(oh0 * (pre + lo8))
    rows[R_POS1] = csum(oh1 * (pre + lo8))
    rows[R_PAD0] = csum(oh0 * (pre + lo16))
    rows[R_PAD1] = csum(oh1 * (pre + lo16))
    rows[R_W0] = w0
    rows[R_W1] = w1
    sub128 = lax.broadcasted_iota(I32, (LANES, TM), 0)
    r128 = jnp.zeros((LANES, TM), F32)
    for j, v in enumerate(rows):
        r128 = jnp.where(sub128 == j, v, r128)
    rrow_ref[:, s * TM:(s + 1) * TM] = r128[:8]
    rcol_ref[s * TM:(s + 1) * TM, :] = r128.T
    cnt_ref[s] = _nt_dot(jnp.ones((8, TM), BF16), mh_b)[0:1].astype(I32)


def _router(oa, hm, x2d, w_out, g_ffn, wr_t, b_rt):
    T = x2d.shape[0]
    nt = T // TM
    full = lambda shape: pl.BlockSpec(shape, lambda i: (0, 0))
    tg = RT_GROUP * TM
    return pl.pallas_call(
        _router_kernel,
        grid=(T // tg,),
        in_specs=[
            pl.BlockSpec((tg, DA_WIDTH), lambda i: (i, 0)),
            pl.BlockSpec((tg, ML_WIDTH), lambda i: (i, 0)),
            pl.BlockSpec((tg, D_MODEL), lambda i: (i, 0)),
            full((D_MODEL, D_MODEL)), full((1, D_MODEL)),
            full((D_MODEL, 2 * LANES)), full((LANES, TM)),
        ],
        out_specs=[
            pl.BlockSpec((tg, D_MODEL), lambda i: (i, 0)),
            pl.BlockSpec((tg, D_MODEL), lambda i: (i, 0)),
            pl.BlockSpec((8, tg), lambda i: (0, i)),
            pl.BlockSpec((tg, LANES), lambda i: (i, 0)),
            pl.BlockSpec((RT_GROUP, 1, LANES), lambda i: (i, 0, 0)),
        ],
        out_shape=[
            jax.ShapeDtypeStruct((T, D_MODEL), F32),
            jax.ShapeDtypeStruct((T, D_MODEL), BF16),
            jax.ShapeDtypeStruct((8, T), F32),
            jax.ShapeDtypeStruct((T, LANES), F32),
            jax.ShapeDtypeStruct((nt, 1, LANES), I32),
        ],
        compiler_params=pltpu.CompilerParams(
            dimension_semantics=("arbitrary",), vmem_limit_bytes=40 * 1024 * 1024),
        name="router",
    )(oa, hm, x2d, w_out, g_ffn, wr_t, b_rt)


def _pack_halves(x):
    lo = lax.bitcast_convert_type(x[:, :D_MODEL // 2], I32)
    hi = lax.bitcast_convert_type(x[:, D_MODEL // 2:], I32)
    return lax.shift_right_logical(lo, 16) | (hi & jnp.int32(-65536))


def _unpack_halves(w):
    lo = lax.bitcast_convert_type(lax.shift_left(w, 16), F32)
    hi = lax.bitcast_convert_type(w & jnp.int32(-65536), F32)
    return lo.astype(BF16), hi.astype(BF16)


def _dispatch_kernel(cnt_sm, lo_sm, dst_sm, extra_sm, gs_sm, gn_sm, hn_ref, rrow_ref, rcol_ref, xb_ref,
                     xs_s, z_s, semx, semx2, semz):
    i = pl.program_id(0)
    nt = pl.num_programs(0)
    slot = i % 2
    half = D_MODEL // 2

    def zcopy(rows, dst):
        return pltpu.make_async_copy(z_s.at[pl.ds(0, rows), :], xb_ref.at[pl.ds(dst, rows), :], semz)

    def gap_fill(wait):
        def ebody(e, carry):
            start = pl.multiple_of(gs_sm[e], SUB)
            left = gn_sm[e]
            for rows in (BLK, DCH, SUB):
                n = left // rows

                def body(c, cc, rows=rows, start=start):
                    cp = zcopy(rows, pl.multiple_of(start + c * rows, SUB))
                    cp.wait() if wait else cp.start()
                    return cc

                lax.fori_loop(0, n, body, 0)
                start = start + n * rows
                left = left - n * rows
            return carry

        lax.fori_loop(0, N_GAPS, ebody, 0)

    @pl.when(i == 0)
    def _():
        for s in range(2):
            xs_s[s, DLOC_ROWS:, :] = jnp.zeros((DCH, XW), I32)
        z_s[...] = jnp.zeros(z_s.shape, I32)
        gap_fill(False)
        gap_fill(True)

    def xcopy(s, src, dst, sem):
        return pltpu.make_async_copy(xs_s.at[s, pl.ds(src, DCH), :], xb_ref.at[pl.ds(dst, DCH), :],
                                     sem.at[s])

    def wait_tile(s, n_extra):
        for _ in range(N_EXPERTS):
            xcopy(s, 0, 0, semx).wait()

        def wbody(j, c):
            xcopy(s, 0, 0, semx2).wait()
            return c
        lax.fori_loop(0, n_extra, wbody, 0)

    @pl.when(i >= 2)
    def _():
        wait_tile(slot, extra_sm[jnp.maximum(i - 2, 0)])

    rr = rrow_ref[...]
    rc = rcol_ref[...]
    r_i = lax.broadcasted_iota(I32, (DLOC_ROWS, TM), 0).astype(F32)
    perm0 = jnp.where(r_i == rr[R_POS0:R_POS0 + 1, :], 1.0, 0.0).astype(BF16)
    perm1 = jnp.where(r_i == rr[R_POS1:R_POS1 + 1, :], 1.0, 0.0).astype(BF16)
    xs = _dot(perm0 + perm1, hn_ref[...])
    xs_s[slot, :DLOC_ROWS, :half] = _pack_halves(xs)
    lane = lax.broadcasted_iota(I32, (TM, LANES), 1)

    def parts(col):
        hi, mid, lo = (p.astype(F32) for p in _split3(jnp.broadcast_to(rc[:, col:col + 1], (TM, LANES))))
        sel = jnp.where(lane == 0, hi, jnp.where(lane == 1, mid, jnp.where(lane == 2, lo, 0.0)))
        return sel.astype(BF16)

    wparts = _dot(perm0, parts(R_W0)) + _dot(perm1, parts(R_W1))
    l_r = lax.broadcasted_iota(I32, (LANES, LANES), 0)
    ones3 = jnp.where(l_r < 3, 1.0, 0.0).astype(BF16)
    wsort = _dot(wparts.astype(BF16), ones3)
    xs_s[slot, :DLOC_ROWS, half:] = lax.bitcast_convert_type(wsort, I32)

    for e in range(N_EXPERTS):
        lo = pl.multiple_of(lo_sm[i * N_EXPERTS + e], SUB)
        dst = pl.multiple_of(dst_sm[i * N_EXPERTS + e], SUB)
        xcopy(slot, lo, dst, semx).start()

    @pl.when(extra_sm[i] > 0)
    def _():
        def ebody(e, carry):
            n = cnt_sm[i * N_EXPERTS + e]
            lo = pl.multiple_of(lo_sm[i * N_EXPERTS + e], SUB)
            dst = pl.multiple_of(dst_sm[i * N_EXPERTS + e], SUB)

            def cbody(c, cc):
                xcopy(slot, lo + c * DCH, dst + c * DCH, semx2).start()
                return cc

            return lax.fori_loop(1, (n + DCH - 1) // DCH, cbody, carry)

        lax.fori_loop(0, N_EXPERTS, ebody, 0)

    @pl.when(i == nt - 1)
    def _():
        @pl.when(nt >= 2)
        def _():
            wait_tile(1 - slot, extra_sm[jnp.maximum(i - 1, 0)])
        wait_tile(slot, extra_sm[i])


def _dispatch(cnt_f, lo_f, dst_f, extra, gap_start, gap_rows, hn, rrow, rcol, nb):
    T = hn.shape[0]
    nt = T // TM
    grid_spec = pltpu.PrefetchScalarGridSpec(
        num_scalar_prefetch=6,
        grid=(nt,),
        in_specs=[
            pl.BlockSpec((TM, D_MODEL), lambda i, *_: (i, 0)),
            pl.BlockSpec((8, TM), lambda i, *_: (0, i)),
            pl.BlockSpec((TM, LANES), lambda i, *_: (i, 0)),
        ],
        out_specs=pl.BlockSpec(memory_space=pl.ANY),
        scratch_shapes=[
            pltpu.VMEM((2, DLOC_ROWS + DCH, XW), I32),
            pltpu.VMEM((BLK, XW), I32),
            pltpu.SemaphoreType.DMA((2,)), pltpu.SemaphoreType.DMA((2,)), pltpu.SemaphoreType.DMA(()),
        ],
    )
    return pl.pallas_call(
        _dispatch_kernel,
        grid_spec=grid_spec,
        out_shape=jax.ShapeDtypeStruct((nb * BLK, XW), I32),
        compiler_params=pltpu.CompilerParams(
            dimension_semantics=("arbitrary",), vmem_limit_bytes=40 * 1024 * 1024),
        name="dispatch",
    )(cnt_f, lo_f, dst_f, extra, gap_start, gap_rows, hn, rrow, rcol)


def _expert_kernel(be_sm, nu_sm, xb_ref, w1_ref, w3_ref, w2_ref, yb_ref, w1_s, w3_s, w2_s):
    p = pl.program_id(0)
    used = p < nu_sm[0]

    @pl.when(used & ((p == 0) | (be_sm[p] != be_sm[jnp.maximum(p - 1, 0)])))
    def _():
        w1_s[...] = w1_ref[...].astype(BF16)
        w3_s[...] = w3_ref[...].astype(BF16)
        w2_s[...] = w2_ref[...].astype(BF16)

    @pl.when(used)
    def _():
        half = D_MODEL // 2
        for r in range(BLK // ESUB):
            rs = slice(r * ESUB, (r + 1) * ESUB)
            xlo, xhi = _unpack_halves(xb_ref[rs, :half])
            h1 = _dot(xlo, w1_s[:half, :]) + _dot(xhi, w1_s[half:, :])
            h3 = _dot(xlo, w3_s[:half, :]) + _dot(xhi, w3_s[half:, :])
            wrep = lax.bitcast_convert_type(xb_ref[rs, half:], F32)
            wfull = jnp.concatenate([wrep] * (D_EXPERT // LANES), axis=1)
            hdn = (h1 * jax.nn.sigmoid(h1) * h3 * wfull).astype(BF16)
            y = _dot(hdn, w2_s[...]).astype(BF16).astype(F32)
            yb_ref[rs, :] = _pack_halves(y)

    @pl.when(jnp.logical_not(used))
    def _():
        yb_ref[...] = jnp.zeros(yb_ref.shape, I32)


def _experts(blk_e, nused, xb, w1, w3, w2, nb):
    def rows(p, be, nu):
        return (jnp.minimum(p, nu[0] - 1), 0)

    def wsel(p, be, nu):
        return (be[jnp.minimum(p, nu[0] - 1)], 0, 0)

    grid_spec = pltpu.PrefetchScalarGridSpec(
        num_scalar_prefetch=2,
        grid=(nb,),
        in_specs=[
            pl.BlockSpec((BLK, XW), rows),
            pl.BlockSpec((None, D_MODEL, D_EXPERT), wsel),
            pl.BlockSpec((None, D_MODEL, D_EXPERT), wsel),
            pl.BlockSpec((None, D_EXPERT, D_MODEL), wsel),
        ],
        out_specs=pl.BlockSpec((BLK, D_MODEL // 2), lambda p, be, nu: (p, 0)),
        scratch_shapes=[
            pltpu.VMEM((D_MODEL, D_EXPERT), BF16), pltpu.VMEM((D_MODEL, D_EXPERT), BF16),
            pltpu.VMEM((D_EXPERT, D_MODEL), BF16),
        ],
    )
    return pl.pallas_call(
        _expert_kernel,
        grid_spec=grid_spec,
        out_shape=jax.ShapeDtypeStruct((nb * BLK, D_MODEL // 2), I32),
        compiler_params=pltpu.CompilerParams(
            dimension_semantics=("arbitrary",), vmem_limit_bytes=48 * 1024 * 1024),
        name="experts",
    )(blk_e, nused, xb, w1, w3, w2)


def _combine_kernel(cnt_sm, lo_sm, dst_sm, used_sm, extra_sm, yb_ref, x2_ref, rcol_ref, g_ref, out_ref,
                    yl_s, y_s, sem, sem2):
    i = pl.program_id(0)
    nt = pl.num_programs(0)
    slot = i % 2

    def ycopy(s, src, dst, sm):
        return pltpu.make_async_copy(yb_ref.at[pl.ds(src, CCH), :], yl_s.at[s, pl.ds(dst, CCH), :],
                                     sm.at[s])

    def issue_first(tile, s, experts):
        for e in experts:
            lo = pl.multiple_of(lo_sm[tile * N_EXPERTS + e], CCH)
            src = pl.multiple_of(dst_sm[tile * N_EXPERTS + e], SUB)
            ycopy(s, src, lo, sem).start()

    def issue_extra(tile, s, enable):
        @pl.when(enable & (extra_sm[tile] > 0))
        def _():
            def ebody(e, carry):
                n = cnt_sm[tile * N_EXPERTS + e]
                lo = pl.multiple_of(lo_sm[tile * N_EXPERTS + e], CCH)
                src = pl.multiple_of(dst_sm[tile * N_EXPERTS + e], SUB)

                def cbody(c, cc):
                    ycopy(s, src + c * CCH, lo + c * CCH, sem2).start()
                    return cc

                return lax.fori_loop(1, (n + CCH - 1) // CCH, cbody, carry)

            lax.fori_loop(0, N_EXPERTS, ebody, 0)

    @pl.when(i == 0)
    def _():
        yl_s[...] = jnp.zeros(yl_s.shape, I32)
        issue_first(0, 0, range(N_EXPERTS))
        issue_extra(0, 0, True)

    nxt = jnp.minimum(i + 1, nt - 1)

    for _ in range(N_EXPERTS):
        ycopy(slot, 0, 0, sem).wait()

    def wbody(j, c):
        ycopy(slot, 0, 0, sem2).wait()
        return c

    lax.fori_loop(0, extra_sm[i], wbody, 0)

    rc = rcol_ref[...]
    pad0 = rc[:, R_PAD0:R_PAD0 + 1]
    pad1 = rc[:, R_PAD1:R_PAD1 + 1]
    lane = lax.broadcasted_iota(I32, (TM, CBLK), 1).astype(F32)

    def chunk(c):
        r = lane + float(c * CBLK)
        selm = (jnp.where(pad0 == r, 1.0, 0.0) + jnp.where(pad1 == r, 1.0, 0.0)).astype(BF16)
        lo_h, hi_h = _unpack_halves(yl_s[slot, c * CBLK:(c + 1) * CBLK, :])
        return _dot(selm, lo_h), _dot(selm, hi_h)

    half = D_MODEL // 2
    ylo = jnp.zeros((TM, half), F32)
    yhi = jnp.zeros((TM, half), F32)
    ncommon = LOC_COMMON // CBLK
    per = N_EXPERTS // ncommon
    for c in range(ncommon):
        dlo, dhi = chunk(c)
        ylo = ylo + dlo
        yhi = yhi + dhi
        issue_first(nxt, 1 - slot, range(c * per, (c + 1) * per))
    y_s[:, :half] = ylo
    y_s[:, half:] = yhi
    issue_extra(nxt, 1 - slot, i + 1 < nt)
    for c in range(LOC_COMMON // CBLK, LOC_ROWS // CBLK):
        @pl.when(used_sm[i] > c * CBLK)
        def _(c=c):
            dlo, dhi = chunk(c)
            y_s[:, :half] += dlo
            y_s[:, half:] += dhi
    out_ref[...] = _rms(x2_ref[...] + y_s[...], g_ref[...])

    @pl.when(i == nt - 1)
    def _():
        for _ in range(N_EXPERTS):
            ycopy(1 - slot, 0, 0, sem).wait()


def _combine(cnt_f, lo16_f, dst_f, used, extra, yb, x2, rcol, g_final):
    T = x2.shape[0]
    nt = T // TM
    grid_spec = pltpu.PrefetchScalarGridSpec(
        num_scalar_prefetch=5,
        grid=(nt,),
        in_specs=[
            pl.BlockSpec(memory_space=pl.ANY),
            pl.BlockSpec((TM, D_MODEL), lambda i, *_: (i, 0)),
            pl.BlockSpec((TM, LANES), lambda i, *_: (i, 0)),
            pl.BlockSpec((1, D_MODEL), lambda i, *_: (0, 0)),
        ],
        out_specs=pl.BlockSpec((TM, D_MODEL), lambda i, *_: (i, 0)),
        scratch_shapes=[
            pltpu.VMEM((2, LOC_ROWS, D_MODEL // 2), I32),
            pltpu.VMEM((TM, D_MODEL), F32),
            pltpu.SemaphoreType.DMA((2,)), pltpu.SemaphoreType.DMA((2,)),
        ],
    )
    return pl.pallas_call(
        _combine_kernel,
        grid_spec=grid_spec,
        out_shape=jax.ShapeDtypeStruct((T, D_MODEL), F32),
        compiler_params=pltpu.CompilerParams(
            dimension_semantics=("arbitrary",), vmem_limit_bytes=40 * 1024 * 1024),
        name="combine",
    )(cnt_f, lo16_f, dst_f, used, extra, yb, x2, rcol, g_final)


def _rope_tables(S):
    half = DA_HEAD_DIM // 2
    inv = (1.0 / (np.float32(ROPE_THETA) ** (np.arange(0, DA_HEAD_DIM, 2, dtype=np.float32)
                                             / np.float32(DA_HEAD_DIM)))).astype(np.float32)
    ang = (np.arange(S, dtype=np.float32)[:, None] * inv[None, :]).astype(np.float32)
    cos, sin = np.cos(ang), np.sin(ang)
    lane = np.arange(LANES)
    idx = lane % half
    lower = (lane % DA_HEAD_DIM) < half
    c = cos[:, idx]
    s = sin[:, idx]
    z = np.zeros_like(s)
    return (jnp.asarray(c, F32), jnp.asarray(np.where(lower[None, :], -s, z), F32),
            jnp.asarray(np.where(lower[None, :], z, s), F32))


def _tri_dot(a, b):
    return jnp.dot(a.astype(F32), b.astype(F32), precision=lax.Precision.HIGHEST).astype(I32)


def kernel(x, w_in, conv_w, conv_b, gate_b, lam_qk, subln_g, mhnorm_g, w_out, g_mix, g_ffn, w_grp,
           b_grp, w_erouter, b_erouter, w1, w3, w2, g_final):
    B, S, D = x.shape
    T = B * S
    nt = T // TM
    x2d = x.reshape(T, D)
    l = 0

    w_main = w_in[l, :, :N_MAIN].astype(BF16)
    wg8 = w_in[l, :, N_MAIN:]
    wg = jnp.pad(wg8, ((0, 0), (0, LANES - 8))).astype(BF16)
    wgt = wg8.T.astype(BF16)
    gb8 = gate_b[l].reshape(8)
    gb_col = jnp.pad(gb8, (0, LANES - 8)).reshape(1, LANES)
    gb_row = jnp.broadcast_to(gb8[:, None], (8, LANES))
    rope_c, rope_sa, rope_sb = _rope_tables(S)
    w_r = jnp.concatenate(
        [w_grp[l], w_erouter[l].transpose(1, 0, 2).reshape(D, N_EXPERTS)], axis=1)
    w_r = jnp.pad(w_r, ((0, 0), (0, LANES - w_r.shape[1])))
    wr_hi = w_r.astype(BF16)
    wr_t = jnp.concatenate([wr_hi, (w_r - wr_hi.astype(F32)).astype(BF16)], axis=1)
    b_r = jnp.pad(jnp.concatenate([b_grp[l], b_erouter[l].reshape(N_EXPERTS)]),
                  (0, LANES - N_GROUPS - N_EXPERTS))
    b_rt = jnp.broadcast_to(b_r[:, None], (LANES, TM))

    q1, q2, kr, va, qc, kc, vm, om, gcol, grow = _inproj(
        x2d, g_mix[l].reshape(1, D), w_main, wg, wgt, rope_c, rope_sa, rope_sb,
        conv_w[l], conv_b[l].reshape(1, -1), S)
    oa = _attention(q1, q2, kr, va, lam_qk[l], subln_g[l].reshape(1, LANES), B, S)
    hm = _mlstm(qc, kc, vm, om, gcol, grow, gb_col, gb_row, mhnorm_g[l].reshape(1, ML_WIDTH), B, S)
    x2, hn, rrow, rcol, cnt3 = _router(oa, hm, x2d, w_out[l].astype(BF16), g_ffn[l].reshape(1, D),
                                       wr_t, b_rt)

    cnt = cnt3[:, 0, EXP_ROW0:EXP_ROW0 + N_EXPERTS]
    c8 = (cnt + SUB - 1) // SUB * SUB
    c16 = jnp.maximum((cnt + CCH - 1) // CCH, 1) * CCH
    odd = (np.arange(nt) % 2)[:, None]
    c8_even = c8 * (1 - odd)
    c8_odd = c8 * odd
    even_tot = jnp.sum(c8_even, axis=0)
    count8 = even_tot + DCH + jnp.sum(c8_odd, axis=0)
    pcount = (count8 + DCH + BLK - 1) // BLK * BLK
    up_e = np.triu(np.ones((N_EXPERTS, N_EXPERTS), np.float32), 1)
    lo_t = np.tril(np.ones((nt, nt), np.float32), -1)
    pstart = _tri_dot(pcount[None, :], up_e)[0]
    pend = pstart + pcount
    dst = pstart[None, :] + jnp.where(odd == 0, _tri_dot(lo_t, c8_even),
                                      (even_tot + DCH)[None, :] + _tri_dot(lo_t, c8_odd))
    lo8 = _tri_dot(c8, up_e)
    lo16 = _tri_dot(c16, up_e)
    nb = (2 * T + nt * N_EXPERTS * (SUB - 1) + N_EXPERTS * (2 * DCH + BLK - 1)) // BLK + 1
    blk_row = jnp.arange(nb, dtype=I32) * BLK
    blk_e = jnp.minimum(jnp.sum((pend[None, :] <= blk_row[:, None]).astype(I32), axis=1),
                        N_EXPERTS - 1).astype(I32)
    nused = (pend[-1] // BLK).astype(I32).reshape(1)
    cnt_f = cnt.reshape(-1).astype(I32)
    dst_f = dst.reshape(-1).astype(I32)

    gap_start = jnp.concatenate([pstart + even_tot, pstart + count8, pend[-1:]]).astype(I32)
    gap_rows = jnp.concatenate([jnp.full((N_EXPERTS,), DCH, I32), pcount - count8,
                                nb * BLK - pend[-1:]]).astype(I32)
    extra = jnp.sum(jnp.maximum((cnt + DCH - 1) // DCH - 1, 0), axis=1).astype(I32)
    xb = _dispatch(cnt_f, lo8.reshape(-1), dst_f, extra, gap_start, gap_rows, hn, rrow, rcol, nb)
    yb = _experts(blk_e, nused, xb, w1[l], w3[l], w2[l], nb)
    used = jnp.sum(c16, axis=1).astype(I32)
    out = _combine(cnt_f, lo16.reshape(-1), dst_f, used, extra, yb, x2, rcol, g_final.reshape(1, D))
    return out.reshape(B, S, D)
```

```python
import functools
import math

import numpy as np
import jax
import jax.numpy as jnp
from jax import lax
from jax.experimental import pallas as pl
from jax.experimental.pallas import tpu as pltpu

F32 = jnp.float32
BF16 = jnp.bfloat16
I32 = jnp.int32

D_MODEL = 1024
DA_HEADS = 4
DA_HEAD_DIM = 64
DA_V_DIM = 128
DA_WIDTH = 512
ML_HEADS = 4
ML_WIDTH = 512
ML_HEAD_DIM = 128
ML_CHUNK = 128
ML_GROUP = 4
CONV_K = 4
ROPE_THETA = 10000.0
RMS_EPS = 1e-6
N_GROUPS = 4
EXPERTS_PER_GROUP = 8
N_EXPERTS = 32
D_EXPERT = 512
LAMBDA_INIT = 0.8 - 0.6 * math.exp(-0.3 * 0)

LANES = 128
SUB = 8
N_MAIN = 7 * 512
NEG = -1e30

TM_PROJ = 1024
TQ = 256
TK = 512
VT_ROWS = DA_V_DIM + 16
TM = 256
BLK = 512
ESUB = 256
CBLK = 256
CTILES = 2
DCH = 32
CCH = 32
DLOC_ROWS = 768
N_GAPS_NEAR = 2 * N_EXPERTS
N_GAPS = N_GAPS_NEAR + N_EXPERTS + 1
LOC_COMMON = N_EXPERTS * CCH
LOC_ROWS = 2 * TM + N_EXPERTS * CCH
XW = D_MODEL // 2 + LANES
EXP_ROW0 = 4
RT_ROWS = 40
RT_GROUP = 4
R_E0, R_E1, R_POS0, R_POS1, R_PAD0, R_PAD1, R_W0, R_W1 = range(8)


def _nt_dot(a, b):
    return lax.dot_general(a, b, (((1,), (1,)), ((), ())), preferred_element_type=F32)


def _tn_dot(a, b):
    return lax.dot_general(a, b, (((0,), (0,)), ((), ())), preferred_element_type=F32)


def _dot(a, b):
    return jnp.dot(a, b, preferred_element_type=F32)


def _split3(x):
    hi = x.astype(BF16)
    r = x - hi.astype(F32)
    mid = r.astype(BF16)
    lo = (r - mid.astype(F32)).astype(BF16)
    return hi, mid, lo


def _rms(x, g):
    return x * lax.rsqrt(jnp.mean(x * x, axis=-1, keepdims=True) + RMS_EPS) * g


def _inproj_kernel(x_ref, g_ref, w_ref, wg_ref, wgt_ref, c_ref, sa_ref, sb_ref, cw_ref, cb_ref,
                   q1_ref, q2_ref, kr_ref, va_ref, qc_ref, kc_ref, vm_ref, om_ref, gcol_ref, grow_ref,
                   h_s, cq_s, ck_s, *, tiles_per_seq):
    tm = x_ref.shape[0]
    cw = 512
    first = (pl.program_id(0) % tiles_per_seq) == 0

    @pl.when(first)
    def _():
        cq_s[0:8, :] = jnp.zeros((8, cw), F32)
        ck_s[0:8, :] = jnp.zeros((8, cw), F32)

    @pl.when(jnp.logical_not(first))
    def _():
        cq_s[0:8, :] = cq_s[tm:tm + 8, :]
        ck_s[0:8, :] = ck_s[tm:tm + 8, :]

    h_s[...] = _rms(x_ref[...], g_ref[...]).astype(BF16)
    nhalf = 2 if tm >= 512 else 1
    th = tm // nhalf
    halves = [slice(p * th, (p + 1) * th) for p in range(nhalf)]

    def cols(c, hv):
        return _dot(h_s[hv, :], w_ref[:, c * cw:(c + 1) * cw])

    rb = min(128, th)
    lane = lax.broadcasted_iota(I32, (rb, LANES), 1)
    qscale = DA_HEAD_DIM ** -0.5 * math.log2(math.e)

    def rope(x, rows):
        return (x * c_ref[rows, :] + pltpu.roll(x, 96, 1) * sa_ref[rows, :]
                + pltpu.roll(x, 32, 1) * sb_ref[rows, :])

    def blocks(hv):
        for r in range(th // rb):
            for hh in range(cw // LANES):
                yield (slice(hv.start + r * rb, hv.start + (r + 1) * rb), slice(r * rb, (r + 1) * rb),
                       slice(hh * LANES, (hh + 1) * LANES))

    def rope_q(qa, hv):
        for rows, loc, hs in blocks(hv):
            qr = rope(qa[loc, hs], rows) * qscale
            q1_ref[rows, hs] = jnp.where(lane < DA_HEAD_DIM, qr, 0.0).astype(BF16)
            q2_ref[rows, hs] = jnp.where(lane >= DA_HEAD_DIM, qr, 0.0).astype(BF16)

    def rope_k(ka, hv):
        for rows, loc, hs in blocks(hv):
            kr_ref[rows, hs] = rope(ka[loc, hs], rows).astype(BF16)

    def conv_silu(src, c0, hv, out_ref, scale):
        for rows, _, hs in blocks(hv):
            cs = slice(c0 + hs.start, c0 + hs.stop)
            w = cw_ref[:, cs]
            y = cb_ref[:, cs] + src[8 + rows.start:8 + rows.stop, hs] * w[3:4]
            for s in (1, 2, 3):
                y = y + src[8 + rows.start - s:8 + rows.stop - s, hs] * w[3 - s:4 - s]
            y = y * jax.nn.sigmoid(y)
            out_ref[rows, hs] = (y if scale is None else y * scale).astype(BF16)

    def to_conv(c, dst, hv):
        dst[8 + hv.start:8 + hv.stop, :] = cols(c, hv)

    qa = [cols(0, hv) for hv in halves]
    ka = []
    for p, hv in enumerate(halves):
        ka.append(cols(1, hv))
        rope_q(qa[p], hv)
    for p, hv in enumerate(halves):
        va_ref[hv, :] = cols(2, hv).astype(BF16)
        rope_k(ka[p], hv)
    for hv in halves:
        to_conv(3, cq_s, hv)
    for hv in halves:
        to_conv(4, ck_s, hv)
        conv_silu(cq_s, 0, hv, qc_ref, None)
    for hv in halves:
        vm_ref[hv, :] = cols(5, hv).astype(BF16)
        conv_silu(ck_s, ML_WIDTH, hv, kc_ref, ML_HEAD_DIM ** -0.5)
    om_ref[...] = _dot(h_s[...], w_ref[:, 6 * cw:7 * cw]).astype(BF16)
    gcol_ref[...] = _dot(h_s[...], wg_ref[...])
    grow_ref[...] = _nt_dot(wgt_ref[...], h_s[...])


def _inproj(x2d, g_mix, w_main, wg, wgt, rope_c, rope_sa, rope_sb, conv_w, conv_b, S):
    T = x2d.shape[0]
    tm = min(TM_PROJ, S)
    assert S % tm == 0
    tps = S // tm
    full = lambda shape: pl.BlockSpec(shape, lambda i: (0, 0))
    rope_spec = pl.BlockSpec((tm, LANES), lambda i: (i % tps, 0))
    col_spec = pl.BlockSpec((tm, 512), lambda i: (i, 0))
    return pl.pallas_call(
        functools.partial(_inproj_kernel, tiles_per_seq=tps),
        grid=(T // tm,),
        in_specs=[
            pl.BlockSpec((tm, D_MODEL), lambda i: (i, 0)),
            full((1, D_MODEL)), full((D_MODEL, N_MAIN)), full((D_MODEL, LANES)), full((8, D_MODEL)),
            rope_spec, rope_spec, rope_spec,
            full((CONV_K, 2 * ML_WIDTH)), full((1, 2 * ML_WIDTH)),
        ],
        out_specs=[col_spec] * 8 + [
            pl.BlockSpec((tm, LANES), lambda i: (i, 0)),
            pl.BlockSpec((8, tm), lambda i: (0, i)),
        ],
        out_shape=[jax.ShapeDtypeStruct((T, 512), BF16)] * 8 + [
            jax.ShapeDtypeStruct((T, LANES), F32),
            jax.ShapeDtypeStruct((8, T), F32),
        ],
        scratch_shapes=[pltpu.VMEM((tm, D_MODEL), BF16),
                        pltpu.VMEM((tm + 8, ML_WIDTH), F32), pltpu.VMEM((tm + 8, ML_WIDTH), F32)],
        compiler_params=pltpu.CompilerParams(
            dimension_semantics=("arbitrary",), vmem_limit_bytes=56 * 1024 * 1024),
        name="inproj",
    )(x2d, g_mix, w_main, wg, wgt, rope_c, rope_sa, rope_sb, conv_w, conv_b)


def _attn_kernel(q1_s, q2_s, k_s, v_ref, lam_ref, g_ref, o_ref, vt_s, acc_s, st_s):
    S = v_ref.shape[0]

    def prep(r, carry):
        rows = pl.ds(pl.multiple_of(r * TQ, TQ), TQ)
        for h in range(DA_HEADS):
            hs = slice(h * LANES, (h + 1) * LANES)
            vt_s[h, :DA_V_DIM, rows] = v_ref[rows, hs].astype(F32).T.astype(BF16)
        return carry

    lax.fori_loop(0, S // TQ, prep, 0)
    vt_s[:, DA_V_DIM:, :] = jnp.ones((DA_HEADS, VT_ROWS - DA_V_DIM, S), BF16)

    lq = lam_ref[...]
    lam = (jnp.exp(jnp.sum(lq[0:1] * lq[1:2], axis=-1, keepdims=True))
           - jnp.exp(jnp.sum(lq[2:3] * lq[3:4], axis=-1, keepdims=True)) + LAMBDA_INIT)

    key = lax.broadcasted_iota(I32, (TK, 2 * TQ), 0)
    qry = lax.broadcasted_iota(I32, (TK, 2 * TQ), 1)
    kq = key - jnp.where(qry >= TQ, qry - TQ, qry)

    def kvstep(qo, ko, tk, ms, causal, slot):
        out = []

        def scores(h):
            hs = slice(h * LANES, (h + 1) * LANES)
            qq = jnp.concatenate([q1_s[pl.ds(qo, TQ), hs], q2_s[pl.ds(qo, TQ), hs]], axis=0)
            st = _nt_dot(k_s[pl.ds(ko, tk), hs], qq)
            st_s[h, :tk, :] = st if causal is None else jnp.where(causal, st, NEG)

        scores(0)
        for h in range(DA_HEADS):
            if h + 1 < DA_HEADS:
                scores(h + 1)
            st = st_s[h, :tk, :]
            m_new = jnp.maximum(ms[h], jnp.max(st, axis=0, keepdims=True))
            alpha = jnp.exp2(ms[h] - m_new)
            p = jnp.exp2(st - m_new).astype(BF16)
            acc_s[slot, h] = alpha * acc_s[slot, h] + _dot(vt_s[h, :, pl.ds(ko, tk)], p)
            out.append(m_new)
        return tuple(out)

    def start(i, ko, tk, off, slot):
        acc_s[slot] = jnp.zeros(acc_s.shape[1:], F32)
        m0 = tuple(jnp.full((1, 2 * TQ), NEG, F32) for _ in range(DA_HEADS))
        return kvstep(pl.multiple_of(i * TQ, TQ), ko, tk, m0, kq[:tk] <= off, slot)

    def full_steps(i, n, ms, slot):
        qo = pl.multiple_of(i * TQ, TQ)
        lax.fori_loop(0, n, lambda j, m: kvstep(qo, pl.multiple_of(j * TK, TK), TK, m, None, slot), ms)

    def finalize(i, slot):
        qo = pl.multiple_of(i * TQ, TQ)
        for h in range(DA_HEADS):
            hs = slice(h * LANES, (h + 1) * LANES)
            acc = acc_s[slot, h]
            on = acc[:DA_V_DIM] * (1.0 / acc[DA_V_DIM:DA_V_DIM + 1])
            ot = on[:, :TQ] - lam * on[:, TQ:]
            o = _rms(ot.T, g_ref[...]) * (1.0 - LAMBDA_INIT)
            o_ref[pl.ds(qo, TQ), hs] = o.astype(BF16)

    def tile_pair(u, carry):
        ko = pl.multiple_of(u * TK, TK)
        finalize(2 * u - 1, 1)
        full_steps(2 * u, u, start(2 * u, ko, TQ, 0, 0), 0)
        finalize(2 * u, 0)
        full_steps(2 * u + 1, u, start(2 * u + 1, ko, TK, TQ, 1), 1)
        return carry

    assert TK == 2 * TQ and (S // TQ) % 2 == 0
    start(0, 0, TQ, 0, 0)
    finalize(0, 0)
    start(1, 0, TK, TQ, 1)
    lax.fori_loop(1, S // TK, tile_pair, 0)
    finalize(S // TQ - 1, 1)


def _attention(q1, q2, kr, va, lam_qk, subln_g, B, S):
    T = B * S
    full = lambda shape: pl.BlockSpec(shape, lambda b: (0, 0))
    seq = pl.BlockSpec((S, DA_WIDTH), lambda b: (b, 0))
    return pl.pallas_call(
        _attn_kernel,
        grid=(B,),
        in_specs=[seq, seq, seq, seq, full((4, DA_HEAD_DIM)), full((1, LANES))],
        out_specs=seq,
        out_shape=jax.ShapeDtypeStruct((T, DA_WIDTH), BF16),
        scratch_shapes=[
            pltpu.VMEM((DA_HEADS, VT_ROWS, S), BF16),
            pltpu.VMEM((2, DA_HEADS, VT_ROWS, 2 * TQ), F32),
            pltpu.VMEM((DA_HEADS, TK, 2 * TQ), F32),
        ],
        compiler_params=pltpu.CompilerParams(
            dimension_semantics=("arbitrary",), vmem_limit_bytes=48 * 1024 * 1024),
        name="attn",
    )(q1, q2, kr, va, lam_qk, subln_g)


def _log_sigmoid(x):
    return jnp.minimum(x, 0.0) - jnp.log(1.0 + jnp.exp(-jnp.abs(x)))


def _mlstm_kernel(qc_s, kc_s, v_ref, o_ref, gcol_ref, grow_ref, gbc_ref, gbr_ref,
                  mg_ref, out_ref, vt_s, ct_s, m_s):
    S = v_ref.shape[0]
    L = ML_CHUNK
    nc = S // L

    def transpose_v(c, carry):
        ro = pl.multiple_of(c * L, L)
        for h in range(ML_HEADS):
            hs = slice(h * ML_HEAD_DIM, (h + 1) * ML_HEAD_DIM)
            vt_s[h, :ML_HEAD_DIM, pl.ds(ro, L)] = v_ref[pl.ds(ro, L), hs].astype(F32).T.astype(BF16)
        return carry

    lax.fori_loop(0, nc, transpose_v, 0)
    vt_s[:, ML_HEAD_DIM:, :] = jnp.ones((ML_HEADS, ML_HEAD_DIM, S), BF16)

    ct_s[...] = jnp.zeros(ct_s.shape, F32)
    m_s[...] = jnp.zeros(m_s.shape, F32)

    ri = lax.broadcasted_iota(I32, (L, L), 0)
    ci = lax.broadcasted_iota(I32, (L, L), 1)
    causal_t = ri <= ci
    tril = jnp.where(ci <= ri, 1.0, 0.0).astype(BF16)
    triu = jnp.where(causal_t, 1.0, 0.0).astype(BF16)

    heads = [slice(h * ML_HEAD_DIM, (h + 1) * ML_HEAD_DIM) for h in range(ML_HEADS)]

    def gates(c):
        ro = pl.multiple_of(c * L, L)
        gc = gcol_ref[pl.ds(ro, L), :] + gbc_ref[...]
        gr = grow_ref[:, pl.ds(ro, L)] + gbr_ref[...]
        b_c = sum(_dot(tril, p) for p in _split3(_log_sigmoid(gc)))
        b_r = sum(_dot(p, triu) for p in _split3(_log_sigmoid(gr)))
        return ro, gc, gr, b_c, b_r

    def read_state(g):
        ro = g[0]
        ks, vts, kq, cq, ms = [], [], [], [], []
        for h, hs in enumerate(heads):
            q = qc_s[pl.ds(ro, L), hs]
            k = kc_s[pl.ds(ro, L), hs]
            ks.append(k)
            vts.append(vt_s[h, :, pl.ds(ro, L)])
            kq.append(_nt_dot(k, q))
            cq.append(_nt_dot(ct_s[h].astype(BF16), q))
            ms.append(m_s[h:h + 1, 0:1])
        return ks, vts, kq, cq, ms

    def update_state(g, d):
        _, _, gr, _, b_r = g
        ks, vts, _, _, ms = d
        for h in range(ML_HEADS):
            br = b_r[4 + h:5 + h, :]
            bl = br[:, L - 1:L]
            dec = bl - br + gr[h:h + 1, :]
            m_new = jnp.maximum(bl + ms[h], jnp.max(dec, axis=1, keepdims=True))
            ws = jnp.exp(dec - m_new)
            sc = jnp.exp(bl + ms[h] - m_new)
            vw = (vts[h].astype(F32) * ws).astype(BF16)
            ct_s[h] = sc * ct_s[h] + _dot(vw, ks[h])
            m_s[h:h + 1, :] = jnp.broadcast_to(m_new, (1, LANES))

    def outputs(g, d):
        ro, gc, _, b_c, b_r = g
        _, vts, kq, cq, ms = d
        for h, hs in enumerate(heads):
            br = b_r[4 + h:5 + h, :]
            a_col = b_c[:, 4 + h:5 + h] - gc[:, h:h + 1]
            dm = jnp.where(causal_t, br - a_col, NEG)
            inter = br + ms[h]
            m_row = jnp.maximum(inter, jnp.max(dm, axis=0, keepdims=True))
            sm = jnp.exp(dm - m_row) * kq[h]
            sc_in = jnp.exp(inter - m_row)
            a = _dot(vts[h], sm.astype(BF16)) + sc_in * cq[h]
            den = a[ML_HEAD_DIM:ML_HEAD_DIM + 1, :]
            hh = (a[:ML_HEAD_DIM] / jnp.maximum(jnp.abs(den), jnp.exp(-m_row))).T
            hg = hh * jax.nn.sigmoid(o_ref[pl.ds(ro, L), hs].astype(F32))
            out_ref[pl.ds(ro, L), hs] = _rms(hg, mg_ref[:, hs]).astype(BF16)

    def chunk_group(cg, carry):
        gs = [gates(ML_GROUP * cg + j) for j in range(ML_GROUP)]
        ds = []
        for g in gs:
            ds.append(read_state(g))
            update_state(g, ds[-1])
        for g, d in zip(gs, ds):
            outputs(g, d)
        return carry

    lax.fori_loop(0, nc // ML_GROUP, chunk_group, 0)


def _mlstm(qc, kc, vm, om, gcol, grow, gb_col, gb_row, mh_g, B, S):
    T = B * S
    full = lambda shape: pl.BlockSpec(shape, lambda b: (0, 0))
    seq = pl.BlockSpec((S, ML_WIDTH), lambda b: (b, 0))
    return pl.pallas_call(
        _mlstm_kernel,
        grid=(B,),
        in_specs=[
            seq, seq, seq, seq,
            pl.BlockSpec((S, LANES), lambda b: (b, 0)),
            pl.BlockSpec((8, S), lambda b: (0, b)),
            full((1, LANES)), full((8, LANES)), full((1, ML_WIDTH)),
        ],
        out_specs=seq,
        out_shape=jax.ShapeDtypeStruct((T, ML_WIDTH), BF16),
        scratch_shapes=[
            pltpu.VMEM((ML_HEADS, 2 * ML_HEAD_DIM, S), BF16),
            pltpu.VMEM((ML_HEADS, 2 * ML_HEAD_DIM, ML_HEAD_DIM), F32),
            pltpu.VMEM((8, LANES), F32),
        ],
        compiler_params=pltpu.CompilerParams(
            dimension_semantics=("arbitrary",), vmem_limit_bytes=56 * 1024 * 1024),
        name="mlstm",
    )(qc, kc, vm, om, gcol, grow, gb_col, gb_row, mh_g)


def _router_kernel(oa_ref, hm_ref, x_ref, wo_ref, g_ref, wr_ref, brt_ref,
                   x2_ref, hn_ref, rrow_ref, rcol_ref, cnt_ref):
    def project(s):
        rs = slice(s * TM, (s + 1) * TM)
        mixo = _dot(oa_ref[rs, :], wo_ref[:DA_WIDTH, :]) + _dot(hm_ref[rs, :], wo_ref[DA_WIDTH:, :])
        x2 = x_ref[rs, :] + mixo
        x2_ref[rs, :] = x2
        hn = _rms(x2, g_ref[...])
        hn_hi = hn.astype(BF16)
        hn_ref[rs, :] = hn_hi
        hn_lo = (hn - hn_hi.astype(F32)).astype(BF16)
        a = _dot(hn_hi, wr_ref[...])
        b = _dot(hn_lo, wr_ref[:, :LANES])
        return (a[:, :LANES] + a[:, LANES:] + b).T[:RT_ROWS] + brt_ref[:RT_ROWS, :]

    lts = [project(0)]
    for s in range(RT_GROUP):
        if s + 1 < RT_GROUP:
            lts.append(project(s + 1))
        _route(s, lts[s], rrow_ref, rcol_ref, cnt_ref)


def _route(s, lt, rrow_ref, rcol_ref, cnt_ref):
    sub = lax.broadcasted_iota(I32, (RT_ROWS, TM), 0)
    sub_f = sub.astype(F32)

    def cmax(v):
        return jnp.max(v, axis=0, keepdims=True)

    def first_idx(mask):
        return jnp.min(jnp.where(mask, sub_f, 1e6), axis=0, keepdims=True).astype(I32)

    gl = jnp.where(sub < N_GROUPS, lt, NEG)
    gmax = cmax(gl)
    gsel = first_idx(gl == gmax)
    g_w = 1.0 / jnp.sum(jnp.exp(gl - gmax), axis=0, keepdims=True)
    elo = EXP_ROW0 + gsel * EXPERTS_PER_GROUP
    el = jnp.where((sub >= elo) & (sub < elo + EXPERTS_PER_GROUP), lt, NEG)
    v1 = cmax(el)
    i1 = first_idx(el == v1)
    el2 = jnp.where(sub == i1, NEG, el)
    v2 = cmax(el2)
    i2 = first_idx(el2 == v2)
    t = jnp.exp(v2 - v1)
    w0 = g_w / (1.0 + t)
    w1 = g_w * t / (1.0 + t)

    oh0 = jnp.where(sub == i1, 1.0, 0.0)
    oh1 = jnp.where(sub == i2, 1.0, 0.0)
    mh = oh0 + oh1
    r_i = lax.broadcasted_iota(I32, (TM, TM), 0)
    c_i = lax.broadcasted_iota(I32, (TM, TM), 1)
    before = jnp.where(r_i < c_i, 1.0, 0.0).astype(BF16)
    mh_f = jnp.concatenate([mh, jnp.zeros((LANES - RT_ROWS, TM), F32)], axis=0)
    mh_b = mh_f.astype(BF16)
    pre = _dot(mh_b, before)[:RT_ROWS]
    cnt = jnp.sum(mh_f, axis=1, keepdims=True)
    cnt8 = jnp.floor((cnt + (SUB - 1)) * (1.0 / SUB)) * SUB
    erow = lax.broadcasted_iota(I32, (LANES, 1), 0)
    is_exp = (erow >= EXP_ROW0) & (erow < EXP_ROW0 + N_EXPERTS)
    cnt16 = jnp.where(is_exp, jnp.maximum(jnp.floor((cnt + (CCH - 1)) * (1.0 / CCH)), 1.0) * CCH, 0.0)
    e_r = lax.broadcasted_iota(I32, (LANES, LANES), 0)
    e_c = lax.broadcasted_iota(I32, (LANES, LANES), 1)
    below = jnp.where(e_c < e_r, 1.0, 0.0).astype(BF16)
    lo8 = _dot(below, jnp.broadcast_to(cnt8, (LANES, LANES)).astype(BF16))[:RT_ROWS, 0:1]
    lo16 = _dot(below, jnp.broadcast_to(cnt16, (LANES, LANES)).astype(BF16))[:RT_ROWS, 0:1]

    def csum(v):
        return jnp.sum(v, axis=0, keepdims=True)

    rows = [None] * 8
    rows[R_E0] = (i1 - EXP_ROW0).astype(F32)
    rows[R_E1] = (i2 - EXP_ROW0).astype(F32)
    rows[R_POS0] = csum(oh0 * (pre + lo8))
    rows[R_POS1] = csum(oh1 * (pre + lo8))
    rows[R_PAD0] = csum(oh0 * (pre + lo16))
    rows[R_PAD1] = csum(oh1 * (pre + lo16))
    rows[R_W0] = w0
    rows[R_W1] = w1
    sub128 = lax.broadcasted_iota(I32, (LANES, TM), 0)
    r128 = jnp.zeros((LANES, TM), F32)
    for j, v in enumerate(rows):
        r128 = jnp.where(sub128 == j, v, r128)
    rrow_ref[:, s * TM:(s + 1) * TM] = r128[:8]
    rcol_ref[s * TM:(s + 1) * TM, :] = r128.T
    cnt_ref[s] = _nt_dot(jnp.ones((8, TM), BF16), mh_b)[0:1].astype(I32)


def _router(oa, hm, x2d, w_out, g_ffn, wr_t, b_rt):
    T = x2d.shape[0]
    nt = T // TM
    full = lambda shape: pl.BlockSpec(shape, lambda i: (0, 0))
    tg = RT_GROUP * TM
    return pl.pallas_call(
        _router_kernel,
        grid=(T // tg,),
        in_specs=[
            pl.BlockSpec((tg, DA_WIDTH), lambda i: (i, 0)),
            pl.BlockSpec((tg, ML_WIDTH), lambda i: (i, 0)),
            pl.BlockSpec((tg, D_MODEL), lambda i: (i, 0)),
            full((D_MODEL, D_MODEL)), full((1, D_MODEL)),
            full((D_MODEL, 2 * LANES)), full((LANES, TM)),
        ],
        out_specs=[
            pl.BlockSpec((tg, D_MODEL), lambda i: (i, 0)),
            pl.BlockSpec((tg, D_MODEL), lambda i: (i, 0)),
            pl.BlockSpec((8, tg), lambda i: (0, i)),
            pl.BlockSpec((tg, LANES), lambda i: (i, 0)),
            pl.BlockSpec((RT_GROUP, 1, LANES), lambda i: (i, 0, 0)),
        ],
        out_shape=[
            jax.ShapeDtypeStruct((T, D_MODEL), F32),
            jax.ShapeDtypeStruct((T, D_MODEL), BF16),
            jax.ShapeDtypeStruct((8, T), F32),
            jax.ShapeDtypeStruct((T, LANES), F32),
            jax.ShapeDtypeStruct((nt, 1, LANES), I32),
        ],
        compiler_params=pltpu.CompilerParams(
            dimension_semantics=("arbitrary",), vmem_limit_bytes=40 * 1024 * 1024),
        name="router",
    )(oa, hm, x2d, w_out, g_ffn, wr_t, b_rt)


def _pack_halves(x):
    lo = lax.bitcast_convert_type(x[:, :D_MODEL // 2], I32)
    hi = lax.bitcast_convert_type(x[:, D_MODEL // 2:], I32)
    return lax.shift_right_logical(lo, 16) | (hi & jnp.int32(-65536))


def _unpack_halves(w):
    lo = lax.bitcast_convert_type(lax.shift_left(w, 16), F32)
    hi = lax.bitcast_convert_type(w & jnp.int32(-65536), F32)
    return lo.astype(BF16), hi.astype(BF16)


def _dispatch_kernel(cnt_sm, lo_sm, dst_sm, extra_sm, gs_sm, gn_sm, hn_ref, rrow_ref, rcol_ref, xb_ref,
                     xs_s, z_s, semx, semx2, semz, semz2):
    i = pl.program_id(0)
    nsteps = pl.num_programs(0)
    half = D_MODEL // 2

    def gap_fill(first, last, sem, wait):
        def ebody(e, carry):
            start = pl.multiple_of(gs_sm[e], SUB)
            left = gn_sm[e]
            for rows in (BLK, DCH, SUB):
                n = left // rows

                def body(c, cc, rows=rows, start=start):
                    cp = pltpu.make_async_copy(
                        z_s.at[pl.ds(0, rows), :],
                        xb_ref.at[pl.ds(pl.multiple_of(start + c * rows, SUB), rows), :], sem)
                    cp.wait() if wait else cp.start()
                    return cc

                lax.fori_loop(0, n, body, 0)
                start = start + n * rows
                left = left - n * rows
            return carry

        lax.fori_loop(first, last, ebody, 0)

    @pl.when(i == 0)
    def _():
        for s in range(2):
            xs_s[s, DLOC_ROWS:, :] = jnp.zeros((DCH, XW), I32)
        z_s[...] = jnp.zeros(z_s.shape, I32)
        gap_fill(0, N_GAPS_NEAR, semz, False)
        gap_fill(N_GAPS_NEAR, N_GAPS, semz2, False)
        gap_fill(0, N_GAPS_NEAR, semz, True)

    def xcopy(s, src, dst, sem):
        return pltpu.make_async_copy(xs_s.at[s, pl.ds(src, DCH), :], xb_ref.at[pl.ds(dst, DCH), :],
                                     sem.at[s])

    def wait_tile(s, n_extra):
        for _ in range(N_EXPERTS):
            xcopy(s, 0, 0, semx).wait()

        def wbody(j, c):
            xcopy(s, 0, 0, semx2).wait()
            return c
        lax.fori_loop(0, n_extra, wbody, 0)

    r_i = lax.broadcasted_iota(I32, (DLOC_ROWS, TM), 0).astype(F32)
    lane = lax.broadcasted_iota(I32, (TM, LANES), 1)
    l_r = lax.broadcasted_iota(I32, (LANES, LANES), 0)
    ones3 = jnp.where(l_r < 3, 1.0, 0.0).astype(BF16)

    for s in range(2):
        t = 2 * i + s
        rows = slice(s * TM, (s + 1) * TM)

        @pl.when(i >= 1)
        def _(s=s, t=t):
            wait_tile(s, extra_sm[jnp.maximum(t - 2, 0)])

        rr = rrow_ref[:, rows]
        rc = rcol_ref[rows, :]
        perm0 = jnp.where(r_i == rr[R_POS0:R_POS0 + 1, :], 1.0, 0.0).astype(BF16)
        perm1 = jnp.where(r_i == rr[R_POS1:R_POS1 + 1, :], 1.0, 0.0).astype(BF16)
        xs = _dot(perm0 + perm1, hn_ref[rows, :])
        xs_s[s, :DLOC_ROWS, :half] = _pack_halves(xs)

        def parts(col, rc=rc):
            hi, mid, lo = (p.astype(F32)
                           for p in _split3(jnp.broadcast_to(rc[:, col:col + 1], (TM, LANES))))
            sel = jnp.where(lane == 0, hi, jnp.where(lane == 1, mid, jnp.where(lane == 2, lo, 0.0)))
            return sel.astype(BF16)

        wparts = _dot(perm0, parts(R_W0)) + _dot(perm1, parts(R_W1))
        wsort = _dot(wparts.astype(BF16), ones3)
        xs_s[s, :DLOC_ROWS, half:] = lax.bitcast_convert_type(wsort, I32)

        for e in range(N_EXPERTS):
            lo = pl.multiple_of(lo_sm[t * N_EXPERTS + e], SUB)
            dst = pl.multiple_of(dst_sm[t * N_EXPERTS + e], SUB)
            xcopy(s, lo, dst, semx).start()

        @pl.when(extra_sm[t] > 0)
        def _(s=s, t=t):
            def ebody(e, carry):
                n = cnt_sm[t * N_EXPERTS + e]
                lo = pl.multiple_of(lo_sm[t * N_EXPERTS + e], SUB)
                dst = pl.multiple_of(dst_sm[t * N_EXPERTS + e], SUB)

                def cbody(c, cc):
                    xcopy(s, lo + c * DCH, dst + c * DCH, semx2).start()
                    return cc

                return lax.fori_loop(1, (n + DCH - 1) // DCH, cbody, carry)

            lax.fori_loop(0, N_EXPERTS, ebody, 0)

    @pl.when(i == nsteps - 1)
    def _():
        for s in range(2):
            wait_tile(s, extra_sm[2 * i + s])
        gap_fill(N_GAPS_NEAR, N_GAPS, semz2, True)


def _dispatch(cnt_f, lo_f, dst_f, extra, gap_start, gap_rows, hn, rrow, rcol, nb):
    T = hn.shape[0]
    assert T % (2 * TM) == 0
    grid_spec = pltpu.PrefetchScalarGridSpec(
        num_scalar_prefetch=6,
        grid=(T // (2 * TM),),
        in_specs=[
            pl.BlockSpec((2 * TM, D_MODEL), lambda i, *_: (i, 0)),
            pl.BlockSpec((8, 2 * TM), lambda i, *_: (0, i)),
            pl.BlockSpec((2 * TM, LANES), lambda i, *_: (i, 0)),
        ],
        out_specs=pl.BlockSpec(memory_space=pl.ANY),
        scratch_shapes=[
            pltpu.VMEM((2, DLOC_ROWS + DCH, XW), I32),
            pltpu.VMEM((BLK, XW), I32),
            pltpu.SemaphoreType.DMA((2,)), pltpu.SemaphoreType.DMA((2,)),
            pltpu.SemaphoreType.DMA(()), pltpu.SemaphoreType.DMA(()),
        ],
    )
    return pl.pallas_call(
        _dispatch_kernel,
        grid_spec=grid_spec,
        out_shape=jax.ShapeDtypeStruct((nb * BLK, XW), I32),
        compiler_params=pltpu.CompilerParams(
            dimension_semantics=("arbitrary",), vmem_limit_bytes=40 * 1024 * 1024),
        name="dispatch",
    )(cnt_f, lo_f, dst_f, extra, gap_start, gap_rows, hn, rrow, rcol)


def _expert_kernel(be_sm, nu_sm, xb_ref, w1_ref, w3_ref, w2_ref, yb_ref, w1_s, w3_s, w2_s):
    p = pl.program_id(0)
    used = p < nu_sm[0]

    @pl.when(used & ((p == 0) | (be_sm[p] != be_sm[jnp.maximum(p - 1, 0)])))
    def _():
        w1_s[...] = w1_ref[...].astype(BF16)
        w3_s[...] = w3_ref[...].astype(BF16)
        w2_s[...] = w2_ref[...].astype(BF16)

    @pl.when(used)
    def _():
        half = D_MODEL // 2
        subs = [slice(r * ESUB, (r + 1) * ESUB) for r in range(BLK // ESUB)]

        def up(rs):
            xlo, xhi = _unpack_halves(xb_ref[rs, :half])
            h1 = _dot(xlo, w1_s[:half, :]) + _dot(xhi, w1_s[half:, :])
            h3 = _dot(xlo, w3_s[:half, :]) + _dot(xhi, w3_s[half:, :])
            return h1, h3

        def down(rs, h1, h3):
            wrep = lax.bitcast_convert_type(xb_ref[rs, half:], F32)
            wfull = jnp.concatenate([wrep] * (D_EXPERT // LANES), axis=1)
            hdn = (h1 * jax.nn.sigmoid(h1) * h3 * wfull).astype(BF16)
            y = _dot(hdn, w2_s[...]).astype(BF16).astype(F32)
            yb_ref[rs, :] = _pack_halves(y)

        hs = [up(subs[0])]
        for r, rs in enumerate(subs):
            if r + 1 < len(subs):
                hs.append(up(subs[r + 1]))
            down(rs, *hs[r])

    @pl.when(jnp.logical_not(used))
    def _():
        yb_ref[...] = jnp.zeros(yb_ref.shape, I32)


def _experts(blk_e, nused, xb, w1, w3, w2, nb):
    def rows(p, be, nu):
        return (jnp.minimum(p, nu[0] - 1), 0)

    def wsel(p, be, nu):
        return (be[jnp.minimum(p, nu[0] - 1)], 0, 0)

    grid_spec = pltpu.PrefetchScalarGridSpec(
        num_scalar_prefetch=2,
        grid=(nb,),
        in_specs=[
            pl.BlockSpec((BLK, XW), rows),
            pl.BlockSpec((None, D_MODEL, D_EXPERT), wsel),
            pl.BlockSpec((None, D_MODEL, D_EXPERT), wsel),
            pl.BlockSpec((None, D_EXPERT, D_MODEL), wsel),
        ],
        out_specs=pl.BlockSpec((BLK, D_MODEL // 2), lambda p, be, nu: (p, 0)),
        scratch_shapes=[
            pltpu.VMEM((D_MODEL, D_EXPERT), BF16), pltpu.VMEM((D_MODEL, D_EXPERT), BF16),
            pltpu.VMEM((D_EXPERT, D_MODEL), BF16),
        ],
    )
    return pl.pallas_call(
        _expert_kernel,
        grid_spec=grid_spec,
        out_shape=jax.ShapeDtypeStruct((nb * BLK, D_MODEL // 2), I32),
        compiler_params=pltpu.CompilerParams(
            dimension_semantics=("arbitrary",), vmem_limit_bytes=48 * 1024 * 1024),
        name="experts",
    )(blk_e, nused, xb, w1, w3, w2)


def _combine_kernel(cnt_sm, lo_sm, dst_sm, used_sm, extra_sm, yb_ref, x2_ref, rcol_ref, g_ref, out_ref,
                    yl_s, y_s, sem, sem2):
    i = pl.program_id(0)
    nsteps = pl.num_programs(0)
    bset = i % 2

    def ycopy(s, src, dst, sm):
        return pltpu.make_async_copy(yb_ref.at[pl.ds(src, CCH), :], yl_s.at[s, pl.ds(dst, CCH), :],
                                     sm.at[s])

    def issue(tile, s):
        for e in range(N_EXPERTS):
            lo = pl.multiple_of(lo_sm[tile * N_EXPERTS + e], CCH)
            src = pl.multiple_of(dst_sm[tile * N_EXPERTS + e], SUB)
            ycopy(s, src, lo, sem).start()

        @pl.when(extra_sm[tile] > 0)
        def _():
            def ebody(e, carry):
                n = cnt_sm[tile * N_EXPERTS + e]
                lo = pl.multiple_of(lo_sm[tile * N_EXPERTS + e], CCH)
                src = pl.multiple_of(dst_sm[tile * N_EXPERTS + e], SUB)

                def cbody(c, cc):
                    ycopy(s, src + c * CCH, lo + c * CCH, sem2).start()
                    return cc

                return lax.fori_loop(1, (n + CCH - 1) // CCH, cbody, carry)

            lax.fori_loop(0, N_EXPERTS, ebody, 0)

    @pl.when(i == 0)
    def _():
        yl_s[...] = jnp.zeros(yl_s.shape, I32)
        for s in range(CTILES):
            issue(s, s)

    @pl.when(i + 1 < nsteps)
    def _():
        for s in range(CTILES):
            issue(CTILES * (i + 1) + s, (1 - bset) * CTILES + s)

    lane = lax.broadcasted_iota(I32, (TM, CBLK), 1).astype(F32)
    half = D_MODEL // 2
    for s in range(CTILES):
        t = CTILES * i + s
        slot = bset * CTILES + s
        rows = slice(s * TM, (s + 1) * TM)
        for _ in range(N_EXPERTS):
            ycopy(slot, 0, 0, sem).wait()

        def wbody(j, c, slot=slot):
            ycopy(slot, 0, 0, sem2).wait()
            return c

        lax.fori_loop(0, extra_sm[t], wbody, 0)

        rc = rcol_ref[rows, :]
        pad0 = rc[:, R_PAD0:R_PAD0 + 1]
        pad1 = rc[:, R_PAD1:R_PAD1 + 1]

        def chunk(c, pad0=pad0, pad1=pad1, slot=slot):
            r = lane + float(c * CBLK)
            selm = (jnp.where(pad0 == r, 1.0, 0.0) + jnp.where(pad1 == r, 1.0, 0.0)).astype(BF16)
            lo_h, hi_h = _unpack_halves(yl_s[slot, c * CBLK:(c + 1) * CBLK, :])
            return _dot(selm, lo_h), _dot(selm, hi_h)

        ylo = jnp.zeros((TM, half), F32)
        yhi = jnp.zeros((TM, half), F32)
        for c in range(LOC_COMMON // CBLK):
            dlo, dhi = chunk(c)
            ylo = ylo + dlo
            yhi = yhi + dhi
        y_s[:, :half] = ylo
        y_s[:, half:] = yhi
        for c in range(LOC_COMMON // CBLK, LOC_ROWS // CBLK):
            @pl.when(used_sm[t] > c * CBLK)
            def _(c=c, chunk=chunk):
                dlo, dhi = chunk(c)
                y_s[:, :half] += dlo
                y_s[:, half:] += dhi
        out_ref[rows, :] = _rms(x2_ref[rows, :] + y_s[...], g_ref[...])


def _combine(cnt_f, lo16_f, dst_f, used, extra, yb, x2, rcol, g_final):
    T = x2.shape[0]
    tg = CTILES * TM
    assert T % tg == 0
    grid_spec = pltpu.PrefetchScalarGridSpec(
        num_scalar_prefetch=5,
        grid=(T // tg,),
        in_specs=[
            pl.BlockSpec(memory_space=pl.ANY),
            pl.BlockSpec((tg, D_MODEL), lambda i, *_: (i, 0)),
            pl.BlockSpec((tg, LANES), lambda i, *_: (i, 0)),
            pl.BlockSpec((1, D_MODEL), lambda i, *_: (0, 0)),
        ],
        out_specs=pl.BlockSpec((tg, D_MODEL), lambda i, *_: (i, 0)),
        scratch_shapes=[
            pltpu.VMEM((2 * CTILES, LOC_ROWS, D_MODEL // 2), I32),
            pltpu.VMEM((TM, D_MODEL), F32),
            pltpu.SemaphoreType.DMA((2 * CTILES,)), pltpu.SemaphoreType.DMA((2 * CTILES,)),
        ],
    )
    return pl.pallas_call(
        _combine_kernel,
        grid_spec=grid_spec,
        out_shape=jax.ShapeDtypeStruct((T, D_MODEL), F32),
        compiler_params=pltpu.CompilerParams(
            dimension_semantics=("arbitrary",), vmem_limit_bytes=40 * 1024 * 1024),
        name="combine",
    )(cnt_f, lo16_f, dst_f, used, extra, yb, x2, rcol, g_final)


def _rope_tables(S):
    half = DA_HEAD_DIM // 2
    inv = (1.0 / (np.float32(ROPE_THETA) ** (np.arange(0, DA_HEAD_DIM, 2, dtype=np.float32)
                                             / np.float32(DA_HEAD_DIM)))).astype(np.float32)
    ang = (np.arange(S, dtype=np.float32)[:, None] * inv[None, :]).astype(np.float32)
    cos, sin = np.cos(ang), np.sin(ang)
    lane = np.arange(LANES)
    idx = lane % half
    lower = (lane % DA_HEAD_DIM) < half
    c = cos[:, idx]
    s = sin[:, idx]
    z = np.zeros_like(s)
    return (jnp.asarray(c, F32), jnp.asarray(np.where(lower[None, :], -s, z), F32),
            jnp.asarray(np.where(lower[None, :], z, s), F32))


def _tri_dot(a, b):
    return jnp.dot(a.astype(F32), b.astype(F32), precision=lax.Precision.HIGHEST).astype(I32)


def kernel(x, w_in, conv_w, conv_b, gate_b, lam_qk, subln_g, mhnorm_g, w_out, g_mix, g_ffn, w_grp,
           b_grp, w_erouter, b_erouter, w1, w3, w2, g_final):
    B, S, D = x.shape
    T = B * S
    nt = T // TM
    x2d = x.reshape(T, D)
    l = 0

    w_main = w_in[l, :, :N_MAIN].astype(BF16)
    wg8 = w_in[l, :, N_MAIN:]
    wg = jnp.pad(wg8, ((0, 0), (0, LANES - 8))).astype(BF16)
    wgt = wg8.T.astype(BF16)
    gb8 = gate_b[l].reshape(8)
    gb_col = jnp.pad(gb8, (0, LANES - 8)).reshape(1, LANES)
    gb_row = jnp.broadcast_to(gb8[:, None], (8, LANES))
    rope_c, rope_sa, rope_sb = _rope_tables(S)
    w_r = jnp.concatenate(
        [w_grp[l], w_erouter[l].transpose(1, 0, 2).reshape(D, N_EXPERTS)], axis=1)
    w_r = jnp.pad(w_r, ((0, 0), (0, LANES - w_r.shape[1])))
    wr_hi = w_r.astype(BF16)
    wr_t = jnp.concatenate([wr_hi, (w_r - wr_hi.astype(F32)).astype(BF16)], axis=1)
    b_r = jnp.pad(jnp.concatenate([b_grp[l], b_erouter[l].reshape(N_EXPERTS)]),
                  (0, LANES - N_GROUPS - N_EXPERTS))
    b_rt = jnp.broadcast_to(b_r[:, None], (LANES, TM))

    q1, q2, kr, va, qc, kc, vm, om, gcol, grow = _inproj(
        x2d, g_mix[l].reshape(1, D), w_main, wg, wgt, rope_c, rope_sa, rope_sb,
        conv_w[l], conv_b[l].reshape(1, -1), S)
    oa = _attention(q1, q2, kr, va, lam_qk[l], subln_g[l].reshape(1, LANES), B, S)
    hm = _mlstm(qc, kc, vm, om, gcol, grow, gb_col, gb_row, mhnorm_g[l].reshape(1, ML_WIDTH), B, S)
    x2, hn, rrow, rcol, cnt3 = _router(oa, hm, x2d, w_out[l].astype(BF16), g_ffn[l].reshape(1, D),
                                       wr_t, b_rt)

    cnt = cnt3[:, 0, EXP_ROW0:EXP_ROW0 + N_EXPERTS]
    c8 = (cnt + SUB - 1) // SUB * SUB
    c16 = jnp.maximum((cnt + CCH - 1) // CCH, 1) * CCH
    odd = (np.arange(nt) % 2)[:, None]
    c8_even = c8 * (1 - odd)
    c8_odd = c8 * odd
    even_tot = jnp.sum(c8_even, axis=0)
    count8 = even_tot + DCH + jnp.sum(c8_odd, axis=0)
    pcount = (count8 + DCH + BLK - 1) // BLK * BLK
    up_e = np.triu(np.ones((N_EXPERTS, N_EXPERTS), np.float32), 1)
    lo_t = np.tril(np.ones((nt, nt), np.float32), -1)
    pstart = _tri_dot(pcount[None, :], up_e)[0]
    pend = pstart + pcount
    dst = pstart[None, :] + jnp.where(odd == 0, _tri_dot(lo_t, c8_even),
                                      (even_tot + DCH)[None, :] + _tri_dot(lo_t, c8_odd))
    lo8 = _tri_dot(c8, up_e)
    lo16 = _tri_dot(c16, up_e)
    nb = (2 * T + nt * N_EXPERTS * (SUB - 1) + N_EXPERTS * (2 * DCH + BLK - 1)) // BLK + 1
    blk_row = jnp.arange(nb, dtype=I32) * BLK
    blk_e = jnp.minimum(jnp.sum((pend[None, :] <= blk_row[:, None]).astype(I32), axis=1),
                        N_EXPERTS - 1).astype(I32)
    nused = (pend[-1] // BLK).astype(I32).reshape(1)
    cnt_f = cnt.reshape(-1).astype(I32)
    dst_f = dst.reshape(-1).astype(I32)

    spare = jnp.full((N_EXPERTS,), DCH, I32)
    gap_start = jnp.concatenate([pstart + even_tot, pstart + count8, pstart + count8 + DCH,
                                 pend[-1:]]).astype(I32)
    gap_rows = jnp.concatenate([spare, spare, pcount - count8 - DCH,
                                nb * BLK - pend[-1:]]).astype(I32)
    extra = jnp.sum(jnp.maximum((cnt + DCH - 1) // DCH - 1, 0), axis=1).astype(I32)
    xb = _dispatch(cnt_f, lo8.reshape(-1), dst_f, extra, gap_start, gap_rows, hn, rrow, rcol, nb)
    yb = _experts(blk_e, nused, xb, w1[l], w3[l], w2[l], nb)
    used = jnp.sum(c16, axis=1).astype(I32)
    out = _combine(cnt_f, lo16.reshape(-1), dst_f, used, extra, yb, x2, rcol, g_final.reshape(1, D))
    return out.reshape(B, S, D)
```

```python
import functools
import math

import numpy as np
import jax
import jax.numpy as jnp
from jax import lax
from jax.experimental import pallas as pl
from jax.experimental.pallas import tpu as pltpu

F32 = jnp.float32
BF16 = jnp.bfloat16
I32 = jnp.int32

D_MODEL = 1024
DA_HEADS = 4
DA_HEAD_DIM = 64
DA_V_DIM = 128
DA_WIDTH = 512
ML_HEADS = 4
ML_WIDTH = 512
ML_HEAD_DIM = 128
ML_CHUNK = 128
ML_GROUP = 4
CONV_K = 4
ROPE_THETA = 10000.0
RMS_EPS = 1e-6
N_GROUPS = 4
EXPERTS_PER_GROUP = 8
N_EXPERTS = 32
D_EXPERT = 512
LAMBDA_INIT = 0.8 - 0.6 * math.exp(-0.3 * 0)

LANES = 128
SUB = 8
N_MAIN = 7 * 512
NEG = -1e30

TM_PROJ = 1024
TQ = 256
TK = 512
VT_ROWS = DA_V_DIM + 16
TM = 256
BLK = 512
ESUB = 256
CBLK = 256
CTILES = 2
DCH = 32
CCH = 32
DLOC_ROWS = 768
N_GAPS_NEAR = 2 * N_EXPERTS
N_GAPS = N_GAPS_NEAR + N_EXPERTS + 1
LOC_COMMON = N_EXPERTS * CCH
LOC_ROWS = 2 * TM + N_EXPERTS * CCH
XW = D_MODEL // 2 + LANES
EXP_ROW0 = 4
RT_ROWS = 40
RT_GROUP = 4
R_E0, R_E1, R_POS0, R_POS1, R_PAD0, R_PAD1, R_W0, R_W1 = range(8)


def _nt_dot(a, b):
    return lax.dot_general(a, b, (((1,), (1,)), ((), ())), preferred_element_type=F32)


def _tn_dot(a, b):
    return lax.dot_general(a, b, (((0,), (0,)), ((), ())), preferred_element_type=F32)


def _dot(a, b):
    return jnp.dot(a, b, preferred_element_type=F32)


def _split3(x):
    hi = x.astype(BF16)
    r = x - hi.astype(F32)
    mid = r.astype(BF16)
    lo = (r - mid.astype(F32)).astype(BF16)
    return hi, mid, lo


def _rms(x, g):
    return x * lax.rsqrt(jnp.mean(x * x, axis=-1, keepdims=True) + RMS_EPS) * g


def _inproj_kernel(x_ref, g_ref, w_ref, wg_ref, wgt_ref, c_ref, sa_ref, sb_ref, cw_ref, cb_ref,
                   q1_ref, q2_ref, kr_ref, va_ref, qc_ref, kc_ref, vm_ref, om_ref, gcol_ref, grow_ref,
                   h_s, cq_s, ck_s, *, tiles_per_seq):
    tm = x_ref.shape[0]
    cw = 512
    first = (pl.program_id(0) % tiles_per_seq) == 0

    @pl.when(first)
    def _():
        cq_s[0:8, :] = jnp.zeros((8, cw), F32)
        ck_s[0:8, :] = jnp.zeros((8, cw), F32)

    @pl.when(jnp.logical_not(first))
    def _():
        cq_s[0:8, :] = cq_s[tm:tm + 8, :]
        ck_s[0:8, :] = ck_s[tm:tm + 8, :]

    h_s[...] = _rms(x_ref[...], g_ref[...]).astype(BF16)
    nhalf = 2 if tm >= 512 else 1
    th = tm // nhalf
    halves = [slice(p * th, (p + 1) * th) for p in range(nhalf)]

    def cols(c, hv):
        return _dot(h_s[hv, :], w_ref[:, c * cw:(c + 1) * cw])

    rb = min(128, th)
    lane = lax.broadcasted_iota(I32, (rb, LANES), 1)
    qscale = DA_HEAD_DIM ** -0.5 * math.log2(math.e)

    def rope(x, rows):
        return (x * c_ref[rows, :] + pltpu.roll(x, 96, 1) * sa_ref[rows, :]
                + pltpu.roll(x, 32, 1) * sb_ref[rows, :])

    def blocks(hv):
        for r in range(th // rb):
            for hh in range(cw // LANES):
                yield (slice(hv.start + r * rb, hv.start + (r + 1) * rb), slice(r * rb, (r + 1) * rb),
                       slice(hh * LANES, (hh + 1) * LANES))

    def rope_q(qa, hv):
        for rows, loc, hs in blocks(hv):
            qr = rope(qa[loc, hs], rows) * qscale
            q1_ref[rows, hs] = jnp.where(lane < DA_HEAD_DIM, qr, 0.0).astype(BF16)
            q2_ref[rows, hs] = jnp.where(lane >= DA_HEAD_DIM, qr, 0.0).astype(BF16)

    def rope_k(ka, hv):
        for rows, loc, hs in blocks(hv):
            kr_ref[rows, hs] = rope(ka[loc, hs], rows).astype(BF16)

    def conv_silu(src, c0, hv, out_ref, scale):
        for rows, _, hs in blocks(hv):
            cs = slice(c0 + hs.start, c0 + hs.stop)
            w = cw_ref[:, cs]
            y = cb_ref[:, cs] + src[8 + rows.start:8 + rows.stop, hs] * w[3:4]
            for s in (1, 2, 3):
                y = y + src[8 + rows.start - s:8 + rows.stop - s, hs] * w[3 - s:4 - s]
            y = y * jax.nn.sigmoid(y)
            out_ref[rows, hs] = (y if scale is None else y * scale).astype(BF16)

    def to_conv(c, dst, hv):
        dst[8 + hv.start:8 + hv.stop, :] = cols(c, hv)

    qa = [cols(0, hv) for hv in halves]
    ka = []
    for p, hv in enumerate(halves):
        ka.append(cols(1, hv))
        rope_q(qa[p], hv)
    for p, hv in enumerate(halves):
        va_ref[hv, :] = cols(2, hv).astype(BF16)
        rope_k(ka[p], hv)
    for hv in halves:
        to_conv(3, cq_s, hv)
    for hv in halves:
        to_conv(4, ck_s, hv)
        conv_silu(cq_s, 0, hv, qc_ref, None)
    for hv in halves:
        vm_ref[hv, :] = cols(5, hv).astype(BF16)
        conv_silu(ck_s, ML_WIDTH, hv, kc_ref, ML_HEAD_DIM ** -0.5)
    om_ref[...] = _dot(h_s[...], w_ref[:, 6 * cw:7 * cw]).astype(BF16)
    gcol_ref[...] = _dot(h_s[...], wg_ref[...])
    grow_ref[...] = _nt_dot(wgt_ref[...], h_s[...])


def _inproj(x2d, g_mix, w_main, wg, wgt, rope_c, rope_sa, rope_sb, conv_w, conv_b, S):
    T = x2d.shape[0]
    tm = min(TM_PROJ, S)
    assert S % tm == 0
    tps = S // tm
    full = lambda shape: pl.BlockSpec(shape, lambda i: (0, 0))
    rope_spec = pl.BlockSpec((tm, LANES), lambda i: (i % tps, 0))
    col_spec = pl.BlockSpec((tm, 512), lambda i: (i, 0))
    return pl.pallas_call(
        functools.partial(_inproj_kernel, tiles_per_seq=tps),
        grid=(T // tm,),
        in_specs=[
            pl.BlockSpec((tm, D_MODEL), lambda i: (i, 0)),
            full((1, D_MODEL)), full((D_MODEL, N_MAIN)), full((D_MODEL, LANES)), full((8, D_MODEL)),
            rope_spec, rope_spec, rope_spec,
            full((CONV_K, 2 * ML_WIDTH)), full((1, 2 * ML_WIDTH)),
        ],
        out_specs=[col_spec] * 8 + [
            pl.BlockSpec((tm, LANES), lambda i: (i, 0)),
            pl.BlockSpec((8, tm), lambda i: (0, i)),
        ],
        out_shape=[jax.ShapeDtypeStruct((T, 512), BF16)] * 8 + [
            jax.ShapeDtypeStruct((T, LANES), F32),
            jax.ShapeDtypeStruct((8, T), F32),
        ],
        scratch_shapes=[pltpu.VMEM((tm, D_MODEL), BF16),
                        pltpu.VMEM((tm + 8, ML_WIDTH), F32), pltpu.VMEM((tm + 8, ML_WIDTH), F32)],
        compiler_params=pltpu.CompilerParams(
            dimension_semantics=("arbitrary",), vmem_limit_bytes=56 * 1024 * 1024),
        name="inproj",
    )(x2d, g_mix, w_main, wg, wgt, rope_c, rope_sa, rope_sb, conv_w, conv_b)


def _attn_kernel(q1_s, q2_s, k_s, v_ref, lam_ref, g_ref, o_ref, vt_s, acc_s, st_s):
    S = v_ref.shape[0]

    def prep(r, carry):
        rows = pl.ds(pl.multiple_of(r * TQ, TQ), TQ)
        for h in range(DA_HEADS):
            hs = slice(h * LANES, (h + 1) * LANES)
            vt_s[h, :DA_V_DIM, rows] = v_ref[rows, hs].astype(F32).T.astype(BF16)
        return carry

    lax.fori_loop(0, S // TQ, prep, 0)
    vt_s[:, DA_V_DIM:, :] = jnp.ones((DA_HEADS, VT_ROWS - DA_V_DIM, S), BF16)

    lq = lam_ref[...]
    lam = (jnp.exp(jnp.sum(lq[0:1] * lq[1:2], axis=-1, keepdims=True))
           - jnp.exp(jnp.sum(lq[2:3] * lq[3:4], axis=-1, keepdims=True)) + LAMBDA_INIT)

    key = lax.broadcasted_iota(I32, (TK, 2 * TQ), 0)
    qry = lax.broadcasted_iota(I32, (TK, 2 * TQ), 1)
    kq = key - jnp.where(qry >= TQ, qry - TQ, qry)

    def scores(qo, ko, tk, h, causal=None):
        hs = slice(h * LANES, (h + 1) * LANES)
        qq = jnp.concatenate([q1_s[pl.ds(qo, TQ), hs], q2_s[pl.ds(qo, TQ), hs]], axis=0)
        st = _nt_dot(k_s[pl.ds(ko, tk), hs], qq)
        st_s[h, :tk, :] = st if causal is None else jnp.where(causal, st, NEG)

    def kvstep(qo, ko, tk, ms, causal, slot, next_ko=None):
        out = []
        if next_ko is None:
            scores(qo, ko, tk, 0, causal)
        for h in range(DA_HEADS):
            if h + 1 < DA_HEADS:
                scores(qo, ko, tk, h + 1, causal)
            elif next_ko is not None:
                scores(qo, next_ko, tk, 0)
            st = st_s[h, :tk, :]
            m_new = jnp.maximum(ms[h], jnp.max(st, axis=0, keepdims=True))
            alpha = jnp.exp2(ms[h] - m_new)
            p = jnp.exp2(st - m_new).astype(BF16)
            acc_s[slot, h] = alpha * acc_s[slot, h] + _dot(vt_s[h, :, pl.ds(ko, tk)], p)
            out.append(m_new)
        return tuple(out)

    def start(i, ko, tk, off, slot):
        acc_s[slot] = jnp.zeros(acc_s.shape[1:], F32)
        m0 = tuple(jnp.full((1, 2 * TQ), NEG, F32) for _ in range(DA_HEADS))
        return kvstep(pl.multiple_of(i * TQ, TQ), ko, tk, m0, kq[:tk] <= off, slot)

    def full_steps(i, n, ms, slot):
        qo = pl.multiple_of(i * TQ, TQ)
        scores(qo, 0, TK, 0)

        def step(j, m):
            nxt = pl.multiple_of(jnp.minimum(j + 1, n - 1) * TK, TK)
            return kvstep(qo, pl.multiple_of(j * TK, TK), TK, m, None, slot, next_ko=nxt)

        lax.fori_loop(0, n, step, ms)

    def finalize(i, slot):
        qo = pl.multiple_of(i * TQ, TQ)
        for h in range(DA_HEADS):
            hs = slice(h * LANES, (h + 1) * LANES)
            acc = acc_s[slot, h]
            on = acc[:DA_V_DIM] * (1.0 / acc[DA_V_DIM:DA_V_DIM + 1])
            ot = on[:, :TQ] - lam * on[:, TQ:]
            o = _rms(ot.T, g_ref[...]) * (1.0 - LAMBDA_INIT)
            o_ref[pl.ds(qo, TQ), hs] = o.astype(BF16)

    def tile_pair(u, carry):
        ko = pl.multiple_of(u * TK, TK)
        finalize(2 * u - 1, 1)
        full_steps(2 * u, u, start(2 * u, ko, TQ, 0, 0), 0)
        finalize(2 * u, 0)
        full_steps(2 * u + 1, u, start(2 * u + 1, ko, TK, TQ, 1), 1)
        return carry

    assert TK == 2 * TQ and (S // TQ) % 2 == 0
    start(0, 0, TQ, 0, 0)
    finalize(0, 0)
    start(1, 0, TK, TQ, 1)
    lax.fori_loop(1, S // TK, tile_pair, 0)
    finalize(S // TQ - 1, 1)


def _attention(q1, q2, kr, va, lam_qk, subln_g, B, S):
    T = B * S
    full = lambda shape: pl.BlockSpec(shape, lambda b: (0, 0))
    seq = pl.BlockSpec((S, DA_WIDTH), lambda b: (b, 0))
    return pl.pallas_call(
        _attn_kernel,
        grid=(B,),
        in_specs=[seq, seq, seq, seq, full((4, DA_HEAD_DIM)), full((1, LANES))],
        out_specs=seq,
        out_shape=jax.ShapeDtypeStruct((T, DA_WIDTH), BF16),
        scratch_shapes=[
            pltpu.VMEM((DA_HEADS, VT_ROWS, S), BF16),
            pltpu.VMEM((2, DA_HEADS, VT_ROWS, 2 * TQ), F32),
            pltpu.VMEM((DA_HEADS, TK, 2 * TQ), F32),
        ],
        compiler_params=pltpu.CompilerParams(
            dimension_semantics=("arbitrary",), vmem_limit_bytes=48 * 1024 * 1024),
        name="attn",
    )(q1, q2, kr, va, lam_qk, subln_g)


def _log_sigmoid(x):
    return jnp.minimum(x, 0.0) - jnp.log(1.0 + jnp.exp(-jnp.abs(x)))


def _mlstm_kernel(qc_s, kc_s, v_ref, o_ref, gcol_ref, grow_ref, gbc_ref, gbr_ref,
                  mg_ref, out_ref, vt_s, ct_s, m_s):
    S = v_ref.shape[0]
    L = ML_CHUNK
    nc = S // L

    def transpose_v(c, carry):
        ro = pl.multiple_of(c * L, L)
        for h in range(ML_HEADS):
            hs = slice(h * ML_HEAD_DIM, (h + 1) * ML_HEAD_DIM)
            vt_s[h, :ML_HEAD_DIM, pl.ds(ro, L)] = v_ref[pl.ds(ro, L), hs].astype(F32).T.astype(BF16)
        return carry

    lax.fori_loop(0, nc, transpose_v, 0)
    vt_s[:, ML_HEAD_DIM:, :] = jnp.ones((ML_HEADS, ML_HEAD_DIM, S), BF16)

    ct_s[...] = jnp.zeros(ct_s.shape, F32)
    m_s[...] = jnp.zeros(m_s.shape, F32)

    ri = lax.broadcasted_iota(I32, (L, L), 0)
    ci = lax.broadcasted_iota(I32, (L, L), 1)
    causal_t = ri <= ci
    tril = jnp.where(ci <= ri, 1.0, 0.0).astype(BF16)
    triu = jnp.where(causal_t, 1.0, 0.0).astype(BF16)

    heads = [slice(h * ML_HEAD_DIM, (h + 1) * ML_HEAD_DIM) for h in range(ML_HEADS)]

    def gates(c):
        ro = pl.multiple_of(c * L, L)
        gc = gcol_ref[pl.ds(ro, L), :] + gbc_ref[...]
        gr = grow_ref[:, pl.ds(ro, L)] + gbr_ref[...]
        b_c = sum(_dot(tril, p) for p in _split3(_log_sigmoid(gc)))
        b_r = sum(_dot(p, triu) for p in _split3(_log_sigmoid(gr)))
        return ro, gc, gr, b_c, b_r

    def read_state(g):
        ro = g[0]
        ks, vts, kq, cq, ms = [], [], [], [], []
        for h, hs in enumerate(heads):
            q = qc_s[pl.ds(ro, L), hs]
            k = kc_s[pl.ds(ro, L), hs]
            ks.append(k)
            vts.append(vt_s[h, :, pl.ds(ro, L)])
            kq.append(_nt_dot(k, q))
            cq.append(_nt_dot(ct_s[h].astype(BF16), q))
            ms.append(m_s[h:h + 1, 0:1])
        return ks, vts, kq, cq, ms

    def update_state(g, d):
        _, _, gr, _, b_r = g
        ks, vts, _, _, ms = d
        for h in range(ML_HEADS):
            br = b_r[4 + h:5 + h, :]
            bl = br[:, L - 1:L]
            dec = bl - br + gr[h:h + 1, :]
            m_new = jnp.maximum(bl + ms[h], jnp.max(dec, axis=1, keepdims=True))
            ws = jnp.exp(dec - m_new)
            sc = jnp.exp(bl + ms[h] - m_new)
            vw = (vts[h].astype(F32) * ws).astype(BF16)
            ct_s[h] = sc * ct_s[h] + _dot(vw, ks[h])
            m_s[h:h + 1, :] = jnp.broadcast_to(m_new, (1, LANES))

    def outputs(g, d):
        ro, gc, _, b_c, b_r = g
        _, vts, kq, cq, ms = d
        for h, hs in enumerate(heads):
            br = b_r[4 + h:5 + h, :]
            a_col = b_c[:, 4 + h:5 + h] - gc[:, h:h + 1]
            dm = jnp.where(causal_t, br - a_col, NEG)
            inter = br + ms[h]
            m_row = jnp.maximum(inter, jnp.max(dm, axis=0, keepdims=True))
            sm = jnp.exp(dm - m_row) * kq[h]
            sc_in = jnp.exp(inter - m_row)
            a = _dot(vts[h], sm.astype(BF16)) + sc_in * cq[h]
            den = a[ML_HEAD_DIM:ML_HEAD_DIM + 1, :]
            hh = (a[:ML_HEAD_DIM] / jnp.maximum(jnp.abs(den), jnp.exp(-m_row))).T
            hg = hh * jax.nn.sigmoid(o_ref[pl.ds(ro, L), hs].astype(F32))
            out_ref[pl.ds(ro, L), hs] = _rms(hg, mg_ref[:, hs]).astype(BF16)

    def chunk_group(cg, carry):
        gs = [gates(ML_GROUP * cg + j) for j in range(ML_GROUP)]
        ds = []
        for g in gs:
            ds.append(read_state(g))
            update_state(g, ds[-1])
        for g, d in zip(gs, ds):
            outputs(g, d)
        return carry

    lax.fori_loop(0, nc // ML_GROUP, chunk_group, 0)


def _mlstm(qc, kc, vm, om, gcol, grow, gb_col, gb_row, mh_g, B, S):
    T = B * S
    full = lambda shape: pl.BlockSpec(shape, lambda b: (0, 0))
    seq = pl.BlockSpec((S, ML_WIDTH), lambda b: (b, 0))
    return pl.pallas_call(
        _mlstm_kernel,
        grid=(B,),
        in_specs=[
            seq, seq, seq, seq,
            pl.BlockSpec((S, LANES), lambda b: (b, 0)),
            pl.BlockSpec((8, S), lambda b: (0, b)),
            full((1, LANES)), full((8, LANES)), full((1, ML_WIDTH)),
        ],
        out_specs=seq,
        out_shape=jax.ShapeDtypeStruct((T, ML_WIDTH), BF16),
        scratch_shapes=[
            pltpu.VMEM((ML_HEADS, 2 * ML_HEAD_DIM, S), BF16),
            pltpu.VMEM((ML_HEADS, 2 * ML_HEAD_DIM, ML_HEAD_DIM), F32),
            pltpu.VMEM((8, LANES), F32),
        ],
        compiler_params=pltpu.CompilerParams(
            dimension_semantics=("arbitrary",), vmem_limit_bytes=56 * 1024 * 1024),
        name="mlstm",
    )(qc, kc, vm, om, gcol, grow, gb_col, gb_row, mh_g)


def _router_kernel(oa_ref, hm_ref, x_ref, wo_ref, g_ref, wr_ref, brt_ref,
                   x2_ref, hn_ref, rrow_ref, rcol_ref, cnt_ref):
    def project(s):
        rs = slice(s * TM, (s + 1) * TM)
        mixo = _dot(oa_ref[rs, :], wo_ref[:DA_WIDTH, :]) + _dot(hm_ref[rs, :], wo_ref[DA_WIDTH:, :])
        x2 = x_ref[rs, :] + mixo
        x2_ref[rs, :] = x2
        hn = _rms(x2, g_ref[...])
        hn_hi = hn.astype(BF16)
        hn_ref[rs, :] = hn_hi
        hn_lo = (hn - hn_hi.astype(F32)).astype(BF16)
        a = _dot(hn_hi, wr_ref[...])
        b = _dot(hn_lo, wr_ref[:, :LANES])
        return (a[:, :LANES] + a[:, LANES:] + b).T[:RT_ROWS] + brt_ref[:RT_ROWS, :]

    lts = [project(0)]
    for s in range(RT_GROUP):
        if s + 1 < RT_GROUP:
            lts.append(project(s + 1))
        _route(s, lts[s], rrow_ref, rcol_ref, cnt_ref)


def _route(s, lt, rrow_ref, rcol_ref, cnt_ref):
    sub = lax.broadcasted_iota(I32, (RT_ROWS, TM), 0)
    sub_f = sub.astype(F32)

    def cmax(v):
        return jnp.max(v, axis=0, keepdims=True)

    def first_idx(mask):
        return jnp.min(jnp.where(mask, sub_f, 1e6), axis=0, keepdims=True).astype(I32)

    gl = jnp.where(sub < N_GROUPS, lt, NEG)
    gmax = cmax(gl)
    gsel = first_idx(gl == gmax)
    g_w = 1.0 / jnp.sum(jnp.exp(gl - gmax), axis=0, keepdims=True)
    elo = EXP_ROW0 + gsel * EXPERTS_PER_GROUP
    el = jnp.where((sub >= elo) & (sub < elo + EXPERTS_PER_GROUP), lt, NEG)
    v1 = cmax(el)
    i1 = first_idx(el == v1)
    el2 = jnp.where(sub == i1, NEG, el)
    v2 = cmax(el2)
    i2 = first_idx(el2 == v2)
    t = jnp.exp(v2 - v1)
    w0 = g_w / (1.0 + t)
    w1 = g_w * t / (1.0 + t)

    oh0 = jnp.where(sub == i1, 1.0, 0.0)
    oh1 = jnp.where(sub == i2, 1.0, 0.0)
    mh = oh0 + oh1
    r_i = lax.broadcasted_iota(I32, (TM, TM), 0)
    c_i = lax.broadcasted_iota(I32, (TM, TM), 1)
    before = jnp.where(r_i < c_i, 1.0, 0.0).astype(BF16)
    mh_f = jnp.concatenate([mh, jnp.zeros((LANES - RT_ROWS, TM), F32)], axis=0)
    mh_b = mh_f.astype(BF16)
    pre = _dot(mh_b, before)[:RT_ROWS]
    cnt = jnp.sum(mh_f, axis=1, keepdims=True)
    cnt8 = jnp.floor((cnt + (SUB - 1)) * (1.0 / SUB)) * SUB
    erow = lax.broadcasted_iota(I32, (LANES, 1), 0)
    is_exp = (erow >= EXP_ROW0) & (erow < EXP_ROW0 + N_EXPERTS)
    cnt16 = jnp.where(is_exp, jnp.maximum(jnp.floor((cnt + (CCH - 1)) * (1.0 / CCH)), 1.0) * CCH, 0.0)
    e_r = lax.broadcasted_iota(I32, (LANES, LANES), 0)
    e_c = lax.broadcasted_iota(I32, (LANES, LANES), 1)
    below = jnp.where(e_c < e_r, 1.0, 0.0).astype(BF16)
    lo8 = _dot(below, jnp.broadcast_to(cnt8, (LANES, LANES)).astype(BF16))[:RT_ROWS, 0:1]
    lo16 = _dot(below, jnp.broadcast_to(cnt16, (LANES, LANES)).astype(BF16))[:RT_ROWS, 0:1]

    def csum(v):
        return jnp.sum(v, axis=0, keepdims=True)

    rows = [None] * 8
    rows[R_E0] = (i1 - EXP_ROW0).astype(F32)
    rows[R_E1] = (i2 - EXP_ROW0).astype(F32)
    rows[R_POS0] = csum(oh0 * (pre + lo8))
    rows[R_POS1] = csum(oh1 * (pre + lo8))
    rows[R_PAD0] = csum(oh0 * (pre + lo16))
    rows[R_PAD1] = csum(oh1 * (pre + lo16))
    rows[R_W0] = w0
    rows[R_W1] = w1
    sub128 = lax.broadcasted_iota(I32, (LANES, TM), 0)
    r128 = jnp.zeros((LANES, TM), F32)
    for j, v in enumerate(rows):
        r128 = jnp.where(sub128 == j, v, r128)
    rrow_ref[:, s * TM:(s + 1) * TM] = r128[:8]
    rcol_ref[s * TM:(s + 1) * TM, :] = r128.T
    cnt_ref[s] = _nt_dot(jnp.ones((8, TM), BF16), mh_b)[0:1].astype(I32)


def _router(oa, hm, x2d, w_out, g_ffn, wr_t, b_rt):
    T = x2d.shape[0]
    nt = T // TM
    full = lambda shape: pl.BlockSpec(shape, lambda i: (0, 0))
    tg = RT_GROUP * TM
    return pl.pallas_call(
        _router_kernel,
        grid=(T // tg,),
        in_specs=[
            pl.BlockSpec((tg, DA_WIDTH), lambda i: (i, 0)),
            pl.BlockSpec((tg, ML_WIDTH), lambda i: (i, 0)),
            pl.BlockSpec((tg, D_MODEL), lambda i: (i, 0)),
            full((D_MODEL, D_MODEL)), full((1, D_MODEL)),
            full((D_MODEL, 2 * LANES)), full((LANES, TM)),
        ],
        out_specs=[
            pl.BlockSpec((tg, D_MODEL), lambda i: (i, 0)),
            pl.BlockSpec((tg, D_MODEL), lambda i: (i, 0)),
            pl.BlockSpec((8, tg), lambda i: (0, i)),
            pl.BlockSpec((tg, LANES), lambda i: (i, 0)),
            pl.BlockSpec((RT_GROUP, 1, LANES), lambda i: (i, 0, 0)),
        ],
        out_shape=[
            jax.ShapeDtypeStruct((T, D_MODEL), F32),
            jax.ShapeDtypeStruct((T, D_MODEL), BF16),
            jax.ShapeDtypeStruct((8, T), F32),
            jax.ShapeDtypeStruct((T, LANES), F32),
            jax.ShapeDtypeStruct((nt, 1, LANES), I32),
        ],
        compiler_params=pltpu.CompilerParams(
            dimension_semantics=("arbitrary",), vmem_limit_bytes=40 * 1024 * 1024),
        name="router",
    )(oa, hm, x2d, w_out, g_ffn, wr_t, b_rt)


def _pack_halves(x):
    lo = lax.bitcast_convert_type(x[:, :D_MODEL // 2], I32)
    hi = lax.bitcast_convert_type(x[:, D_MODEL // 2:], I32)
    return lax.shift_right_logical(lo, 16) | (hi & jnp.int32(-65536))


def _unpack_halves(w):
    lo = lax.bitcast_convert_type(lax.shift_left(w, 16), F32)
    hi = lax.bitcast_convert_type(w & jnp.int32(-65536), F32)
    return lo.astype(BF16), hi.astype(BF16)


def _dispatch_kernel(cnt_sm, lo_sm, dst_sm, extra_sm, gs_sm, gn_sm, hn_ref, rrow_ref, rcol_ref, xb_ref,
                     xs_s, z_s, semx, semx2, semz, semz2):
    i = pl.program_id(0)
    nsteps = pl.num_programs(0)
    half = D_MODEL // 2

    def gap_fill(first, last, sem, wait):
        def ebody(e, carry):
            start = pl.multiple_of(gs_sm[e], SUB)
            left = gn_sm[e]
            for rows in (BLK, DCH, SUB):
                n = left // rows

                def body(c, cc, rows=rows, start=start):
                    cp = pltpu.make_async_copy(
                        z_s.at[pl.ds(0, rows), :],
                        xb_ref.at[pl.ds(pl.multiple_of(start + c * rows, SUB), rows), :], sem)
                    cp.wait() if wait else cp.start()
                    return cc

                lax.fori_loop(0, n, body, 0)
                start = start + n * rows
                left = left - n * rows
            return carry

        lax.fori_loop(first, last, ebody, 0)

    @pl.when(i == 0)
    def _():
        for s in range(2):
            xs_s[s, DLOC_ROWS:, :] = jnp.zeros((DCH, XW), I32)
        z_s[...] = jnp.zeros(z_s.shape, I32)
        gap_fill(0, N_GAPS_NEAR, semz, False)
        gap_fill(N_GAPS_NEAR, N_GAPS, semz2, False)
        gap_fill(0, N_GAPS_NEAR, semz, True)

    def xcopy(s, src, dst, sem):
        return pltpu.make_async_copy(xs_s.at[s, pl.ds(src, DCH), :], xb_ref.at[pl.ds(dst, DCH), :],
                                     sem.at[s])

    def wait_tile(s, n_extra):
        for _ in range(N_EXPERTS):
            xcopy(s, 0, 0, semx).wait()

        def wbody(j, c):
            xcopy(s, 0, 0, semx2).wait()
            return c
        lax.fori_loop(0, n_extra, wbody, 0)

    r_i = lax.broadcasted_iota(I32, (DLOC_ROWS, TM), 0).astype(F32)
    lane = lax.broadcasted_iota(I32, (TM, LANES), 1)
    l_r = lax.broadcasted_iota(I32, (LANES, LANES), 0)
    ones3 = jnp.where(l_r < 3, 1.0, 0.0).astype(BF16)

    for s in range(2):
        t = 2 * i + s
        rows = slice(s * TM, (s + 1) * TM)

        @pl.when(i >= 1)
        def _(s=s, t=t):
            wait_tile(s, extra_sm[jnp.maximum(t - 2, 0)])

        rr = rrow_ref[:, rows]
        rc = rcol_ref[rows, :]
        perm0 = jnp.where(r_i == rr[R_POS0:R_POS0 + 1, :], 1.0, 0.0).astype(BF16)
        perm1 = jnp.where(r_i == rr[R_POS1:R_POS1 + 1, :], 1.0, 0.0).astype(BF16)
        xs = _dot(perm0 + perm1, hn_ref[rows, :])
        xs_s[s, :DLOC_ROWS, :half] = _pack_halves(xs)

        def parts(col, rc=rc):
            hi, mid, lo = (p.astype(F32)
                           for p in _split3(jnp.broadcast_to(rc[:, col:col + 1], (TM, LANES))))
            sel = jnp.where(lane == 0, hi, jnp.where(lane == 1, mid, jnp.where(lane == 2, lo, 0.0)))
            return sel.astype(BF16)

        wparts = _dot(perm0, parts(R_W0)) + _dot(perm1, parts(R_W1))
        wsort = _dot(wparts.astype(BF16), ones3)
        xs_s[s, :DLOC_ROWS, half:] = lax.bitcast_convert_type(wsort, I32)

        for e in range(N_EXPERTS):
            lo = pl.multiple_of(lo_sm[t * N_EXPERTS + e], SUB)
            dst = pl.multiple_of(dst_sm[t * N_EXPERTS + e], SUB)
            xcopy(s, lo, dst, semx).start()

        @pl.when(extra_sm[t] > 0)
        def _(s=s, t=t):
            def ebody(e, carry):
                n = cnt_sm[t * N_EXPERTS + e]
                lo = pl.multiple_of(lo_sm[t * N_EXPERTS + e], SUB)
                dst = pl.multiple_of(dst_sm[t * N_EXPERTS + e], SUB)

                def cbody(c, cc):
                    xcopy(s, lo + c * DCH, dst + c * DCH, semx2).start()
                    return cc

                return lax.fori_loop(1, (n + DCH - 1) // DCH, cbody, carry)

            lax.fori_loop(0, N_EXPERTS, ebody, 0)

    @pl.when(i == nsteps - 1)
    def _():
        for s in range(2):
            wait_tile(s, extra_sm[2 * i + s])
        gap_fill(N_GAPS_NEAR, N_GAPS, semz2, True)


def _dispatch(cnt_f, lo_f, dst_f, extra, gap_start, gap_rows, hn, rrow, rcol, nb):
    T = hn.shape[0]
    assert T % (2 * TM) == 0
    grid_spec = pltpu.PrefetchScalarGridSpec(
        num_scalar_prefetch=6,
        grid=(T // (2 * TM),),
        in_specs=[
            pl.BlockSpec((2 * TM, D_MODEL), lambda i, *_: (i, 0)),
            pl.BlockSpec((8, 2 * TM), lambda i, *_: (0, i)),
            pl.BlockSpec((2 * TM, LANES), lambda i, *_: (i, 0)),
        ],
        out_specs=pl.BlockSpec(memory_space=pl.ANY),
        scratch_shapes=[
            pltpu.VMEM((2, DLOC_ROWS + DCH, XW), I32),
            pltpu.VMEM((BLK, XW), I32),
            pltpu.SemaphoreType.DMA((2,)), pltpu.SemaphoreType.DMA((2,)),
            pltpu.SemaphoreType.DMA(()), pltpu.SemaphoreType.DMA(()),
        ],
    )
    return pl.pallas_call(
        _dispatch_kernel,
        grid_spec=grid_spec,
        out_shape=jax.ShapeDtypeStruct((nb * BLK, XW), I32),
        compiler_params=pltpu.CompilerParams(
            dimension_semantics=("arbitrary",), vmem_limit_bytes=40 * 1024 * 1024),
        name="dispatch",
    )(cnt_f, lo_f, dst_f, extra, gap_start, gap_rows, hn, rrow, rcol)


def _expert_kernel(be_sm, nu_sm, xb_ref, w1_ref, w3_ref, w2_ref, yb_ref, w1_s, w3_s, w2_s):
    p = pl.program_id(0)
    used = p < nu_sm[0]

    @pl.when(used & ((p == 0) | (be_sm[p] != be_sm[jnp.maximum(p - 1, 0)])))
    def _():
        w1_s[...] = w1_ref[...].astype(BF16)
        w3_s[...] = w3_ref[...].astype(BF16)
        w2_s[...] = w2_ref[...].astype(BF16)

    @pl.when(used)
    def _():
        half = D_MODEL // 2
        subs = [slice(r * ESUB, (r + 1) * ESUB) for r in range(BLK // ESUB)]

        def up(rs):
            xlo, xhi = _unpack_halves(xb_ref[rs, :half])
            h1 = _dot(xlo, w1_s[:half, :]) + _dot(xhi, w1_s[half:, :])
            h3 = _dot(xlo, w3_s[:half, :]) + _dot(xhi, w3_s[half:, :])
            return h1, h3

        def down(rs, h1, h3):
            wrep = lax.bitcast_convert_type(xb_ref[rs, half:], F32)
            wfull = jnp.concatenate([wrep] * (D_EXPERT // LANES), axis=1)
            hdn = (h1 * jax.nn.sigmoid(h1) * h3 * wfull).astype(BF16)
            y = _dot(hdn, w2_s[...]).astype(BF16).astype(F32)
            yb_ref[rs, :] = _pack_halves(y)

        hs = [up(subs[0])]
        for r, rs in enumerate(subs):
            if r + 1 < len(subs):
                hs.append(up(subs[r + 1]))
            down(rs, *hs[r])

    @pl.when(jnp.logical_not(used))
    def _():
        yb_ref[...] = jnp.zeros(yb_ref.shape, I32)


def _experts(blk_e, nused, xb, w1, w3, w2, nb):
    def rows(p, be, nu):
        return (jnp.minimum(p, nu[0] - 1), 0)

    def wsel(p, be, nu):
        return (be[jnp.minimum(p, nu[0] - 1)], 0, 0)

    grid_spec = pltpu.PrefetchScalarGridSpec(
        num_scalar_prefetch=2,
        grid=(nb,),
        in_specs=[
            pl.BlockSpec((BLK, XW), rows),
            pl.BlockSpec((None, D_MODEL, D_EXPERT), wsel),
            pl.BlockSpec((None, D_MODEL, D_EXPERT), wsel),
            pl.BlockSpec((None, D_EXPERT, D_MODEL), wsel),
        ],
        out_specs=pl.BlockSpec((BLK, D_MODEL // 2), lambda p, be, nu: (p, 0)),
        scratch_shapes=[
            pltpu.VMEM((D_MODEL, D_EXPERT), BF16), pltpu.VMEM((D_MODEL, D_EXPERT), BF16),
            pltpu.VMEM((D_EXPERT, D_MODEL), BF16),
        ],
    )
    return pl.pallas_call(
        _expert_kernel,
        grid_spec=grid_spec,
        out_shape=jax.ShapeDtypeStruct((nb * BLK, D_MODEL // 2), I32),
        compiler_params=pltpu.CompilerParams(
            dimension_semantics=("arbitrary",), vmem_limit_bytes=48 * 1024 * 1024),
        name="experts",
    )(blk_e, nused, xb, w1, w3, w2)


def _combine_kernel(cnt_sm, lo_sm, dst_sm, used_sm, extra_sm, yb_ref, x2_ref, rcol_ref, g_ref, out_ref,
                    yl_s, y_s, sem, sem2):
    i = pl.program_id(0)
    nsteps = pl.num_programs(0)
    bset = i % 2

    def ycopy(s, src, dst, sm):
        return pltpu.make_async_copy(yb_ref.at[pl.ds(src, CCH), :], yl_s.at[s, pl.ds(dst, CCH), :],
                                     sm.at[s])

    def issue(tile, s):
        for e in range(N_EXPERTS):
            lo = pl.multiple_of(lo_sm[tile * N_EXPERTS + e], CCH)
            src = pl.multiple_of(dst_sm[tile * N_EXPERTS + e], SUB)
            ycopy(s, src, lo, sem).start()

        @pl.when(extra_sm[tile] > 0)
        def _():
            def ebody(e, carry):
                n = cnt_sm[tile * N_EXPERTS + e]
                lo = pl.multiple_of(lo_sm[tile * N_EXPERTS + e], CCH)
                src = pl.multiple_of(dst_sm[tile * N_EXPERTS + e], SUB)

                def cbody(c, cc):
                    ycopy(s, src + c * CCH, lo + c * CCH, sem2).start()
                    return cc

                return lax.fori_loop(1, (n + CCH - 1) // CCH, cbody, carry)

            lax.fori_loop(0, N_EXPERTS, ebody, 0)

    @pl.when(i == 0)
    def _():
        yl_s[...] = jnp.zeros(yl_s.shape, I32)
        for s in range(CTILES):
            issue(s, s)

    @pl.when(i + 1 < nsteps)
    def _():
        for s in range(CTILES):
            issue(CTILES * (i + 1) + s, (1 - bset) * CTILES + s)

    lane = lax.broadcasted_iota(I32, (TM, CBLK), 1).astype(F32)
    half = D_MODEL // 2
    for s in range(CTILES):
        t = CTILES * i + s
        slot = bset * CTILES + s
        rows = slice(s * TM, (s + 1) * TM)
        for _ in range(N_EXPERTS):
            ycopy(slot, 0, 0, sem).wait()

        def wbody(j, c, slot=slot):
            ycopy(slot, 0, 0, sem2).wait()
            return c

        lax.fori_loop(0, extra_sm[t], wbody, 0)

        rc = rcol_ref[rows, :]
        pad0 = rc[:, R_PAD0:R_PAD0 + 1]
        pad1 = rc[:, R_PAD1:R_PAD1 + 1]

        def chunk(c, pad0=pad0, pad1=pad1, slot=slot):
            r = lane + float(c * CBLK)
            selm = (jnp.where(pad0 == r, 1.0, 0.0) + jnp.where(pad1 == r, 1.0, 0.0)).astype(BF16)
            lo_h, hi_h = _unpack_halves(yl_s[slot, c * CBLK:(c + 1) * CBLK, :])
            return _dot(selm, lo_h), _dot(selm, hi_h)

        ylo = jnp.zeros((TM, half), F32)
        yhi = jnp.zeros((TM, half), F32)
        for c in range(LOC_COMMON // CBLK):
            dlo, dhi = chunk(c)
            ylo = ylo + dlo
            yhi = yhi + dhi
        y_s[:, :half] = ylo
        y_s[:, half:] = yhi
        for c in range(LOC_COMMON // CBLK, LOC_ROWS // CBLK):
            @pl.when(used_sm[t] > c * CBLK)
            def _(c=c, chunk=chunk):
                dlo, dhi = chunk(c)
                y_s[:, :half] += dlo
                y_s[:, half:] += dhi
        out_ref[rows, :] = _rms(x2_ref[rows, :] + y_s[...], g_ref[...])


def _combine(cnt_f, lo16_f, dst_f, used, extra, yb, x2, rcol, g_final):
    T = x2.shape[0]
    tg = CTILES * TM
    assert T % tg == 0
    grid_spec = pltpu.PrefetchScalarGridSpec(
        num_scalar_prefetch=5,
        grid=(T // tg,),
        in_specs=[
            pl.BlockSpec(memory_space=pl.ANY),
            pl.BlockSpec((tg, D_MODEL), lambda i, *_: (i, 0)),
            pl.BlockSpec((tg, LANES), lambda i, *_: (i, 0)),
            pl.BlockSpec((1, D_MODEL), lambda i, *_: (0, 0)),
        ],
        out_specs=pl.BlockSpec((tg, D_MODEL), lambda i, *_: (i, 0)),
        scratch_shapes=[
            pltpu.VMEM((2 * CTILES, LOC_ROWS, D_MODEL // 2), I32),
            pltpu.VMEM((TM, D_MODEL), F32),
            pltpu.SemaphoreType.DMA((2 * CTILES,)), pltpu.SemaphoreType.DMA((2 * CTILES,)),
        ],
    )
    return pl.pallas_call(
        _combine_kernel,
        grid_spec=grid_spec,
        out_shape=jax.ShapeDtypeStruct((T, D_MODEL), F32),
        compiler_params=pltpu.CompilerParams(
            dimension_semantics=("arbitrary",), vmem_limit_bytes=40 * 1024 * 1024),
        name="combine",
    )(cnt_f, lo16_f, dst_f, used, extra, yb, x2, rcol, g_final)


def _rope_tables(S):
    half = DA_HEAD_DIM // 2
    inv = (1.0 / (np.float32(ROPE_THETA) ** (np.arange(0, DA_HEAD_DIM, 2, dtype=np.float32)
                                             / np.float32(DA_HEAD_DIM)))).astype(np.float32)
    ang = (np.arange(S, dtype=np.float32)[:, None] * inv[None, :]).astype(np.float32)
    cos, sin = np.cos(ang), np.sin(ang)
    lane = np.arange(LANES)
    idx = lane % half
    lower = (lane % DA_HEAD_DIM) < half
    c = cos[:, idx]
    s = sin[:, idx]
    z = np.zeros_like(s)
    return (jnp.asarray(c, F32), jnp.asarray(np.where(lower[None, :], -s, z), F32),
            jnp.asarray(np.where(lower[None, :], z, s), F32))


def _tri_dot(a, b):
    return jnp.dot(a.astype(F32), b.astype(F32), precision=lax.Precision.HIGHEST).astype(I32)


def kernel(x, w_in, conv_w, conv_b, gate_b, lam_qk, subln_g, mhnorm_g, w_out, g_mix, g_ffn, w_grp,
           b_grp, w_erouter, b_erouter, w1, w3, w2, g_final):
    B, S, D = x.shape
    T = B * S
    nt = T // TM
    x2d = x.reshape(T, D)
    l = 0

    w_main = w_in[l, :, :N_MAIN].astype(BF16)
    wg8 = w_in[l, :, N_MAIN:]
    wg = jnp.pad(wg8, ((0, 0), (0, LANES - 8))).astype(BF16)
    wgt = wg8.T.astype(BF16)
    gb8 = gate_b[l].reshape(8)
    gb_col = jnp.pad(gb8, (0, LANES - 8)).reshape(1, LANES)
    gb_row = jnp.broadcast_to(gb8[:, None], (8, LANES))
    rope_c, rope_sa, rope_sb = _rope_tables(S)
    w_r = jnp.concatenate(
        [w_grp[l], w_erouter[l].transpose(1, 0, 2).reshape(D, N_EXPERTS)], axis=1)
    w_r = jnp.pad(w_r, ((0, 0), (0, LANES - w_r.shape[1])))
    wr_hi = w_r.astype(BF16)
    wr_t = jnp.concatenate([wr_hi, (w_r - wr_hi.astype(F32)).astype(BF16)], axis=1)
    b_r = jnp.pad(jnp.concatenate([b_grp[l], b_erouter[l].reshape(N_EXPERTS)]),
                  (0, LANES - N_GROUPS - N_EXPERTS))
    b_rt = jnp.broadcast_to(b_r[:, None], (LANES, TM))

    q1, q2, kr, va, qc, kc, vm, om, gcol, grow = _inproj(
        x2d, g_mix[l].reshape(1, D), w_main, wg, wgt, rope_c, rope_sa, rope_sb,
        conv_w[l], conv_b[l].reshape(1, -1), S)
    oa = _attention(q1, q2, kr, va, lam_qk[l], subln_g[l].reshape(1, LANES), B, S)
    hm = _mlstm(qc, kc, vm, om, gcol, grow, gb_col, gb_row, mhnorm_g[l].reshape(1, ML_WIDTH), B, S)
    x2, hn, rrow, rcol, cnt3 = _router(oa, hm, x2d, w_out[l].astype(BF16), g_ffn[l].reshape(1, D),
                                       wr_t, b_rt)

    cnt = cnt3[:, 0, EXP_ROW0:EXP_ROW0 + N_EXPERTS]
    c8 = (cnt + SUB - 1) // SUB * SUB
    c16 = jnp.maximum((cnt + CCH - 1) // CCH, 1) * CCH
    odd = (np.arange(nt) % 2)[:, None]
    c8_even = c8 * (1 - odd)
    c8_odd = c8 * odd
    even_tot = jnp.sum(c8_even, axis=0)
    count8 = even_tot + DCH + jnp.sum(c8_odd, axis=0)
    pcount = (count8 + DCH + BLK - 1) // BLK * BLK
    up_e = np.triu(np.ones((N_EXPERTS, N_EXPERTS), np.float32), 1)
    lo_t = np.tril(np.ones((nt, nt), np.float32), -1)
    pstart = _tri_dot(pcount[None, :], up_e)[0]
    pend = pstart + pcount
    dst = pstart[None, :] + jnp.where(odd == 0, _tri_dot(lo_t, c8_even),
                                      (even_tot + DCH)[None, :] + _tri_dot(lo_t, c8_odd))
    lo8 = _tri_dot(c8, up_e)
    lo16 = _tri_dot(c16, up_e)
    nb = (2 * T + nt * N_EXPERTS * (SUB - 1) + N_EXPERTS * (2 * DCH + BLK - 1)) // BLK + 1
    blk_row = jnp.arange(nb, dtype=I32) * BLK
    blk_e = jnp.minimum(jnp.sum((pend[None, :] <= blk_row[:, None]).astype(I32), axis=1),
                        N_EXPERTS - 1).astype(I32)
    nused = (pend[-1] // BLK).astype(I32).reshape(1)
    cnt_f = cnt.reshape(-1).astype(I32)
    dst_f = dst.reshape(-1).astype(I32)

    spare = jnp.full((N_EXPERTS,), DCH, I32)
    gap_start = jnp.concatenate([pstart + even_tot, pstart + count8, pstart + count8 + DCH,
                                 pend[-1:]]).astype(I32)
    gap_rows = jnp.concatenate([spare, spare, pcount - count8 - DCH,
                                nb * BLK - pend[-1:]]).astype(I32)
    extra = jnp.sum(jnp.maximum((cnt + DCH - 1) // DCH - 1, 0), axis=1).astype(I32)
    xb = _dispatch(cnt_f, lo8.reshape(-1), dst_f, extra, gap_start, gap_rows, hn, rrow, rcol, nb)
    yb = _experts(blk_e, nused, xb, w1[l], w3[l], w2[l], nb)
    used = jnp.sum(c16, axis=1).astype(I32)
    out = _combine(cnt_f, lo16.reshape(-1), dst_f, used, extra, yb, x2, rcol, g_final.reshape(1, D))
    return out.reshape(B, S, D)
```

```python
import functools
import math

import numpy as np
import jax
import jax.numpy as jnp
from jax import lax
from jax.experimental import pallas as pl
from jax.experimental.pallas import tpu as pltpu

F32 = jnp.float32
BF16 = jnp.bfloat16
I32 = jnp.int32

D_MODEL = 1024
DA_HEADS = 4
DA_HEAD_DIM = 64
DA_V_DIM = 128
DA_WIDTH = 512
ML_HEADS = 4
ML_WIDTH = 512
ML_HEAD_DIM = 128
ML_CHUNK = 128
ML_GROUP = 4
CONV_K = 4
ROPE_THETA = 10000.0
RMS_EPS = 1e-6
N_GROUPS = 4
EXPERTS_PER_GROUP = 8
N_EXPERTS = 32
D_EXPERT = 512
LAMBDA_INIT = 0.8 - 0.6 * math.exp(-0.3 * 0)

LANES = 128
SUB = 8
N_MAIN = 7 * 512
NEG = -1e30

TM_PROJ = 1024
TQ = 256
TK = 512
VT_ROWS = DA_V_DIM + 16
TM = 256
BLK = 512
ESUB = 256
CBLK = 256
CTILES = 2
DCH = 32
CCH = 32
DLOC_ROWS = 768
N_GAPS_NEAR = 2 * N_EXPERTS
N_GAPS = N_GAPS_NEAR + N_EXPERTS + 1
LOC_COMMON = N_EXPERTS * CCH
LOC_ROWS = 2 * TM + N_EXPERTS * CCH
XW = D_MODEL // 2 + LANES
EXP_ROW0 = 4
RT_ROWS = 40
RT_GROUP = 4
R_E0, R_E1, R_POS0, R_POS1, R_PAD0, R_PAD1, R_W0, R_W1 = range(8)


def _nt_dot(a, b):
    return lax.dot_general(a, b, (((1,), (1,)), ((), ())), preferred_element_type=F32)


def _dot(a, b):
    return jnp.dot(a, b, preferred_element_type=F32)


def _split3(x):
    hi = x.astype(BF16)
    r = x - hi.astype(F32)
    mid = r.astype(BF16)
    lo = (r - mid.astype(F32)).astype(BF16)
    return hi, mid, lo


def _rms(x, g):
    return x * lax.rsqrt(jnp.mean(x * x, axis=-1, keepdims=True) + RMS_EPS) * g


def _inproj_kernel(x_ref, g_ref, w_ref, wg_ref, wgt_ref, c_ref, sa_ref, sb_ref, cw_ref, cb_ref,
                   q1_ref, q2_ref, kr_ref, va_ref, qc_ref, kc_ref, vm_ref, om_ref, gcol_ref, grow_ref,
                   h_s, cq_s, ck_s, *, tiles_per_seq):
    tm = x_ref.shape[0]
    cw = 512
    first = (pl.program_id(0) % tiles_per_seq) == 0

    @pl.when(first)
    def _():
        cq_s[0:8, :] = jnp.zeros((8, cw), F32)
        ck_s[0:8, :] = jnp.zeros((8, cw), F32)

    @pl.when(jnp.logical_not(first))
    def _():
        cq_s[0:8, :] = cq_s[tm:tm + 8, :]
        ck_s[0:8, :] = ck_s[tm:tm + 8, :]

    h_s[...] = _rms(x_ref[...], g_ref[...]).astype(BF16)
    nhalf = 2 if tm >= 512 else 1
    th = tm // nhalf
    halves = [slice(p * th, (p + 1) * th) for p in range(nhalf)]

    def cols(c, hv):
        return _dot(h_s[hv, :], w_ref[:, c * cw:(c + 1) * cw])

    rb = min(128, th)
    lane = lax.broadcasted_iota(I32, (rb, LANES), 1)
    qscale = DA_HEAD_DIM ** -0.5 * math.log2(math.e)

    def rope(x, rows):
        return (x * c_ref[rows, :] + pltpu.roll(x, 96, 1) * sa_ref[rows, :]
                + pltpu.roll(x, 32, 1) * sb_ref[rows, :])

    def blocks(hv):
        for r in range(th // rb):
            for hh in range(cw // LANES):
                yield (slice(hv.start + r * rb, hv.start + (r + 1) * rb), slice(r * rb, (r + 1) * rb),
                       slice(hh * LANES, (hh + 1) * LANES))

    def rope_q(qa, hv):
        for rows, loc, hs in blocks(hv):
            qr = rope(qa[loc, hs], rows) * qscale
            q1_ref[rows, hs] = jnp.where(lane < DA_HEAD_DIM, qr, 0.0).astype(BF16)
            q2_ref[rows, hs] = jnp.where(lane >= DA_HEAD_DIM, qr, 0.0).astype(BF16)

    def rope_k(ka, hv):
        for rows, loc, hs in blocks(hv):
            kr_ref[rows, hs] = rope(ka[loc, hs], rows).astype(BF16)

    def conv_silu(src, c0, hv, out_ref, scale):
        for rows, _, hs in blocks(hv):
            cs = slice(c0 + hs.start, c0 + hs.stop)
            w = cw_ref[:, cs]
            y = cb_ref[:, cs] + src[8 + rows.start:8 + rows.stop, hs] * w[3:4]
            for s in (1, 2, 3):
                y = y + src[8 + rows.start - s:8 + rows.stop - s, hs] * w[3 - s:4 - s]
            y = y * jax.nn.sigmoid(y)
            out_ref[rows, hs] = (y if scale is None else y * scale).astype(BF16)

    def to_conv(c, dst, hv):
        dst[8 + hv.start:8 + hv.stop, :] = cols(c, hv)

    qa = [cols(0, hv) for hv in halves]
    ka = []
    for p, hv in enumerate(halves):
        ka.append(cols(1, hv))
        rope_q(qa[p], hv)
    for p, hv in enumerate(halves):
        va_ref[hv, :] = cols(2, hv).astype(BF16)
        rope_k(ka[p], hv)
    for hv in halves:
        to_conv(3, cq_s, hv)
    for hv in halves:
        to_conv(4, ck_s, hv)
        conv_silu(cq_s, 0, hv, qc_ref, None)
    for hv in halves:
        vm_ref[hv, :] = cols(5, hv).astype(BF16)
        conv_silu(ck_s, ML_WIDTH, hv, kc_ref, ML_HEAD_DIM ** -0.5)
    om_ref[...] = _dot(h_s[...], w_ref[:, 6 * cw:7 * cw]).astype(BF16)
    gcol_ref[...] = _dot(h_s[...], wg_ref[...])
    grow_ref[...] = _nt_dot(wgt_ref[...], h_s[...])


def _inproj(x2d, g_mix, w_main, wg, wgt, rope_c, rope_sa, rope_sb, conv_w, conv_b, S):
    T = x2d.shape[0]
    tm = min(TM_PROJ, S)
    assert S % tm == 0
    tps = S // tm
    full = lambda shape: pl.BlockSpec(shape, lambda i: (0, 0))
    rope_spec = pl.BlockSpec((tm, LANES), lambda i: (i % tps, 0))
    col_spec = pl.BlockSpec((tm, 512), lambda i: (i, 0))
    return pl.pallas_call(
        functools.partial(_inproj_kernel, tiles_per_seq=tps),
        grid=(T // tm,),
        in_specs=[
            pl.BlockSpec((tm, D_MODEL), lambda i: (i, 0)),
            full((1, D_MODEL)), full((D_MODEL, N_MAIN)), full((D_MODEL, LANES)), full((8, D_MODEL)),
            rope_spec, rope_spec, rope_spec,
            full((CONV_K, 2 * ML_WIDTH)), full((1, 2 * ML_WIDTH)),
        ],
        out_specs=[col_spec] * 8 + [
            pl.BlockSpec((tm, LANES), lambda i: (i, 0)),
            pl.BlockSpec((8, tm), lambda i: (0, i)),
        ],
        out_shape=[jax.ShapeDtypeStruct((T, 512), BF16)] * 8 + [
            jax.ShapeDtypeStruct((T, LANES), F32),
            jax.ShapeDtypeStruct((8, T), F32),
        ],
        scratch_shapes=[pltpu.VMEM((tm, D_MODEL), BF16),
                        pltpu.VMEM((tm + 8, ML_WIDTH), F32), pltpu.VMEM((tm + 8, ML_WIDTH), F32)],
        compiler_params=pltpu.CompilerParams(
            dimension_semantics=("arbitrary",), vmem_limit_bytes=56 * 1024 * 1024),
        name="inproj",
    )(x2d, g_mix, w_main, wg, wgt, rope_c, rope_sa, rope_sb, conv_w, conv_b)


def _attn_kernel(q1_s, q2_s, k_s, v_ref, lam_ref, g_ref, o_ref, vt_s, acc_s, st_s):
    S = v_ref.shape[0]

    def prep(r, carry):
        rows = pl.ds(pl.multiple_of(r * TQ, TQ), TQ)
        for h in range(DA_HEADS):
            hs = slice(h * LANES, (h + 1) * LANES)
            vt_s[h, :DA_V_DIM, rows] = v_ref[rows, hs].astype(F32).T.astype(BF16)
        return carry

    lax.fori_loop(0, S // TQ, prep, 0)
    vt_s[:, DA_V_DIM:, :] = jnp.ones((DA_HEADS, VT_ROWS - DA_V_DIM, S), BF16)

    lq = lam_ref[...]
    lam = (jnp.exp(jnp.sum(lq[0:1] * lq[1:2], axis=-1, keepdims=True))
           - jnp.exp(jnp.sum(lq[2:3] * lq[3:4], axis=-1, keepdims=True)) + LAMBDA_INIT)

    key = lax.broadcasted_iota(I32, (TK, 2 * TQ), 0)
    qry = lax.broadcasted_iota(I32, (TK, 2 * TQ), 1)
    kq = key - jnp.where(qry >= TQ, qry - TQ, qry)

    def scores(qo, ko, tk, h, causal=None):
        hs = slice(h * LANES, (h + 1) * LANES)
        qq = jnp.concatenate([q1_s[pl.ds(qo, TQ), hs], q2_s[pl.ds(qo, TQ), hs]], axis=0)
        st = _nt_dot(k_s[pl.ds(ko, tk), hs], qq)
        st_s[h, :tk, :] = st if causal is None else jnp.where(causal, st, NEG)

    def kvstep(qo, ko, tk, ms, causal, slot, next_ko=None):
        out = []
        if next_ko is None:
            scores(qo, ko, tk, 0, causal)
        for h in range(DA_HEADS):
            if h + 1 < DA_HEADS:
                scores(qo, ko, tk, h + 1, causal)
            elif next_ko is not None:
                scores(qo, next_ko, tk, 0)
            st = st_s[h, :tk, :]
            m_new = jnp.maximum(ms[h], jnp.max(st, axis=0, keepdims=True))
            alpha = jnp.exp2(ms[h] - m_new)
            p = jnp.exp2(st - m_new).astype(BF16)
            acc_s[slot, h] = alpha * acc_s[slot, h] + _dot(vt_s[h, :, pl.ds(ko, tk)], p)
            out.append(m_new)
        return tuple(out)

    def start(i, ko, tk, off, slot):
        acc_s[slot] = jnp.zeros(acc_s.shape[1:], F32)
        m0 = tuple(jnp.full((1, 2 * TQ), NEG, F32) for _ in range(DA_HEADS))
        return kvstep(pl.multiple_of(i * TQ, TQ), ko, tk, m0, kq[:tk] <= off, slot)

    def full_steps(i, n, ms, slot):
        qo = pl.multiple_of(i * TQ, TQ)
        scores(qo, 0, TK, 0)

        def step(j, m):
            nxt = pl.multiple_of(jnp.minimum(j + 1, n - 1) * TK, TK)
            return kvstep(qo, pl.multiple_of(j * TK, TK), TK, m, None, slot, next_ko=nxt)

        lax.fori_loop(0, n, step, ms)

    def finalize(i, slot):
        qo = pl.multiple_of(i * TQ, TQ)
        for h in range(DA_HEADS):
            hs = slice(h * LANES, (h + 1) * LANES)
            acc = acc_s[slot, h]
            on = acc[:DA_V_DIM] * (1.0 / acc[DA_V_DIM:DA_V_DIM + 1])
            ot = on[:, :TQ] - lam * on[:, TQ:]
            o = _rms(ot.T, g_ref[...]) * (1.0 - LAMBDA_INIT)
            o_ref[pl.ds(qo, TQ), hs] = o.astype(BF16)

    def tile_pair(u, carry):
        ko = pl.multiple_of(u * TK, TK)
        finalize(2 * u - 1, 1)
        full_steps(2 * u, u, start(2 * u, ko, TQ, 0, 0), 0)
        finalize(2 * u, 0)
        full_steps(2 * u + 1, u, start(2 * u + 1, ko, TK, TQ, 1), 1)
        return carry

    assert TK == 2 * TQ and (S // TQ) % 2 == 0
    start(0, 0, TQ, 0, 0)
    finalize(0, 0)
    start(1, 0, TK, TQ, 1)
    lax.fori_loop(1, S // TK, tile_pair, 0)
    finalize(S // TQ - 1, 1)


def _attention(q1, q2, kr, va, lam_qk, subln_g, B, S):
    T = B * S
    full = lambda shape: pl.BlockSpec(shape, lambda b: (0, 0))
    seq = pl.BlockSpec((S, DA_WIDTH), lambda b: (b, 0))
    return pl.pallas_call(
        _attn_kernel,
        grid=(B,),
        in_specs=[seq, seq, seq, seq, full((4, DA_HEAD_DIM)), full((1, LANES))],
        out_specs=seq,
        out_shape=jax.ShapeDtypeStruct((T, DA_WIDTH), BF16),
        scratch_shapes=[
            pltpu.VMEM((DA_HEADS, VT_ROWS, S), BF16),
            pltpu.VMEM((2, DA_HEADS, VT_ROWS, 2 * TQ), F32),
            pltpu.VMEM((DA_HEADS, TK, 2 * TQ), F32),
        ],
        compiler_params=pltpu.CompilerParams(
            dimension_semantics=("arbitrary",), vmem_limit_bytes=48 * 1024 * 1024),
        name="attn",
    )(q1, q2, kr, va, lam_qk, subln_g)


def _log_sigmoid(x):
    return jnp.minimum(x, 0.0) - jnp.log(1.0 + jnp.exp(-jnp.abs(x)))


def _mlstm_kernel(qc_s, kc_s, v_ref, o_ref, gcol_ref, grow_ref, gbc_ref, gbr_ref,
                  mg_ref, out_ref, vt_s, ct_s, m_s):
    S = v_ref.shape[0]
    L = ML_CHUNK
    nc = S // L

    def transpose_v(c, carry):
        ro = pl.multiple_of(c * L, L)
        for h in range(ML_HEADS):
            hs = slice(h * ML_HEAD_DIM, (h + 1) * ML_HEAD_DIM)
            vt_s[h, :ML_HEAD_DIM, pl.ds(ro, L)] = v_ref[pl.ds(ro, L), hs].astype(F32).T.astype(BF16)
        return carry

    lax.fori_loop(0, nc, transpose_v, 0)
    vt_s[:, ML_HEAD_DIM:, :] = jnp.ones((ML_HEADS, ML_HEAD_DIM, S), BF16)

    ct_s[...] = jnp.zeros(ct_s.shape, F32)
    m_s[...] = jnp.zeros(m_s.shape, F32)

    ri = lax.broadcasted_iota(I32, (L, L), 0)
    ci = lax.broadcasted_iota(I32, (L, L), 1)
    causal_t = ri <= ci
    tril = jnp.where(ci <= ri, 1.0, 0.0).astype(BF16)
    triu = jnp.where(causal_t, 1.0, 0.0).astype(BF16)

    heads = [slice(h * ML_HEAD_DIM, (h + 1) * ML_HEAD_DIM) for h in range(ML_HEADS)]

    def gates(c):
        ro = pl.multiple_of(c * L, L)
        gc = gcol_ref[pl.ds(ro, L), :] + gbc_ref[...]
        gr = grow_ref[:, pl.ds(ro, L)] + gbr_ref[...]
        b_c = sum(_dot(tril, p) for p in _split3(_log_sigmoid(gc)))
        b_r = sum(_dot(p, triu) for p in _split3(_log_sigmoid(gr)))
        return ro, gc, gr, b_c, b_r

    def read_state(g):
        ro = g[0]
        ks, vts, kq, cq, ms = [], [], [], [], []
        for h, hs in enumerate(heads):
            q = qc_s[pl.ds(ro, L), hs]
            k = kc_s[pl.ds(ro, L), hs]
            ks.append(k)
            vts.append(vt_s[h, :, pl.ds(ro, L)])
            kq.append(_nt_dot(k, q))
            cq.append(_nt_dot(ct_s[h].astype(BF16), q))
            ms.append(m_s[h:h + 1, 0:1])
        return ks, vts, kq, cq, ms

    def update_state(g, d):
        _, _, gr, _, b_r = g
        ks, vts, _, _, ms = d
        for h in range(ML_HEADS):
            br = b_r[4 + h:5 + h, :]
            bl = br[:, L - 1:L]
            dec = bl - br + gr[h:h + 1, :]
            m_new = jnp.maximum(bl + ms[h], jnp.max(dec, axis=1, keepdims=True))
            ws = jnp.exp(dec - m_new)
            sc = jnp.exp(bl + ms[h] - m_new)
            vw = (vts[h].astype(F32) * ws).astype(BF16)
            ct_s[h] = sc * ct_s[h] + _dot(vw, ks[h])
            m_s[h:h + 1, :] = jnp.broadcast_to(m_new, (1, LANES))

    def outputs(g, d):
        ro, gc, _, b_c, b_r = g
        _, vts, kq, cq, ms = d
        for h, hs in enumerate(heads):
            br = b_r[4 + h:5 + h, :]
            a_col = b_c[:, 4 + h:5 + h] - gc[:, h:h + 1]
            dm = jnp.where(causal_t, br - a_col, NEG)
            inter = br + ms[h]
            m_row = jnp.maximum(inter, jnp.max(dm, axis=0, keepdims=True))
            sm = jnp.exp(dm - m_row) * kq[h]
            sc_in = jnp.exp(inter - m_row)
            a = _dot(vts[h], sm.astype(BF16)) + sc_in * cq[h]
            den = a[ML_HEAD_DIM:ML_HEAD_DIM + 1, :]
            hh = (a[:ML_HEAD_DIM] / jnp.maximum(jnp.abs(den), jnp.exp(-m_row))).T
            hg = hh * jax.nn.sigmoid(o_ref[pl.ds(ro, L), hs].astype(F32))
            out_ref[pl.ds(ro, L), hs] = _rms(hg, mg_ref[:, hs]).astype(BF16)

    def chunk_group(cg, carry):
        gs = [gates(ML_GROUP * cg + j) for j in range(ML_GROUP)]
        ds = []
        for g in gs:
            ds.append(read_state(g))
            update_state(g, ds[-1])
        for g, d in zip(gs, ds):
            outputs(g, d)
        return carry

    lax.fori_loop(0, nc // ML_GROUP, chunk_group, 0)


def _mlstm(qc, kc, vm, om, gcol, grow, gb_col, gb_row, mh_g, B, S):
    T = B * S
    full = lambda shape: pl.BlockSpec(shape, lambda b: (0, 0))
    seq = pl.BlockSpec((S, ML_WIDTH), lambda b: (b, 0))
    return pl.pallas_call(
        _mlstm_kernel,
        grid=(B,),
        in_specs=[
            seq, seq, seq, seq,
            pl.BlockSpec((S, LANES), lambda b: (b, 0)),
            pl.BlockSpec((8, S), lambda b: (0, b)),
            full((1, LANES)), full((8, LANES)), full((1, ML_WIDTH)),
        ],
        out_specs=seq,
        out_shape=jax.ShapeDtypeStruct((T, ML_WIDTH), BF16),
        scratch_shapes=[
            pltpu.VMEM((ML_HEADS, 2 * ML_HEAD_DIM, S), BF16),
            pltpu.VMEM((ML_HEADS, 2 * ML_HEAD_DIM, ML_HEAD_DIM), F32),
            pltpu.VMEM((8, LANES), F32),
        ],
        compiler_params=pltpu.CompilerParams(
            dimension_semantics=("arbitrary",), vmem_limit_bytes=56 * 1024 * 1024),
        name="mlstm",
    )(qc, kc, vm, om, gcol, grow, gb_col, gb_row, mh_g)


def _router_kernel(oa_ref, hm_ref, x_ref, wo_ref, g_ref, wr_ref, brt_ref,
                   x2_ref, hn_ref, rrow_ref, rcol_ref, cnt_ref):
    def project(s):
        rs = slice(s * TM, (s + 1) * TM)
        mixo = _dot(oa_ref[rs, :], wo_ref[:DA_WIDTH, :]) + _dot(hm_ref[rs, :], wo_ref[DA_WIDTH:, :])
        x2 = x_ref[rs, :] + mixo
        x2_ref[rs, :] = x2
        hn = _rms(x2, g_ref[...])
        hn_hi = hn.astype(BF16)
        hn_ref[rs, :] = hn_hi
        hn_lo = (hn - hn_hi.astype(F32)).astype(BF16)
        a = _dot(hn_hi, wr_ref[...])
        b = _dot(hn_lo, wr_ref[:, :LANES])
        return (a[:, :LANES] + a[:, LANES:] + b).T[:RT_ROWS] + brt_ref[:RT_ROWS, :]

    lts = [project(0)]
    for s in range(RT_GROUP):
        if s + 1 < RT_GROUP:
            lts.append(project(s + 1))
        _route(s, lts[s], rrow_ref, rcol_ref, cnt_ref)


def _route(s, lt, rrow_ref, rcol_ref, cnt_ref):
    sub = lax.broadcasted_iota(I32, (RT_ROWS, TM), 0)
    sub_f = sub.astype(F32)

    def cmax(v):
        return jnp.max(v, axis=0, keepdims=True)

    def first_idx(mask):
        return jnp.min(jnp.where(mask, sub_f, 1e6), axis=0, keepdims=True).astype(I32)

    gl = jnp.where(sub < N_GROUPS, lt, NEG)
    gmax = cmax(gl)
    gsel = first_idx(gl == gmax)
    g_w = 1.0 / jnp.sum(jnp.exp(gl - gmax), axis=0, keepdims=True)
    elo = EXP_ROW0 + gsel * EXPERTS_PER_GROUP
    el = jnp.where((sub >= elo) & (sub < elo + EXPERTS_PER_GROUP), lt, NEG)
    v1 = cmax(el)
    i1 = first_idx(el == v1)
    el2 = jnp.where(sub == i1, NEG, el)
    v2 = cmax(el2)
    i2 = first_idx(el2 == v2)
    t = jnp.exp(v2 - v1)
    w0 = g_w / (1.0 + t)
    w1 = g_w * t / (1.0 + t)

    oh0 = jnp.where(sub == i1, 1.0, 0.0)
    oh1 = jnp.where(sub == i2, 1.0, 0.0)
    mh = oh0 + oh1
    r_i = lax.broadcasted_iota(I32, (TM, TM), 0)
    c_i = lax.broadcasted_iota(I32, (TM, TM), 1)
    before = jnp.where(r_i < c_i, 1.0, 0.0).astype(BF16)
    mh_f = jnp.concatenate([mh, jnp.zeros((LANES - RT_ROWS, TM), F32)], axis=0)
    mh_b = mh_f.astype(BF16)
    pre = _dot(mh_b, before)[:RT_ROWS]
    cnt = jnp.sum(mh_f, axis=1, keepdims=True)
    cnt8 = jnp.floor((cnt + (SUB - 1)) * (1.0 / SUB)) * SUB
    erow = lax.broadcasted_iota(I32, (LANES, 1), 0)
    is_exp = (erow >= EXP_ROW0) & (erow < EXP_ROW0 + N_EXPERTS)
    cnt_slot = jnp.where(is_exp, jnp.maximum(jnp.floor((cnt + (CCH - 1)) * (1.0 / CCH)), 1.0) * CCH, 0.0)
    e_r = lax.broadcasted_iota(I32, (LANES, LANES), 0)
    e_c = lax.broadcasted_iota(I32, (LANES, LANES), 1)
    below = jnp.where(e_c < e_r, 1.0, 0.0).astype(BF16)
    lo8 = _dot(below, jnp.broadcast_to(cnt8, (LANES, LANES)).astype(BF16))[:RT_ROWS, 0:1]
    lo_slot = _dot(below, jnp.broadcast_to(cnt_slot, (LANES, LANES)).astype(BF16))[:RT_ROWS, 0:1]

    def csum(v):
        return jnp.sum(v, axis=0, keepdims=True)

    rows = [None] * 8
    rows[R_E0] = (i1 - EXP_ROW0).astype(F32)
    rows[R_E1] = (i2 - EXP_ROW0).astype(F32)
    rows[R_POS0] = csum(oh0 * (pre + lo8))
    rows[R_POS1] = csum(oh1 * (pre + lo8))
    rows[R_PAD0] = csum(oh0 * (pre + lo_slot))
    rows[R_PAD1] = csum(oh1 * (pre + lo_slot))
    rows[R_W0] = w0
    rows[R_W1] = w1
    sub128 = lax.broadcasted_iota(I32, (LANES, TM), 0)
    r128 = jnp.zeros((LANES, TM), F32)
    for j, v in enumerate(rows):
        r128 = jnp.where(sub128 == j, v, r128)
    rrow_ref[:, s * TM:(s + 1) * TM] = r128[:8]
    rcol_ref[s * TM:(s + 1) * TM, :] = r128.T
    cnt_ref[s] = _nt_dot(jnp.ones((8, TM), BF16), mh_b)[0:1].astype(I32)


def _router(oa, hm, x2d, w_out, g_ffn, wr_t, b_rt):
    T = x2d.shape[0]
    nt = T // TM
    full = lambda shape: pl.BlockSpec(shape, lambda i: (0, 0))
    tg = RT_GROUP * TM
    return pl.pallas_call(
        _router_kernel,
        grid=(T // tg,),
        in_specs=[
            pl.BlockSpec((tg, DA_WIDTH), lambda i: (i, 0)),
            pl.BlockSpec((tg, ML_WIDTH), lambda i: (i, 0)),
            pl.BlockSpec((tg, D_MODEL), lambda i: (i, 0)),
            full((D_MODEL, D_MODEL)), full((1, D_MODEL)),
            full((D_MODEL, 2 * LANES)), full((LANES, TM)),
        ],
        out_specs=[
            pl.BlockSpec((tg, D_MODEL), lambda i: (i, 0)),
            pl.BlockSpec((tg, D_MODEL), lambda i: (i, 0)),
            pl.BlockSpec((8, tg), lambda i: (0, i)),
            pl.BlockSpec((tg, LANES), lambda i: (i, 0)),
            pl.BlockSpec((RT_GROUP, 1, LANES), lambda i: (i, 0, 0)),
        ],
        out_shape=[
            jax.ShapeDtypeStruct((T, D_MODEL), F32),
            jax.ShapeDtypeStruct((T, D_MODEL), BF16),
            jax.ShapeDtypeStruct((8, T), F32),
            jax.ShapeDtypeStruct((T, LANES), F32),
            jax.ShapeDtypeStruct((nt, 1, LANES), I32),
        ],
        compiler_params=pltpu.CompilerParams(
            dimension_semantics=("arbitrary",), vmem_limit_bytes=40 * 1024 * 1024),
        name="router",
    )(oa, hm, x2d, w_out, g_ffn, wr_t, b_rt)


def _pack_halves(x):
    lo = lax.bitcast_convert_type(x[:, :D_MODEL // 2], I32)
    hi = lax.bitcast_convert_type(x[:, D_MODEL // 2:], I32)
    return lax.shift_right_logical(lo, 16) | (hi & jnp.int32(-65536))


def _unpack_halves(w):
    lo = lax.bitcast_convert_type(lax.shift_left(w, 16), F32)
    hi = lax.bitcast_convert_type(w & jnp.int32(-65536), F32)
    return lo.astype(BF16), hi.astype(BF16)


def _dispatch_kernel(cnt_sm, lo_sm, dst_sm, extra_sm, gs_sm, gn_sm, hn_ref, rrow_ref, rcol_ref, xb_ref,
                     xs_s, z_s, semx, semx2, semz, semz2):
    i = pl.program_id(0)
    nsteps = pl.num_programs(0)
    half = D_MODEL // 2

    def gap_fill(first, last, sem, wait):
        def ebody(e, carry):
            start = pl.multiple_of(gs_sm[e], SUB)
            left = gn_sm[e]
            for rows in (BLK, DCH, SUB):
                n = left // rows

                def body(c, cc, rows=rows, start=start):
                    cp = pltpu.make_async_copy(
                        z_s.at[pl.ds(0, rows), :],
                        xb_ref.at[pl.ds(pl.multiple_of(start + c * rows, SUB), rows), :], sem)
                    cp.wait() if wait else cp.start()
                    return cc

                lax.fori_loop(0, n, body, 0)
                start = start + n * rows
                left = left - n * rows
            return carry

        lax.fori_loop(first, last, ebody, 0)

    @pl.when(i == 0)
    def _():
        for s in range(2):
            xs_s[s, DLOC_ROWS:, :] = jnp.zeros((DCH, XW), I32)
        z_s[...] = jnp.zeros(z_s.shape, I32)
        gap_fill(0, N_GAPS_NEAR, semz, False)
        gap_fill(N_GAPS_NEAR, N_GAPS, semz2, False)
        gap_fill(0, N_GAPS_NEAR, semz, True)

    def xcopy(s, src, dst, sem):
        return pltpu.make_async_copy(xs_s.at[s, pl.ds(src, DCH), :], xb_ref.at[pl.ds(dst, DCH), :],
                                     sem.at[s])

    def wait_tile(s, n_extra):
        for _ in range(N_EXPERTS):
            xcopy(s, 0, 0, semx).wait()

        def wbody(j, c):
            xcopy(s, 0, 0, semx2).wait()
            return c
        lax.fori_loop(0, n_extra, wbody, 0)

    r_i = lax.broadcasted_iota(I32, (DLOC_ROWS, TM), 0).astype(F32)
    lane = lax.broadcasted_iota(I32, (TM, LANES), 1)
    l_r = lax.broadcasted_iota(I32, (LANES, LANES), 0)
    ones3 = jnp.where(l_r < 3, 1.0, 0.0).astype(BF16)

    for s in range(2):
        t = 2 * i + s
        rows = slice(s * TM, (s + 1) * TM)

        @pl.when(i >= 1)
        def _(s=s, t=t):
            wait_tile(s, extra_sm[jnp.maximum(t - 2, 0)])

        rr = rrow_ref[:, rows]
        rc = rcol_ref[rows, :]
        perm0 = jnp.where(r_i == rr[R_POS0:R_POS0 + 1, :], 1.0, 0.0).astype(BF16)
        perm1 = jnp.where(r_i == rr[R_POS1:R_POS1 + 1, :], 1.0, 0.0).astype(BF16)
        xs = _dot(perm0 + perm1, hn_ref[rows, :])
        xs_s[s, :DLOC_ROWS, :half] = _pack_halves(xs)

        def parts(col, rc=rc):
            hi, mid, lo = (p.astype(F32)
                           for p in _split3(jnp.broadcast_to(rc[:, col:col + 1], (TM, LANES))))
            sel = jnp.where(lane == 0, hi, jnp.where(lane == 1, mid, jnp.where(lane == 2, lo, 0.0)))
            return sel.astype(BF16)

        wparts = _dot(perm0, parts(R_W0)) + _dot(perm1, parts(R_W1))
        wsort = _dot(wparts.astype(BF16), ones3)
        xs_s[s, :DLOC_ROWS, half:] = lax.bitcast_convert_type(wsort, I32)

        for e in range(N_EXPERTS):
            lo = pl.multiple_of(lo_sm[t * N_EXPERTS + e], SUB)
            dst = pl.multiple_of(dst_sm[t * N_EXPERTS + e], SUB)
            xcopy(s, lo, dst, semx).start()

        @pl.when(extra_sm[t] > 0)
        def _(s=s, t=t):
            def ebody(e, carry):
                n = cnt_sm[t * N_EXPERTS + e]
                lo = pl.multiple_of(lo_sm[t * N_EXPERTS + e], SUB)
                dst = pl.multiple_of(dst_sm[t * N_EXPERTS + e], SUB)

                def cbody(c, cc):
                    xcopy(s, lo + c * DCH, dst + c * DCH, semx2).start()
                    return cc

                return lax.fori_loop(1, (n + DCH - 1) // DCH, cbody, carry)

            lax.fori_loop(0, N_EXPERTS, ebody, 0)

    @pl.when(i == nsteps - 1)
    def _():
        for s in range(2):
            wait_tile(s, extra_sm[2 * i + s])
        gap_fill(N_GAPS_NEAR, N_GAPS, semz2, True)


def _dispatch(cnt_f, lo_f, dst_f, extra, gap_start, gap_rows, hn, rrow, rcol, nb):
    T = hn.shape[0]
    assert T % (2 * TM) == 0
    grid_spec = pltpu.PrefetchScalarGridSpec(
        num_scalar_prefetch=6,
        grid=(T // (2 * TM),),
        in_specs=[
            pl.BlockSpec((2 * TM, D_MODEL), lambda i, *_: (i, 0)),
            pl.BlockSpec((8, 2 * TM), lambda i, *_: (0, i)),
            pl.BlockSpec((2 * TM, LANES), lambda i, *_: (i, 0)),
        ],
        out_specs=pl.BlockSpec(memory_space=pl.ANY),
        scratch_shapes=[
            pltpu.VMEM((2, DLOC_ROWS + DCH, XW), I32),
            pltpu.VMEM((BLK, XW), I32),
            pltpu.SemaphoreType.DMA((2,)), pltpu.SemaphoreType.DMA((2,)),
            pltpu.SemaphoreType.DMA(()), pltpu.SemaphoreType.DMA(()),
        ],
    )
    return pl.pallas_call(
        _dispatch_kernel,
        grid_spec=grid_spec,
        out_shape=jax.ShapeDtypeStruct((nb * BLK, XW), I32),
        compiler_params=pltpu.CompilerParams(
            dimension_semantics=("arbitrary",), vmem_limit_bytes=40 * 1024 * 1024),
        name="dispatch",
    )(cnt_f, lo_f, dst_f, extra, gap_start, gap_rows, hn, rrow, rcol)


def _expert_kernel(be_sm, nu_sm, xb_ref, w1_ref, w3_ref, w2_ref, yb_ref, w1_s, w3_s, w2_s):
    p = pl.program_id(0)
    used = p < nu_sm[0]

    @pl.when(used & ((p == 0) | (be_sm[p] != be_sm[jnp.maximum(p - 1, 0)])))
    def _():
        w1_s[...] = w1_ref[...].astype(BF16)
        w3_s[...] = w3_ref[...].astype(BF16)
        w2_s[...] = w2_ref[...].astype(BF16)

    @pl.when(used)
    def _():
        half = D_MODEL // 2
        subs = [slice(r * ESUB, (r + 1) * ESUB) for r in range(BLK // ESUB)]

        def up(rs):
            xlo, xhi = _unpack_halves(xb_ref[rs, :half])
            h1 = _dot(xlo, w1_s[:half, :]) + _dot(xhi, w1_s[half:, :])
            h3 = _dot(xlo, w3_s[:half, :]) + _dot(xhi, w3_s[half:, :])
            return h1, h3

        def down(rs, h1, h3):
            wrep = lax.bitcast_convert_type(xb_ref[rs, half:], F32)
            wfull = jnp.concatenate([wrep] * (D_EXPERT // LANES), axis=1)
            hdn = (h1 * jax.nn.sigmoid(h1) * h3 * wfull).astype(BF16)
            y = _dot(hdn, w2_s[...]).astype(BF16).astype(F32)
            yb_ref[rs, :] = _pack_halves(y)

        hs = [up(subs[0])]
        for r, rs in enumerate(subs):
            if r + 1 < len(subs):
                hs.append(up(subs[r + 1]))
            down(rs, *hs[r])

    @pl.when(jnp.logical_not(used))
    def _():
        yb_ref[...] = jnp.zeros(yb_ref.shape, I32)


def _experts(blk_e, nused, xb, w1, w3, w2, nb):
    def rows(p, be, nu):
        return (jnp.minimum(p, nu[0] - 1), 0)

    def wsel(p, be, nu):
        return (be[jnp.minimum(p, nu[0] - 1)], 0, 0)

    grid_spec = pltpu.PrefetchScalarGridSpec(
        num_scalar_prefetch=2,
        grid=(nb,),
        in_specs=[
            pl.BlockSpec((BLK, XW), rows),
            pl.BlockSpec((None, D_MODEL, D_EXPERT), wsel),
            pl.BlockSpec((None, D_MODEL, D_EXPERT), wsel),
            pl.BlockSpec((None, D_EXPERT, D_MODEL), wsel),
        ],
        out_specs=pl.BlockSpec((BLK, D_MODEL // 2), lambda p, be, nu: (p, 0)),
        scratch_shapes=[
            pltpu.VMEM((D_MODEL, D_EXPERT), BF16), pltpu.VMEM((D_MODEL, D_EXPERT), BF16),
            pltpu.VMEM((D_EXPERT, D_MODEL), BF16),
        ],
    )
    return pl.pallas_call(
        _expert_kernel,
        grid_spec=grid_spec,
        out_shape=jax.ShapeDtypeStruct((nb * BLK, D_MODEL // 2), I32),
        compiler_params=pltpu.CompilerParams(
            dimension_semantics=("arbitrary",), vmem_limit_bytes=48 * 1024 * 1024),
        name="experts",
    )(blk_e, nused, xb, w1, w3, w2)


def _combine_kernel(cnt_sm, lo_sm, dst_sm, used_sm, extra_sm, yb_ref, x2_ref, rcol_ref, g_ref, out_ref,
                    yl_s, y_s, sem, sem2):
    i = pl.program_id(0)
    nsteps = pl.num_programs(0)
    bset = i % 2

    def ycopy(s, src, dst, sm):
        return pltpu.make_async_copy(yb_ref.at[pl.ds(src, CCH), :], yl_s.at[s, pl.ds(dst, CCH), :],
                                     sm.at[s])

    def issue(tile, s):
        for e in range(N_EXPERTS):
            lo = pl.multiple_of(lo_sm[tile * N_EXPERTS + e], CCH)
            src = pl.multiple_of(dst_sm[tile * N_EXPERTS + e], SUB)
            ycopy(s, src, lo, sem).start()

        @pl.when(extra_sm[tile] > 0)
        def _():
            def ebody(e, carry):
                n = cnt_sm[tile * N_EXPERTS + e]
                lo = pl.multiple_of(lo_sm[tile * N_EXPERTS + e], CCH)
                src = pl.multiple_of(dst_sm[tile * N_EXPERTS + e], SUB)

                def cbody(c, cc):
                    ycopy(s, src + c * CCH, lo + c * CCH, sem2).start()
                    return cc

                return lax.fori_loop(1, (n + CCH - 1) // CCH, cbody, carry)

            lax.fori_loop(0, N_EXPERTS, ebody, 0)

    @pl.when(i == 0)
    def _():
        yl_s[...] = jnp.zeros(yl_s.shape, I32)
        for s in range(CTILES):
            issue(s, s)

    @pl.when(i + 1 < nsteps)
    def _():
        for s in range(CTILES):
            issue(CTILES * (i + 1) + s, (1 - bset) * CTILES + s)

    lane = lax.broadcasted_iota(I32, (TM, CBLK), 1).astype(F32)
    half = D_MODEL // 2
    for s in range(CTILES):
        t = CTILES * i + s
        slot = bset * CTILES + s
        rows = slice(s * TM, (s + 1) * TM)
        for _ in range(N_EXPERTS):
            ycopy(slot, 0, 0, sem).wait()

        def wbody(j, c, slot=slot):
            ycopy(slot, 0, 0, sem2).wait()
            return c

        lax.fori_loop(0, extra_sm[t], wbody, 0)

        rc = rcol_ref[rows, :]
        pad0 = rc[:, R_PAD0:R_PAD0 + 1]
        pad1 = rc[:, R_PAD1:R_PAD1 + 1]

        def chunk(c, pad0=pad0, pad1=pad1, slot=slot):
            r = lane + float(c * CBLK)
            selm = (jnp.where(pad0 == r, 1.0, 0.0) + jnp.where(pad1 == r, 1.0, 0.0)).astype(BF16)
            lo_h, hi_h = _unpack_halves(yl_s[slot, c * CBLK:(c + 1) * CBLK, :])
            return _dot(selm, lo_h), _dot(selm, hi_h)

        ylo = jnp.zeros((TM, half), F32)
        yhi = jnp.zeros((TM, half), F32)
        for c in range(LOC_COMMON // CBLK):
            dlo, dhi = chunk(c)
            ylo = ylo + dlo
            yhi = yhi + dhi
        y_s[:, :half] = ylo
        y_s[:, half:] = yhi
        for c in range(LOC_COMMON // CBLK, LOC_ROWS // CBLK):
            @pl.when(used_sm[t] > c * CBLK)
            def _(c=c, chunk=chunk):
                dlo, dhi = chunk(c)
                y_s[:, :half] += dlo
                y_s[:, half:] += dhi
        out_ref[rows, :] = _rms(x2_ref[rows, :] + y_s[...], g_ref[...])


def _combine(cnt_f, lo_slot_f, dst_f, used, extra, yb, x2, rcol, g_final):
    T = x2.shape[0]
    tg = CTILES * TM
    assert T % tg == 0
    grid_spec = pltpu.PrefetchScalarGridSpec(
        num_scalar_prefetch=5,
        grid=(T // tg,),
        in_specs=[
            pl.BlockSpec(memory_space=pl.ANY),
            pl.BlockSpec((tg, D_MODEL), lambda i, *_: (i, 0)),
            pl.BlockSpec((tg, LANES), lambda i, *_: (i, 0)),
            pl.BlockSpec((1, D_MODEL), lambda i, *_: (0, 0)),
        ],
        out_specs=pl.BlockSpec((tg, D_MODEL), lambda i, *_: (i, 0)),
        scratch_shapes=[
            pltpu.VMEM((2 * CTILES, LOC_ROWS, D_MODEL // 2), I32),
            pltpu.VMEM((TM, D_MODEL), F32),
            pltpu.SemaphoreType.DMA((2 * CTILES,)), pltpu.SemaphoreType.DMA((2 * CTILES,)),
        ],
    )
    return pl.pallas_call(
        _combine_kernel,
        grid_spec=grid_spec,
        out_shape=jax.ShapeDtypeStruct((T, D_MODEL), F32),
        compiler_params=pltpu.CompilerParams(
            dimension_semantics=("arbitrary",), vmem_limit_bytes=40 * 1024 * 1024),
        name="combine",
    )(cnt_f, lo_slot_f, dst_f, used, extra, yb, x2, rcol, g_final)


def _rope_tables(S):
    half = DA_HEAD_DIM // 2
    inv = (1.0 / (np.float32(ROPE_THETA) ** (np.arange(0, DA_HEAD_DIM, 2, dtype=np.float32)
                                             / np.float32(DA_HEAD_DIM)))).astype(np.float32)
    ang = (np.arange(S, dtype=np.float32)[:, None] * inv[None, :]).astype(np.float32)
    cos, sin = np.cos(ang), np.sin(ang)
    lane = np.arange(LANES)
    idx = lane % half
    lower = (lane % DA_HEAD_DIM) < half
    c = cos[:, idx]
    s = sin[:, idx]
    z = np.zeros_like(s)
    return (jnp.asarray(c, F32), jnp.asarray(np.where(lower[None, :], -s, z), F32),
            jnp.asarray(np.where(lower[None, :], z, s), F32))


def _tri_dot(a, b):
    return jnp.dot(a.astype(F32), b.astype(F32), precision=lax.Precision.HIGHEST).astype(I32)


def kernel(x, w_in, conv_w, conv_b, gate_b, lam_qk, subln_g, mhnorm_g, w_out, g_mix, g_ffn, w_grp,
           b_grp, w_erouter, b_erouter, w1, w3, w2, g_final):
    B, S, D = x.shape
    T = B * S
    nt = T // TM
    x2d = x.reshape(T, D)
    l = 0

    w_main = w_in[l, :, :N_MAIN].astype(BF16)
    wg8 = w_in[l, :, N_MAIN:]
    wg = jnp.pad(wg8, ((0, 0), (0, LANES - 8))).astype(BF16)
    wgt = wg8.T.astype(BF16)
    gb8 = gate_b[l].reshape(8)
    gb_col = jnp.pad(gb8, (0, LANES - 8)).reshape(1, LANES)
    gb_row = jnp.broadcast_to(gb8[:, None], (8, LANES))
    rope_c, rope_sa, rope_sb = _rope_tables(S)
    w_r = jnp.concatenate(
        [w_grp[l], w_erouter[l].transpose(1, 0, 2).reshape(D, N_EXPERTS)], axis=1)
    w_r = jnp.pad(w_r, ((0, 0), (0, LANES - w_r.shape[1])))
    wr_hi = w_r.astype(BF16)
    wr_t = jnp.concatenate([wr_hi, (w_r - wr_hi.astype(F32)).astype(BF16)], axis=1)
    b_r = jnp.pad(jnp.concatenate([b_grp[l], b_erouter[l].reshape(N_EXPERTS)]),
                  (0, LANES - N_GROUPS - N_EXPERTS))
    b_rt = jnp.broadcast_to(b_r[:, None], (LANES, TM))

    q1, q2, kr, va, qc, kc, vm, om, gcol, grow = _inproj(
        x2d, g_mix[l].reshape(1, D), w_main, wg, wgt, rope_c, rope_sa, rope_sb,
        conv_w[l], conv_b[l].reshape(1, -1), S)
    oa = _attention(q1, q2, kr, va, lam_qk[l], subln_g[l].reshape(1, LANES), B, S)
    hm = _mlstm(qc, kc, vm, om, gcol, grow, gb_col, gb_row, mhnorm_g[l].reshape(1, ML_WIDTH), B, S)
    x2, hn, rrow, rcol, cnt3 = _router(oa, hm, x2d, w_out[l].astype(BF16), g_ffn[l].reshape(1, D),
                                       wr_t, b_rt)

    cnt = cnt3[:, 0, EXP_ROW0:EXP_ROW0 + N_EXPERTS]
    c8 = (cnt + SUB - 1) // SUB * SUB
    c_slot = jnp.maximum((cnt + CCH - 1) // CCH, 1) * CCH
    odd = (np.arange(nt) % 2)[:, None]
    c8_even = c8 * (1 - odd)
    c8_odd = c8 * odd
    even_tot = jnp.sum(c8_even, axis=0)
    count8 = even_tot + DCH + jnp.sum(c8_odd, axis=0)
    pcount = (count8 + DCH + BLK - 1) // BLK * BLK
    up_e = np.triu(np.ones((N_EXPERTS, N_EXPERTS), np.float32), 1)
    lo_t = np.tril(np.ones((nt, nt), np.float32), -1)
    pstart = _tri_dot(pcount[None, :], up_e)[0]
    pend = pstart + pcount
    dst = pstart[None, :] + jnp.where(odd == 0, _tri_dot(lo_t, c8_even),
                                      (even_tot + DCH)[None, :] + _tri_dot(lo_t, c8_odd))
    lo8 = _tri_dot(c8, up_e)
    lo_slot = _tri_dot(c_slot, up_e)
    nb = (2 * T + nt * N_EXPERTS * (SUB - 1) + N_EXPERTS * (2 * DCH + BLK - 1)) // BLK + 1
    blk_row = jnp.arange(nb, dtype=I32) * BLK
    blk_e = jnp.minimum(jnp.sum((pend[None, :] <= blk_row[:, None]).astype(I32), axis=1),
                        N_EXPERTS - 1).astype(I32)
    nused = (pend[-1] // BLK).astype(I32).reshape(1)
    cnt_f = cnt.reshape(-1).astype(I32)
    dst_f = dst.reshape(-1).astype(I32)

    spare = jnp.full((N_EXPERTS,), DCH, I32)
    gap_start = jnp.concatenate([pstart + even_tot, pstart + count8, pstart + count8 + DCH,
                                 pend[-1:]]).astype(I32)
    gap_rows = jnp.concatenate([spare, spare, pcount - count8 - DCH,
                                nb * BLK - pend[-1:]]).astype(I32)
    extra = jnp.sum(jnp.maximum((cnt + DCH - 1) // DCH - 1, 0), axis=1).astype(I32)
    xb = _dispatch(cnt_f, lo8.reshape(-1), dst_f, extra, gap_start, gap_rows, hn, rrow, rcol, nb)
    yb = _experts(blk_e, nused, xb, w1[l], w3[l], w2[l], nb)
    used = jnp.sum(c_slot, axis=1).astype(I32)
    out = _combine(cnt_f, lo_slot.reshape(-1), dst_f, used, extra, yb, x2, rcol, g_final.reshape(1, D))
    return out.reshape(B, S, D)
```

```python
import functools
import math

import numpy as np
import jax
import jax.numpy as jnp
from jax import lax
from jax.experimental import pallas as pl
from jax.experimental.pallas import tpu as pltpu

F32 = jnp.float32
BF16 = jnp.bfloat16
I32 = jnp.int32

D_MODEL = 1024
DA_HEADS = 4
DA_HEAD_DIM = 64
DA_V_DIM = 128
DA_WIDTH = 512
ML_HEADS = 4
ML_WIDTH = 512
ML_HEAD_DIM = 128
ML_CHUNK = 128
ML_GROUP = 4
CONV_K = 4
ROPE_THETA = 10000.0
RMS_EPS = 1e-6
N_GROUPS = 4
EXPERTS_PER_GROUP = 8
N_EXPERTS = 32
D_EXPERT = 512
LAMBDA_INIT = 0.8 - 0.6 * math.exp(-0.3 * 0)

LANES = 128
SUB = 8
N_MAIN = 7 * 512
NEG = -1e30

TM_PROJ = 1024
TQ = 256
TK = 512
VT_ROWS = DA_V_DIM + 16
TM = 256
BLK = 512
ESUB = 256
CBLK = 256
CTILES = 2
DCH = 32
CCH = 32
DLOC_ROWS = 768
N_GAPS_NEAR = 2 * N_EXPERTS
N_GAPS = N_GAPS_NEAR + N_EXPERTS + 1
LOC_COMMON = N_EXPERTS * CCH
LOC_ROWS = 2 * TM + N_EXPERTS * CCH
XW = D_MODEL // 2 + LANES
EXP_ROW0 = 4
RT_ROWS = 40
RT_GROUP = 4
R_E0, R_E1, R_POS0, R_POS1, R_PAD0, R_PAD1, R_W0, R_W1 = range(8)


def _nt_dot(a, b):
    return lax.dot_general(a, b, (((1,), (1,)), ((), ())), preferred_element_type=F32)


def _dot(a, b):
    return jnp.dot(a, b, preferred_element_type=F32)


def _split3(x):
    hi = x.astype(BF16)
    r = x - hi.astype(F32)
    mid = r.astype(BF16)
    lo = (r - mid.astype(F32)).astype(BF16)
    return hi, mid, lo


def _rms(x, g):
    return x * lax.rsqrt(jnp.mean(x * x, axis=-1, keepdims=True) + RMS_EPS) * g


def _inproj_kernel(x_ref, g_ref, w_ref, wg_ref, wgt_ref, c_ref, sa_ref, sb_ref, cw_ref, cb_ref,
                   q1_ref, q2_ref, kr_ref, va_ref, qc_ref, kc_ref, vm_ref, om_ref, gcol_ref, grow_ref,
                   h_s, cq_s, ck_s, *, tiles_per_seq):
    tm = x_ref.shape[0]
    cw = 512
    first = (pl.program_id(0) % tiles_per_seq) == 0

    @pl.when(first)
    def _():
        cq_s[0:8, :] = jnp.zeros((8, cw), F32)
        ck_s[0:8, :] = jnp.zeros((8, cw), F32)

    @pl.when(jnp.logical_not(first))
    def _():
        cq_s[0:8, :] = cq_s[tm:tm + 8, :]
        ck_s[0:8, :] = ck_s[tm:tm + 8, :]

    h_s[...] = _rms(x_ref[...], g_ref[...]).astype(BF16)
    nhalf = 2 if tm >= 512 else 1
    th = tm // nhalf
    halves = [slice(p * th, (p + 1) * th) for p in range(nhalf)]

    def cols(c, hv):
        return _dot(h_s[hv, :], w_ref[:, c * cw:(c + 1) * cw])

    rb = min(128, th)
    lane = lax.broadcasted_iota(I32, (rb, LANES), 1)
    qscale = DA_HEAD_DIM ** -0.5 * math.log2(math.e)

    def rope(x, rows):
        return (x * c_ref[rows, :] + pltpu.roll(x, 96, 1) * sa_ref[rows, :]
                + pltpu.roll(x, 32, 1) * sb_ref[rows, :])

    def blocks(hv):
        for r in range(th // rb):
            for hh in range(cw // LANES):
                yield (slice(hv.start + r * rb, hv.start + (r + 1) * rb), slice(r * rb, (r + 1) * rb),
                       slice(hh * LANES, (hh + 1) * LANES))

    def rope_q(qa, hv):
        for rows, loc, hs in blocks(hv):
            qr = rope(qa[loc, hs], rows) * qscale
            q1_ref[rows, hs] = jnp.where(lane < DA_HEAD_DIM, qr, 0.0).astype(BF16)
            q2_ref[rows, hs] = jnp.where(lane >= DA_HEAD_DIM, qr, 0.0).astype(BF16)

    def rope_k(ka, hv):
        for rows, loc, hs in blocks(hv):
            kr_ref[rows, hs] = rope(ka[loc, hs], rows).astype(BF16)

    def conv_silu(src, c0, hv, out_ref, scale):
        for rows, _, hs in blocks(hv):
            cs = slice(c0 + hs.start, c0 + hs.stop)
            w = cw_ref[:, cs]
            y = cb_ref[:, cs] + src[8 + rows.start:8 + rows.stop, hs] * w[3:4]
            for s in (1, 2, 3):
                y = y + src[8 + rows.start - s:8 + rows.stop - s, hs] * w[3 - s:4 - s]
            y = y * jax.nn.sigmoid(y)
            out_ref[rows, hs] = (y if scale is None else y * scale).astype(BF16)

    def store_t(out_ref, r, hv):
        for j in range(cw // LANES):
            out_ref[j * LANES:(j + 1) * LANES, hv] = r[:, j * LANES:(j + 1) * LANES].T.astype(BF16)

    def to_conv(c, dst, hv):
        dst[8 + hv.start:8 + hv.stop, :] = cols(c, hv)

    qa = [cols(0, hv) for hv in halves]
    ka = []
    for p, hv in enumerate(halves):
        ka.append(cols(1, hv))
        rope_q(qa[p], hv)
    for p, hv in enumerate(halves):
        store_t(va_ref, cols(2, hv), hv)
        rope_k(ka[p], hv)
    for hv in halves:
        to_conv(3, cq_s, hv)
    for hv in halves:
        to_conv(4, ck_s, hv)
        conv_silu(cq_s, 0, hv, qc_ref, None)
    for hv in halves:
        store_t(vm_ref, cols(5, hv), hv)
        conv_silu(ck_s, ML_WIDTH, hv, kc_ref, ML_HEAD_DIM ** -0.5)
    om_ref[...] = _dot(h_s[...], w_ref[:, 6 * cw:7 * cw]).astype(BF16)
    gcol_ref[...] = _dot(h_s[...], wg_ref[...])
    grow_ref[...] = _nt_dot(wgt_ref[...], h_s[...])


def _inproj(x2d, g_mix, w_main, wg, wgt, rope_c, rope_sa, rope_sb, conv_w, conv_b, S):
    T = x2d.shape[0]
    tm = min(TM_PROJ, S)
    assert S % tm == 0
    tps = S // tm
    full = lambda shape: pl.BlockSpec(shape, lambda i: (0, 0))
    rope_spec = pl.BlockSpec((tm, LANES), lambda i: (i % tps, 0))
    col_spec = pl.BlockSpec((tm, 512), lambda i: (i, 0))
    tcol_spec = pl.BlockSpec((512, tm), lambda i: (0, i))
    row_shape = jax.ShapeDtypeStruct((T, 512), BF16)
    t_shape = jax.ShapeDtypeStruct((512, T), BF16)
    return pl.pallas_call(
        functools.partial(_inproj_kernel, tiles_per_seq=tps),
        grid=(T // tm,),
        in_specs=[
            pl.BlockSpec((tm, D_MODEL), lambda i: (i, 0)),
            full((1, D_MODEL)), full((D_MODEL, N_MAIN)), full((D_MODEL, LANES)), full((8, D_MODEL)),
            rope_spec, rope_spec, rope_spec,
            full((CONV_K, 2 * ML_WIDTH)), full((1, 2 * ML_WIDTH)),
        ],
        out_specs=[col_spec, col_spec, col_spec, tcol_spec, col_spec, col_spec, tcol_spec, col_spec,
                   pl.BlockSpec((tm, LANES), lambda i: (i, 0)),
                   pl.BlockSpec((8, tm), lambda i: (0, i))],
        out_shape=[row_shape, row_shape, row_shape, t_shape, row_shape, row_shape, t_shape, row_shape,
                   jax.ShapeDtypeStruct((T, LANES), F32),
                   jax.ShapeDtypeStruct((8, T), F32)],
        scratch_shapes=[pltpu.VMEM((tm, D_MODEL), BF16),
                        pltpu.VMEM((tm + 8, ML_WIDTH), F32), pltpu.VMEM((tm + 8, ML_WIDTH), F32)],
        compiler_params=pltpu.CompilerParams(
            dimension_semantics=("arbitrary",), vmem_limit_bytes=56 * 1024 * 1024),
        name="inproj",
    )(x2d, g_mix, w_main, wg, wgt, rope_c, rope_sa, rope_sb, conv_w, conv_b)


def _attn_kernel(q1_s, q2_s, k_s, vt_ref, lam_ref, g_ref, o_ref, acc_s, st_s):
    S = vt_ref.shape[1]

    def values_t(h, ko, tk):
        return jnp.concatenate([vt_ref[h * DA_V_DIM:(h + 1) * DA_V_DIM, pl.ds(ko, tk)],
                                jnp.ones((VT_ROWS - DA_V_DIM, tk), BF16)], axis=0)

    lq = lam_ref[...]
    lam = (jnp.exp(jnp.sum(lq[0:1] * lq[1:2], axis=-1, keepdims=True))
           - jnp.exp(jnp.sum(lq[2:3] * lq[3:4], axis=-1, keepdims=True)) + LAMBDA_INIT)

    key = lax.broadcasted_iota(I32, (TK, 2 * TQ), 0)
    qry = lax.broadcasted_iota(I32, (TK, 2 * TQ), 1)
    kq = key - jnp.where(qry >= TQ, qry - TQ, qry)

    def scores(qo, ko, tk, h, causal=None):
        hs = slice(h * LANES, (h + 1) * LANES)
        qq = jnp.concatenate([q1_s[pl.ds(qo, TQ), hs], q2_s[pl.ds(qo, TQ), hs]], axis=0)
        st = _nt_dot(k_s[pl.ds(ko, tk), hs], qq)
        st_s[h, :tk, :] = st if causal is None else jnp.where(causal, st, NEG)

    def kvstep(qo, ko, tk, ms, causal, slot, next_ko=None):
        out = []
        if next_ko is None:
            scores(qo, ko, tk, 0, causal)
        for h in range(DA_HEADS):
            if h + 1 < DA_HEADS:
                scores(qo, ko, tk, h + 1, causal)
            elif next_ko is not None:
                scores(qo, next_ko, tk, 0)
            st = st_s[h, :tk, :]
            m_new = jnp.maximum(ms[h], jnp.max(st, axis=0, keepdims=True))
            alpha = jnp.exp2(ms[h] - m_new)
            p = jnp.exp2(st - m_new).astype(BF16)
            acc_s[slot, h] = alpha * acc_s[slot, h] + _dot(values_t(h, ko, tk), p)
            out.append(m_new)
        return tuple(out)

    def start(i, ko, tk, off, slot):
        acc_s[slot] = jnp.zeros(acc_s.shape[1:], F32)
        m0 = tuple(jnp.full((1, 2 * TQ), NEG, F32) for _ in range(DA_HEADS))
        return kvstep(pl.multiple_of(i * TQ, TQ), ko, tk, m0, kq[:tk] <= off, slot)

    def full_steps(i, n, ms, slot):
        qo = pl.multiple_of(i * TQ, TQ)
        scores(qo, 0, TK, 0)

        def step(j, m):
            nxt = pl.multiple_of(jnp.minimum(j + 1, n - 1) * TK, TK)
            return kvstep(qo, pl.multiple_of(j * TK, TK), TK, m, None, slot, next_ko=nxt)

        lax.fori_loop(0, n, step, ms)

    def finalize(i, slot):
        qo = pl.multiple_of(i * TQ, TQ)
        for h in range(DA_HEADS):
            hs = slice(h * LANES, (h + 1) * LANES)
            acc = acc_s[slot, h]
            on = acc[:DA_V_DIM] * (1.0 / acc[DA_V_DIM:DA_V_DIM + 1])
            ot = on[:, :TQ] - lam * on[:, TQ:]
            o = _rms(ot.T, g_ref[...]) * (1.0 - LAMBDA_INIT)
            o_ref[pl.ds(qo, TQ), hs] = o.astype(BF16)

    def tile_pair(u, carry):
        ko = pl.multiple_of(u * TK, TK)
        finalize(2 * u - 1, 1)
        full_steps(2 * u, u, start(2 * u, ko, TQ, 0, 0), 0)
        finalize(2 * u, 0)
        full_steps(2 * u + 1, u, start(2 * u + 1, ko, TK, TQ, 1), 1)
        return carry

    assert TK == 2 * TQ and (S // TQ) % 2 == 0
    start(0, 0, TQ, 0, 0)
    finalize(0, 0)
    start(1, 0, TK, TQ, 1)
    lax.fori_loop(1, S // TK, tile_pair, 0)
    finalize(S // TQ - 1, 1)


def _attention(q1, q2, kr, vat, lam_qk, subln_g, B, S):
    T = B * S
    full = lambda shape: pl.BlockSpec(shape, lambda b: (0, 0))
    seq = pl.BlockSpec((S, DA_WIDTH), lambda b: (b, 0))
    seq_t = pl.BlockSpec((DA_WIDTH, S), lambda b: (0, b))
    return pl.pallas_call(
        _attn_kernel,
        grid=(B,),
        in_specs=[seq, seq, seq, seq_t, full((4, DA_HEAD_DIM)), full((1, LANES))],
        out_specs=seq,
        out_shape=jax.ShapeDtypeStruct((T, DA_WIDTH), BF16),
        scratch_shapes=[
            pltpu.VMEM((2, DA_HEADS, VT_ROWS, 2 * TQ), F32),
            pltpu.VMEM((DA_HEADS, TK, 2 * TQ), F32),
        ],
        compiler_params=pltpu.CompilerParams(
            dimension_semantics=("arbitrary",), vmem_limit_bytes=48 * 1024 * 1024),
        name="attn",
    )(q1, q2, kr, vat, lam_qk, subln_g)


def _log_sigmoid(x):
    return jnp.minimum(x, 0.0) - jnp.log(1.0 + jnp.exp(-jnp.abs(x)))


def _mlstm_kernel(qc_s, kc_s, vt_ref, o_ref, gcol_ref, grow_ref, gbc_ref, gbr_ref,
                  mg_ref, out_ref, ct_s, m_s):
    S = vt_ref.shape[1]
    L = ML_CHUNK
    nc = S // L
    ones_rows = jnp.ones((ML_HEAD_DIM, L), BF16)

    ct_s[...] = jnp.zeros(ct_s.shape, F32)
    m_s[...] = jnp.zeros(m_s.shape, F32)

    ri = lax.broadcasted_iota(I32, (L, L), 0)
    ci = lax.broadcasted_iota(I32, (L, L), 1)
    causal_t = ri <= ci
    tril = jnp.where(ci <= ri, 1.0, 0.0).astype(BF16)
    triu = jnp.where(causal_t, 1.0, 0.0).astype(BF16)

    heads = [slice(h * ML_HEAD_DIM, (h + 1) * ML_HEAD_DIM) for h in range(ML_HEADS)]

    def gates(c):
        ro = pl.multiple_of(c * L, L)
        gc = gcol_ref[pl.ds(ro, L), :] + gbc_ref[...]
        gr = grow_ref[:, pl.ds(ro, L)] + gbr_ref[...]
        b_c = sum(_dot(tril, p) for p in _split3(_log_sigmoid(gc)))
        b_r = sum(_dot(p, triu) for p in _split3(_log_sigmoid(gr)))
        return ro, gc, gr, b_c, b_r

    def read_state(g):
        ro = g[0]
        ks, vts, kq, cq, ms = [], [], [], [], []
        for h, hs in enumerate(heads):
            q = qc_s[pl.ds(ro, L), hs]
            k = kc_s[pl.ds(ro, L), hs]
            ks.append(k)
            vts.append(jnp.concatenate([vt_ref[hs, pl.ds(ro, L)], ones_rows], axis=0))
            kq.append(_nt_dot(k, q))
            cq.append(_nt_dot(ct_s[h].astype(BF16), q))
            ms.append(m_s[h:h + 1, 0:1])
        return ks, vts, kq, cq, ms

    def update_state(g, d):
        _, _, gr, _, b_r = g
        ks, vts, _, _, ms = d
        for h in range(ML_HEADS):
            br = b_r[4 + h:5 + h, :]
            bl = br[:, L - 1:L]
            dec = bl - br + gr[h:h + 1, :]
            m_new = jnp.maximum(bl + ms[h], jnp.max(dec, axis=1, keepdims=True))
            ws = jnp.exp(dec - m_new)
            sc = jnp.exp(bl + ms[h] - m_new)
            vw = (vts[h].astype(F32) * ws).astype(BF16)
            ct_s[h] = sc * ct_s[h] + _dot(vw, ks[h])
            m_s[h:h + 1, :] = jnp.broadcast_to(m_new, (1, LANES))

    def outputs(g, d):
        ro, gc, _, b_c, b_r = g
        _, vts, kq, cq, ms = d
        for h, hs in enumerate(heads):
            br = b_r[4 + h:5 + h, :]
            a_col = b_c[:, 4 + h:5 + h] - gc[:, h:h + 1]
            dm = jnp.where(causal_t, br - a_col, NEG)
            inter = br + ms[h]
            m_row = jnp.maximum(inter, jnp.max(dm, axis=0, keepdims=True))
            sm = jnp.exp(dm - m_row) * kq[h]
            sc_in = jnp.exp(inter - m_row)
            a = _dot(vts[h], sm.astype(BF16)) + sc_in * cq[h]
            den = a[ML_HEAD_DIM:ML_HEAD_DIM + 1, :]
            hh = (a[:ML_HEAD_DIM] / jnp.maximum(jnp.abs(den), jnp.exp(-m_row))).T
            hg = hh * jax.nn.sigmoid(o_ref[pl.ds(ro, L), hs].astype(F32))
            out_ref[pl.ds(ro, L), hs] = _rms(hg, mg_ref[:, hs]).astype(BF16)

    def chunk_group(cg, carry):
        gs = [gates(ML_GROUP * cg + j) for j in range(ML_GROUP)]
        ds = []
        for g in gs:
            ds.append(read_state(g))
            update_state(g, ds[-1])
        for g, d in zip(gs, ds):
            outputs(g, d)
        return carry

    lax.fori_loop(0, nc // ML_GROUP, chunk_group, 0)


def _mlstm(qc, kc, vmt, om, gcol, grow, gb_col, gb_row, mh_g, B, S):
    T = B * S
    full = lambda shape: pl.BlockSpec(shape, lambda b: (0, 0))
    seq = pl.BlockSpec((S, ML_WIDTH), lambda b: (b, 0))
    return pl.pallas_call(
        _mlstm_kernel,
        grid=(B,),
        in_specs=[
            seq, seq, pl.BlockSpec((ML_WIDTH, S), lambda b: (0, b)), seq,
            pl.BlockSpec((S, LANES), lambda b: (b, 0)),
            pl.BlockSpec((8, S), lambda b: (0, b)),
            full((1, LANES)), full((8, LANES)), full((1, ML_WIDTH)),
        ],
        out_specs=seq,
        out_shape=jax.ShapeDtypeStruct((T, ML_WIDTH), BF16),
        scratch_shapes=[
            pltpu.VMEM((ML_HEADS, 2 * ML_HEAD_DIM, ML_HEAD_DIM), F32),
            pltpu.VMEM((8, LANES), F32),
        ],
        compiler_params=pltpu.CompilerParams(
            dimension_semantics=("arbitrary",), vmem_limit_bytes=56 * 1024 * 1024),
        name="mlstm",
    )(qc, kc, vmt, om, gcol, grow, gb_col, gb_row, mh_g)


def _router_kernel(oa_ref, hm_ref, x_ref, wo_ref, g_ref, wr_ref, brt_ref,
                   x2_ref, hn_ref, rrow_ref, rcol_ref, cnt_ref):
    def project(s):
        rs = slice(s * TM, (s + 1) * TM)
        mixo = _dot(oa_ref[rs, :], wo_ref[:DA_WIDTH, :]) + _dot(hm_ref[rs, :], wo_ref[DA_WIDTH:, :])
        x2 = x_ref[rs, :] + mixo
        x2_ref[rs, :] = x2
        hn = _rms(x2, g_ref[...])
        hn_hi = hn.astype(BF16)
        hn_ref[rs, :] = hn_hi
        hn_lo = (hn - hn_hi.astype(F32)).astype(BF16)
        a = _dot(hn_hi, wr_ref[...])
        b = _dot(hn_lo, wr_ref[:, :LANES])
        return (a[:, :LANES] + a[:, LANES:] + b).T[:RT_ROWS] + brt_ref[:RT_ROWS, :]

    lts = [project(0)]
    for s in range(RT_GROUP):
        if s + 1 < RT_GROUP:
            lts.append(project(s + 1))
        _route(s, lts[s], rrow_ref, rcol_ref, cnt_ref)


def _route(s, lt, rrow_ref, rcol_ref, cnt_ref):
    sub = lax.broadcasted_iota(I32, (RT_ROWS, TM), 0)
    sub_f = sub.astype(F32)

    def cmax(v):
        return jnp.max(v, axis=0, keepdims=True)

    def first_idx(mask):
        return jnp.min(jnp.where(mask, sub_f, 1e6), axis=0, keepdims=True).astype(I32)

    gl = jnp.where(sub < N_GROUPS, lt, NEG)
    gmax = cmax(gl)
    gsel = first_idx(gl == gmax)
    g_w = 1.0 / jnp.sum(jnp.exp(gl - gmax), axis=0, keepdims=True)
    elo = EXP_ROW0 + gsel * EXPERTS_PER_GROUP
    el = jnp.where((sub >= elo) & (sub < elo + EXPERTS_PER_GROUP), lt, NEG)
    v1 = cmax(el)
    i1 = first_idx(el == v1)
    el2 = jnp.where(sub == i1, NEG, el)
    v2 = cmax(el2)
    i2 = first_idx(el2 == v2)
    t = jnp.exp(v2 - v1)
    w0 = g_w / (1.0 + t)
    w1 = g_w * t / (1.0 + t)

    oh0 = jnp.where(sub == i1, 1.0, 0.0)
    oh1 = jnp.where(sub == i2, 1.0, 0.0)
    mh = oh0 + oh1
    r_i = lax.broadcasted_iota(I32, (TM, TM), 0)
    c_i = lax.broadcasted_iota(I32, (TM, TM), 1)
    before = jnp.where(r_i < c_i, 1.0, 0.0).astype(BF16)
    mh_f = jnp.concatenate([mh, jnp.zeros((LANES - RT_ROWS, TM), F32)], axis=0)
    mh_b = mh_f.astype(BF16)
    pre = _dot(mh_b, before)[:RT_ROWS]
    cnt = jnp.sum(mh_f, axis=1, keepdims=True)
    cnt8 = jnp.floor((cnt + (SUB - 1)) * (1.0 / SUB)) * SUB
    erow = lax.broadcasted_iota(I32, (LANES, 1), 0)
    is_exp = (erow >= EXP_ROW0) & (erow < EXP_ROW0 + N_EXPERTS)
    cnt_slot = jnp.where(is_exp, jnp.maximum(jnp.floor((cnt + (CCH - 1)) * (1.0 / CCH)), 1.0) * CCH, 0.0)
    e_r = lax.broadcasted_iota(I32, (LANES, LANES), 0)
    e_c = lax.broadcasted_iota(I32, (LANES, LANES), 1)
    below = jnp.where(e_c < e_r, 1.0, 0.0).astype(BF16)
    lo8 = _dot(below, jnp.broadcast_to(cnt8, (LANES, LANES)).astype(BF16))[:RT_ROWS, 0:1]
    lo_slot = _dot(below, jnp.broadcast_to(cnt_slot, (LANES, LANES)).astype(BF16))[:RT_ROWS, 0:1]

    def csum(v):
        return jnp.sum(v, axis=0, keepdims=True)

    rows = [None] * 8
    rows[R_E0] = (i1 - EXP_ROW0).astype(F32)
    rows[R_E1] = (i2 - EXP_ROW0).astype(F32)
    rows[R_POS0] = csum(oh0 * (pre + lo8))
    rows[R_POS1] = csum(oh1 * (pre + lo8))
    rows[R_PAD0] = csum(oh0 * (pre + lo_slot))
    rows[R_PAD1] = csum(oh1 * (pre + lo_slot))
    rows[R_W0] = w0
    rows[R_W1] = w1
    sub128 = lax.broadcasted_iota(I32, (LANES, TM), 0)
    r128 = jnp.zeros((LANES, TM), F32)
    for j, v in enumerate(rows):
        r128 = jnp.where(sub128 == j, v, r128)
    rrow_ref[:, s * TM:(s + 1) * TM] = r128[:8]
    rcol_ref[s * TM:(s + 1) * TM, :] = r128.T
    cnt_ref[s] = _nt_dot(jnp.ones((8, TM), BF16), mh_b)[0:1].astype(I32)


def _router(oa, hm, x2d, w_out, g_ffn, wr_t, b_rt):
    T = x2d.shape[0]
    nt = T // TM
    full = lambda shape: pl.BlockSpec(shape, lambda i: (0, 0))
    tg = RT_GROUP * TM
    return pl.pallas_call(
        _router_kernel,
        grid=(T // tg,),
        in_specs=[
            pl.BlockSpec((tg, DA_WIDTH), lambda i: (i, 0)),
            pl.BlockSpec((tg, ML_WIDTH), lambda i: (i, 0)),
            pl.BlockSpec((tg, D_MODEL), lambda i: (i, 0)),
            full((D_MODEL, D_MODEL)), full((1, D_MODEL)),
            full((D_MODEL, 2 * LANES)), full((LANES, TM)),
        ],
        out_specs=[
            pl.BlockSpec((tg, D_MODEL), lambda i: (i, 0)),
            pl.BlockSpec((tg, D_MODEL), lambda i: (i, 0)),
            pl.BlockSpec((8, tg), lambda i: (0, i)),
            pl.BlockSpec((tg, LANES), lambda i: (i, 0)),
            pl.BlockSpec((RT_GROUP, 1, LANES), lambda i: (i, 0, 0)),
        ],
        out_shape=[
            jax.ShapeDtypeStruct((T, D_MODEL), F32),
            jax.ShapeDtypeStruct((T, D_MODEL), BF16),
            jax.ShapeDtypeStruct((8, T), F32),
            jax.ShapeDtypeStruct((T, LANES), F32),
            jax.ShapeDtypeStruct((nt, 1, LANES), I32),
        ],
        compiler_params=pltpu.CompilerParams(
            dimension_semantics=("arbitrary",), vmem_limit_bytes=40 * 1024 * 1024),
        name="router",
    )(oa, hm, x2d, w_out, g_ffn, wr_t, b_rt)


def _pack_halves(x):
    lo = lax.bitcast_convert_type(x[:, :D_MODEL // 2], I32)
    hi = lax.bitcast_convert_type(x[:, D_MODEL // 2:], I32)
    return lax.shift_right_logical(lo, 16) | (hi & jnp.int32(-65536))


def _unpack_halves(w):
    lo = lax.bitcast_convert_type(lax.shift_left(w, 16), F32)
    hi = lax.bitcast_convert_type(w & jnp.int32(-65536), F32)
    return lo.astype(BF16), hi.astype(BF16)


def _dispatch_kernel(cnt_sm, lo_sm, dst_sm, extra_sm, gs_sm, gn_sm, hn_ref, rrow_ref, rcol_ref, xb_ref,
                     xs_s, z_s, semx, semx2, semz, semz2):
    i = pl.program_id(0)
    nsteps = pl.num_programs(0)
    half = D_MODEL // 2

    def gap_fill(first, last, sem, wait):
        def ebody(e, carry):
            start = pl.multiple_of(gs_sm[e], SUB)
            left = gn_sm[e]
            for rows in (BLK, DCH, SUB):
                n = left // rows

                def body(c, cc, rows=rows, start=start):
                    cp = pltpu.make_async_copy(
                        z_s.at[pl.ds(0, rows), :],
                        xb_ref.at[pl.ds(pl.multiple_of(start + c * rows, SUB), rows), :], sem)
                    cp.wait() if wait else cp.start()
                    return cc

                lax.fori_loop(0, n, body, 0)
                start = start + n * rows
                left = left - n * rows
            return carry

        lax.fori_loop(first, last, ebody, 0)

    @pl.when(i == 0)
    def _():
        for s in range(2):
            xs_s[s, DLOC_ROWS:, :] = jnp.zeros((DCH, XW), I32)
        z_s[...] = jnp.zeros(z_s.shape, I32)
        gap_fill(0, N_GAPS_NEAR, semz, False)
        gap_fill(N_GAPS_NEAR, N_GAPS, semz2, False)
        gap_fill(0, N_GAPS_NEAR, semz, True)

    def xcopy(s, src, dst, sem):
        return pltpu.make_async_copy(xs_s.at[s, pl.ds(src, DCH), :], xb_ref.at[pl.ds(dst, DCH), :],
                                     sem.at[s])

    def wait_tile(s, n_extra):
        for _ in range(N_EXPERTS):
            xcopy(s, 0, 0, semx).wait()

        def wbody(j, c):
            xcopy(s, 0, 0, semx2).wait()
            return c
        lax.fori_loop(0, n_extra, wbody, 0)

    r_i = lax.broadcasted_iota(I32, (DLOC_ROWS, TM), 0).astype(F32)
    lane = lax.broadcasted_iota(I32, (TM, LANES), 1)
    l_r = lax.broadcasted_iota(I32, (LANES, LANES), 0)
    ones3 = jnp.where(l_r < 3, 1.0, 0.0).astype(BF16)

    for s in range(2):
        t = 2 * i + s
        rows = slice(s * TM, (s + 1) * TM)

        @pl.when(i >= 1)
        def _(s=s, t=t):
            wait_tile(s, extra_sm[jnp.maximum(t - 2, 0)])

        rr = rrow_ref[:, rows]
        rc = rcol_ref[rows, :]
        perm0 = jnp.where(r_i == rr[R_POS0:R_POS0 + 1, :], 1.0, 0.0).astype(BF16)
        perm1 = jnp.where(r_i == rr[R_POS1:R_POS1 + 1, :], 1.0, 0.0).astype(BF16)
        xs = _dot(perm0 + perm1, hn_ref[rows, :])
        xs_s[s, :DLOC_ROWS, :half] = _pack_halves(xs)

        def parts(col, rc=rc):
            hi, mid, lo = (p.astype(F32)
                           for p in _split3(jnp.broadcast_to(rc[:, col:col + 1], (TM, LANES))))
            sel = jnp.where(lane == 0, hi, jnp.where(lane == 1, mid, jnp.where(lane == 2, lo, 0.0)))
            return sel.astype(BF16)

        wparts = _dot(perm0, parts(R_W0)) + _dot(perm1, parts(R_W1))
        wsort = _dot(wparts.astype(BF16), ones3)
        xs_s[s, :DLOC_ROWS, half:] = lax.bitcast_convert_type(wsort, I32)

        for e in range(N_EXPERTS):
            lo = pl.multiple_of(lo_sm[t * N_EXPERTS + e], SUB)
            dst = pl.multiple_of(dst_sm[t * N_EXPERTS + e], SUB)
            xcopy(s, lo, dst, semx).start()

        @pl.when(extra_sm[t] > 0)
        def _(s=s, t=t):
            def ebody(e, carry):
                n = cnt_sm[t * N_EXPERTS + e]
                lo = pl.multiple_of(lo_sm[t * N_EXPERTS + e], SUB)
                dst = pl.multiple_of(dst_sm[t * N_EXPERTS + e], SUB)

                def cbody(c, cc):
                    xcopy(s, lo + c * DCH, dst + c * DCH, semx2).start()
                    return cc

                return lax.fori_loop(1, (n + DCH - 1) // DCH, cbody, carry)

            lax.fori_loop(0, N_EXPERTS, ebody, 0)

    @pl.when(i == nsteps - 1)
    def _():
        for s in range(2):
            wait_tile(s, extra_sm[2 * i + s])
        gap_fill(N_GAPS_NEAR, N_GAPS, semz2, True)


def _dispatch(cnt_f, lo_f, dst_f, extra, gap_start, gap_rows, hn, rrow, rcol, nb):
    T = hn.shape[0]
    assert T % (2 * TM) == 0
    grid_spec = pltpu.PrefetchScalarGridSpec(
        num_scalar_prefetch=6,
        grid=(T // (2 * TM),),
        in_specs=[
            pl.BlockSpec((2 * TM, D_MODEL), lambda i, *_: (i, 0)),
            pl.BlockSpec((8, 2 * TM), lambda i, *_: (0, i)),
            pl.BlockSpec((2 * TM, LANES), lambda i, *_: (i, 0)),
        ],
        out_specs=pl.BlockSpec(memory_space=pl.ANY),
        scratch_shapes=[
            pltpu.VMEM((2, DLOC_ROWS + DCH, XW), I32),
            pltpu.VMEM((BLK, XW), I32),
            pltpu.SemaphoreType.DMA((2,)), pltpu.SemaphoreType.DMA((2,)),
            pltpu.SemaphoreType.DMA(()), pltpu.SemaphoreType.DMA(()),
        ],
    )
    return pl.pallas_call(
        _dispatch_kernel,
        grid_spec=grid_spec,
        out_shape=jax.ShapeDtypeStruct((nb * BLK, XW), I32),
        compiler_params=pltpu.CompilerParams(
            dimension_semantics=("arbitrary",), vmem_limit_bytes=40 * 1024 * 1024),
        name="dispatch",
    )(cnt_f, lo_f, dst_f, extra, gap_start, gap_rows, hn, rrow, rcol)


def _expert_kernel(be_sm, nu_sm, xb_ref, w1_ref, w3_ref, w2_ref, yb_ref, w1_s, w3_s, w2_s):
    p = pl.program_id(0)
    used = p < nu_sm[0]

    @pl.when(used & ((p == 0) | (be_sm[p] != be_sm[jnp.maximum(p - 1, 0)])))
    def _():
        w1_s[...] = w1_ref[...].astype(BF16)
        w3_s[...] = w3_ref[...].astype(BF16)
        w2_s[...] = w2_ref[...].astype(BF16)

    @pl.when(used)
    def _():
        half = D_MODEL // 2
        subs = [slice(r * ESUB, (r + 1) * ESUB) for r in range(BLK // ESUB)]

        def up(rs):
            xlo, xhi = _unpack_halves(xb_ref[rs, :half])
            h1 = _dot(xlo, w1_s[:half, :]) + _dot(xhi, w1_s[half:, :])
            h3 = _dot(xlo, w3_s[:half, :]) + _dot(xhi, w3_s[half:, :])
            return h1, h3

        def down(rs, h1, h3):
            wrep = lax.bitcast_convert_type(xb_ref[rs, half:], F32)
            wfull = jnp.concatenate([wrep] * (D_EXPERT // LANES), axis=1)
            hdn = (h1 * jax.nn.sigmoid(h1) * h3 * wfull).astype(BF16)
            y = _dot(hdn, w2_s[...]).astype(BF16).astype(F32)
            yb_ref[rs, :] = _pack_halves(y)

        hs = [up(subs[0])]
        for r, rs in enumerate(subs):
            if r + 1 < len(subs):
                hs.append(up(subs[r + 1]))
            down(rs, *hs[r])

    @pl.when(jnp.logical_not(used))
    def _():
        yb_ref[...] = jnp.zeros(yb_ref.shape, I32)


def _experts(blk_e, nused, xb, w1, w3, w2, nb):
    def rows(p, be, nu):
        return (jnp.minimum(p, nu[0] - 1), 0)

    def wsel(p, be, nu):
        return (be[jnp.minimum(p, nu[0] - 1)], 0, 0)

    grid_spec = pltpu.PrefetchScalarGridSpec(
        num_scalar_prefetch=2,
        grid=(nb,),
        in_specs=[
            pl.BlockSpec((BLK, XW), rows),
            pl.BlockSpec((None, D_MODEL, D_EXPERT), wsel),
            pl.BlockSpec((None, D_MODEL, D_EXPERT), wsel),
            pl.BlockSpec((None, D_EXPERT, D_MODEL), wsel),
        ],
        out_specs=pl.BlockSpec((BLK, D_MODEL // 2), lambda p, be, nu: (p, 0)),
        scratch_shapes=[
            pltpu.VMEM((D_MODEL, D_EXPERT), BF16), pltpu.VMEM((D_MODEL, D_EXPERT), BF16),
            pltpu.VMEM((D_EXPERT, D_MODEL), BF16),
        ],
    )
    return pl.pallas_call(
        _expert_kernel,
        grid_spec=grid_spec,
        out_shape=jax.ShapeDtypeStruct((nb * BLK, D_MODEL // 2), I32),
        compiler_params=pltpu.CompilerParams(
            dimension_semantics=("arbitrary",), vmem_limit_bytes=48 * 1024 * 1024),
        name="experts",
    )(blk_e, nused, xb, w1, w3, w2)


def _combine_kernel(cnt_sm, lo_sm, dst_sm, used_sm, extra_sm, yb_ref, x2_ref, rcol_ref, g_ref, out_ref,
                    yl_s, y_s, sem, sem2):
    i = pl.program_id(0)
    nsteps = pl.num_programs(0)
    bset = i % 2

    def ycopy(s, src, dst, sm):
        return pltpu.make_async_copy(yb_ref.at[pl.ds(src, CCH), :], yl_s.at[s, pl.ds(dst, CCH), :],
                                     sm.at[s])

    def issue(tile, s):
        for e in range(N_EXPERTS):
            lo = pl.multiple_of(lo_sm[tile * N_EXPERTS + e], CCH)
            src = pl.multiple_of(dst_sm[tile * N_EXPERTS + e], SUB)
            ycopy(s, src, lo, sem).start()

        @pl.when(extra_sm[tile] > 0)
        def _():
            def ebody(e, carry):
                n = cnt_sm[tile * N_EXPERTS + e]
                lo = pl.multiple_of(lo_sm[tile * N_EXPERTS + e], CCH)
                src = pl.multiple_of(dst_sm[tile * N_EXPERTS + e], SUB)

                def cbody(c, cc):
                    ycopy(s, src + c * CCH, lo + c * CCH, sem2).start()
                    return cc

                return lax.fori_loop(1, (n + CCH - 1) // CCH, cbody, carry)

            lax.fori_loop(0, N_EXPERTS, ebody, 0)

    @pl.when(i == 0)
    def _():
        yl_s[...] = jnp.zeros(yl_s.shape, I32)
        for s in range(CTILES):
            issue(s, s)

    @pl.when(i + 1 < nsteps)
    def _():
        for s in range(CTILES):
            issue(CTILES * (i + 1) + s, (1 - bset) * CTILES + s)

    lane = lax.broadcasted_iota(I32, (TM, CBLK), 1).astype(F32)
    half = D_MODEL // 2
    for s in range(CTILES):
        t = CTILES * i + s
        slot = bset * CTILES + s
        rows = slice(s * TM, (s + 1) * TM)
        for _ in range(N_EXPERTS):
            ycopy(slot, 0, 0, sem).wait()

        def wbody(j, c, slot=slot):
            ycopy(slot, 0, 0, sem2).wait()
            return c

        lax.fori_loop(0, extra_sm[t], wbody, 0)

        rc = rcol_ref[rows, :]
        pad0 = rc[:, R_PAD0:R_PAD0 + 1]
        pad1 = rc[:, R_PAD1:R_PAD1 + 1]

        def chunk(c, pad0=pad0, pad1=pad1, slot=slot):
            r = lane + float(c * CBLK)
            selm = (jnp.where(pad0 == r, 1.0, 0.0) + jnp.where(pad1 == r, 1.0, 0.0)).astype(BF16)
            lo_h, hi_h = _unpack_halves(yl_s[slot, c * CBLK:(c + 1) * CBLK, :])
            return _dot(selm, lo_h), _dot(selm, hi_h)

        ylo = jnp.zeros((TM, half), F32)
        yhi = jnp.zeros((TM, half), F32)
        for c in range(LOC_COMMON // CBLK):
            dlo, dhi = chunk(c)
            ylo = ylo + dlo
            yhi = yhi + dhi
        out_ref[rows, :] = _rms(x2_ref[rows, :] + jnp.concatenate([ylo, yhi], axis=1), g_ref[...])

        @pl.when(used_sm[t] > LOC_COMMON)
        def _(ylo=ylo, yhi=yhi, chunk=chunk, rows=rows, t=t):
            y_s[:, :half] = ylo
            y_s[:, half:] = yhi
            for c in range(LOC_COMMON // CBLK, LOC_ROWS // CBLK):
                @pl.when(used_sm[t] > c * CBLK)
                def _(c=c):
                    dlo, dhi = chunk(c)
                    y_s[:, :half] += dlo
                    y_s[:, half:] += dhi
            out_ref[rows, :] = _rms(x2_ref[rows, :] + y_s[...], g_ref[...])


def _combine(cnt_f, lo_slot_f, dst_f, used, extra, yb, x2, rcol, g_final):
    T = x2.shape[0]
    tg = CTILES * TM
    assert T % tg == 0
    grid_spec = pltpu.PrefetchScalarGridSpec(
        num_scalar_prefetch=5,
        grid=(T // tg,),
        in_specs=[
            pl.BlockSpec(memory_space=pl.ANY),
            pl.BlockSpec((tg, D_MODEL), lambda i, *_: (i, 0)),
            pl.BlockSpec((tg, LANES), lambda i, *_: (i, 0)),
            pl.BlockSpec((1, D_MODEL), lambda i, *_: (0, 0)),
        ],
        out_specs=pl.BlockSpec((tg, D_MODEL), lambda i, *_: (i, 0)),
        scratch_shapes=[
            pltpu.VMEM((2 * CTILES, LOC_ROWS, D_MODEL // 2), I32),
            pltpu.VMEM((TM, D_MODEL), F32),
            pltpu.SemaphoreType.DMA((2 * CTILES,)), pltpu.SemaphoreType.DMA((2 * CTILES,)),
        ],
    )
    return pl.pallas_call(
        _combine_kernel,
        grid_spec=grid_spec,
        out_shape=jax.ShapeDtypeStruct((T, D_MODEL), F32),
        compiler_params=pltpu.CompilerParams(
            dimension_semantics=("arbitrary",), vmem_limit_bytes=40 * 1024 * 1024),
        name="combine",
    )(cnt_f, lo_slot_f, dst_f, used, extra, yb, x2, rcol, g_final)


def _rope_tables(S):
    half = DA_HEAD_DIM // 2
    inv = (1.0 / (np.float32(ROPE_THETA) ** (np.arange(0, DA_HEAD_DIM, 2, dtype=np.float32)
                                             / np.float32(DA_HEAD_DIM)))).astype(np.float32)
    ang = (np.arange(S, dtype=np.float32)[:, None] * inv[None, :]).astype(np.float32)
    cos, sin = np.cos(ang), np.sin(ang)
    lane = np.arange(LANES)
    idx = lane % half
    lower = (lane % DA_HEAD_DIM) < half
    c = cos[:, idx]
    s = sin[:, idx]
    z = np.zeros_like(s)
    return (jnp.asarray(c, F32), jnp.asarray(np.where(lower[None, :], -s, z), F32),
            jnp.asarray(np.where(lower[None, :], z, s), F32))


def _tri_dot(a, b):
    return jnp.dot(a.astype(F32), b.astype(F32), precision=lax.Precision.HIGHEST).astype(I32)


def kernel(x, w_in, conv_w, conv_b, gate_b, lam_qk, subln_g, mhnorm_g, w_out, g_mix, g_ffn, w_grp,
           b_grp, w_erouter, b_erouter, w1, w3, w2, g_final):
    B, S, D = x.shape
    T = B * S
    nt = T // TM
    x2d = x.reshape(T, D)
    l = 0

    w_main = w_in[l, :, :N_MAIN].astype(BF16)
    wg8 = w_in[l, :, N_MAIN:]
    wg = jnp.pad(wg8, ((0, 0), (0, LANES - 8))).astype(BF16)
    wgt = wg8.T.astype(BF16)
    gb8 = gate_b[l].reshape(8)
    gb_col = jnp.pad(gb8, (0, LANES - 8)).reshape(1, LANES)
    gb_row = jnp.broadcast_to(gb8[:, None], (8, LANES))
    rope_c, rope_sa, rope_sb = _rope_tables(S)
    w_r = jnp.concatenate(
        [w_grp[l], w_erouter[l].transpose(1, 0, 2).reshape(D, N_EXPERTS)], axis=1)
    w_r = jnp.pad(w_r, ((0, 0), (0, LANES - w_r.shape[1])))
    wr_hi = w_r.astype(BF16)
    wr_t = jnp.concatenate([wr_hi, (w_r - wr_hi.astype(F32)).astype(BF16)], axis=1)
    b_r = jnp.pad(jnp.concatenate([b_grp[l], b_erouter[l].reshape(N_EXPERTS)]),
                  (0, LANES - N_GROUPS - N_EXPERTS))
    b_rt = jnp.broadcast_to(b_r[:, None], (LANES, TM))

    q1, q2, kr, vat, qc, kc, vmt, om, gcol, grow = _inproj(
        x2d, g_mix[l].reshape(1, D), w_main, wg, wgt, rope_c, rope_sa, rope_sb,
        conv_w[l], conv_b[l].reshape(1, -1), S)
    oa = _attention(q1, q2, kr, vat, lam_qk[l], subln_g[l].reshape(1, LANES), B, S)
    hm = _mlstm(qc, kc, vmt, om, gcol, grow, gb_col, gb_row, mhnorm_g[l].reshape(1, ML_WIDTH), B, S)
    x2, hn, rrow, rcol, cnt3 = _router(oa, hm, x2d, w_out[l].astype(BF16), g_ffn[l].reshape(1, D),
                                       wr_t, b_rt)

    cnt = cnt3[:, 0, EXP_ROW0:EXP_ROW0 + N_EXPERTS]
    c8 = (cnt + SUB - 1) // SUB * SUB
    c_slot = jnp.maximum((cnt + CCH - 1) // CCH, 1) * CCH
    odd = (np.arange(nt) % 2)[:, None]
    c8_even = c8 * (1 - odd)
    c8_odd = c8 * odd
    even_tot = jnp.sum(c8_even, axis=0)
    count8 = even_tot + DCH + jnp.sum(c8_odd, axis=0)
    pcount = (count8 + DCH + BLK - 1) // BLK * BLK
    up_e = np.triu(np.ones((N_EXPERTS, N_EXPERTS), np.float32), 1)
    lo_t = np.tril(np.ones((nt, nt), np.float32), -1)
    pstart = _tri_dot(pcount[None, :], up_e)[0]
    pend = pstart + pcount
    dst = pstart[None, :] + jnp.where(odd == 0, _tri_dot(lo_t, c8_even),
                                      (even_tot + DCH)[None, :] + _tri_dot(lo_t, c8_odd))
    lo8 = _tri_dot(c8, up_e)
    lo_slot = _tri_dot(c_slot, up_e)
    nb = (2 * T + nt * N_EXPERTS * (SUB - 1) + N_EXPERTS * (2 * DCH + BLK - 1)) // BLK + 1
    blk_row = jnp.arange(nb, dtype=I32) * BLK
    blk_e = jnp.minimum(jnp.sum((pend[None, :] <= blk_row[:, None]).astype(I32), axis=1),
                        N_EXPERTS - 1).astype(I32)
    nused = (pend[-1] // BLK).astype(I32).reshape(1)
    cnt_f = cnt.reshape(-1).astype(I32)
    dst_f = dst.reshape(-1).astype(I32)

    spare = jnp.full((N_EXPERTS,), DCH, I32)
    gap_start = jnp.concatenate([pstart + even_tot, pstart + count8, pstart + count8 + DCH,
                                 pend[-1:]]).astype(I32)
    gap_rows = jnp.concatenate([spare, spare, pcount - count8 - DCH,
                                nb * BLK - pend[-1:]]).astype(I32)
    extra = jnp.sum(jnp.maximum((cnt + DCH - 1) // DCH - 1, 0), axis=1).astype(I32)
    xb = _dispatch(cnt_f, lo8.reshape(-1), dst_f, extra, gap_start, gap_rows, hn, rrow, rcol, nb)
    yb = _experts(blk_e, nused, xb, w1[l], w3[l], w2[l], nb)
    used = jnp.sum(c_slot, axis=1).astype(I32)
    out = _combine(cnt_f, lo_slot.reshape(-1), dst_f, used, extra, yb, x2, rcol, g_final.reshape(1, D))
    return out.reshape(B, S, D)
```

```python
import functools
import math

import numpy as np
import jax
import jax.numpy as jnp
from jax import lax
from jax.experimental import pallas as pl
from jax.experimental.pallas import tpu as pltpu

F32 = jnp.float32
BF16 = jnp.bfloat16
I32 = jnp.int32

D_MODEL = 1024
DA_HEADS = 4
DA_HEAD_DIM = 64
DA_V_DIM = 128
DA_WIDTH = 512
ML_HEADS = 4
ML_WIDTH = 512
ML_HEAD_DIM = 128
ML_CHUNK = 128
ML_GROUP = 4
CONV_K = 4
ROPE_THETA = 10000.0
RMS_EPS = 1e-6
N_GROUPS = 4
EXPERTS_PER_GROUP = 8
N_EXPERTS = 32
D_EXPERT = 512
LAMBDA_INIT = 0.8 - 0.6 * math.exp(-0.3 * 0)

LANES = 128
SUB = 8
N_MAIN = 7 * 512
NEG = -1e30

TM_PROJ = 1024
TQ = 256
TK = 512
VT_ROWS = DA_V_DIM + 16
TM = 256
BLK = 1024
ESUB = 256
CBLK = 256
CTILES = 2
DCH = 32
CCH = 32
DLOC_ROWS = 768
N_GAPS_NEAR = 2 * N_EXPERTS
N_GAPS = N_GAPS_NEAR + N_EXPERTS + 1
LOC_COMMON = N_EXPERTS * CCH
LOC_ROWS = 2 * TM + N_EXPERTS * CCH
XW = D_MODEL // 2 + LANES
EXP_ROW0 = 4
RT_ROWS = 40
RT_GROUP = 4
R_E0, R_E1, R_POS0, R_POS1, R_PAD0, R_PAD1, R_W0, R_W1 = range(8)


def _nt_dot(a, b):
    return lax.dot_general(a, b, (((1,), (1,)), ((), ())), preferred_element_type=F32)


def _dot(a, b):
    return jnp.dot(a, b, preferred_element_type=F32)


def _split3(x):
    hi = x.astype(BF16)
    r = x - hi.astype(F32)
    mid = r.astype(BF16)
    lo = (r - mid.astype(F32)).astype(BF16)
    return hi, mid, lo


def _rms(x, g):
    return x * lax.rsqrt(jnp.mean(x * x, axis=-1, keepdims=True) + RMS_EPS) * g


def _inproj_kernel(x_ref, g_ref, w_ref, wg_ref, wgt_ref, c_ref, sa_ref, sb_ref, cw_ref, cb_ref,
                   q1_ref, q2_ref, kr_ref, va_ref, qc_ref, kc_ref, vm_ref, om_ref, gcol_ref, grow_ref,
                   h_s, cq_s, ck_s, *, tiles_per_seq):
    tm = x_ref.shape[0]
    cw = 512
    first = (pl.program_id(0) % tiles_per_seq) == 0

    @pl.when(first)
    def _():
        cq_s[0:8, :] = jnp.zeros((8, cw), F32)
        ck_s[0:8, :] = jnp.zeros((8, cw), F32)

    @pl.when(jnp.logical_not(first))
    def _():
        cq_s[0:8, :] = cq_s[tm:tm + 8, :]
        ck_s[0:8, :] = ck_s[tm:tm + 8, :]

    h_s[...] = _rms(x_ref[...], g_ref[...]).astype(BF16)
    nhalf = 2 if tm >= 512 else 1
    th = tm // nhalf
    halves = [slice(p * th, (p + 1) * th) for p in range(nhalf)]

    def cols(c, hv):
        return _dot(h_s[hv, :], w_ref[:, c * cw:(c + 1) * cw])

    rb = min(128, th)
    lane = lax.broadcasted_iota(I32, (rb, LANES), 1)
    qscale = DA_HEAD_DIM ** -0.5 * math.log2(math.e)

    def rope(x, rows):
        return (x * c_ref[rows, :] + pltpu.roll(x, 96, 1) * sa_ref[rows, :]
                + pltpu.roll(x, 32, 1) * sb_ref[rows, :])

    def blocks(hv):
        for r in range(th // rb):
            for hh in range(cw // LANES):
                yield (slice(hv.start + r * rb, hv.start + (r + 1) * rb), slice(r * rb, (r + 1) * rb),
                       slice(hh * LANES, (hh + 1) * LANES))

    def rope_q(qa, hv):
        for rows, loc, hs in blocks(hv):
            qr = rope(qa[loc, hs], rows) * qscale
            q1_ref[rows, hs] = jnp.where(lane < DA_HEAD_DIM, qr, 0.0).astype(BF16)
            q2_ref[rows, hs] = jnp.where(lane >= DA_HEAD_DIM, qr, 0.0).astype(BF16)

    def rope_k(ka, hv):
        for rows, loc, hs in blocks(hv):
            kr_ref[rows, hs] = rope(ka[loc, hs], rows).astype(BF16)

    def conv_silu(src, c0, hv, out_ref, scale):
        for rows, _, hs in blocks(hv):
            cs = slice(c0 + hs.start, c0 + hs.stop)
            w = cw_ref[:, cs]
            y = cb_ref[:, cs] + src[8 + rows.start:8 + rows.stop, hs] * w[3:4]
            for s in (1, 2, 3):
                y = y + src[8 + rows.start - s:8 + rows.stop - s, hs] * w[3 - s:4 - s]
            y = y * jax.nn.sigmoid(y)
            out_ref[rows, hs] = (y if scale is None else y * scale).astype(BF16)

    def store_t(out_ref, r, hv):
        for j in range(cw // LANES):
            out_ref[j * LANES:(j + 1) * LANES, hv] = r[:, j * LANES:(j + 1) * LANES].T.astype(BF16)

    def to_conv(c, dst, hv):
        dst[8 + hv.start:8 + hv.stop, :] = cols(c, hv)

    qa = [cols(0, hv) for hv in halves]
    ka = []
    for p, hv in enumerate(halves):
        ka.append(cols(1, hv))
        rope_q(qa[p], hv)
    for p, hv in enumerate(halves):
        store_t(va_ref, cols(2, hv), hv)
        rope_k(ka[p], hv)
    for hv in halves:
        to_conv(3, cq_s, hv)
    for hv in halves:
        to_conv(4, ck_s, hv)
        conv_silu(cq_s, 0, hv, qc_ref, None)
    for hv in halves:
        store_t(vm_ref, cols(5, hv), hv)
        conv_silu(ck_s, ML_WIDTH, hv, kc_ref, ML_HEAD_DIM ** -0.5)
    om_ref[...] = _dot(h_s[...], w_ref[:, 6 * cw:7 * cw]).astype(BF16)
    gcol_ref[...] = _dot(h_s[...], wg_ref[...])
    grow_ref[...] = _nt_dot(wgt_ref[...], h_s[...])


def _inproj(x2d, g_mix, w_main, wg, wgt, rope_c, rope_sa, rope_sb, conv_w, conv_b, S):
    T = x2d.shape[0]
    tm = min(TM_PROJ, S)
    assert S % tm == 0
    tps = S // tm
    full = lambda shape: pl.BlockSpec(shape, lambda i: (0, 0))
    rope_spec = pl.BlockSpec((tm, LANES), lambda i: (i % tps, 0))
    col_spec = pl.BlockSpec((tm, 512), lambda i: (i, 0))
    tcol_spec = pl.BlockSpec((512, tm), lambda i: (0, i))
    row_shape = jax.ShapeDtypeStruct((T, 512), BF16)
    t_shape = jax.ShapeDtypeStruct((512, T), BF16)
    return pl.pallas_call(
        functools.partial(_inproj_kernel, tiles_per_seq=tps),
        grid=(T // tm,),
        in_specs=[
            pl.BlockSpec((tm, D_MODEL), lambda i: (i, 0)),
            full((1, D_MODEL)), full((D_MODEL, N_MAIN)), full((D_MODEL, LANES)), full((8, D_MODEL)),
            rope_spec, rope_spec, rope_spec,
            full((CONV_K, 2 * ML_WIDTH)), full((1, 2 * ML_WIDTH)),
        ],
        out_specs=[col_spec, col_spec, col_spec, tcol_spec, col_spec, col_spec, tcol_spec, col_spec,
                   pl.BlockSpec((tm, LANES), lambda i: (i, 0)),
                   pl.BlockSpec((8, tm), lambda i: (0, i))],
        out_shape=[row_shape, row_shape, row_shape, t_shape, row_shape, row_shape, t_shape, row_shape,
                   jax.ShapeDtypeStruct((T, LANES), F32),
                   jax.ShapeDtypeStruct((8, T), F32)],
        scratch_shapes=[pltpu.VMEM((tm, D_MODEL), BF16),
                        pltpu.VMEM((tm + 8, ML_WIDTH), F32), pltpu.VMEM((tm + 8, ML_WIDTH), F32)],
        compiler_params=pltpu.CompilerParams(
            dimension_semantics=("arbitrary",), vmem_limit_bytes=56 * 1024 * 1024),
        name="inproj",
    )(x2d, g_mix, w_main, wg, wgt, rope_c, rope_sa, rope_sb, conv_w, conv_b)


def _attn_kernel(q1_s, q2_s, k_s, vt_ref, lam_ref, g_ref, o_ref, acc_s, st_s):
    S = vt_ref.shape[1]

    def values_t(h, ko, tk):
        return jnp.concatenate([vt_ref[h * DA_V_DIM:(h + 1) * DA_V_DIM, pl.ds(ko, tk)],
                                jnp.ones((VT_ROWS - DA_V_DIM, tk), BF16)], axis=0)

    lq = lam_ref[...]
    lam = (jnp.exp(jnp.sum(lq[0:1] * lq[1:2], axis=-1, keepdims=True))
           - jnp.exp(jnp.sum(lq[2:3] * lq[3:4], axis=-1, keepdims=True)) + LAMBDA_INIT)

    key = lax.broadcasted_iota(I32, (TK, 2 * TQ), 0)
    qry = lax.broadcasted_iota(I32, (TK, 2 * TQ), 1)
    kq = key - jnp.where(qry >= TQ, qry - TQ, qry)

    def scores(qo, ko, tk, h, causal=None):
        hs = slice(h * LANES, (h + 1) * LANES)
        qq = jnp.concatenate([q1_s[pl.ds(qo, TQ), hs], q2_s[pl.ds(qo, TQ), hs]], axis=0)
        st = _nt_dot(k_s[pl.ds(ko, tk), hs], qq)
        st_s[h, :tk, :] = st if causal is None else jnp.where(causal, st, NEG)

    def kvstep(qo, ko, tk, ms, causal, slot, next_ko=None):
        out = []
        if next_ko is None:
            scores(qo, ko, tk, 0, causal)
        for h in range(DA_HEADS):
            if h + 1 < DA_HEADS:
                scores(qo, ko, tk, h + 1, causal)
            elif next_ko is not None:
                scores(qo, next_ko, tk, 0)
            st = st_s[h, :tk, :]
            m_new = jnp.maximum(ms[h], jnp.max(st, axis=0, keepdims=True))
            alpha = jnp.exp2(ms[h] - m_new)
            p = jnp.exp2(st - m_new).astype(BF16)
            acc_s[slot, h] = alpha * acc_s[slot, h] + _dot(values_t(h, ko, tk), p)
            out.append(m_new)
        return tuple(out)

    def start(i, ko, tk, off, slot):
        acc_s[slot] = jnp.zeros(acc_s.shape[1:], F32)
        m0 = tuple(jnp.full((1, 2 * TQ), NEG, F32) for _ in range(DA_HEADS))
        return kvstep(pl.multiple_of(i * TQ, TQ), ko, tk, m0, kq[:tk] <= off, slot)

    def full_steps(i, n, ms, slot):
        qo = pl.multiple_of(i * TQ, TQ)
        scores(qo, 0, TK, 0)

        def step(j, m):
            nxt = pl.multiple_of(jnp.minimum(j + 1, n - 1) * TK, TK)
            return kvstep(qo, pl.multiple_of(j * TK, TK), TK, m, None, slot, next_ko=nxt)

        lax.fori_loop(0, n, step, ms)

    def finalize(i, slot):
        qo = pl.multiple_of(i * TQ, TQ)
        for h in range(DA_HEADS):
            hs = slice(h * LANES, (h + 1) * LANES)
            acc = acc_s[slot, h]
            on = acc[:DA_V_DIM] * (1.0 / acc[DA_V_DIM:DA_V_DIM + 1])
            ot = on[:, :TQ] - lam * on[:, TQ:]
            o = _rms(ot.T, g_ref[...]) * (1.0 - LAMBDA_INIT)
            o_ref[pl.ds(qo, TQ), hs] = o.astype(BF16)

    def tile_pair(u, carry):
        ko = pl.multiple_of(u * TK, TK)
        finalize(2 * u - 1, 1)
        full_steps(2 * u, u, start(2 * u, ko, TQ, 0, 0), 0)
        finalize(2 * u, 0)
        full_steps(2 * u + 1, u, start(2 * u + 1, ko, TK, TQ, 1), 1)
        return carry

    assert TK == 2 * TQ and (S // TQ) % 2 == 0
    start(0, 0, TQ, 0, 0)
    finalize(0, 0)
    start(1, 0, TK, TQ, 1)
    lax.fori_loop(1, S // TK, tile_pair, 0)
    finalize(S // TQ - 1, 1)


def _attention(q1, q2, kr, vat, lam_qk, subln_g, B, S):
    T = B * S
    full = lambda shape: pl.BlockSpec(shape, lambda b: (0, 0))
    seq = pl.BlockSpec((S, DA_WIDTH), lambda b: (b, 0))
    seq_t = pl.BlockSpec((DA_WIDTH, S), lambda b: (0, b))
    return pl.pallas_call(
        _attn_kernel,
        grid=(B,),
        in_specs=[seq, seq, seq, seq_t, full((4, DA_HEAD_DIM)), full((1, LANES))],
        out_specs=seq,
        out_shape=jax.ShapeDtypeStruct((T, DA_WIDTH), BF16),
        scratch_shapes=[
            pltpu.VMEM((2, DA_HEADS, VT_ROWS, 2 * TQ), F32),
            pltpu.VMEM((DA_HEADS, TK, 2 * TQ), F32),
        ],
        compiler_params=pltpu.CompilerParams(
            dimension_semantics=("arbitrary",), vmem_limit_bytes=48 * 1024 * 1024),
        name="attn",
    )(q1, q2, kr, vat, lam_qk, subln_g)


def _log_sigmoid(x):
    return jnp.minimum(x, 0.0) - jnp.log(1.0 + jnp.exp(-jnp.abs(x)))


def _mlstm_kernel(qc_s, kc_s, vt_ref, o_ref, gcol_ref, grow_ref, gbc_ref, gbr_ref,
                  mg_ref, out_ref, ct_s, m_s):
    S = vt_ref.shape[1]
    L = ML_CHUNK
    nc = S // L
    ones_rows = jnp.ones((ML_HEAD_DIM, L), BF16)

    ct_s[...] = jnp.zeros(ct_s.shape, F32)
    m_s[...] = jnp.zeros(m_s.shape, F32)

    ri = lax.broadcasted_iota(I32, (L, L), 0)
    ci = lax.broadcasted_iota(I32, (L, L), 1)
    causal_t = ri <= ci
    tril = jnp.where(ci <= ri, 1.0, 0.0).astype(BF16)
    triu = jnp.where(causal_t, 1.0, 0.0).astype(BF16)

    heads = [slice(h * ML_HEAD_DIM, (h + 1) * ML_HEAD_DIM) for h in range(ML_HEADS)]

    def gates(c):
        ro = pl.multiple_of(c * L, L)
        gc = gcol_ref[pl.ds(ro, L), :] + gbc_ref[...]
        gr = grow_ref[:, pl.ds(ro, L)] + gbr_ref[...]
        b_c = sum(_dot(tril, p) for p in _split3(_log_sigmoid(gc)))
        b_r = sum(_dot(p, triu) for p in _split3(_log_sigmoid(gr)))
        return ro, gc, gr, b_c, b_r

    def read_state(g):
        ro = g[0]
        ks, vts, kq, cq, ms = [], [], [], [], []
        for h, hs in enumerate(heads):
            q = qc_s[pl.ds(ro, L), hs]
            k = kc_s[pl.ds(ro, L), hs]
            ks.append(k)
            vts.append(jnp.concatenate([vt_ref[hs, pl.ds(ro, L)], ones_rows], axis=0))
            kq.append(_nt_dot(k, q))
            cq.append(_nt_dot(ct_s[h].astype(BF16), q))
            ms.append(m_s[h:h + 1, 0:1])
        return ks, vts, kq, cq, ms

    def update_state(g, d):
        _, _, gr, _, b_r = g
        ks, vts, _, _, ms = d
        for h in range(ML_HEADS):
            br = b_r[4 + h:5 + h, :]
            bl = br[:, L - 1:L]
            dec = bl - br + gr[h:h + 1, :]
            m_new = jnp.maximum(bl + ms[h], jnp.max(dec, axis=1, keepdims=True))
            ws = jnp.exp(dec - m_new)
            sc = jnp.exp(bl + ms[h] - m_new)
            vw = (vts[h].astype(F32) * ws).astype(BF16)
            ct_s[h] = sc * ct_s[h] + _dot(vw, ks[h])
            m_s[h:h + 1, :] = jnp.broadcast_to(m_new, (1, LANES))

    def outputs(g, d):
        ro, gc, _, b_c, b_r = g
        _, vts, kq, cq, ms = d
        for h, hs in enumerate(heads):
            br = b_r[4 + h:5 + h, :]
            a_col = b_c[:, 4 + h:5 + h] - gc[:, h:h + 1]
            dm = jnp.where(causal_t, br - a_col, NEG)
            inter = br + ms[h]
            m_row = jnp.maximum(inter, jnp.max(dm, axis=0, keepdims=True))
            sm = jnp.exp(dm - m_row) * kq[h]
            sc_in = jnp.exp(inter - m_row)
            a = _dot(vts[h], sm.astype(BF16)) + sc_in * cq[h]
            den = a[ML_HEAD_DIM:ML_HEAD_DIM + 1, :]
            hh = (a[:ML_HEAD_DIM] / jnp.maximum(jnp.abs(den), jnp.exp(-m_row))).T
            hg = hh * jax.nn.sigmoid(o_ref[pl.ds(ro, L), hs].astype(F32))
            out_ref[pl.ds(ro, L), hs] = _rms(hg, mg_ref[:, hs]).astype(BF16)

    def chunk_group(cg, carry):
        gs = [gates(ML_GROUP * cg + j) for j in range(ML_GROUP)]
        ds = []
        for g in gs:
            ds.append(read_state(g))
            update_state(g, ds[-1])
        for g, d in zip(gs, ds):
            outputs(g, d)
        return carry

    lax.fori_loop(0, nc // ML_GROUP, chunk_group, 0)


def _mlstm(qc, kc, vmt, om, gcol, grow, gb_col, gb_row, mh_g, B, S):
    T = B * S
    full = lambda shape: pl.BlockSpec(shape, lambda b: (0, 0))
    seq = pl.BlockSpec((S, ML_WIDTH), lambda b: (b, 0))
    return pl.pallas_call(
        _mlstm_kernel,
        grid=(B,),
        in_specs=[
            seq, seq, pl.BlockSpec((ML_WIDTH, S), lambda b: (0, b)), seq,
            pl.BlockSpec((S, LANES), lambda b: (b, 0)),
            pl.BlockSpec((8, S), lambda b: (0, b)),
            full((1, LANES)), full((8, LANES)), full((1, ML_WIDTH)),
        ],
        out_specs=seq,
        out_shape=jax.ShapeDtypeStruct((T, ML_WIDTH), BF16),
        scratch_shapes=[
            pltpu.VMEM((ML_HEADS, 2 * ML_HEAD_DIM, ML_HEAD_DIM), F32),
            pltpu.VMEM((8, LANES), F32),
        ],
        compiler_params=pltpu.CompilerParams(
            dimension_semantics=("arbitrary",), vmem_limit_bytes=56 * 1024 * 1024),
        name="mlstm",
    )(qc, kc, vmt, om, gcol, grow, gb_col, gb_row, mh_g)


def _router_kernel(oa_ref, hm_ref, x_ref, wo_ref, g_ref, wr_ref, brt_ref,
                   x2_ref, hn_ref, rrow_ref, rcol_ref, cnt_ref):
    def project(s):
        rs = slice(s * TM, (s + 1) * TM)
        mixo = _dot(oa_ref[rs, :], wo_ref[:DA_WIDTH, :]) + _dot(hm_ref[rs, :], wo_ref[DA_WIDTH:, :])
        x2 = x_ref[rs, :] + mixo
        x2_ref[rs, :] = x2
        hn = _rms(x2, g_ref[...])
        hn_hi = hn.astype(BF16)
        hn_ref[rs, :] = hn_hi
        hn_lo = (hn - hn_hi.astype(F32)).astype(BF16)
        a = _dot(hn_hi, wr_ref[...])
        b = _dot(hn_lo, wr_ref[:, :LANES])
        return (a[:, :LANES] + a[:, LANES:] + b).T[:RT_ROWS] + brt_ref[:RT_ROWS, :]

    lts = [project(0)]
    for s in range(RT_GROUP):
        if s + 1 < RT_GROUP:
            lts.append(project(s + 1))
        _route(s, lts[s], rrow_ref, rcol_ref, cnt_ref)


def _route(s, lt, rrow_ref, rcol_ref, cnt_ref):
    sub = lax.broadcasted_iota(I32, (RT_ROWS, TM), 0)
    sub_f = sub.astype(F32)

    def cmax(v):
        return jnp.max(v, axis=0, keepdims=True)

    def first_idx(mask):
        return jnp.min(jnp.where(mask, sub_f, 1e6), axis=0, keepdims=True).astype(I32)

    gl = jnp.where(sub < N_GROUPS, lt, NEG)
    gmax = cmax(gl)
    gsel = first_idx(gl == gmax)
    g_w = 1.0 / jnp.sum(jnp.exp(gl - gmax), axis=0, keepdims=True)
    elo = EXP_ROW0 + gsel * EXPERTS_PER_GROUP
    el = jnp.where((sub >= elo) & (sub < elo + EXPERTS_PER_GROUP), lt, NEG)
    v1 = cmax(el)
    i1 = first_idx(el == v1)
    el2 = jnp.where(sub == i1, NEG, el)
    v2 = cmax(el2)
    i2 = first_idx(el2 == v2)
    t = jnp.exp(v2 - v1)
    w0 = g_w / (1.0 + t)
    w1 = g_w * t / (1.0 + t)

    oh0 = jnp.where(sub == i1, 1.0, 0.0)
    oh1 = jnp.where(sub == i2, 1.0, 0.0)
    mh = oh0 + oh1
    r_i = lax.broadcasted_iota(I32, (TM, TM), 0)
    c_i = lax.broadcasted_iota(I32, (TM, TM), 1)
    before = jnp.where(r_i < c_i, 1.0, 0.0).astype(BF16)
    mh_f = jnp.concatenate([mh, jnp.zeros((LANES - RT_ROWS, TM), F32)], axis=0)
    mh_b = mh_f.astype(BF16)
    pre = _dot(mh_b, before)[:RT_ROWS]
    cnt = jnp.sum(mh_f, axis=1, keepdims=True)
    cnt8 = jnp.floor((cnt + (SUB - 1)) * (1.0 / SUB)) * SUB
    erow = lax.broadcasted_iota(I32, (LANES, 1), 0)
    is_exp = (erow >= EXP_ROW0) & (erow < EXP_ROW0 + N_EXPERTS)
    cnt_slot = jnp.where(is_exp, jnp.maximum(jnp.floor((cnt + (CCH - 1)) * (1.0 / CCH)), 1.0) * CCH, 0.0)
    e_r = lax.broadcasted_iota(I32, (LANES, LANES), 0)
    e_c = lax.broadcasted_iota(I32, (LANES, LANES), 1)
    below = jnp.where(e_c < e_r, 1.0, 0.0).astype(BF16)
    lo8 = _dot(below, jnp.broadcast_to(cnt8, (LANES, LANES)).astype(BF16))[:RT_ROWS, 0:1]
    lo_slot = _dot(below, jnp.broadcast_to(cnt_slot, (LANES, LANES)).astype(BF16))[:RT_ROWS, 0:1]

    def csum(v):
        return jnp.sum(v, axis=0, keepdims=True)

    rows = [None] * 8
    rows[R_E0] = (i1 - EXP_ROW0).astype(F32)
    rows[R_E1] = (i2 - EXP_ROW0).astype(F32)
    rows[R_POS0] = csum(oh0 * (pre + lo8))
    rows[R_POS1] = csum(oh1 * (pre + lo8))
    rows[R_PAD0] = csum(oh0 * (pre + lo_slot))
    rows[R_PAD1] = csum(oh1 * (pre + lo_slot))
    rows[R_W0] = w0
    rows[R_W1] = w1
    sub128 = lax.broadcasted_iota(I32, (LANES, TM), 0)
    r128 = jnp.zeros((LANES, TM), F32)
    for j, v in enumerate(rows):
        r128 = jnp.where(sub128 == j, v, r128)
    rrow_ref[:, s * TM:(s + 1) * TM] = r128[:8]
    rcol_ref[s * TM:(s + 1) * TM, :] = r128.T
    cnt_ref[s] = _nt_dot(jnp.ones((8, TM), BF16), mh_b)[0:1].astype(I32)


def _router(oa, hm, x2d, w_out, g_ffn, wr_t, b_rt):
    T = x2d.shape[0]
    nt = T // TM
    full = lambda shape: pl.BlockSpec(shape, lambda i: (0, 0))
    tg = RT_GROUP * TM
    return pl.pallas_call(
        _router_kernel,
        grid=(T // tg,),
        in_specs=[
            pl.BlockSpec((tg, DA_WIDTH), lambda i: (i, 0)),
            pl.BlockSpec((tg, ML_WIDTH), lambda i: (i, 0)),
            pl.BlockSpec((tg, D_MODEL), lambda i: (i, 0)),
            full((D_MODEL, D_MODEL)), full((1, D_MODEL)),
            full((D_MODEL, 2 * LANES)), full((LANES, TM)),
        ],
        out_specs=[
            pl.BlockSpec((tg, D_MODEL), lambda i: (i, 0)),
            pl.BlockSpec((tg, D_MODEL), lambda i: (i, 0)),
            pl.BlockSpec((8, tg), lambda i: (0, i)),
            pl.BlockSpec((tg, LANES), lambda i: (i, 0)),
            pl.BlockSpec((RT_GROUP, 1, LANES), lambda i: (i, 0, 0)),
        ],
        out_shape=[
            jax.ShapeDtypeStruct((T, D_MODEL), F32),
            jax.ShapeDtypeStruct((T, D_MODEL), BF16),
            jax.ShapeDtypeStruct((8, T), F32),
            jax.ShapeDtypeStruct((T, LANES), F32),
            jax.ShapeDtypeStruct((nt, 1, LANES), I32),
        ],
        compiler_params=pltpu.CompilerParams(
            dimension_semantics=("arbitrary",), vmem_limit_bytes=40 * 1024 * 1024),
        name="router",
    )(oa, hm, x2d, w_out, g_ffn, wr_t, b_rt)


def _pack_halves(x):
    lo = lax.bitcast_convert_type(x[:, :D_MODEL // 2], I32)
    hi = lax.bitcast_convert_type(x[:, D_MODEL // 2:], I32)
    return lax.shift_right_logical(lo, 16) | (hi & jnp.int32(-65536))


def _unpack_halves(w):
    lo = lax.bitcast_convert_type(lax.shift_left(w, 16), F32)
    hi = lax.bitcast_convert_type(w & jnp.int32(-65536), F32)
    return lo.astype(BF16), hi.astype(BF16)


def _dispatch_kernel(cnt_sm, lo_sm, dst_sm, extra_sm, gs_sm, gn_sm, hn_ref, rrow_ref, rcol_ref, xb_ref,
                     xs_s, z_s, semx, semx2, semz, semz2):
    i = pl.program_id(0)
    nsteps = pl.num_programs(0)
    half = D_MODEL // 2

    def gap_fill(first, last, sem, wait):
        def ebody(e, carry):
            start = pl.multiple_of(gs_sm[e], SUB)
            left = gn_sm[e]
            for rows in (BLK, DCH, SUB):
                n = left // rows

                def body(c, cc, rows=rows, start=start):
                    cp = pltpu.make_async_copy(
                        z_s.at[pl.ds(0, rows), :],
                        xb_ref.at[pl.ds(pl.multiple_of(start + c * rows, SUB), rows), :], sem)
                    cp.wait() if wait else cp.start()
                    return cc

                lax.fori_loop(0, n, body, 0)
                start = start + n * rows
                left = left - n * rows
            return carry

        lax.fori_loop(first, last, ebody, 0)

    @pl.when(i == 0)
    def _():
        for s in range(2):
            xs_s[s, DLOC_ROWS:, :] = jnp.zeros((DCH, XW), I32)
        z_s[...] = jnp.zeros(z_s.shape, I32)
        gap_fill(0, N_GAPS_NEAR, semz, False)
        gap_fill(N_GAPS_NEAR, N_GAPS, semz2, False)
        gap_fill(0, N_GAPS_NEAR, semz, True)

    def xcopy(s, src, dst, sem):
        return pltpu.make_async_copy(xs_s.at[s, pl.ds(src, DCH), :], xb_ref.at[pl.ds(dst, DCH), :],
                                     sem.at[s])

    def wait_tile(s, n_extra):
        for _ in range(N_EXPERTS):
            xcopy(s, 0, 0, semx).wait()

        def wbody(j, c):
            xcopy(s, 0, 0, semx2).wait()
            return c
        lax.fori_loop(0, n_extra, wbody, 0)

    r_i = lax.broadcasted_iota(I32, (DLOC_ROWS, TM), 0).astype(F32)
    lane = lax.broadcasted_iota(I32, (TM, LANES), 1)
    l_r = lax.broadcasted_iota(I32, (LANES, LANES), 0)
    ones3 = jnp.where(l_r < 3, 1.0, 0.0).astype(BF16)

    for s in range(2):
        t = 2 * i + s
        rows = slice(s * TM, (s + 1) * TM)

        @pl.when(i >= 1)
        def _(s=s, t=t):
            wait_tile(s, extra_sm[jnp.maximum(t - 2, 0)])

        rr = rrow_ref[:, rows]
        rc = rcol_ref[rows, :]
        perm0 = jnp.where(r_i == rr[R_POS0:R_POS0 + 1, :], 1.0, 0.0).astype(BF16)
        perm1 = jnp.where(r_i == rr[R_POS1:R_POS1 + 1, :], 1.0, 0.0).astype(BF16)
        xs = _dot(perm0 + perm1, hn_ref[rows, :])
        xs_s[s, :DLOC_ROWS, :half] = _pack_halves(xs)

        def parts(col, rc=rc):
            hi, mid, lo = (p.astype(F32)
                           for p in _split3(jnp.broadcast_to(rc[:, col:col + 1], (TM, LANES))))
            sel = jnp.where(lane == 0, hi, jnp.where(lane == 1, mid, jnp.where(lane == 2, lo, 0.0)))
            return sel.astype(BF16)

        wparts = _dot(perm0, parts(R_W0)) + _dot(perm1, parts(R_W1))
        wsort = _dot(wparts.astype(BF16), ones3)
        xs_s[s, :DLOC_ROWS, half:] = lax.bitcast_convert_type(wsort, I32)

        for e in range(N_EXPERTS):
            lo = pl.multiple_of(lo_sm[t * N_EXPERTS + e], SUB)
            dst = pl.multiple_of(dst_sm[t * N_EXPERTS + e], SUB)
            xcopy(s, lo, dst, semx).start()

        @pl.when(extra_sm[t] > 0)
        def _(s=s, t=t):
            def ebody(e, carry):
                n = cnt_sm[t * N_EXPERTS + e]
                lo = pl.multiple_of(lo_sm[t * N_EXPERTS + e], SUB)
                dst = pl.multiple_of(dst_sm[t * N_EXPERTS + e], SUB)

                def cbody(c, cc):
                    xcopy(s, lo + c * DCH, dst + c * DCH, semx2).start()
                    return cc

                return lax.fori_loop(1, (n + DCH - 1) // DCH, cbody, carry)

            lax.fori_loop(0, N_EXPERTS, ebody, 0)

    @pl.when(i == nsteps - 1)
    def _():
        for s in range(2):
            wait_tile(s, extra_sm[2 * i + s])
        gap_fill(N_GAPS_NEAR, N_GAPS, semz2, True)


def _dispatch(cnt_f, lo_f, dst_f, extra, gap_start, gap_rows, hn, rrow, rcol, nb):
    T = hn.shape[0]
    assert T % (2 * TM) == 0
    grid_spec = pltpu.PrefetchScalarGridSpec(
        num_scalar_prefetch=6,
        grid=(T // (2 * TM),),
        in_specs=[
            pl.BlockSpec((2 * TM, D_MODEL), lambda i, *_: (i, 0)),
            pl.BlockSpec((8, 2 * TM), lambda i, *_: (0, i)),
            pl.BlockSpec((2 * TM, LANES), lambda i, *_: (i, 0)),
        ],
        out_specs=pl.BlockSpec(memory_space=pl.ANY),
        scratch_shapes=[
            pltpu.VMEM((2, DLOC_ROWS + DCH, XW), I32),
            pltpu.VMEM((BLK, XW), I32),
            pltpu.SemaphoreType.DMA((2,)), pltpu.SemaphoreType.DMA((2,)),
            pltpu.SemaphoreType.DMA(()), pltpu.SemaphoreType.DMA(()),
        ],
    )
    return pl.pallas_call(
        _dispatch_kernel,
        grid_spec=grid_spec,
        out_shape=jax.ShapeDtypeStruct((nb * BLK, XW), I32),
        compiler_params=pltpu.CompilerParams(
            dimension_semantics=("arbitrary",), vmem_limit_bytes=40 * 1024 * 1024),
        name="dispatch",
    )(cnt_f, lo_f, dst_f, extra, gap_start, gap_rows, hn, rrow, rcol)


def _expert_kernel(be_sm, nu_sm, xb_ref, w1_ref, w3_ref, w2_ref, yb_ref, w1_s, w3_s, w2_s):
    p = pl.program_id(0)
    used = p < nu_sm[0]

    @pl.when(used & ((p == 0) | (be_sm[p] != be_sm[jnp.maximum(p - 1, 0)])))
    def _():
        w1_s[...] = w1_ref[...].astype(BF16)
        w3_s[...] = w3_ref[...].astype(BF16)
        w2_s[...] = w2_ref[...].astype(BF16)

    @pl.when(used)
    def _():
        half = D_MODEL // 2
        subs = [slice(r * ESUB, (r + 1) * ESUB) for r in range(BLK // ESUB)]

        def up(rs):
            xlo, xhi = _unpack_halves(xb_ref[rs, :half])
            h1 = _dot(xlo, w1_s[:half, :]) + _dot(xhi, w1_s[half:, :])
            h3 = _dot(xlo, w3_s[:half, :]) + _dot(xhi, w3_s[half:, :])
            return h1, h3

        def down(rs, h1, h3):
            wrep = lax.bitcast_convert_type(xb_ref[rs, half:], F32)
            wfull = jnp.concatenate([wrep] * (D_EXPERT // LANES), axis=1)
            hdn = (h1 * jax.nn.sigmoid(h1) * h3 * wfull).astype(BF16)
            y = _dot(hdn, w2_s[...]).astype(BF16).astype(F32)
            yb_ref[rs, :] = _pack_halves(y)

        hs = [up(subs[0])]
        for r, rs in enumerate(subs):
            if r + 1 < len(subs):
                hs.append(up(subs[r + 1]))
            down(rs, *hs[r])

    @pl.when(jnp.logical_not(used))
    def _():
        yb_ref[...] = jnp.zeros(yb_ref.shape, I32)


def _experts(blk_e, nused, xb, w1, w3, w2, nb):
    def rows(p, be, nu):
        return (jnp.minimum(p, nu[0] - 1), 0)

    def wsel(p, be, nu):
        return (be[jnp.minimum(p, nu[0] - 1)], 0, 0)

    grid_spec = pltpu.PrefetchScalarGridSpec(
        num_scalar_prefetch=2,
        grid=(nb,),
        in_specs=[
            pl.BlockSpec((BLK, XW), rows),
            pl.BlockSpec((None, D_MODEL, D_EXPERT), wsel),
            pl.BlockSpec((None, D_MODEL, D_EXPERT), wsel),
            pl.BlockSpec((None, D_EXPERT, D_MODEL), wsel),
        ],
        out_specs=pl.BlockSpec((BLK, D_MODEL // 2), lambda p, be, nu: (p, 0)),
        scratch_shapes=[
            pltpu.VMEM((D_MODEL, D_EXPERT), BF16), pltpu.VMEM((D_MODEL, D_EXPERT), BF16),
            pltpu.VMEM((D_EXPERT, D_MODEL), BF16),
        ],
    )
    return pl.pallas_call(
        _expert_kernel,
        grid_spec=grid_spec,
        out_shape=jax.ShapeDtypeStruct((nb * BLK, D_MODEL // 2), I32),
        compiler_params=pltpu.CompilerParams(
            dimension_semantics=("arbitrary",), vmem_limit_bytes=48 * 1024 * 1024),
        name="experts",
    )(blk_e, nused, xb, w1, w3, w2)


def _combine_kernel(cnt_sm, lo_sm, dst_sm, used_sm, extra_sm, yb_ref, x2_ref, rcol_ref, g_ref, out_ref,
                    yl_s, y_s, sem, sem2):
    i = pl.program_id(0)
    nsteps = pl.num_programs(0)
    bset = i % 2

    def ycopy(s, src, dst, sm):
        return pltpu.make_async_copy(yb_ref.at[pl.ds(src, CCH), :], yl_s.at[s, pl.ds(dst, CCH), :],
                                     sm.at[s])

    def issue(tile, s):
        for e in range(N_EXPERTS):
            lo = pl.multiple_of(lo_sm[tile * N_EXPERTS + e], CCH)
            src = pl.multiple_of(dst_sm[tile * N_EXPERTS + e], SUB)
            ycopy(s, src, lo, sem).start()

        @pl.when(extra_sm[tile] > 0)
        def _():
            def ebody(e, carry):
                n = cnt_sm[tile * N_EXPERTS + e]
                lo = pl.multiple_of(lo_sm[tile * N_EXPERTS + e], CCH)
                src = pl.multiple_of(dst_sm[tile * N_EXPERTS + e], SUB)

                def cbody(c, cc):
                    ycopy(s, src + c * CCH, lo + c * CCH, sem2).start()
                    return cc

                return lax.fori_loop(1, (n + CCH - 1) // CCH, cbody, carry)

            lax.fori_loop(0, N_EXPERTS, ebody, 0)

    @pl.when(i == 0)
    def _():
        yl_s[...] = jnp.zeros(yl_s.shape, I32)
        for s in range(CTILES):
            issue(s, s)

    @pl.when(i + 1 < nsteps)
    def _():
        for s in range(CTILES):
            issue(CTILES * (i + 1) + s, (1 - bset) * CTILES + s)

    lane = lax.broadcasted_iota(I32, (TM, CBLK), 1).astype(F32)
    half = D_MODEL // 2
    for s in range(CTILES):
        t = CTILES * i + s
        slot = bset * CTILES + s
        rows = slice(s * TM, (s + 1) * TM)
        for _ in range(N_EXPERTS):
            ycopy(slot, 0, 0, sem).wait()

        def wbody(j, c, slot=slot):
            ycopy(slot, 0, 0, sem2).wait()
            return c

        lax.fori_loop(0, extra_sm[t], wbody, 0)

        rc = rcol_ref[rows, :]
        pad0 = rc[:, R_PAD0:R_PAD0 + 1]
        pad1 = rc[:, R_PAD1:R_PAD1 + 1]

        def chunk(c, pad0=pad0, pad1=pad1, slot=slot):
            r = lane + float(c * CBLK)
            selm = (jnp.where(pad0 == r, 1.0, 0.0) + jnp.where(pad1 == r, 1.0, 0.0)).astype(BF16)
            lo_h, hi_h = _unpack_halves(yl_s[slot, c * CBLK:(c + 1) * CBLK, :])
            return _dot(selm, lo_h), _dot(selm, hi_h)

        ylo = jnp.zeros((TM, half), F32)
        yhi = jnp.zeros((TM, half), F32)
        for c in range(LOC_COMMON // CBLK):
            dlo, dhi = chunk(c)
            ylo = ylo + dlo
            yhi = yhi + dhi
        y_s[:, :half] = ylo
        y_s[:, half:] = yhi
        for c in range(LOC_COMMON // CBLK, LOC_ROWS // CBLK):
            @pl.when(used_sm[t] > c * CBLK)
            def _(c=c, chunk=chunk):
                dlo, dhi = chunk(c)
                y_s[:, :half] += dlo
                y_s[:, half:] += dhi
        out_ref[rows, :] = _rms(x2_ref[rows, :] + y_s[...], g_ref[...])


def _combine(cnt_f, lo_slot_f, dst_f, used, extra, yb, x2, rcol, g_final):
    T = x2.shape[0]
    tg = CTILES * TM
    assert T % tg == 0
    grid_spec = pltpu.PrefetchScalarGridSpec(
        num_scalar_prefetch=5,
        grid=(T // tg,),
        in_specs=[
            pl.BlockSpec(memory_space=pl.ANY),
            pl.BlockSpec((tg, D_MODEL), lambda i, *_: (i, 0)),
            pl.BlockSpec((tg, LANES), lambda i, *_: (i, 0)),
            pl.BlockSpec((1, D_MODEL), lambda i, *_: (0, 0)),
        ],
        out_specs=pl.BlockSpec((tg, D_MODEL), lambda i, *_: (i, 0)),
        scratch_shapes=[
            pltpu.VMEM((2 * CTILES, LOC_ROWS, D_MODEL // 2), I32),
            pltpu.VMEM((TM, D_MODEL), F32),
            pltpu.SemaphoreType.DMA((2 * CTILES,)), pltpu.SemaphoreType.DMA((2 * CTILES,)),
        ],
    )
    return pl.pallas_call(
        _combine_kernel,
        grid_spec=grid_spec,
        out_shape=jax.ShapeDtypeStruct((T, D_MODEL), F32),
        compiler_params=pltpu.CompilerParams(
            dimension_semantics=("arbitrary",), vmem_limit_bytes=40 * 1024 * 1024),
        name="combine",
    )(cnt_f, lo_slot_f, dst_f, used, extra, yb, x2, rcol, g_final)


def _rope_tables(S):
    half = DA_HEAD_DIM // 2
    inv = (1.0 / (np.float32(ROPE_THETA) ** (np.arange(0, DA_HEAD_DIM, 2, dtype=np.float32)
                                             / np.float32(DA_HEAD_DIM)))).astype(np.float32)
    ang = (np.arange(S, dtype=np.float32)[:, None] * inv[None, :]).astype(np.float32)
    cos, sin = np.cos(ang), np.sin(ang)
    lane = np.arange(LANES)
    idx = lane % half
    lower = (lane % DA_HEAD_DIM) < half
    c = cos[:, idx]
    s = sin[:, idx]
    z = np.zeros_like(s)
    return (jnp.asarray(c, F32), jnp.asarray(np.where(lower[None, :], -s, z), F32),
            jnp.asarray(np.where(lower[None, :], z, s), F32))


def _tri_dot(a, b):
    return jnp.dot(a.astype(F32), b.astype(F32), precision=lax.Precision.HIGHEST).astype(I32)


def kernel(x, w_in, conv_w, conv_b, gate_b, lam_qk, subln_g, mhnorm_g, w_out, g_mix, g_ffn, w_grp,
           b_grp, w_erouter, b_erouter, w1, w3, w2, g_final):
    B, S, D = x.shape
    T = B * S
    nt = T // TM
    x2d = x.reshape(T, D)
    l = 0

    w_main = w_in[l, :, :N_MAIN].astype(BF16)
    wg8 = w_in[l, :, N_MAIN:]
    wg = jnp.pad(wg8, ((0, 0), (0, LANES - 8))).astype(BF16)
    wgt = wg8.T.astype(BF16)
    gb8 = gate_b[l].reshape(8)
    gb_col = jnp.pad(gb8, (0, LANES - 8)).reshape(1, LANES)
    gb_row = jnp.broadcast_to(gb8[:, None], (8, LANES))
    rope_c, rope_sa, rope_sb = _rope_tables(S)
    w_r = jnp.concatenate(
        [w_grp[l], w_erouter[l].transpose(1, 0, 2).reshape(D, N_EXPERTS)], axis=1)
    w_r = jnp.pad(w_r, ((0, 0), (0, LANES - w_r.shape[1])))
    wr_hi = w_r.astype(BF16)
    wr_t = jnp.concatenate([wr_hi, (w_r - wr_hi.astype(F32)).astype(BF16)], axis=1)
    b_r = jnp.pad(jnp.concatenate([b_grp[l], b_erouter[l].reshape(N_EXPERTS)]),
                  (0, LANES - N_GROUPS - N_EXPERTS))
    b_rt = jnp.broadcast_to(b_r[:, None], (LANES, TM))

    q1, q2, kr, vat, qc, kc, vmt, om, gcol, grow = _inproj(
        x2d, g_mix[l].reshape(1, D), w_main, wg, wgt, rope_c, rope_sa, rope_sb,
        conv_w[l], conv_b[l].reshape(1, -1), S)
    oa = _attention(q1, q2, kr, vat, lam_qk[l], subln_g[l].reshape(1, LANES), B, S)
    hm = _mlstm(qc, kc, vmt, om, gcol, grow, gb_col, gb_row, mhnorm_g[l].reshape(1, ML_WIDTH), B, S)
    x2, hn, rrow, rcol, cnt3 = _router(oa, hm, x2d, w_out[l].astype(BF16), g_ffn[l].reshape(1, D),
                                       wr_t, b_rt)

    cnt = cnt3[:, 0, EXP_ROW0:EXP_ROW0 + N_EXPERTS]
    c8 = (cnt + SUB - 1) // SUB * SUB
    c_slot = jnp.maximum((cnt + CCH - 1) // CCH, 1) * CCH
    odd = (np.arange(nt) % 2)[:, None]
    c8_even = c8 * (1 - odd)
    c8_odd = c8 * odd
    even_tot = jnp.sum(c8_even, axis=0)
    count8 = even_tot + DCH + jnp.sum(c8_odd, axis=0)
    pcount = (count8 + DCH + BLK - 1) // BLK * BLK
    up_e = np.triu(np.ones((N_EXPERTS, N_EXPERTS), np.float32), 1)
    lo_t = np.tril(np.ones((nt, nt), np.float32), -1)
    pstart = _tri_dot(pcount[None, :], up_e)[0]
    pend = pstart + pcount
    dst = pstart[None, :] + jnp.where(odd == 0, _tri_dot(lo_t, c8_even),
                                      (even_tot + DCH)[None, :] + _tri_dot(lo_t, c8_odd))
    lo8 = _tri_dot(c8, up_e)
    lo_slot = _tri_dot(c_slot, up_e)
    nb = (2 * T + nt * N_EXPERTS * (SUB - 1) + N_EXPERTS * (2 * DCH + BLK - 1)) // BLK + 1
    blk_row = jnp.arange(nb, dtype=I32) * BLK
    blk_e = jnp.minimum(jnp.sum((pend[None, :] <= blk_row[:, None]).astype(I32), axis=1),
                        N_EXPERTS - 1).astype(I32)
    nused = (pend[-1] // BLK).astype(I32).reshape(1)
    cnt_f = cnt.reshape(-1).astype(I32)
    dst_f = dst.reshape(-1).astype(I32)

    spare = jnp.full((N_EXPERTS,), DCH, I32)
    gap_start = jnp.concatenate([pstart + even_tot, pstart + count8, pstart + count8 + DCH,
                                 pend[-1:]]).astype(I32)
    gap_rows = jnp.concatenate([spare, spare, pcount - count8 - DCH,
                                nb * BLK - pend[-1:]]).astype(I32)
    extra = jnp.sum(jnp.maximum((cnt + DCH - 1) // DCH - 1, 0), axis=1).astype(I32)
    xb = _dispatch(cnt_f, lo8.reshape(-1), dst_f, extra, gap_start, gap_rows, hn, rrow, rcol, nb)
    yb = _experts(blk_e, nused, xb, w1[l], w3[l], w2[l], nb)
    used = jnp.sum(c_slot, axis=1).astype(I32)
    out = _combine(cnt_f, lo_slot.reshape(-1), dst_f, used, extra, yb, x2, rcol, g_final.reshape(1, D))
    return out.reshape(B, S, D)
```
